```python
import jax, jax.numpy as jnp
from jax import lax
import numpy as np

D_MODEL = 2048
BATCH = 2
SEQ = 4096
DEPTH = 1

SWA_Q_HEADS = 32
SWA_KV_HEADS = 4
SWA_HEAD_DIM = 64
SWA_WINDOW = 128
SWA_BLOCK = 128
ROPE_THETA = 500000.0
ROPE_DIM = SWA_HEAD_DIM // 4
DN_K_HEADS = 16
DN_V_HEADS = 32
DN_HEAD_K = 128
DN_HEAD_V = 128
DN_CONV = 4
DN_CHUNK = 64
EPS = 1e-6

SWA_Q = SWA_Q_HEADS * SWA_HEAD_DIM
SWA_KV = SWA_KV_HEADS * SWA_HEAD_DIM
DN_KEY = DN_K_HEADS * DN_HEAD_K
DN_VAL = DN_V_HEADS * DN_HEAD_V
DN_CONV_CH = 2 * DN_KEY + DN_VAL
IN_SIZES = (SWA_Q, SWA_KV, SWA_KV, SWA_Q, DN_CONV_CH, DN_VAL, DN_V_HEADS, DN_V_HEADS, D_MODEL, D_MODEL)
IN_WIDTH = sum(IN_SIZES)

kernel_name = "hybrid_swa_sink_gated_deltanet_adaln"


def _split_points():
    pts, acc = [], 0
    for s in IN_SIZES[:-1]:
        acc += s
        pts.append(acc)
    return pts


def rms_norm(x, w):
    xf = x.astype(jnp.float32)
    y = xf * lax.rsqrt(jnp.mean(xf * xf, axis=-1, keepdims=True) + EPS)
    return (y * w.astype(jnp.float32)).astype(x.dtype)


def l2_norm(x):
    xf = x.astype(jnp.float32)
    return xf * lax.rsqrt(jnp.sum(xf * xf, axis=-1, keepdims=True) + EPS)


def partial_rope(x, positions):
    half = ROPE_DIM // 2
    inv_freq = ROPE_THETA ** (-jnp.arange(half, dtype=jnp.float32) * (2.0 / ROPE_DIM))
    ang = positions.astype(jnp.float32)[..., None] * inv_freq
    cos = jnp.cos(ang)[:, :, None, :]
    sin = jnp.sin(ang)[:, :, None, :]
    xr = x[..., :ROPE_DIM].astype(jnp.float32)
    x1, x2 = xr[..., :half], xr[..., half:]
    rot = jnp.concatenate([x1 * cos - x2 * sin, x2 * cos + x1 * sin], axis=-1).astype(x.dtype)
    return jnp.concatenate([rot, x[..., ROPE_DIM:]], axis=-1)


def swa_attention(q, k, v, sinks):
    B, T = q.shape[0], q.shape[1]
    nb = T // SWA_BLOCK
    G = SWA_Q_HEADS // SWA_KV_HEADS
    f32 = jnp.float32
    qb = q.astype(f32).reshape(B, nb, SWA_BLOCK, SWA_KV_HEADS, G, SWA_HEAD_DIM)

    def band(t):
        tb = t.astype(f32).reshape(B, nb, SWA_BLOCK, SWA_KV_HEADS, SWA_HEAD_DIM)
        prev = jnp.concatenate([jnp.zeros_like(tb[:, :1]), tb[:, :-1]], axis=1)
        return jnp.concatenate([prev, tb], axis=2)

    kb, vb = band(k), band(v)
    s = jnp.einsum('bnqhgd,bnkhd->bnhgqk', qb, kb) * (SWA_HEAD_DIM ** -0.5)
    qi = jnp.arange(SWA_BLOCK)[:, None]
    kj = jnp.arange(2 * SWA_BLOCK)[None, :]
    rel = qi + SWA_BLOCK - kj
    in_window = (rel >= 0) & (rel < SWA_WINDOW)
    first = (jnp.arange(nb) == 0)[:, None, None]
    pad_key = (kj < SWA_BLOCK)[None]
    valid = in_window[None] & ~(first & pad_key)
    s = jnp.where(valid[None, :, None, None], s, -jnp.inf)
    sink = sinks.astype(f32).reshape(SWA_KV_HEADS, G)[None, None, :, :, None, None]
    m = jnp.maximum(jnp.max(s, axis=-1, keepdims=True), sink)
    p = jnp.exp(s - m)
    denom = jnp.sum(p, axis=-1, keepdims=True) + jnp.exp(sink - m)
    o = jnp.einsum('bnhgqk,bnkhd->bnqhgd', p / denom, vb)
    return o.reshape(B, T, SWA_Q)


def causal_conv_silu(x, w):
    T = x.shape[1]
    xp = jnp.pad(x, ((0, 0), (DN_CONV - 1, 0), (0, 0)))
    y = xp[:, 0:T] * w[0]
    for j in range(1, DN_CONV):
        y = y + xp[:, j:j + T] * w[j]
    return jax.nn.silu(y)


def gated_delta_rule(q, k, v, g, beta):
    B, T, H, dk = q.shape
    dv = v.shape[-1]
    C = DN_CHUNK
    N = T // C
    f32 = jnp.float32

    def chunks(t):
        t = t.astype(f32).reshape((B, N, C, H) + t.shape[3:])
        return jnp.moveaxis(t, 3, 1)

    q = chunks(q) * (dk ** -0.5)
    k, v, beta, g = chunks(k), chunks(v), chunks(beta), chunks(g)
    g = jnp.cumsum(g, axis=-1)
    tril = jnp.tril(jnp.ones((C, C), dtype=bool))
    strict = jnp.tril(jnp.ones((C, C), dtype=bool), -1)
    decay = jnp.exp(jnp.where(tril, g[..., :, None] - g[..., None, :], -jnp.inf))
    k_beta = k * beta[..., None]
    v_beta = v * beta[..., None]
    L = jnp.where(strict, jnp.einsum('bhncd,bhnsd->bhncs', k_beta, k) * decay, 0.0)
    eye = jnp.eye(C, dtype=f32)
    t_inv = lax.linalg.triangular_solve(eye + L, jnp.broadcast_to(eye, L.shape),
                                        left_side=True, lower=True, unit_diagonal=True)
    u = t_inv @ v_beta
    w = t_inv @ (k_beta * jnp.exp(g)[..., None])
    qk = jnp.where(tril, jnp.einsum('bhncd,bhnsd->bhncs', q, k) * decay, 0.0)
    q_dec = q * jnp.exp(g)[..., None]
    k_dec = k * jnp.exp(g[..., -1:] - g)[..., None]
    g_last = jnp.exp(g[..., -1])

    def step(S, xs):
        u_c, w_c, qk_c, qd_c, kd_c, gl_c = xs
        v_new = u_c - w_c @ S
        o = qd_c @ S + qk_c @ v_new
        S = S * gl_c[..., None, None] + jnp.swapaxes(kd_c, -1, -2) @ v_new
        return S, o

    xs = tuple(jnp.moveaxis(t, 2, 0) for t in (u, w, qk, q_dec, k_dec, g_last))
    S0 = jnp.zeros((B, H, dk, dv), f32)
    _, o = lax.scan(step, S0, xs)
    return jnp.transpose(o, (1, 0, 3, 2, 4)).reshape(B, T, H, dv)


def hybrid_layer(x, c, positions, w_ada, b_ada, norm_w, w_in, q_norm_w, k_norm_w, sinks,
                 conv_w, a_log, dt_bias, dn_norm_w, w_o_swa, w_o_dn, w_out):
    B, T, _ = x.shape
    f32 = jnp.float32
    mod = jax.nn.silu(c) @ w_ada + b_ada
    shift, scale, gate = jnp.split(mod[:, None, :], 3, axis=-1)
    h = rms_norm(x, norm_w) * (1.0 + scale) + shift
    proj = h @ w_in
    aq, ak, av, ag, d_qkv, dz, db, da, mg_a, mg_b = jnp.split(proj, _split_points(), axis=-1)

    aq = rms_norm(aq.reshape(B, T, SWA_Q_HEADS, SWA_HEAD_DIM), q_norm_w)
    ak = rms_norm(ak.reshape(B, T, SWA_KV_HEADS, SWA_HEAD_DIM), k_norm_w)
    aq = partial_rope(aq, positions)
    ak = partial_rope(ak, positions)
    av = av.reshape(B, T, SWA_KV_HEADS, SWA_HEAD_DIM)
    a_out = swa_attention(aq, ak, av, sinks).astype(x.dtype) * jax.nn.silu(ag)
    y_a = a_out @ w_o_swa

    d_qkv = causal_conv_silu(d_qkv, conv_w)
    dq, dk, dv = jnp.split(d_qkv, [DN_KEY, 2 * DN_KEY], axis=-1)
    rep = DN_V_HEADS // DN_K_HEADS
    dq = jnp.repeat(l2_norm(dq.reshape(B, T, DN_K_HEADS, DN_HEAD_K)), rep, axis=2)
    dk = jnp.repeat(l2_norm(dk.reshape(B, T, DN_K_HEADS, DN_HEAD_K)), rep, axis=2)
    dv = dv.reshape(B, T, DN_V_HEADS, DN_HEAD_V)
    beta = jax.nn.sigmoid(db.astype(f32))
    g = -jnp.exp(a_log.astype(f32)) * jax.nn.softplus(da.astype(f32) + dt_bias.astype(f32))
    o = gated_delta_rule(dq, dk, dv, g, beta)
    o = rms_norm(o, dn_norm_w) * jax.nn.silu(dz.reshape(B, T, DN_V_HEADS, DN_HEAD_V).astype(f32))
    y_b = o.reshape(B, T, DN_VAL).astype(x.dtype) @ w_o_dn

    y = jax.nn.sigmoid(mg_a) * y_a + jax.nn.sigmoid(mg_b) * y_b
    return x + gate * (y @ w_out)


def setup_inputs(seed: int = 0) -> dict:
    key = jax.random.key(seed)
    ks = jax.random.split(key, 20)
    D = D_MODEL
    nrm = jax.random.normal
    x = nrm(ks[0], (BATCH, SEQ, D), jnp.float32)
    c = nrm(ks[1], (BATCH, D), jnp.float32)
    offs = jax.random.randint(ks[2], (BATCH, 1), 0, 4096, dtype=jnp.int32)
    positions = (jnp.arange(SEQ, dtype=jnp.int32)[None, :] + offs).astype(jnp.int32)
    w_ada = nrm(ks[3], (DEPTH, D, 3 * D), jnp.float32) * (0.5 * D ** -0.5)
    b_ada = nrm(ks[4], (DEPTH, 3 * D), jnp.float32) * 0.01
    norm_w = 1.0 + 0.02 * nrm(ks[5], (DEPTH, D), jnp.float32)
    w_in = nrm(ks[6], (DEPTH, D, IN_WIDTH), jnp.float32) * (D ** -0.5)
    q_norm_w = 1.0 + 0.02 * nrm(ks[7], (DEPTH, SWA_HEAD_DIM), jnp.float32)
    k_norm_w = 1.0 + 0.02 * nrm(ks[8], (DEPTH, SWA_HEAD_DIM), jnp.float32)
    sinks = nrm(ks[9], (DEPTH, SWA_Q_HEADS), jnp.float32)
    conv_w = nrm(ks[10], (DEPTH, DN_CONV, DN_CONV_CH), jnp.float32) * 0.5
    a_log = jnp.log(jax.random.uniform(ks[11], (DEPTH, DN_V_HEADS), jnp.float32, 1.0, 16.0))
    dt = jnp.exp(jax.random.uniform(ks[12], (DEPTH, DN_V_HEADS), jnp.float32,
                                    float(np.log(1e-3)), float(np.log(1e-1))))
    dt_bias = dt + jnp.log(-jnp.expm1(-dt))
    dn_norm_w = 1.0 + 0.02 * nrm(ks[13], (DEPTH, DN_HEAD_V), jnp.float32)
    w_o_swa = nrm(ks[14], (DEPTH, SWA_Q, D), jnp.float32) * (SWA_Q ** -0.5)
    w_o_dn = nrm(ks[15], (DEPTH, DN_VAL, D), jnp.float32) * (DN_VAL ** -0.5)
    w_out = nrm(ks[16], (DEPTH, D, D), jnp.float32) * (D ** -0.5)
    return {"x": x, "c": c, "positions": positions, "w_ada": w_ada, "b_ada": b_ada,
            "norm_w": norm_w, "w_in": w_in, "q_norm_w": q_norm_w, "k_norm_w": k_norm_w,
            "sinks": sinks, "conv_w": conv_w, "a_log": a_log, "dt_bias": dt_bias,
            "dn_norm_w": dn_norm_w, "w_o_swa": w_o_swa, "w_o_dn": w_o_dn, "w_out": w_out}


def reference(x, c, positions, w_ada, b_ada, norm_w, w_in, q_norm_w, k_norm_w, sinks,
              conv_w, a_log, dt_bias, dn_norm_w, w_o_swa, w_o_dn, w_out):
    for l in range(DEPTH):
        x = hybrid_layer(x, c, positions, w_ada[l], b_ada[l], norm_w[l], w_in[l], q_norm_w[l],
                         k_norm_w[l], sinks[l], conv_w[l], a_log[l], dt_bias[l], dn_norm_w[l],
                         w_o_swa[l], w_o_dn[l], w_out[l])
    return x
```

```python
import functools

import numpy as np
import jax
import jax.numpy as jnp
from jax import lax
from jax.experimental import pallas as pl
from jax.experimental.pallas import tpu as pltpu

F32 = jnp.float32
BF16 = jnp.bfloat16

D_MODEL = 2048
SWA_Q_HEADS = 32
SWA_KV_HEADS = 4
SWA_HEAD_DIM = 64
SWA_BLOCK = 128
ROPE_THETA = 500000.0
ROPE_DIM = SWA_HEAD_DIM // 4
DN_K_HEADS = 16
DN_V_HEADS = 32
DN_HEAD = 128
DN_CONV = 4
DN_CHUNK = 64
EPS = 1e-6

SWA_Q = SWA_Q_HEADS * SWA_HEAD_DIM
SWA_KV = SWA_KV_HEADS * SWA_HEAD_DIM
DN_KEY = DN_K_HEADS * DN_HEAD
DN_VAL = DN_V_HEADS * DN_HEAD
DN_CONV_CH = 2 * DN_KEY + DN_VAL

LANES = 128
SUBLANES = 8
VMEM_LIMIT = 56 * 1024 * 1024


def _cparams(sem):
    return pltpu.CompilerParams(dimension_semantics=sem, vmem_limit_bytes=VMEM_LIMIT)


def _sigmoid(x):
    return 1.0 / (1.0 + jnp.exp(-x))


def _silu(x):
    return x * _sigmoid(x)


def _adaln_body(c_ref, w_ref, b_ref, o_ref):
    s = _silu(c_ref[...]).astype(BF16)
    o_ref[...] = jnp.dot(s, w_ref[...].astype(BF16), preferred_element_type=F32) + b_ref[...]


def _adaln(c, w_ada, b_ada, tn=512):
    B, D = c.shape
    n_out = w_ada.shape[1]
    c8 = jnp.zeros((SUBLANES, D), F32).at[:B].set(c)
    out = pl.pallas_call(
        _adaln_body,
        grid=(n_out // tn,),
        in_specs=[pl.BlockSpec((SUBLANES, D), lambda j: (0, 0)),
                  pl.BlockSpec((D, tn), lambda j: (0, j)),
                  pl.BlockSpec((1, tn), lambda j: (0, j))],
        out_specs=pl.BlockSpec((SUBLANES, tn), lambda j: (0, j)),
        out_shape=jax.ShapeDtypeStruct((SUBLANES, n_out), F32),
        compiler_params=_cparams(("arbitrary",)),
        name="adaln_mod",
    )(c8, w_ada, b_ada.reshape(1, n_out))
    return out[:B]


def _normmod_body(x_ref, nw_ref, sc_ref, sh_ref, o_ref):
    x = x_ref[...]
    ms = jnp.mean(x * x, axis=-1, keepdims=True)
    y = x * lax.rsqrt(ms + EPS) * nw_ref[...]
    o_ref[...] = (y * (1.0 + sc_ref[0]) + sh_ref[0]).astype(BF16)


def _normmod(x2d, norm_w, scale, shift, T, tm=512):
    N, D = x2d.shape
    B = scale.shape[0]
    tps = T // tm
    return pl.pallas_call(
        _normmod_body,
        grid=(N // tm,),
        in_specs=[pl.BlockSpec((tm, D), lambda i: (i, 0)),
                  pl.BlockSpec((1, D), lambda i: (0, 0)),
                  pl.BlockSpec((1, 1, D), lambda i: (i // tps, 0, 0)),
                  pl.BlockSpec((1, 1, D), lambda i: (i // tps, 0, 0))],
        out_specs=pl.BlockSpec((tm, D), lambda i: (i, 0)),
        out_shape=jax.ShapeDtypeStruct((N, D), BF16),
        compiler_params=_cparams(("arbitrary",)),
        name="norm_mod",
    )(x2d, norm_w.reshape(1, D), scale.reshape(B, 1, D), shift.reshape(B, 1, D))


def _proj_body(h_ref, w_ref, o_ref, *, act):
    acc = jnp.dot(h_ref[...], w_ref[...], preferred_element_type=F32)
    if act == "silu":
        acc = _silu(acc)
    elif act == "sigmoid":
        acc = _sigmoid(acc)
    o_ref[...] = acc.astype(o_ref.dtype)


def _proj(h, w, out_dtype, act=None, tm=1024, tn=512, name="proj"):
    N, K = h.shape
    n_out = w.shape[1]
    tm = min(tm, N)
    tn = min(tn, n_out)
    return pl.pallas_call(
        functools.partial(_proj_body, act=act),
        grid=(N // tm, n_out // tn),
        in_specs=[pl.BlockSpec((tm, K), lambda i, j: (i, 0)),
                  pl.BlockSpec((K, tn), lambda i, j: (0, j))],
        out_specs=pl.BlockSpec((tm, tn), lambda i, j: (i, j)),
        out_shape=jax.ShapeDtypeStruct((N, n_out), out_dtype),
        compiler_params=_cparams(("arbitrary", "arbitrary")),
        name=name,
    )(h, w)


def _rope_inv_freq_lanes():
    half = ROPE_DIM // 2
    inv = (np.float32(ROPE_THETA) ** (-np.arange(half, dtype=np.float32) * np.float32(2.0 / ROPE_DIM))).astype(np.float32)
    lanes = np.zeros((1, LANES), np.float32)
    for l in range(LANES):
        m = l % SWA_HEAD_DIM
        if m < ROPE_DIM:
            lanes[0, l] = inv[m % half]
    return lanes


def _swa_body(sink_ref, q_ref, kv_ref, g_ref, pos_ref, invf_ref, qw_ref, kw_ref, o_ref,
              kprev, vprev):
    n = pl.program_id(1)
    BLK = SWA_BLOCK
    half = ROPE_DIM // 2

    @pl.when(n == 0)
    def _():
        kprev[...] = jnp.zeros_like(kprev)
        vprev[...] = jnp.zeros_like(vprev)

    lane = lax.broadcasted_iota(jnp.int32, (1, LANES), 1)
    l64 = lane & (SWA_HEAD_DIM - 1)
    is_a = l64 < half
    is_b = (l64 >= half) & (l64 < ROPE_DIM)
    lo = lane < SWA_HEAD_DIM

    ang = pos_ref[...] * invf_ref[...]
    cs = jnp.cos(ang)
    sn = jnp.sin(ang)
    s1 = jnp.where(is_b, sn, 0.0)
    s2 = jnp.where(is_a, -sn, 0.0)

    def normrope(x, w):
        x2 = x * x
        s_lo = jnp.sum(jnp.where(lo, x2, 0.0), axis=-1, keepdims=True)
        s_hi = jnp.sum(jnp.where(lo, 0.0, x2), axis=-1, keepdims=True)
        r = jnp.where(lo, lax.rsqrt(s_lo * (1.0 / SWA_HEAD_DIM) + EPS),
                      lax.rsqrt(s_hi * (1.0 / SWA_HEAD_DIM) + EPS))
        xn = x * r * w
        return xn * cs + pltpu.roll(xn, half, 1) * s1 + pltpu.roll(xn, LANES - half, 1) * s2

    kw = kw_ref[...]
    qw = qw_ref[...]
    n_kc = SWA_KV // LANES
    kv = kv_ref[...]
    kband, vband = [], []
    for variant in range(2):
        kband.append([])
        vband.append([])
    for c in range(n_kc):
        kc = normrope(kv[:, c * LANES:(c + 1) * LANES].astype(F32), kw)
        vc = kv[:, SWA_KV + c * LANES:SWA_KV + (c + 1) * LANES].astype(F32)
        for variant in range(2):
            if variant == 1:
                kc_v = pltpu.roll(kc, SWA_HEAD_DIM, 1)
                vc_v = pltpu.roll(vc, SWA_HEAD_DIM, 1)
            else:
                kc_v, vc_v = kc, vc
            kc_b = kc_v.astype(BF16)
            vc_b = vc_v.astype(BF16)
            idx = variant * n_kc + c
            kband[variant].append(jnp.concatenate([kprev[idx], kc_b], axis=0))
            vband[variant].append(jnp.concatenate([vprev[idx], vc_b], axis=0))
            kprev[idx] = kc_b
            vprev[idx] = vc_b

    qi = lax.broadcasted_iota(jnp.int32, (BLK, 2 * BLK), 0)
    kj = lax.broadcasted_iota(jnp.int32, (BLK, 2 * BLK), 1)
    d = kj - qi
    kmin = jnp.where(n == 0, BLK, 0)
    valid = (d >= 1) & (d <= BLK) & (kj >= kmin)

    G = SWA_Q_HEADS // SWA_KV_HEADS
    for qc in range(SWA_Q // LANES):
        qn = normrope(q_ref[:, qc * LANES:(qc + 1) * LANES].astype(F32), qw) * (SWA_HEAD_DIM ** -0.5)
        outs = []
        for a in range(2):
            h = 2 * qc + a
            j = h // G
            cj, b = j // 2, j % 2
            variant = 0 if a == b else 1
            qm = jnp.where(lo if a == 0 else jnp.logical_not(lo), qn, 0.0).astype(BF16)
            s = lax.dot_general(qm, kband[variant][cj], (((1,), (1,)), ((), ())),
                                preferred_element_type=F32)
            s = jnp.where(valid, s, -jnp.inf)
            sink = sink_ref[h]
            m = jnp.maximum(jnp.max(s, axis=-1, keepdims=True), sink)
            p = jnp.exp(s - m)
            denom = jnp.sum(p, axis=-1, keepdims=True) + jnp.exp(sink - m)
            o = jnp.dot(p.astype(BF16), vband[variant][cj], preferred_element_type=F32)
            outs.append(o / denom)
        o_pair = jnp.where(lo, outs[0], outs[1])
        gate = g_ref[:, qc * LANES:(qc + 1) * LANES].astype(F32)
        o_ref[:, qc * LANES:(qc + 1) * LANES] = (o_pair * gate).astype(BF16)


def _swa(p_qkv, sag, posf, q_norm_w, k_norm_w, sinks, B, T):
    N = B * T
    nb = T // SWA_BLOCK
    invf = jnp.asarray(_rope_inv_freq_lanes())
    qw = jnp.tile(q_norm_w.reshape(1, SWA_HEAD_DIM), (1, LANES // SWA_HEAD_DIM))
    kw = jnp.tile(k_norm_w.reshape(1, SWA_HEAD_DIM), (1, LANES // SWA_HEAD_DIM))
    kvw = 2 * SWA_KV
    n_kc = SWA_KV // LANES
    row = lambda b, n: b * nb + n
    return pl.pallas_call(
        _swa_body,
        grid=(B, nb),
        in_specs=[pl.BlockSpec(memory_space=pltpu.SMEM),
                  pl.BlockSpec((SWA_BLOCK, SWA_Q), lambda b, n: (row(b, n), 0)),
                  pl.BlockSpec((SWA_BLOCK, kvw), lambda b, n: (row(b, n), SWA_Q // kvw)),
                  pl.BlockSpec((SWA_BLOCK, SWA_Q), lambda b, n: (row(b, n), 0)),
                  pl.BlockSpec((SWA_BLOCK, 1), lambda b, n: (row(b, n), 0)),
                  pl.BlockSpec((1, LANES), lambda b, n: (0, 0)),
                  pl.BlockSpec((1, LANES), lambda b, n: (0, 0)),
                  pl.BlockSpec((1, LANES), lambda b, n: (0, 0))],
        out_specs=pl.BlockSpec((SWA_BLOCK, SWA_Q), lambda b, n: (row(b, n), 0)),
        out_shape=jax.ShapeDtypeStruct((N, SWA_Q), BF16),
        scratch_shapes=[pltpu.VMEM((2 * n_kc, SWA_BLOCK, LANES), BF16),
                        pltpu.VMEM((2 * n_kc, SWA_BLOCK, LANES), BF16)],
        compiler_params=_cparams(("arbitrary", "arbitrary")),
        name="swa_attn",
    )(sinks, p_qkv, p_qkv, sag, posf, invf, qw, kw)


def _conv_body(x_ref, halo_ref, w_ref, o_ref, buf, *, tm, tps, norm):
    i = pl.program_id(0)
    first = (i % tps) == 0
    hal = halo_ref[...].astype(F32)
    buf[0:SUBLANES, :] = jnp.where(first, 0.0, hal)
    buf[SUBLANES:SUBLANES + tm, :] = x_ref[...].astype(F32)
    w = w_ref[...]
    y = buf[SUBLANES:SUBLANES + tm, :] * w[DN_CONV - 1:DN_CONV, :]
    for j in range(1, DN_CONV):
        y = y + buf[SUBLANES - j:SUBLANES - j + tm, :] * w[DN_CONV - 1 - j:DN_CONV - j, :]
    y = _silu(y)
    tc = y.shape[1]
    for c in range(tc // DN_HEAD):
        yc = y[:, c * DN_HEAD:(c + 1) * DN_HEAD]
        if norm:
            yc = yc * lax.rsqrt(jnp.sum(yc * yc, axis=-1, keepdims=True) + EPS)
        o_ref[:, c * DN_HEAD:(c + 1) * DN_HEAD] = yc.astype(BF16)


def _conv(p_d, conv_w, col0, ncols, norm, T, tm=256, tc=1024, name="dn_conv"):
    N = p_d.shape[0]
    tps = T // tm
    cb0 = col0 // tc
    hb = tm // SUBLANES
    return pl.pallas_call(
        functools.partial(_conv_body, tm=tm, tps=tps, norm=norm),
        grid=(N // tm, ncols // tc),
        in_specs=[pl.BlockSpec((tm, tc), lambda i, j: (i, cb0 + j)),
                  pl.BlockSpec((SUBLANES, tc), lambda i, j: (jnp.maximum(i * hb - 1, 0), cb0 + j)),
                  pl.BlockSpec((DN_CONV, tc), lambda i, j: (0, cb0 + j))],
        out_specs=pl.BlockSpec((tm, tc), lambda i, j: (i, j)),
        out_shape=jax.ShapeDtypeStruct((N, ncols), BF16),
        scratch_shapes=[pltpu.VMEM((tm + SUBLANES, tc), F32)],
        compiler_params=_cparams(("arbitrary", "arbitrary")),
        name=name,
    )(p_d, p_d, conv_w)


def _gate_body(p_ref, al_ref, dt_ref, beta_ref, gc_ref):
    db = p_ref[:, 0:LANES]
    da = p_ref[:, LANES:2 * LANES]
    beta_ref[...] = _sigmoid(db)
    x = da + dt_ref[...]
    sp = jnp.maximum(x, 0.0) + jnp.log(1.0 + jnp.exp(-jnp.abs(x)))
    g = -jnp.exp(al_ref[...]) * sp
    r64 = lax.broadcasted_iota(jnp.int32, g.shape, 0) & (DN_CHUNK - 1)
    s = 1
    while s < DN_CHUNK:
        g = g + jnp.where(r64 >= s, pltpu.roll(g, s, 0), 0.0)
        s *= 2
    gc_ref[...] = g


def _gates(p_g, a_log, dt_bias, tm=512):
    N = p_g.shape[0]
    al = jnp.zeros((1, LANES), F32).at[0, :DN_V_HEADS].set(a_log)
    dt = jnp.zeros((1, LANES), F32).at[0, :DN_V_HEADS].set(dt_bias)
    return pl.pallas_call(
        _gate_body,
        grid=(N // tm,),
        in_specs=[pl.BlockSpec((tm, 2 * LANES), lambda i: (i, 0)),
                  pl.BlockSpec((1, LANES), lambda i: (0, 0)),
                  pl.BlockSpec((1, LANES), lambda i: (0, 0))],
        out_specs=[pl.BlockSpec((tm, LANES), lambda i: (i, 0)),
                   pl.BlockSpec((tm, LANES), lambda i: (i, 0))],
        out_shape=[jax.ShapeDtypeStruct((N, LANES), F32),
                   jax.ShapeDtypeStruct((N, LANES), F32)],
        compiler_params=_cparams(("arbitrary",)),
        name="dn_gates",
    )(p_g, al, dt)


def _delta_body(q_ref, k_ref, v_ref, z_ref, bc_ref, gc_ref, gr_ref, nw_ref, o_ref, S_ref,
                *, KH, TT):
    grp = pl.program_id(1)
    t = pl.program_id(2)
    C = DN_CHUNK
    NC = TT // C
    scale = DN_HEAD ** -0.5
    NT = (((1,), (1,)), ((), ()))

    @pl.when(t == 0)
    def _():
        S_ref[...] = jnp.zeros_like(S_ref)

    row = lax.broadcasted_iota(jnp.int32, (TT, TT), 0)
    col = lax.broadcasted_iota(jnp.int32, (TT, TT), 1)
    same = (row // C) == (col // C)
    tril = same & (row >= col)
    strict = same & (row > col)
    eye = (row == col).astype(F32)
    lane = lax.broadcasted_iota(jnp.int32, (TT, LANES), 1)
    lchunk = lax.broadcasted_iota(jnp.int32, (1, TT), 1) // C
    bcall = bc_ref[...]
    gcall = gc_ref[...]
    nw = nw_ref[...]

    for kh in range(KH):
        q = q_ref[:, kh * DN_HEAD:(kh + 1) * DN_HEAD]
        k = k_ref[:, kh * DN_HEAD:(kh + 1) * DN_HEAD]
        kk = lax.dot_general(k, k, NT, preferred_element_type=F32)
        qk = lax.dot_general(q, k, NT, preferred_element_type=F32)
        kf = k.astype(F32)
        qf = q.astype(F32)
        kT = kf.T
        for vh in range(2):
            hh = 2 * kh + vh
            hglob = grp * (2 * KH) + hh
            sel = lane == hglob
            bcol = jnp.sum(jnp.where(sel, bcall, 0.0), axis=-1, keepdims=True)
            gcol = jnp.sum(jnp.where(sel, gcall, 0.0), axis=-1, keepdims=True)
            grow = gr_ref[0, pl.ds(hglob, 1), :]
            dec = jnp.exp(jnp.where(tril, gcol - grow, -jnp.inf))
            L = jnp.where(strict, (bcol * kk) * dec, 0.0)
            qkm = (qk * dec * scale).astype(BF16)
            P = eye - L
            Lp = L.astype(BF16)
            step = 2
            while step < C:
                Lp_f = jnp.dot(Lp, Lp, preferred_element_type=F32)
                Lp = Lp_f.astype(BF16)
                P = P + jnp.dot(P.astype(BF16), Lp, preferred_element_type=F32)
                step *= 2
            eg = jnp.exp(gcol)
            vf = v_ref[:, hh * DN_HEAD:(hh + 1) * DN_HEAD].astype(F32)
            rhs = jnp.concatenate([vf * bcol, kf * (bcol * eg)], axis=1).astype(BF16)
            uw = jnp.dot(P.astype(BF16), rhs, preferred_element_type=F32)
            u = uw[:, :DN_HEAD]
            w = uw[:, DN_HEAD:].astype(BF16)
            qd = (qf * (eg * scale)).astype(BF16)
            glrow = jnp.zeros((1, TT), F32)
            for c in range(NC):
                glrow = jnp.where(lchunk == c, gcol[c * C + C - 1:c * C + C, :], glrow)
            kdT = kT * jnp.exp(glrow - grow)
            S = S_ref[hh]
            vn_parts, o_inter = [], []
            for c in range(NC):
                r0, r1 = c * C, (c + 1) * C
                lhs = jnp.concatenate([w[r0:r1], qd[r0:r1]], axis=0)
                res = jnp.dot(lhs, S.astype(BF16), preferred_element_type=F32)
                vn = u[r0:r1] - res[:C]
                o_inter.append(res[C:])
                vn_parts.append(vn.astype(BF16))
                pad = [jnp.zeros((C, DN_HEAD), BF16)] * (NC - 1 - c)
                vn_all = jnp.concatenate(vn_parts + pad, axis=0)
                kd_c = jnp.where(lchunk == c, kdT, 0.0).astype(BF16)
                S = S * jnp.exp(gcol[r1 - 1:r1, :]) + jnp.dot(kd_c, vn_all, preferred_element_type=F32)
            S_ref[hh] = S
            o = jnp.concatenate(o_inter, axis=0) + jnp.dot(qkm, vn_all, preferred_element_type=F32)
            ms = jnp.mean(o * o, axis=-1, keepdims=True)
            z = z_ref[:, hh * DN_HEAD:(hh + 1) * DN_HEAD].astype(F32)
            o_ref[:, hh * DN_HEAD:(hh + 1) * DN_HEAD] = (o * lax.rsqrt(ms + EPS) * nw * _silu(z)).astype(BF16)


def _delta(qk_n, vv, p_z, beta, gc, gcT, dn_norm_w, B, T, KH=2, TT=256):
    N = B * T
    nt = T // TT
    ng = DN_K_HEADS // KH
    row = lambda b, g, t: b * nt + t
    kw = KH * DN_HEAD
    return pl.pallas_call(
        functools.partial(_delta_body, KH=KH, TT=TT),
        grid=(B, ng, nt),
        in_specs=[pl.BlockSpec((TT, kw), lambda b, g, t: (row(b, g, t), g)),
                  pl.BlockSpec((TT, kw), lambda b, g, t: (row(b, g, t), DN_KEY // kw + g)),
                  pl.BlockSpec((TT, 2 * kw), lambda b, g, t: (row(b, g, t), g)),
                  pl.BlockSpec((TT, 2 * kw), lambda b, g, t: (row(b, g, t), g)),
                  pl.BlockSpec((TT, LANES), lambda b, g, t: (row(b, g, t), 0)),
                  pl.BlockSpec((TT, LANES), lambda b, g, t: (row(b, g, t), 0)),
                  pl.BlockSpec((1, DN_V_HEADS, TT), lambda b, g, t: (b, 0, t)),
                  pl.BlockSpec((1, DN_HEAD), lambda b, g, t: (0, 0))],
        out_specs=pl.BlockSpec((TT, 2 * kw), lambda b, g, t: (row(b, g, t), g)),
        out_shape=jax.ShapeDtypeStruct((N, DN_VAL), BF16),
        scratch_shapes=[pltpu.VMEM((2 * KH, DN_HEAD, DN_HEAD), F32)],
        compiler_params=_cparams(("arbitrary", "arbitrary", "arbitrary")),
        name="delta_rule",
    )(qk_n, qk_n, vv, p_z, beta, gc, gcT, dn_norm_w.reshape(1, DN_HEAD))


def _merge_body(a_ref, o_ref, ga_ref, gb_ref, wa_ref, wb_ref, y_ref):
    ya = jnp.dot(a_ref[...], wa_ref[...], preferred_element_type=F32)
    yb = jnp.dot(o_ref[...], wb_ref[...], preferred_element_type=F32)
    y = _sigmoid(ga_ref[...].astype(F32)) * ya + _sigmoid(gb_ref[...].astype(F32)) * yb
    y_ref[...] = y.astype(BF16)


def _merge(a_g, o_g, p_m, wa, wb, tm=512, tn=512):
    N = a_g.shape[0]
    D = wa.shape[1]
    nj = D // tn
    return pl.pallas_call(
        _merge_body,
        grid=(N // tm, nj),
        in_specs=[pl.BlockSpec((tm, a_g.shape[1]), lambda i, j: (i, 0)),
                  pl.BlockSpec((tm, o_g.shape[1]), lambda i, j: (i, 0)),
                  pl.BlockSpec((tm, tn), lambda i, j: (i, j)),
                  pl.BlockSpec((tm, tn), lambda i, j: (i, nj + j)),
                  pl.BlockSpec((wa.shape[0], tn), lambda i, j: (0, j)),
                  pl.BlockSpec((wb.shape[0], tn), lambda i, j: (0, j))],
        out_specs=pl.BlockSpec((tm, tn), lambda i, j: (i, j)),
        out_shape=jax.ShapeDtypeStruct((N, D), BF16),
        compiler_params=_cparams(("arbitrary", "arbitrary")),
        name="merge",
    )(a_g, o_g, p_m, p_m, wa, wb)


def _out_body(y_ref, w_ref, x_ref, g_ref, o_ref):
    acc = jnp.dot(y_ref[...], w_ref[...], preferred_element_type=F32)
    o_ref[...] = x_ref[...] + g_ref[0] * acc


def _outproj(y, w, x2d, gate, T, tm=512, tn=512):
    N, D = x2d.shape
    B = gate.shape[0]
    tps = T // tm
    return pl.pallas_call(
        _out_body,
        grid=(N // tm, D // tn),
        in_specs=[pl.BlockSpec((tm, y.shape[1]), lambda i, j: (i, 0)),
                  pl.BlockSpec((w.shape[0], tn), lambda i, j: (0, j)),
                  pl.BlockSpec((tm, tn), lambda i, j: (i, j)),
                  pl.BlockSpec((1, 1, tn), lambda i, j: (i // tps, 0, j))],
        out_specs=pl.BlockSpec((tm, tn), lambda i, j: (i, j)),
        out_shape=jax.ShapeDtypeStruct((N, D), F32),
        compiler_params=_cparams(("arbitrary", "arbitrary")),
        name="out_proj",
    )(y, w, x2d, gate.reshape(B, 1, D))


def _layer(x, c, positions, w_ada, b_ada, norm_w, w_in, q_norm_w, k_norm_w, sinks,
           conv_w, a_log, dt_bias, dn_norm_w, w_o_swa, w_o_dn, w_out):
    B, T, D = x.shape
    N = B * T
    x2d = x.reshape(N, D)

    mod = _adaln(c, w_ada, b_ada)
    shift, scale, gate = mod[:, :D], mod[:, D:2 * D], mod[:, 2 * D:]
    h = _normmod(x2d, norm_w, scale, shift, T)

    o_q, o_g = 0, SWA_Q + 2 * SWA_KV
    o_d = o_g + SWA_Q
    o_z = o_d + DN_CONV_CH
    o_b = o_z + DN_VAL
    o_a = o_b + DN_V_HEADS
    o_m = o_a + DN_V_HEADS
    wb16 = lambda lo, hi: w_in[:, lo:hi].astype(BF16)
    p_qkv = _proj(h, wb16(o_q, o_g), BF16, name="proj_qkv")
    sag = _proj(h, wb16(o_g, o_d), BF16, act="silu", name="proj_swa_gate")
    p_d = _proj(h, wb16(o_d, o_z), BF16, name="proj_dn_qkv")
    p_z = _proj(h, wb16(o_z, o_b), BF16, name="proj_dn_z")
    w_g = jnp.zeros((D, 2 * LANES), BF16)
    w_g = w_g.at[:, :DN_V_HEADS].set(wb16(o_b, o_a)).at[:, LANES:LANES + DN_V_HEADS].set(wb16(o_a, o_m))
    p_g = _proj(h, w_g, F32, name="proj_dn_gates")
    p_m = _proj(h, wb16(o_m, o_m + 2 * D), BF16, name="proj_merge_gates")

    posf = positions.astype(F32).reshape(N, 1)
    a_g = _swa(p_qkv, sag, posf, q_norm_w, k_norm_w, sinks, B, T)

    qk_n = _conv(p_d, conv_w, 0, 2 * DN_KEY, True, T, name="dn_conv_qk")
    vv = _conv(p_d, conv_w, 2 * DN_KEY, DN_VAL, False, T, name="dn_conv_v")
    beta, gc = _gates(p_g, a_log, dt_bias)
    gcT = gc[:, :DN_V_HEADS].reshape(B, T, DN_V_HEADS).transpose(0, 2, 1)
    o_g2 = _delta(qk_n, vv, p_z, beta, gc, gcT, dn_norm_w, B, T)

    y = _merge(a_g, o_g2, p_m, w_o_swa.astype(BF16), w_o_dn.astype(BF16))
    out = _outproj(y, w_out.astype(BF16), x2d, gate, T)
    return out.reshape(B, T, D)


def kernel(x, c, positions, w_ada, b_ada, norm_w, w_in, q_norm_w, k_norm_w, sinks, conv_w,
           a_log, dt_bias, dn_norm_w, w_o_swa, w_o_dn, w_out):
    depth = w_ada.shape[0]
    for l in range(depth):
        x = _layer(x, c, positions, w_ada[l], b_ada[l], norm_w[l], w_in[l], q_norm_w[l],
                   k_norm_w[l], sinks[l], conv_w[l], a_log[l], dt_bias[l], dn_norm_w[l],
                   w_o_swa[l], w_o_dn[l], w_out[l])
    return x
```

```python
import functools

import numpy as np
import jax
import jax.numpy as jnp
from jax import lax
from jax.experimental import pallas as pl
from jax.experimental.pallas import tpu as pltpu

F32 = jnp.float32
BF16 = jnp.bfloat16

D_MODEL = 2048
SWA_Q_HEADS = 32
SWA_KV_HEADS = 4
SWA_HEAD_DIM = 64
SWA_BLOCK = 128
ROPE_THETA = 500000.0
ROPE_DIM = SWA_HEAD_DIM // 4
DN_K_HEADS = 16
DN_V_HEADS = 32
DN_HEAD = 128
DN_CONV = 4
DN_CHUNK = 64
EPS = 1e-6

SWA_Q = SWA_Q_HEADS * SWA_HEAD_DIM
SWA_KV = SWA_KV_HEADS * SWA_HEAD_DIM
DN_KEY = DN_K_HEADS * DN_HEAD
DN_VAL = DN_V_HEADS * DN_HEAD
DN_CONV_CH = 2 * DN_KEY + DN_VAL

LANES = 128
SUBLANES = 8
VMEM_LIMIT = 56 * 1024 * 1024


def _cparams(sem):
    return pltpu.CompilerParams(dimension_semantics=sem, vmem_limit_bytes=VMEM_LIMIT)


def _sigmoid(x):
    return 1.0 / (1.0 + jnp.exp(-x))


def _silu(x):
    return x * _sigmoid(x)


def _adaln_body(c_ref, w_ref, b_ref, o_ref):
    s = _silu(c_ref[...]).astype(BF16)
    o_ref[...] = jnp.dot(s, w_ref[...].astype(BF16), preferred_element_type=F32) + b_ref[...]


def _adaln(c, w_ada, b_ada, tn=512):
    B, D = c.shape
    n_out = w_ada.shape[1]
    c8 = jnp.zeros((SUBLANES, D), F32).at[:B].set(c)
    out = pl.pallas_call(
        _adaln_body,
        grid=(n_out // tn,),
        in_specs=[pl.BlockSpec((SUBLANES, D), lambda j: (0, 0)),
                  pl.BlockSpec((D, tn), lambda j: (0, j)),
                  pl.BlockSpec((1, tn), lambda j: (0, j))],
        out_specs=pl.BlockSpec((SUBLANES, tn), lambda j: (0, j)),
        out_shape=jax.ShapeDtypeStruct((SUBLANES, n_out), F32),
        compiler_params=_cparams(("arbitrary",)),
        name="adaln_mod",
    )(c8, w_ada, b_ada.reshape(1, n_out))
    return out[:B]


def _normmod_body(x_ref, nw_ref, sc_ref, sh_ref, o_ref):
    x = x_ref[...]
    ms = jnp.mean(x * x, axis=-1, keepdims=True)
    y = x * lax.rsqrt(ms + EPS) * nw_ref[...]
    o_ref[...] = (y * (1.0 + sc_ref[0]) + sh_ref[0]).astype(BF16)


def _normmod(x2d, norm_w, scale, shift, T, tm=512):
    N, D = x2d.shape
    B = scale.shape[0]
    tps = T // tm
    return pl.pallas_call(
        _normmod_body,
        grid=(N // tm,),
        in_specs=[pl.BlockSpec((tm, D), lambda i: (i, 0)),
                  pl.BlockSpec((1, D), lambda i: (0, 0)),
                  pl.BlockSpec((1, 1, D), lambda i: (i // tps, 0, 0)),
                  pl.BlockSpec((1, 1, D), lambda i: (i // tps, 0, 0))],
        out_specs=pl.BlockSpec((tm, D), lambda i: (i, 0)),
        out_shape=jax.ShapeDtypeStruct((N, D), BF16),
        compiler_params=_cparams(("arbitrary",)),
        name="norm_mod",
    )(x2d, norm_w.reshape(1, D), scale.reshape(B, 1, D), shift.reshape(B, 1, D))


def _proj_body(h_ref, w_ref, o_ref, *, act):
    acc = jnp.dot(h_ref[...], w_ref[...], preferred_element_type=F32)
    if act == "silu":
        acc = _silu(acc)
    elif act == "sigmoid":
        acc = _sigmoid(acc)
    o_ref[...] = acc.astype(o_ref.dtype)


def _proj(h, w, out_dtype, act=None, tm=1024, tn=512, name="proj"):
    N, K = h.shape
    n_out = w.shape[1]
    tm = min(tm, N)
    tn = min(tn, n_out)
    return pl.pallas_call(
        functools.partial(_proj_body, act=act),
        grid=(N // tm, n_out // tn),
        in_specs=[pl.BlockSpec((tm, K), lambda i, j: (i, 0)),
                  pl.BlockSpec((K, tn), lambda i, j: (0, j))],
        out_specs=pl.BlockSpec((tm, tn), lambda i, j: (i, j)),
        out_shape=jax.ShapeDtypeStruct((N, n_out), out_dtype),
        compiler_params=_cparams(("arbitrary", "arbitrary")),
        name=name,
    )(h, w)


def _rope_inv_freq_lanes():
    half = ROPE_DIM // 2
    inv = (np.float32(ROPE_THETA) ** (-np.arange(half, dtype=np.float32) * np.float32(2.0 / ROPE_DIM))).astype(np.float32)
    lanes = np.zeros((1, LANES), np.float32)
    for l in range(LANES):
        m = l % SWA_HEAD_DIM
        if m < ROPE_DIM:
            lanes[0, l] = inv[m % half]
    return lanes


def _swa_body(sink_ref, q_ref, kv_ref, g_ref, pos_ref, invf_ref, qw_ref, kw_ref, o_ref,
              kprev, vprev):
    n = pl.program_id(1)
    BLK = SWA_BLOCK
    half = ROPE_DIM // 2

    @pl.when(n == 0)
    def _():
        kprev[...] = jnp.zeros_like(kprev)
        vprev[...] = jnp.zeros_like(vprev)

    lane = lax.broadcasted_iota(jnp.int32, (1, LANES), 1)
    l64 = lane & (SWA_HEAD_DIM - 1)
    is_a = l64 < half
    is_b = (l64 >= half) & (l64 < ROPE_DIM)
    lo = lane < SWA_HEAD_DIM

    ang = pos_ref[...] * invf_ref[...]
    cs = jnp.cos(ang)
    sn = jnp.sin(ang)
    s1 = jnp.where(is_b, sn, 0.0)
    s2 = jnp.where(is_a, -sn, 0.0)

    def normrope(x, w):
        x2 = x * x
        s_lo = jnp.sum(jnp.where(lo, x2, 0.0), axis=-1, keepdims=True)
        s_hi = jnp.sum(jnp.where(lo, 0.0, x2), axis=-1, keepdims=True)
        r = jnp.where(lo, lax.rsqrt(s_lo * (1.0 / SWA_HEAD_DIM) + EPS),
                      lax.rsqrt(s_hi * (1.0 / SWA_HEAD_DIM) + EPS))
        xn = x * r * w
        return xn * cs + pltpu.roll(xn, half, 1) * s1 + pltpu.roll(xn, LANES - half, 1) * s2

    kw = kw_ref[...]
    qw = qw_ref[...]
    n_kc = SWA_KV // LANES
    kv = kv_ref[...]
    kband, vband = [], []
    for variant in range(2):
        kband.append([])
        vband.append([])
    for c in range(n_kc):
        kc = normrope(kv[:, c * LANES:(c + 1) * LANES].astype(F32), kw)
        vc = kv[:, SWA_KV + c * LANES:SWA_KV + (c + 1) * LANES].astype(F32)
        for variant in range(2):
            if variant == 1:
                kc_v = pltpu.roll(kc, SWA_HEAD_DIM, 1)
                vc_v = pltpu.roll(vc, SWA_HEAD_DIM, 1)
            else:
                kc_v, vc_v = kc, vc
            kc_b = kc_v.astype(BF16)
            vc_b = vc_v.astype(BF16)
            idx = variant * n_kc + c
            kband[variant].append(jnp.concatenate([kprev[idx], kc_b], axis=0))
            vband[variant].append(jnp.concatenate([vprev[idx], vc_b], axis=0))
            kprev[idx] = kc_b
            vprev[idx] = vc_b

    qi = lax.broadcasted_iota(jnp.int32, (BLK, 2 * BLK), 0)
    kj = lax.broadcasted_iota(jnp.int32, (BLK, 2 * BLK), 1)
    d = kj - qi
    kmin = jnp.where(n == 0, BLK, 0)
    valid = (d >= 1) & (d <= BLK) & (kj >= kmin)

    G = SWA_Q_HEADS // SWA_KV_HEADS
    for qc in range(SWA_Q // LANES):
        qn = normrope(q_ref[:, qc * LANES:(qc + 1) * LANES].astype(F32), qw) * (SWA_HEAD_DIM ** -0.5)
        outs = []
        for a in range(2):
            h = 2 * qc + a
            j = h // G
            cj, b = j // 2, j % 2
            variant = 0 if a == b else 1
            qm = jnp.where(lo if a == 0 else jnp.logical_not(lo), qn, 0.0).astype(BF16)
            s = lax.dot_general(qm, kband[variant][cj], (((1,), (1,)), ((), ())),
                                preferred_element_type=F32)
            s = jnp.where(valid, s, -jnp.inf)
            sink = sink_ref[h]
            m = jnp.maximum(jnp.max(s, axis=-1, keepdims=True), sink)
            p = jnp.exp(s - m)
            denom = jnp.sum(p, axis=-1, keepdims=True) + jnp.exp(sink - m)
            o = jnp.dot(p.astype(BF16), vband[variant][cj], preferred_element_type=F32)
            outs.append(o / denom)
        o_pair = jnp.where(lo, outs[0], outs[1])
        gate = g_ref[:, qc * LANES:(qc + 1) * LANES].astype(F32)
        o_ref[:, qc * LANES:(qc + 1) * LANES] = (o_pair * gate).astype(BF16)


def _swa(p_qkv, sag, posf, q_norm_w, k_norm_w, sinks, B, T):
    N = B * T
    nb = T // SWA_BLOCK
    invf = jnp.asarray(_rope_inv_freq_lanes())
    qw = jnp.tile(q_norm_w.reshape(1, SWA_HEAD_DIM), (1, LANES // SWA_HEAD_DIM))
    kw = jnp.tile(k_norm_w.reshape(1, SWA_HEAD_DIM), (1, LANES // SWA_HEAD_DIM))
    kvw = 2 * SWA_KV
    n_kc = SWA_KV // LANES
    row = lambda b, n: b * nb + n
    return pl.pallas_call(
        _swa_body,
        grid=(B, nb),
        in_specs=[pl.BlockSpec(memory_space=pltpu.SMEM),
                  pl.BlockSpec((SWA_BLOCK, SWA_Q), lambda b, n: (row(b, n), 0)),
                  pl.BlockSpec((SWA_BLOCK, kvw), lambda b, n: (row(b, n), SWA_Q // kvw)),
                  pl.BlockSpec((SWA_BLOCK, SWA_Q), lambda b, n: (row(b, n), 0)),
                  pl.BlockSpec((SWA_BLOCK, 1), lambda b, n: (row(b, n), 0)),
                  pl.BlockSpec((1, LANES), lambda b, n: (0, 0)),
                  pl.BlockSpec((1, LANES), lambda b, n: (0, 0)),
                  pl.BlockSpec((1, LANES), lambda b, n: (0, 0))],
        out_specs=pl.BlockSpec((SWA_BLOCK, SWA_Q), lambda b, n: (row(b, n), 0)),
        out_shape=jax.ShapeDtypeStruct((N, SWA_Q), BF16),
        scratch_shapes=[pltpu.VMEM((2 * n_kc, SWA_BLOCK, LANES), BF16),
                        pltpu.VMEM((2 * n_kc, SWA_BLOCK, LANES), BF16)],
        compiler_params=_cparams(("arbitrary", "arbitrary")),
        name="swa_attn",
    )(sinks, p_qkv, p_qkv, sag, posf, invf, qw, kw)


def _conv_body(x_ref, halo_ref, w_ref, o_ref, buf, *, tm, tps, norm):
    i = pl.program_id(0)
    first = (i % tps) == 0
    hal = halo_ref[...].astype(F32)
    buf[0:SUBLANES, :] = jnp.where(first, 0.0, hal)
    buf[SUBLANES:SUBLANES + tm, :] = x_ref[...].astype(F32)
    w = w_ref[...]
    y = buf[SUBLANES:SUBLANES + tm, :] * w[DN_CONV - 1:DN_CONV, :]
    for j in range(1, DN_CONV):
        y = y + buf[SUBLANES - j:SUBLANES - j + tm, :] * w[DN_CONV - 1 - j:DN_CONV - j, :]
    y = _silu(y)
    tc = y.shape[1]
    for c in range(tc // DN_HEAD):
        yc = y[:, c * DN_HEAD:(c + 1) * DN_HEAD]
        if norm:
            yc = yc * lax.rsqrt(jnp.sum(yc * yc, axis=-1, keepdims=True) + EPS)
        o_ref[:, c * DN_HEAD:(c + 1) * DN_HEAD] = yc.astype(BF16)


def _conv(p_d, conv_w, col0, ncols, norm, T, tm=256, tc=1024, name="dn_conv"):
    N = p_d.shape[0]
    tps = T // tm
    cb0 = col0 // tc
    hb = tm // SUBLANES
    return pl.pallas_call(
        functools.partial(_conv_body, tm=tm, tps=tps, norm=norm),
        grid=(N // tm, ncols // tc),
        in_specs=[pl.BlockSpec((tm, tc), lambda i, j: (i, cb0 + j)),
                  pl.BlockSpec((SUBLANES, tc), lambda i, j: (jnp.maximum(i * hb - 1, 0), cb0 + j)),
                  pl.BlockSpec((DN_CONV, tc), lambda i, j: (0, cb0 + j))],
        out_specs=pl.BlockSpec((tm, tc), lambda i, j: (i, j)),
        out_shape=jax.ShapeDtypeStruct((N, ncols), BF16),
        scratch_shapes=[pltpu.VMEM((tm + SUBLANES, tc), F32)],
        compiler_params=_cparams(("arbitrary", "arbitrary")),
        name=name,
    )(p_d, p_d, conv_w)


def _gate_body(p_ref, al_ref, dt_ref, beta_ref, gc_ref):
    db = p_ref[:, 0:LANES]
    da = p_ref[:, LANES:2 * LANES]
    beta_ref[...] = _sigmoid(db)
    x = da + dt_ref[...]
    sp = jnp.maximum(x, 0.0) + jnp.log(1.0 + jnp.exp(-jnp.abs(x)))
    g = -jnp.exp(al_ref[...]) * sp
    r64 = lax.broadcasted_iota(jnp.int32, g.shape, 0) & (DN_CHUNK - 1)
    s = 1
    while s < DN_CHUNK:
        g = g + jnp.where(r64 >= s, pltpu.roll(g, s, 0), 0.0)
        s *= 2
    gc_ref[...] = g


def _gates(p_g, a_log, dt_bias, tm=512):
    N = p_g.shape[0]
    al = jnp.zeros((1, LANES), F32).at[0, :DN_V_HEADS].set(a_log)
    dt = jnp.zeros((1, LANES), F32).at[0, :DN_V_HEADS].set(dt_bias)
    return pl.pallas_call(
        _gate_body,
        grid=(N // tm,),
        in_specs=[pl.BlockSpec((tm, 2 * LANES), lambda i: (i, 0)),
                  pl.BlockSpec((1, LANES), lambda i: (0, 0)),
                  pl.BlockSpec((1, LANES), lambda i: (0, 0))],
        out_specs=[pl.BlockSpec((tm, LANES), lambda i: (i, 0)),
                   pl.BlockSpec((tm, LANES), lambda i: (i, 0))],
        out_shape=[jax.ShapeDtypeStruct((N, LANES), F32),
                   jax.ShapeDtypeStruct((N, LANES), F32)],
        compiler_params=_cparams(("arbitrary",)),
        name="dn_gates",
    )(p_g, al, dt)


def _delta_body(q_ref, k_ref, v_ref, z_ref, bc_ref, gc_ref, gr_ref, nw_ref, o_ref, S_ref,
                *, KH, TT):
    grp = pl.program_id(1)
    t = pl.program_id(2)
    C = DN_CHUNK
    NC = TT // C
    scale = DN_HEAD ** -0.5
    NT = (((1,), (1,)), ((), ()))

    @pl.when(t == 0)
    def _():
        S_ref[...] = jnp.zeros_like(S_ref)

    NV = 2 * KH
    row = lax.broadcasted_iota(jnp.int32, (TT, TT), 0)
    col = lax.broadcasted_iota(jnp.int32, (TT, TT), 1)
    same = (row // C) == (col // C)
    ri = lax.broadcasted_iota(jnp.int32, (C, TT), 0)
    cj = lax.broadcasted_iota(jnp.int32, (C, TT), 1)
    cjc = cj // C
    cjj = cj & (C - 1)
    tril_c = ri >= cjj
    strict_c = ri > cjj
    eye_c = (ri == cjj).astype(F32)
    lane = lax.broadcasted_iota(jnp.int32, (TT, LANES), 1)
    lchunk = lax.broadcasted_iota(jnp.int32, (1, TT), 1) // C
    bcall = bc_ref[...]
    gcall = gc_ref[...]
    nw = nw_ref[...]

    def compress(x):
        out = x[(NC - 1) * C:NC * C]
        for c in range(NC - 2, -1, -1):
            out = jnp.where(cjc == c, x[c * C:(c + 1) * C], out)
        return out

    def by_row(colvec):
        out = jnp.broadcast_to(colvec[(NC - 1) * C:NC * C], (C, TT))
        for c in range(NC - 2, -1, -1):
            out = jnp.where(cjc == c, colvec[c * C:(c + 1) * C], out)
        return out

    def bd(xc):
        return jnp.where(same, jnp.concatenate([xc] * NC, axis=0), 0.0).astype(BF16)

    kf, qf, kT, kkc, qkc = [], [], [], [], []
    for kh in range(KH):
        q = q_ref[:, kh * DN_HEAD:(kh + 1) * DN_HEAD]
        k = k_ref[:, kh * DN_HEAD:(kh + 1) * DN_HEAD]
        kkc.append(compress(lax.dot_general(k, k, NT, preferred_element_type=F32)))
        qkc.append(compress(lax.dot_general(q, k, NT, preferred_element_type=F32)))
        kf.append(k.astype(F32))
        qf.append(q.astype(F32))
        kT.append(kf[-1].T)

    bcol, gcol, grow, Lc, qkm, P = [], [], [], [], [], []
    for hh in range(NV):
        hglob = grp * NV + hh
        sel = lane == hglob
        bcol.append(jnp.sum(jnp.where(sel, bcall, 0.0), axis=-1, keepdims=True))
        gcol.append(jnp.sum(jnp.where(sel, gcall, 0.0), axis=-1, keepdims=True))
        grow.append(gr_ref[0, pl.ds(hglob, 1), :])
        dec = jnp.exp(jnp.where(tril_c, by_row(gcol[hh]) - grow[hh], -jnp.inf))
        Lc.append(jnp.where(strict_c, by_row(bcol[hh]) * kkc[hh // 2] * dec, 0.0))
        qkm.append(qkc[hh // 2] * dec * scale)
        P.append(eye_c - Lc[hh])

    Lp = [jnp.dot(Lc[hh].astype(BF16), bd(Lc[hh]), preferred_element_type=F32) for hh in range(NV)]
    step = 4
    while step < C:
        res = [jnp.dot(jnp.concatenate([Lp[hh], P[hh]], axis=0).astype(BF16), bd(Lp[hh]),
                       preferred_element_type=F32) for hh in range(NV)]
        Lp = [r[:C] for r in res]
        P = [P[hh] + res[hh][C:] for hh in range(NV)]
        step *= 2
    P = [P[hh] + jnp.dot(P[hh].astype(BF16), bd(Lp[hh]), preferred_element_type=F32) for hh in range(NV)]

    uw, qd, kdT, eg_last = [], [], [], []
    for hh in range(NV):
        eg = jnp.exp(gcol[hh])
        vf = v_ref[:, hh * DN_HEAD:(hh + 1) * DN_HEAD].astype(F32)
        rhs = jnp.concatenate([vf * bcol[hh], kf[hh // 2] * (bcol[hh] * eg)], axis=1).astype(BF16)
        uw.append(jnp.dot(bd(P[hh]), rhs, preferred_element_type=F32))
        qd.append((qf[hh // 2] * (eg * scale)).astype(BF16))
        glrow = jnp.zeros((1, TT), F32)
        for c in range(NC):
            glrow = jnp.where(lchunk == c, gcol[hh][c * C + C - 1:c * C + C, :], glrow)
        kdT.append(kT[hh // 2] * jnp.exp(glrow - grow[hh]))
        eg_last.append([eg[c * C + C - 1:c * C + C, :] for c in range(NC)])

    S = [S_ref[hh] for hh in range(NV)]
    vn = [[] for _ in range(NV)]
    o_inter = [[] for _ in range(NV)]
    for c in range(NC):
        r0, r1 = c * C, (c + 1) * C
        for hh in range(NV):
            w_c = uw[hh][r0:r1, DN_HEAD:].astype(BF16)
            lhs = jnp.concatenate([w_c, qd[hh][r0:r1]], axis=0)
            res = jnp.dot(lhs, S[hh].astype(BF16), preferred_element_type=F32)
            vn_c = (uw[hh][r0:r1, :DN_HEAD] - res[:C]).astype(BF16)
            vn[hh].append(vn_c)
            o_inter[hh].append(res[C:])
            kd_c = kdT[hh][:, r0:r1].astype(BF16)
            S[hh] = S[hh] * eg_last[hh][c] + jnp.dot(kd_c, vn_c, preferred_element_type=F32)

    for hh in range(NV):
        S_ref[hh] = S[hh]
        vn_all = jnp.concatenate(vn[hh], axis=0)
        o = jnp.concatenate(o_inter[hh], axis=0) + jnp.dot(bd(qkm[hh]), vn_all, preferred_element_type=F32)
        ms = jnp.mean(o * o, axis=-1, keepdims=True)
        z = z_ref[:, hh * DN_HEAD:(hh + 1) * DN_HEAD].astype(F32)
        o_ref[:, hh * DN_HEAD:(hh + 1) * DN_HEAD] = (o * lax.rsqrt(ms + EPS) * nw * _silu(z)).astype(BF16)


def _delta(qk_n, vv, p_z, beta, gc, gcT, dn_norm_w, B, T, KH=4, TT=256):
    N = B * T
    nt = T // TT
    ng = DN_K_HEADS // KH
    row = lambda b, g, t: b * nt + t
    kw = KH * DN_HEAD
    return pl.pallas_call(
        functools.partial(_delta_body, KH=KH, TT=TT),
        grid=(B, ng, nt),
        in_specs=[pl.BlockSpec((TT, kw), lambda b, g, t: (row(b, g, t), g)),
                  pl.BlockSpec((TT, kw), lambda b, g, t: (row(b, g, t), DN_KEY // kw + g)),
                  pl.BlockSpec((TT, 2 * kw), lambda b, g, t: (row(b, g, t), g)),
                  pl.BlockSpec((TT, 2 * kw), lambda b, g, t: (row(b, g, t), g)),
                  pl.BlockSpec((TT, LANES), lambda b, g, t: (row(b, g, t), 0)),
                  pl.BlockSpec((TT, LANES), lambda b, g, t: (row(b, g, t), 0)),
                  pl.BlockSpec((1, DN_V_HEADS, TT), lambda b, g, t: (b, 0, t)),
                  pl.BlockSpec((1, DN_HEAD), lambda b, g, t: (0, 0))],
        out_specs=pl.BlockSpec((TT, 2 * kw), lambda b, g, t: (row(b, g, t), g)),
        out_shape=jax.ShapeDtypeStruct((N, DN_VAL), BF16),
        scratch_shapes=[pltpu.VMEM((2 * KH, DN_HEAD, DN_HEAD), F32)],
        compiler_params=_cparams(("arbitrary", "arbitrary", "arbitrary")),
        name="delta_rule",
    )(qk_n, qk_n, vv, p_z, beta, gc, gcT, dn_norm_w.reshape(1, DN_HEAD))


def _merge_body(a_ref, o_ref, ga_ref, gb_ref, wa_ref, wb_ref, y_ref):
    ya = jnp.dot(a_ref[...], wa_ref[...], preferred_element_type=F32)
    yb = jnp.dot(o_ref[...], wb_ref[...], preferred_element_type=F32)
    y = _sigmoid(ga_ref[...].astype(F32)) * ya + _sigmoid(gb_ref[...].astype(F32)) * yb
    y_ref[...] = y.astype(BF16)


def _merge(a_g, o_g, p_m, wa, wb, tm=512, tn=512):
    N = a_g.shape[0]
    D = wa.shape[1]
    nj = D // tn
    return pl.pallas_call(
        _merge_body,
        grid=(N // tm, nj),
        in_specs=[pl.BlockSpec((tm, a_g.shape[1]), lambda i, j: (i, 0)),
                  pl.BlockSpec((tm, o_g.shape[1]), lambda i, j: (i, 0)),
                  pl.BlockSpec((tm, tn), lambda i, j: (i, j)),
                  pl.BlockSpec((tm, tn), lambda i, j: (i, nj + j)),
                  pl.BlockSpec((wa.shape[0], tn), lambda i, j: (0, j)),
                  pl.BlockSpec((wb.shape[0], tn), lambda i, j: (0, j))],
        out_specs=pl.BlockSpec((tm, tn), lambda i, j: (i, j)),
        out_shape=jax.ShapeDtypeStruct((N, D), BF16),
        compiler_params=_cparams(("arbitrary", "arbitrary")),
        name="merge",
    )(a_g, o_g, p_m, p_m, wa, wb)


def _out_body(y_ref, w_ref, x_ref, g_ref, o_ref):
    acc = jnp.dot(y_ref[...], w_ref[...], preferred_element_type=F32)
    o_ref[...] = x_ref[...] + g_ref[0] * acc


def _outproj(y, w, x2d, gate, T, tm=512, tn=512):
    N, D = x2d.shape
    B = gate.shape[0]
    tps = T // tm
    return pl.pallas_call(
        _out_body,
        grid=(N // tm, D // tn),
        in_specs=[pl.BlockSpec((tm, y.shape[1]), lambda i, j: (i, 0)),
                  pl.BlockSpec((w.shape[0], tn), lambda i, j: (0, j)),
                  pl.BlockSpec((tm, tn), lambda i, j: (i, j)),
                  pl.BlockSpec((1, 1, tn), lambda i, j: (i // tps, 0, j))],
        out_specs=pl.BlockSpec((tm, tn), lambda i, j: (i, j)),
        out_shape=jax.ShapeDtypeStruct((N, D), F32),
        compiler_params=_cparams(("arbitrary", "arbitrary")),
        name="out_proj",
    )(y, w, x2d, gate.reshape(B, 1, D))


def _layer(x, c, positions, w_ada, b_ada, norm_w, w_in, q_norm_w, k_norm_w, sinks,
           conv_w, a_log, dt_bias, dn_norm_w, w_o_swa, w_o_dn, w_out):
    B, T, D = x.shape
    N = B * T
    x2d = x.reshape(N, D)

    mod = _adaln(c, w_ada, b_ada)
    shift, scale, gate = mod[:, :D], mod[:, D:2 * D], mod[:, 2 * D:]
    h = _normmod(x2d, norm_w, scale, shift, T)

    o_q, o_g = 0, SWA_Q + 2 * SWA_KV
    o_d = o_g + SWA_Q
    o_z = o_d + DN_CONV_CH
    o_b = o_z + DN_VAL
    o_a = o_b + DN_V_HEADS
    o_m = o_a + DN_V_HEADS
    wb16 = lambda lo, hi: w_in[:, lo:hi].astype(BF16)
    p_qkv = _proj(h, wb16(o_q, o_g), BF16, name="proj_qkv")
    sag = _proj(h, wb16(o_g, o_d), BF16, act="silu", name="proj_swa_gate")
    p_d = _proj(h, wb16(o_d, o_z), BF16, name="proj_dn_qkv")
    p_z = _proj(h, wb16(o_z, o_b), BF16, name="proj_dn_z")
    w_g = jnp.zeros((D, 2 * LANES), BF16)
    w_g = w_g.at[:, :DN_V_HEADS].set(wb16(o_b, o_a)).at[:, LANES:LANES + DN_V_HEADS].set(wb16(o_a, o_m))
    p_g = _proj(h, w_g, F32, name="proj_dn_gates")
    p_m = _proj(h, wb16(o_m, o_m + 2 * D), BF16, name="proj_merge_gates")

    posf = positions.astype(F32).reshape(N, 1)
    a_g = _swa(p_qkv, sag, posf, q_norm_w, k_norm_w, sinks, B, T)

    qk_n = _conv(p_d, conv_w, 0, 2 * DN_KEY, True, T, name="dn_conv_qk")
    vv = _conv(p_d, conv_w, 2 * DN_KEY, DN_VAL, False, T, name="dn_conv_v")
    beta, gc = _gates(p_g, a_log, dt_bias)
    gcT = gc[:, :DN_V_HEADS].reshape(B, T, DN_V_HEADS).transpose(0, 2, 1)
    o_g2 = _delta(qk_n, vv, p_z, beta, gc, gcT, dn_norm_w, B, T)

    y = _merge(a_g, o_g2, p_m, w_o_swa.astype(BF16), w_o_dn.astype(BF16))
    out = _outproj(y, w_out.astype(BF16), x2d, gate, T)
    return out.reshape(B, T, D)


def kernel(x, c, positions, w_ada, b_ada, norm_w, w_in, q_norm_w, k_norm_w, sinks, conv_w,
           a_log, dt_bias, dn_norm_w, w_o_swa, w_o_dn, w_out):
    depth = w_ada.shape[0]
    for l in range(depth):
        x = _layer(x, c, positions, w_ada[l], b_ada[l], norm_w[l], w_in[l], q_norm_w[l],
                   k_norm_w[l], sinks[l], conv_w[l], a_log[l], dt_bias[l], dn_norm_w[l],
                   w_o_swa[l], w_o_dn[l], w_out[l])
    return x
```

```python
import functools

import numpy as np
import jax
import jax.numpy as jnp
from jax import lax
from jax.experimental import pallas as pl
from jax.experimental.pallas import tpu as pltpu

F32 = jnp.float32
BF16 = jnp.bfloat16

D_MODEL = 2048
SWA_Q_HEADS = 32
SWA_KV_HEADS = 4
SWA_HEAD_DIM = 64
SWA_BLOCK = 128
ROPE_THETA = 500000.0
ROPE_DIM = SWA_HEAD_DIM // 4
DN_K_HEADS = 16
DN_V_HEADS = 32
DN_HEAD = 128
DN_CONV = 4
DN_CHUNK = 64
EPS = 1e-6

SWA_Q = SWA_Q_HEADS * SWA_HEAD_DIM
SWA_KV = SWA_KV_HEADS * SWA_HEAD_DIM
DN_KEY = DN_K_HEADS * DN_HEAD
DN_VAL = DN_V_HEADS * DN_HEAD
DN_CONV_CH = 2 * DN_KEY + DN_VAL

LANES = 128
SUBLANES = 8
VMEM_LIMIT = 56 * 1024 * 1024


def _cparams(sem):
    return pltpu.CompilerParams(dimension_semantics=sem, vmem_limit_bytes=VMEM_LIMIT)


def _sigmoid(x):
    return 0.5 * jnp.tanh(0.5 * x) + 0.5


def _silu(x):
    return x * _sigmoid(x)


def _adaln_body(c_ref, w_ref, b_ref, o_ref):
    s = _silu(c_ref[...]).astype(BF16)
    o_ref[...] = jnp.dot(s, w_ref[...].astype(BF16), preferred_element_type=F32) + b_ref[...]


def _adaln(c, w_ada, b_ada, tn=512):
    B, D = c.shape
    n_out = w_ada.shape[1]
    c8 = jnp.zeros((SUBLANES, D), F32).at[:B].set(c)
    out = pl.pallas_call(
        _adaln_body,
        grid=(n_out // tn,),
        in_specs=[pl.BlockSpec((SUBLANES, D), lambda j: (0, 0)),
                  pl.BlockSpec((D, tn), lambda j: (0, j)),
                  pl.BlockSpec((1, tn), lambda j: (0, j))],
        out_specs=pl.BlockSpec((SUBLANES, tn), lambda j: (0, j)),
        out_shape=jax.ShapeDtypeStruct((SUBLANES, n_out), F32),
        compiler_params=_cparams(("arbitrary",)),
        name="adaln_mod",
    )(c8, w_ada, b_ada.reshape(1, n_out))
    return out[:B]


def _normmod_body(x_ref, nw_ref, sc_ref, sh_ref, o_ref):
    x = x_ref[...]
    ms = jnp.mean(x * x, axis=-1, keepdims=True)
    y = x * lax.rsqrt(ms + EPS) * nw_ref[...]
    o_ref[...] = (y * (1.0 + sc_ref[0]) + sh_ref[0]).astype(BF16)


def _normmod(x2d, norm_w, scale, shift, T, tm=512):
    N, D = x2d.shape
    B = scale.shape[0]
    tps = T // tm
    return pl.pallas_call(
        _normmod_body,
        grid=(N // tm,),
        in_specs=[pl.BlockSpec((tm, D), lambda i: (i, 0)),
                  pl.BlockSpec((1, D), lambda i: (0, 0)),
                  pl.BlockSpec((1, 1, D), lambda i: (i // tps, 0, 0)),
                  pl.BlockSpec((1, 1, D), lambda i: (i // tps, 0, 0))],
        out_specs=pl.BlockSpec((tm, D), lambda i: (i, 0)),
        out_shape=jax.ShapeDtypeStruct((N, D), BF16),
        compiler_params=_cparams(("arbitrary",)),
        name="norm_mod",
    )(x2d, norm_w.reshape(1, D), scale.reshape(B, 1, D), shift.reshape(B, 1, D))


HALF = LANES // 2


def _cast_weights(w_ref, w2_ref, wbf):
    if w2_ref is None:
        wbf[...] = w_ref[...].astype(BF16)
        return
    lo = lax.broadcasted_iota(jnp.int32, (1, LANES), 1) < HALF
    nchunk = wbf.shape[1] // LANES
    prev = pltpu.roll(w_ref[:, 0:LANES], HALF, 1)
    for c in range(nchunk):
        src = w_ref[:, (c + 1) * LANES:(c + 2) * LANES] if c + 1 < nchunk else w2_ref[...]
        nxt = pltpu.roll(src, HALF, 1)
        wbf[:, c * LANES:(c + 1) * LANES] = jnp.where(lo, prev, nxt).astype(BF16)
        prev = nxt


def _cast_once(w_ref, w2_ref, wbf):
    @pl.when(pl.program_id(1) == 0)
    def _():
        _cast_weights(w_ref, w2_ref, wbf)


def _proj_act_body(*refs, act, shifted, sub):
    if shifted:
        h_ref, w_ref, w2_ref, o_ref, wbf = refs
    else:
        (h_ref, w_ref, o_ref, wbf), w2_ref = refs, None
    _cast_once(w_ref, w2_ref, wbf)
    for r0 in range(0, h_ref.shape[0], sub):
        acc = jnp.dot(h_ref[r0:r0 + sub, :], wbf[...], preferred_element_type=F32)
        if act == "silu":
            acc = _silu(acc)
        elif act == "sigmoid":
            acc = _sigmoid(acc)
        o_ref[r0:r0 + sub, :] = acc.astype(o_ref.dtype)


def _proj_conv_body(h_ref, w_ref, cw_ref, o_ref, wbf, buf, *, tm, tps, norm, sub):
    i = pl.program_id(1)
    first = (i % tps) == 0
    _cast_once(w_ref, None, wbf)

    @pl.when(first)
    def _():
        buf[0:SUBLANES, :] = jnp.zeros((SUBLANES, buf.shape[1]), F32)

    @pl.when(jnp.logical_not(first))
    def _():
        buf[0:SUBLANES, :] = buf[tm:tm + SUBLANES, :]

    cw = cw_ref[...]
    for r0 in range(0, tm, sub):
        acc = jnp.dot(h_ref[r0:r0 + sub, :], wbf[...], preferred_element_type=F32)
        b0 = SUBLANES + r0
        buf[b0:b0 + sub, :] = acc
        y = acc * cw[DN_CONV - 1:DN_CONV, :]
        win = buf[b0 - SUBLANES:b0 + sub, :]
        for d in range(1, DN_CONV):
            y = y + pltpu.roll(win, d, 0)[SUBLANES:, :] * cw[DN_CONV - 1 - d:DN_CONV - d, :]
        y = _silu(y)
        for c in range(y.shape[1] // DN_HEAD):
            yc = y[:, c * DN_HEAD:(c + 1) * DN_HEAD]
            if norm:
                yc = yc * lax.rsqrt(jnp.sum(yc * yc, axis=-1, keepdims=True) + EPS)
            o_ref[r0:r0 + sub, c * DN_HEAD:(c + 1) * DN_HEAD] = yc.astype(o_ref.dtype)


def _proj_gates_body(h_ref, w_ref, al_ref, dt_ref, beta_ref, gc_ref, wbf):
    _cast_once(w_ref, None, wbf)
    acc = jnp.dot(h_ref[...], wbf[...], preferred_element_type=F32)
    beta_ref[...] = _sigmoid(acc)
    x = pltpu.roll(acc, LANES - DN_V_HEADS, 1) + dt_ref[...]
    sp = jnp.maximum(x, 0.0) + jnp.log(1.0 + jnp.exp(-jnp.abs(x)))
    g = -jnp.exp(al_ref[...]) * sp
    r64 = lax.broadcasted_iota(jnp.int32, g.shape, 0) & (DN_CHUNK - 1)
    s = 1
    while s < DN_CHUNK:
        g = g + jnp.where(r64 >= s, pltpu.roll(g, s, 0), 0.0)
        s *= 2
    gc_ref[...] = g


def _proj(h, w_in, col0, width, mode, T, extras=(), tm=1024, tn=512, sub=128, name="proj"):
    N, K = h.shape
    out_dtype = BF16
    tm = min(tm, T)
    sub = min(sub, tm)
    off = col0 % LANES
    base = col0 - off
    shifted = off != 0
    if mode == "gates":
        tn = LANES
    assert off in (0, HALF) and base % tn == 0
    assert mode == "gates" or width % tn == 0
    jb = base // tn
    grid = (max(width // tn, 1), N // tm)
    in_specs = [pl.BlockSpec((tm, K), lambda j, i: (i, 0)),
                pl.BlockSpec((K, tn), lambda j, i: (0, jb + j))]
    args = [h, w_in]
    scratch = [pltpu.VMEM((K, tn), BF16)]
    out_spec = pl.BlockSpec((tm, tn), lambda j, i: (i, j))
    out_shape = jax.ShapeDtypeStruct((N, max(width, tn)), out_dtype)
    if shifted:
        cpt = tn // LANES
        in_specs.append(pl.BlockSpec((K, LANES), lambda j, i: (0, base // LANES + (j + 1) * cpt)))
        args.append(w_in)
    if mode == "conv":
        cw, cw_col0, norm = extras
        in_specs.append(pl.BlockSpec((DN_CONV, tn), lambda j, i: (0, cw_col0 // tn + j)))
        args.append(cw)
        scratch.append(pltpu.VMEM((tm + SUBLANES, tn), F32))
        body = functools.partial(_proj_conv_body, tm=tm, tps=T // tm, norm=norm, sub=sub)
    elif mode == "gates":
        in_specs += [pl.BlockSpec((1, LANES), lambda j, i: (0, 0))] * 2
        args += list(extras)
        body = _proj_gates_body
        out_spec = [out_spec, out_spec]
        out_shape = [jax.ShapeDtypeStruct((N, LANES), F32)] * 2
    else:
        body = functools.partial(_proj_act_body, act=mode, shifted=shifted, sub=sub if mode else tm)
    return pl.pallas_call(
        body, grid=grid, in_specs=in_specs, out_specs=out_spec, out_shape=out_shape,
        scratch_shapes=scratch,
        compiler_params=_cparams(("arbitrary", "arbitrary")),
        name=name,
    )(*args)


def _rope_inv_freq_lanes():
    half = ROPE_DIM // 2
    inv = (np.float32(ROPE_THETA) ** (-np.arange(half, dtype=np.float32) * np.float32(2.0 / ROPE_DIM))).astype(np.float32)
    lanes = np.zeros((1, LANES), np.float32)
    for l in range(LANES):
        m = l % SWA_HEAD_DIM
        if m < ROPE_DIM:
            lanes[0, l] = inv[m % half]
    return lanes


def _swa_body(sink_ref, q_ref, kv_ref, g_ref, pos_ref, invf_ref, qw_ref, kw_ref, o_ref,
              kprev, vprev):
    n = pl.program_id(1)
    BLK = SWA_BLOCK
    half = ROPE_DIM // 2

    @pl.when(n == 0)
    def _():
        kprev[...] = jnp.zeros_like(kprev)
        vprev[...] = jnp.zeros_like(vprev)

    lane = lax.broadcasted_iota(jnp.int32, (1, LANES), 1)
    l64 = lane & (SWA_HEAD_DIM - 1)
    is_a = l64 < half
    is_b = (l64 >= half) & (l64 < ROPE_DIM)
    lo = lane < SWA_HEAD_DIM

    ang = pos_ref[...] * invf_ref[...]
    cs = jnp.cos(ang)
    sn = jnp.sin(ang)
    s1 = jnp.where(is_b, sn, 0.0)
    s2 = jnp.where(is_a, -sn, 0.0)

    def normrope(x, w):
        x2 = x * x
        s_lo = jnp.sum(jnp.where(lo, x2, 0.0), axis=-1, keepdims=True)
        s_hi = jnp.sum(jnp.where(lo, 0.0, x2), axis=-1, keepdims=True)
        r = jnp.where(lo, lax.rsqrt(s_lo * (1.0 / SWA_HEAD_DIM) + EPS),
                      lax.rsqrt(s_hi * (1.0 / SWA_HEAD_DIM) + EPS))
        xn = x * r * w
        return xn * cs + pltpu.roll(xn, half, 1) * s1 + pltpu.roll(xn, LANES - half, 1) * s2

    kw = kw_ref[...]
    qw = qw_ref[...]
    n_kc = SWA_KV // LANES
    kv = kv_ref[...]
    kband, vband = [], []
    for variant in range(2):
        kband.append([])
        vband.append([])
    for c in range(n_kc):
        kc = normrope(kv[:, c * LANES:(c + 1) * LANES].astype(F32), kw)
        vc = kv[:, SWA_KV + c * LANES:SWA_KV + (c + 1) * LANES].astype(F32)
        for variant in range(2):
            if variant == 1:
                kc_v = pltpu.roll(kc, SWA_HEAD_DIM, 1)
                vc_v = pltpu.roll(vc, SWA_HEAD_DIM, 1)
            else:
                kc_v, vc_v = kc, vc
            kc_b = kc_v.astype(BF16)
            vc_b = vc_v.astype(BF16)
            idx = variant * n_kc + c
            kband[variant].append(jnp.concatenate([kprev[idx], kc_b], axis=0))
            vband[variant].append(jnp.concatenate([vprev[idx], vc_b], axis=0))
            kprev[idx] = kc_b
            vprev[idx] = vc_b

    qi = lax.broadcasted_iota(jnp.int32, (BLK, 2 * BLK), 0)
    kj = lax.broadcasted_iota(jnp.int32, (BLK, 2 * BLK), 1)
    d = kj - qi
    kmin = jnp.where(n == 0, BLK, 0)
    valid = (d >= 1) & (d <= BLK) & (kj >= kmin)

    G = SWA_Q_HEADS // SWA_KV_HEADS
    cpk = G // 2
    hi = jnp.logical_not(lo)
    for j in range(SWA_KV_HEADS):
        cj, b = j // 2, j % 2
        chunks = range(j * cpk, (j + 1) * cpk)
        qn = [normrope(q_ref[:, c * LANES:(c + 1) * LANES].astype(F32), qw) * (SWA_HEAD_DIM ** -0.5)
              for c in chunks]
        outs = []
        for a in range(2):
            variant = 0 if a == b else 1
            msk = lo if a == 0 else hi
            lhs = jnp.concatenate([jnp.where(msk, x, 0.0).astype(BF16) for x in qn], axis=0)
            s = lax.dot_general(lhs, kband[variant][cj], (((1,), (1,)), ((), ())),
                                preferred_element_type=F32)
            ps, inv = [], []
            for r, c in enumerate(chunks):
                sr = jnp.where(valid, s[r * BLK:(r + 1) * BLK], -jnp.inf)
                sink = sink_ref[2 * c + a]
                m = jnp.maximum(jnp.max(sr, axis=-1, keepdims=True), sink)
                p = jnp.exp(sr - m)
                inv.append(1.0 / (jnp.sum(p, axis=-1, keepdims=True) + jnp.exp(sink - m)))
                ps.append(p.astype(BF16))
            o = jnp.dot(jnp.concatenate(ps, axis=0), vband[variant][cj], preferred_element_type=F32)
            outs.append([o[r * BLK:(r + 1) * BLK] * inv[r] for r in range(cpk)])
        for r, c in enumerate(chunks):
            o_pair = jnp.where(lo, outs[0][r], outs[1][r])
            gate = g_ref[:, c * LANES:(c + 1) * LANES].astype(F32)
            o_ref[:, c * LANES:(c + 1) * LANES] = (o_pair * gate).astype(BF16)


def _swa(p_qkv, sag, posf, q_norm_w, k_norm_w, sinks, B, T):
    N = B * T
    nb = T // SWA_BLOCK
    invf = jnp.asarray(_rope_inv_freq_lanes())
    qw = jnp.tile(q_norm_w.reshape(1, SWA_HEAD_DIM), (1, LANES // SWA_HEAD_DIM))
    kw = jnp.tile(k_norm_w.reshape(1, SWA_HEAD_DIM), (1, LANES // SWA_HEAD_DIM))
    kvw = 2 * SWA_KV
    n_kc = SWA_KV // LANES
    row = lambda b, n: b * nb + n
    return pl.pallas_call(
        _swa_body,
        grid=(B, nb),
        in_specs=[pl.BlockSpec(memory_space=pltpu.SMEM),
                  pl.BlockSpec((SWA_BLOCK, SWA_Q), lambda b, n: (row(b, n), 0)),
                  pl.BlockSpec((SWA_BLOCK, kvw), lambda b, n: (row(b, n), SWA_Q // kvw)),
                  pl.BlockSpec((SWA_BLOCK, SWA_Q), lambda b, n: (row(b, n), 0)),
                  pl.BlockSpec((SWA_BLOCK, 1), lambda b, n: (row(b, n), 0)),
                  pl.BlockSpec((1, LANES), lambda b, n: (0, 0)),
                  pl.BlockSpec((1, LANES), lambda b, n: (0, 0)),
                  pl.BlockSpec((1, LANES), lambda b, n: (0, 0))],
        out_specs=pl.BlockSpec((SWA_BLOCK, SWA_Q), lambda b, n: (row(b, n), 0)),
        out_shape=jax.ShapeDtypeStruct((N, SWA_Q), BF16),
        scratch_shapes=[pltpu.VMEM((2 * n_kc, SWA_BLOCK, LANES), BF16),
                        pltpu.VMEM((2 * n_kc, SWA_BLOCK, LANES), BF16)],
        compiler_params=_cparams(("arbitrary", "arbitrary")),
        name="swa_attn",
    )(sinks, p_qkv, p_qkv, sag, posf, invf, qw, kw)


def _delta_body(q_ref, k_ref, v_ref, z_ref, bc_ref, gc_ref, gr_ref, nw_ref, o_ref, S_ref,
                *, KH, TT):
    grp = pl.program_id(1)
    t = pl.program_id(2)
    C = DN_CHUNK
    NC = TT // C
    scale = DN_HEAD ** -0.5
    NT = (((1,), (1,)), ((), ()))

    @pl.when(t == 0)
    def _():
        S_ref[...] = jnp.zeros_like(S_ref)

    NV = 2 * KH
    row = lax.broadcasted_iota(jnp.int32, (TT, TT), 0)
    col = lax.broadcasted_iota(jnp.int32, (TT, TT), 1)
    same = (row // C) == (col // C)
    ri = lax.broadcasted_iota(jnp.int32, (C, TT), 0)
    cj = lax.broadcasted_iota(jnp.int32, (C, TT), 1)
    cjc = cj // C
    cjj = cj & (C - 1)
    tril_c = ri >= cjj
    strict_c = ri > cjj
    eye_c = (ri == cjj).astype(F32)
    lane = lax.broadcasted_iota(jnp.int32, (TT, LANES), 1)
    lchunk = lax.broadcasted_iota(jnp.int32, (1, TT), 1) // C
    bcall = bc_ref[...]
    gcall = gc_ref[...]
    nw = nw_ref[...]

    def compress(x):
        out = x[(NC - 1) * C:NC * C]
        for c in range(NC - 2, -1, -1):
            out = jnp.where(cjc == c, x[c * C:(c + 1) * C], out)
        return out

    def by_row(colvec):
        out = jnp.broadcast_to(colvec[(NC - 1) * C:NC * C], (C, TT))
        for c in range(NC - 2, -1, -1):
            out = jnp.where(cjc == c, colvec[c * C:(c + 1) * C], out)
        return out

    def bd(xc):
        return jnp.where(same, jnp.concatenate([xc] * NC, axis=0), 0.0).astype(BF16)

    kf, qf, kT, kkc, qkc = [], [], [], [], []
    for kh in range(KH):
        q = q_ref[:, kh * DN_HEAD:(kh + 1) * DN_HEAD]
        k = k_ref[:, kh * DN_HEAD:(kh + 1) * DN_HEAD]
        kkc.append(compress(lax.dot_general(k, k, NT, preferred_element_type=F32)))
        qkc.append(compress(lax.dot_general(q, k, NT, preferred_element_type=F32)))
        kf.append(k.astype(F32))
        qf.append(q.astype(F32))
        kT.append(kf[-1].T)

    bcol, gcol, grow, Lc, qkm, P = [], [], [], [], [], []
    for hh in range(NV):
        hglob = grp * NV + hh
        sel = lane == hglob
        bcol.append(jnp.sum(jnp.where(sel, bcall, 0.0), axis=-1, keepdims=True))
        gcol.append(jnp.sum(jnp.where(sel, gcall, 0.0), axis=-1, keepdims=True))
        grow.append(gr_ref[0, pl.ds(hglob, 1), :])
        dec = jnp.exp(jnp.where(tril_c, by_row(gcol[hh]) - grow[hh], -jnp.inf))
        Lc.append(jnp.where(strict_c, by_row(bcol[hh]) * kkc[hh // 2] * dec, 0.0))
        qkm.append(qkc[hh // 2] * dec * scale)
        P.append(eye_c - Lc[hh])

    Lp = [jnp.dot(Lc[hh].astype(BF16), bd(Lc[hh]), preferred_element_type=F32) for hh in range(NV)]
    step = 4
    while step < C:
        res = [jnp.dot(jnp.concatenate([Lp[hh], P[hh]], axis=0).astype(BF16), bd(Lp[hh]),
                       preferred_element_type=F32) for hh in range(NV)]
        Lp = [r[:C] for r in res]
        P = [P[hh] + res[hh][C:] for hh in range(NV)]
        step *= 2
    P = [P[hh] + jnp.dot(P[hh].astype(BF16), bd(Lp[hh]), preferred_element_type=F32) for hh in range(NV)]

    uw, qd, kdT, eg_last = [], [], [], []
    for hh in range(NV):
        eg = jnp.exp(gcol[hh])
        vf = v_ref[:, hh * DN_HEAD:(hh + 1) * DN_HEAD].astype(F32)
        rhs = jnp.concatenate([vf * bcol[hh], kf[hh // 2] * (bcol[hh] * eg)], axis=1).astype(BF16)
        uw.append(jnp.dot(bd(P[hh]), rhs, preferred_element_type=F32))
        qd.append((qf[hh // 2] * (eg * scale)).astype(BF16))
        glrow = jnp.zeros((1, TT), F32)
        for c in range(NC):
            glrow = jnp.where(lchunk == c, gcol[hh][c * C + C - 1:c * C + C, :], glrow)
        kdT.append(kT[hh // 2] * jnp.exp(glrow - grow[hh]))
        eg_last.append([eg[c * C + C - 1:c * C + C, :] for c in range(NC)])

    S = [S_ref[hh] for hh in range(NV)]
    vn = [[] for _ in range(NV)]
    o_inter = [[] for _ in range(NV)]
    for c in range(NC):
        r0, r1 = c * C, (c + 1) * C
        for hh in range(NV):
            w_c = uw[hh][r0:r1, DN_HEAD:].astype(BF16)
            lhs = jnp.concatenate([w_c, qd[hh][r0:r1]], axis=0)
            res = jnp.dot(lhs, S[hh].astype(BF16), preferred_element_type=F32)
            vn_c = (uw[hh][r0:r1, :DN_HEAD] - res[:C]).astype(BF16)
            vn[hh].append(vn_c)
            o_inter[hh].append(res[C:])
            kd_c = kdT[hh][:, r0:r1].astype(BF16)
            S[hh] = S[hh] * eg_last[hh][c] + jnp.dot(kd_c, vn_c, preferred_element_type=F32)

    for hh in range(NV):
        S_ref[hh] = S[hh]
        vn_all = jnp.concatenate(vn[hh], axis=0)
        o = jnp.concatenate(o_inter[hh], axis=0) + jnp.dot(bd(qkm[hh]), vn_all, preferred_element_type=F32)
        ms = jnp.mean(o * o, axis=-1, keepdims=True)
        z = z_ref[:, hh * DN_HEAD:(hh + 1) * DN_HEAD].astype(F32)
        o_ref[:, hh * DN_HEAD:(hh + 1) * DN_HEAD] = (o * lax.rsqrt(ms + EPS) * nw * z).astype(BF16)


def _delta(qk_n, vv, sz, beta, gc, gcT, dn_norm_w, B, T, KH=4, TT=256):
    N = B * T
    nt = T // TT
    ng = DN_K_HEADS // KH
    row = lambda b, g, t: b * nt + t
    kw = KH * DN_HEAD
    return pl.pallas_call(
        functools.partial(_delta_body, KH=KH, TT=TT),
        grid=(B, ng, nt),
        in_specs=[pl.BlockSpec((TT, kw), lambda b, g, t: (row(b, g, t), g)),
                  pl.BlockSpec((TT, kw), lambda b, g, t: (row(b, g, t), DN_KEY // kw + g)),
                  pl.BlockSpec((TT, 2 * kw), lambda b, g, t: (row(b, g, t), g)),
                  pl.BlockSpec((TT, 2 * kw), lambda b, g, t: (row(b, g, t), g)),
                  pl.BlockSpec((TT, LANES), lambda b, g, t: (row(b, g, t), 0)),
                  pl.BlockSpec((TT, LANES), lambda b, g, t: (row(b, g, t), 0)),
                  pl.BlockSpec((1, DN_V_HEADS, TT), lambda b, g, t: (b, 0, t)),
                  pl.BlockSpec((1, DN_HEAD), lambda b, g, t: (0, 0))],
        out_specs=pl.BlockSpec((TT, 2 * kw), lambda b, g, t: (row(b, g, t), g)),
        out_shape=jax.ShapeDtypeStruct((N, DN_VAL), BF16),
        scratch_shapes=[pltpu.VMEM((2 * KH, DN_HEAD, DN_HEAD), F32)],
        compiler_params=_cparams(("arbitrary", "arbitrary", "arbitrary")),
        name="delta_rule",
    )(qk_n, qk_n, vv, sz, beta, gc, gcT, dn_norm_w.reshape(1, DN_HEAD))


def _merge_body(a_ref, o_ref, ga_ref, gb_ref, wa_ref, wb_ref, y_ref):
    ya = jnp.dot(a_ref[...], wa_ref[...], preferred_element_type=F32)
    yb = jnp.dot(o_ref[...], wb_ref[...], preferred_element_type=F32)
    y = ga_ref[...].astype(F32) * ya + gb_ref[...].astype(F32) * yb
    y_ref[...] = y.astype(BF16)


def _merge(a_g, o_g, p_m, wa, wb, tm=512, tn=512):
    N = a_g.shape[0]
    D = wa.shape[1]
    nj = D // tn
    return pl.pallas_call(
        _merge_body,
        grid=(N // tm, nj),
        in_specs=[pl.BlockSpec((tm, a_g.shape[1]), lambda i, j: (i, 0)),
                  pl.BlockSpec((tm, o_g.shape[1]), lambda i, j: (i, 0)),
                  pl.BlockSpec((tm, tn), lambda i, j: (i, j)),
                  pl.BlockSpec((tm, tn), lambda i, j: (i, nj + j)),
                  pl.BlockSpec((wa.shape[0], tn), lambda i, j: (0, j)),
                  pl.BlockSpec((wb.shape[0], tn), lambda i, j: (0, j))],
        out_specs=pl.BlockSpec((tm, tn), lambda i, j: (i, j)),
        out_shape=jax.ShapeDtypeStruct((N, D), BF16),
        compiler_params=_cparams(("arbitrary", "arbitrary")),
        name="merge",
    )(a_g, o_g, p_m, p_m, wa, wb)


def _out_body(y_ref, w_ref, x_ref, g_ref, o_ref):
    acc = jnp.dot(y_ref[...], w_ref[...], preferred_element_type=F32)
    o_ref[...] = x_ref[...] + g_ref[0] * acc


def _outproj(y, w, x2d, gate, T, tm=512, tn=512):
    N, D = x2d.shape
    B = gate.shape[0]
    tps = T // tm
    return pl.pallas_call(
        _out_body,
        grid=(N // tm, D // tn),
        in_specs=[pl.BlockSpec((tm, y.shape[1]), lambda i, j: (i, 0)),
                  pl.BlockSpec((w.shape[0], tn), lambda i, j: (0, j)),
                  pl.BlockSpec((tm, tn), lambda i, j: (i, j)),
                  pl.BlockSpec((1, 1, tn), lambda i, j: (i // tps, 0, j))],
        out_specs=pl.BlockSpec((tm, tn), lambda i, j: (i, j)),
        out_shape=jax.ShapeDtypeStruct((N, D), F32),
        compiler_params=_cparams(("arbitrary", "arbitrary")),
        name="out_proj",
    )(y, w, x2d, gate.reshape(B, 1, D))


def _layer(x, c, positions, w_ada, b_ada, norm_w, w_in, q_norm_w, k_norm_w, sinks,
           conv_w, a_log, dt_bias, dn_norm_w, w_o_swa, w_o_dn, w_out):
    B, T, D = x.shape
    N = B * T
    x2d = x.reshape(N, D)

    mod = _adaln(c, w_ada, b_ada)
    shift, scale, gate = mod[:, :D], mod[:, D:2 * D], mod[:, 2 * D:]
    h = _normmod(x2d, norm_w, scale, shift, T)

    o_q, o_g = 0, SWA_Q + 2 * SWA_KV
    o_d = o_g + SWA_Q
    o_z = o_d + DN_CONV_CH
    o_b = o_z + DN_VAL
    o_a = o_b + DN_V_HEADS
    o_m = o_a + DN_V_HEADS
    assert o_a == o_b + DN_V_HEADS and o_b % LANES == 0
    p_qkv = _proj(h, w_in, o_q, o_g - o_q, None, T, name="proj_qkv")
    sag = _proj(h, w_in, o_g, o_d - o_g, "silu", T, name="proj_swa_gate")
    qk_n = _proj(h, w_in, o_d, 2 * DN_KEY, "conv", T, extras=(conv_w, 0, True), name="proj_dn_qk")
    vv = _proj(h, w_in, o_d + 2 * DN_KEY, DN_VAL, "conv", T, extras=(conv_w, 2 * DN_KEY, False),
               name="proj_dn_v")
    sz = _proj(h, w_in, o_z, o_b - o_z, "silu", T, name="proj_dn_z")
    al = jnp.zeros((1, LANES), F32).at[0, :DN_V_HEADS].set(a_log)
    dt = jnp.zeros((1, LANES), F32).at[0, :DN_V_HEADS].set(dt_bias)
    beta, gc = _proj(h, w_in, o_b, 2 * DN_V_HEADS, "gates", T, extras=(al, dt), name="proj_dn_gates")
    p_m = _proj(h, w_in, o_m, 2 * D, "sigmoid", T, name="proj_merge_gates")

    posf = positions.astype(F32).reshape(N, 1)
    a_g = _swa(p_qkv, sag, posf, q_norm_w, k_norm_w, sinks, B, T)

    gcT = gc[:, :DN_V_HEADS].reshape(B, T, DN_V_HEADS).transpose(0, 2, 1)
    o_g2 = _delta(qk_n, vv, sz, beta, gc, gcT, dn_norm_w, B, T)

    y = _merge(a_g, o_g2, p_m, w_o_swa.astype(BF16), w_o_dn.astype(BF16))
    out = _outproj(y, w_out.astype(BF16), x2d, gate, T)
    return out.reshape(B, T, D)


def kernel(x, c, positions, w_ada, b_ada, norm_w, w_in, q_norm_w, k_norm_w, sinks, conv_w,
           a_log, dt_bias, dn_norm_w, w_o_swa, w_o_dn, w_out):
    depth = w_ada.shape[0]
    for l in range(depth):
        x = _layer(x, c, positions, w_ada[l], b_ada[l], norm_w[l], w_in[l], q_norm_w[l],
                   k_norm_w[l], sinks[l], conv_w[l], a_log[l], dt_bias[l], dn_norm_w[l],
                   w_o_swa[l], w_o_dn[l], w_out[l])
    return x
```

```python
import functools

import numpy as np
import jax
import jax.numpy as jnp
from jax import lax
from jax.experimental import pallas as pl
from jax.experimental.pallas import tpu as pltpu

F32 = jnp.float32
BF16 = jnp.bfloat16

D_MODEL = 2048
SWA_Q_HEADS = 32
SWA_KV_HEADS = 4
SWA_HEAD_DIM = 64
SWA_BLOCK = 128
ROPE_THETA = 500000.0
ROPE_DIM = SWA_HEAD_DIM // 4
DN_K_HEADS = 16
DN_V_HEADS = 32
DN_HEAD = 128
DN_CONV = 4
DN_CHUNK = 64
EPS = 1e-6

SWA_Q = SWA_Q_HEADS * SWA_HEAD_DIM
SWA_KV = SWA_KV_HEADS * SWA_HEAD_DIM
DN_KEY = DN_K_HEADS * DN_HEAD
DN_VAL = DN_V_HEADS * DN_HEAD
DN_CONV_CH = 2 * DN_KEY + DN_VAL

LANES = 128
SUBLANES = 8
VMEM_LIMIT = 56 * 1024 * 1024


def _cparams(sem):
    return pltpu.CompilerParams(dimension_semantics=sem, vmem_limit_bytes=VMEM_LIMIT)


def _sigmoid(x):
    return 0.5 * jnp.tanh(0.5 * x) + 0.5


def _silu(x):
    return x * _sigmoid(x)


def _adaln_body(c_ref, w_ref, b_ref, o_ref):
    s = _silu(c_ref[...]).astype(BF16)
    o_ref[...] = jnp.dot(s, w_ref[...].astype(BF16), preferred_element_type=F32) + b_ref[...]


def _adaln(c, w_ada, b_ada, tn=512):
    B, D = c.shape
    n_out = w_ada.shape[1]
    c8 = jnp.zeros((SUBLANES, D), F32).at[:B].set(c)
    out = pl.pallas_call(
        _adaln_body,
        grid=(n_out // tn,),
        in_specs=[pl.BlockSpec((SUBLANES, D), lambda j: (0, 0)),
                  pl.BlockSpec((D, tn), lambda j: (0, j)),
                  pl.BlockSpec((1, tn), lambda j: (0, j))],
        out_specs=pl.BlockSpec((SUBLANES, tn), lambda j: (0, j)),
        out_shape=jax.ShapeDtypeStruct((SUBLANES, n_out), F32),
        compiler_params=_cparams(("arbitrary",)),
        name="adaln_mod",
    )(c8, w_ada, b_ada.reshape(1, n_out))
    return out[:B]


def _normmod_body(x_ref, nw_ref, sc_ref, sh_ref, o_ref):
    x = x_ref[...]
    ms = jnp.mean(x * x, axis=-1, keepdims=True)
    y = x * lax.rsqrt(ms + EPS) * nw_ref[...]
    o_ref[...] = (y * (1.0 + sc_ref[0]) + sh_ref[0]).astype(BF16)


def _normmod(x2d, norm_w, scale, shift, T, tm=512):
    N, D = x2d.shape
    B = scale.shape[0]
    tps = T // tm
    return pl.pallas_call(
        _normmod_body,
        grid=(N // tm,),
        in_specs=[pl.BlockSpec((tm, D), lambda i: (i, 0)),
                  pl.BlockSpec((1, D), lambda i: (0, 0)),
                  pl.BlockSpec((1, 1, D), lambda i: (i // tps, 0, 0)),
                  pl.BlockSpec((1, 1, D), lambda i: (i // tps, 0, 0))],
        out_specs=pl.BlockSpec((tm, D), lambda i: (i, 0)),
        out_shape=jax.ShapeDtypeStruct((N, D), BF16),
        compiler_params=_cparams(("arbitrary",)),
        name="norm_mod",
    )(x2d, norm_w.reshape(1, D), scale.reshape(B, 1, D), shift.reshape(B, 1, D))


HALF = LANES // 2


def _cast_weights(w_refs, w2_ref, wbf):
    wblk = w_refs[0].shape[1]
    if w2_ref is None:
        for b, w_ref in enumerate(w_refs):
            wbf[:, b * wblk:(b + 1) * wblk] = w_ref[...].astype(BF16)
        return
    lo = lax.broadcasted_iota(jnp.int32, (1, LANES), 1) < HALF
    chunks = [(r, c) for r in w_refs for c in range(wblk // LANES)]

    def rolled(n):
        if n < len(chunks):
            r, c = chunks[n]
            return pltpu.roll(r[:, c * LANES:(c + 1) * LANES], HALF, 1)
        return pltpu.roll(w2_ref[...], HALF, 1)

    prev = rolled(0)
    for n in range(len(chunks)):
        nxt = rolled(n + 1)
        wbf[:, n * LANES:(n + 1) * LANES] = jnp.where(lo, prev, nxt).astype(BF16)
        prev = nxt


def _cast_once(w_refs, w2_ref, wbf):
    @pl.when(pl.program_id(1) == 0)
    def _():
        _cast_weights(w_refs, w2_ref, wbf)


def _proj_act_body(*refs, act, shifted, sub, nblk):
    h_ref, w_refs = refs[0], refs[1:1 + nblk]
    w2_ref = refs[1 + nblk] if shifted else None
    o_ref, wbf = refs[-2:]
    _cast_once(w_refs, w2_ref, wbf)
    for r0 in range(0, h_ref.shape[0], sub):
        acc = jnp.dot(h_ref[r0:r0 + sub, :], wbf[...], preferred_element_type=F32)
        if act == "silu":
            acc = _silu(acc)
        elif act == "sigmoid":
            acc = _sigmoid(acc)
        o_ref[r0:r0 + sub, :] = acc.astype(o_ref.dtype)


def _proj_conv_body(*refs, tm, tps, norm, sub, nblk):
    h_ref, w_refs = refs[0], refs[1:1 + nblk]
    cw_ref, o_ref, wbf, buf = refs[1 + nblk:]
    i = pl.program_id(1)
    first = (i % tps) == 0
    _cast_once(w_refs, None, wbf)

    @pl.when(first)
    def _():
        buf[0:SUBLANES, :] = jnp.zeros((SUBLANES, buf.shape[1]), F32)

    @pl.when(jnp.logical_not(first))
    def _():
        buf[0:SUBLANES, :] = buf[tm:tm + SUBLANES, :]

    cw = cw_ref[...]
    for r0 in range(0, tm, sub):
        acc = jnp.dot(h_ref[r0:r0 + sub, :], wbf[...], preferred_element_type=F32)
        b0 = SUBLANES + r0
        buf[b0:b0 + sub, :] = acc
        y = acc * cw[DN_CONV - 1:DN_CONV, :]
        win = buf[b0 - SUBLANES:b0 + sub, :]
        for d in range(1, DN_CONV):
            y = y + pltpu.roll(win, d, 0)[SUBLANES:, :] * cw[DN_CONV - 1 - d:DN_CONV - d, :]
        y = _silu(y)
        for c in range(y.shape[1] // DN_HEAD):
            yc = y[:, c * DN_HEAD:(c + 1) * DN_HEAD]
            if norm:
                yc = yc * lax.rsqrt(jnp.sum(yc * yc, axis=-1, keepdims=True) + EPS)
            o_ref[r0:r0 + sub, c * DN_HEAD:(c + 1) * DN_HEAD] = yc.astype(o_ref.dtype)


def _proj_gates_body(h_ref, w_ref, al_ref, dt_ref, beta_ref, gc_ref, wbf):
    _cast_once([w_ref], None, wbf)
    acc = jnp.dot(h_ref[...], wbf[...], preferred_element_type=F32)
    beta_ref[...] = _sigmoid(acc)
    x = pltpu.roll(acc, LANES - DN_V_HEADS, 1) + dt_ref[...]
    sp = jnp.maximum(x, 0.0) + jnp.log(1.0 + jnp.exp(-jnp.abs(x)))
    g = -jnp.exp(al_ref[...]) * sp
    r64 = lax.broadcasted_iota(jnp.int32, g.shape, 0) & (DN_CHUNK - 1)
    s = 1
    while s < DN_CHUNK:
        g = g + jnp.where(r64 >= s, pltpu.roll(g, s, 0), 0.0)
        s *= 2
    gc_ref[...] = g


def _proj(h, w_in3, layer, col0, width, mode, T, extras=(), tm=1024, wblk=512, nblk=2, sub=128,
          name="proj"):
    N, K = h.shape
    out_dtype = BF16
    tm = min(tm, T)
    sub = min(sub, tm)
    off = col0 % LANES
    base = col0 - off
    shifted = off != 0
    if mode == "gates":
        wblk, nblk = LANES, 1
    tn = wblk * nblk
    assert off in (0, HALF) and base % wblk == 0
    assert mode == "gates" or width % tn == 0
    jb = base // wblk
    grid = (max(width // tn, 1), N // tm)
    in_specs = [pl.BlockSpec((tm, K), lambda j, i: (i, 0))]
    in_specs += [pl.BlockSpec((None, K, wblk), lambda j, i, b=b: (layer, 0, jb + j * nblk + b))
                 for b in range(nblk)]
    args = [h] + [w_in3] * nblk
    scratch = [pltpu.VMEM((K, tn), BF16)]
    out_spec = pl.BlockSpec((tm, tn), lambda j, i: (i, j))
    out_shape = jax.ShapeDtypeStruct((N, max(width, tn)), out_dtype)
    if shifted:
        cpt = tn // LANES
        in_specs.append(pl.BlockSpec((None, K, LANES),
                                     lambda j, i: (layer, 0, base // LANES + (j + 1) * cpt)))
        args.append(w_in3)
    if mode == "conv":
        cw, cw_col0, norm = extras
        in_specs.append(pl.BlockSpec((DN_CONV, tn), lambda j, i: (0, cw_col0 // tn + j)))
        args.append(cw)
        scratch.append(pltpu.VMEM((tm + SUBLANES, tn), F32))
        body = functools.partial(_proj_conv_body, tm=tm, tps=T // tm, norm=norm, sub=sub, nblk=nblk)
    elif mode == "gates":
        in_specs += [pl.BlockSpec((1, LANES), lambda j, i: (0, 0))] * 2
        args += list(extras)
        body = _proj_gates_body
        out_spec = [out_spec, out_spec]
        out_shape = [jax.ShapeDtypeStruct((N, LANES), F32)] * 2
    else:
        body = functools.partial(_proj_act_body, act=mode, shifted=shifted, sub=sub if mode else tm,
                                 nblk=nblk)
    return pl.pallas_call(
        body, grid=grid, in_specs=in_specs, out_specs=out_spec, out_shape=out_shape,
        scratch_shapes=scratch,
        compiler_params=_cparams(("arbitrary", "arbitrary")),
        name=name,
    )(*args)


def _rope_inv_freq_lanes():
    half = ROPE_DIM // 2
    inv = (np.float32(ROPE_THETA) ** (-np.arange(half, dtype=np.float32) * np.float32(2.0 / ROPE_DIM))).astype(np.float32)
    lanes = np.zeros((1, LANES), np.float32)
    for l in range(LANES):
        m = l % SWA_HEAD_DIM
        if m < ROPE_DIM:
            lanes[0, l] = inv[m % half]
    return lanes


def _swa_body(sink_ref, q_ref, kv_ref, g_ref, pos_ref, invf_ref, qw_ref, kw_ref, o_ref,
              kprev, vprev):
    n = pl.program_id(1)
    BLK = SWA_BLOCK
    half = ROPE_DIM // 2

    @pl.when(n == 0)
    def _():
        kprev[...] = jnp.zeros_like(kprev)
        vprev[...] = jnp.zeros_like(vprev)

    lane = lax.broadcasted_iota(jnp.int32, (1, LANES), 1)
    l64 = lane & (SWA_HEAD_DIM - 1)
    is_a = l64 < half
    is_b = (l64 >= half) & (l64 < ROPE_DIM)
    lo = lane < SWA_HEAD_DIM

    ang = pos_ref[...] * invf_ref[...]
    cs = jnp.cos(ang)
    sn = jnp.sin(ang)
    s1 = jnp.where(is_b, sn, 0.0)
    s2 = jnp.where(is_a, -sn, 0.0)

    def normrope(x, w):
        x2 = x * x
        s_lo = jnp.sum(jnp.where(lo, x2, 0.0), axis=-1, keepdims=True)
        s_hi = jnp.sum(jnp.where(lo, 0.0, x2), axis=-1, keepdims=True)
        r = jnp.where(lo, lax.rsqrt(s_lo * (1.0 / SWA_HEAD_DIM) + EPS),
                      lax.rsqrt(s_hi * (1.0 / SWA_HEAD_DIM) + EPS))
        xn = x * r * w
        return xn * cs + pltpu.roll(xn, half, 1) * s1 + pltpu.roll(xn, LANES - half, 1) * s2

    kw = kw_ref[...]
    qw = qw_ref[...]
    n_kc = SWA_KV // LANES
    kv = kv_ref[...]
    kband, vband = [], []
    for variant in range(2):
        kband.append([])
        vband.append([])
    for c in range(n_kc):
        kc = normrope(kv[:, c * LANES:(c + 1) * LANES].astype(F32), kw)
        vc = kv[:, SWA_KV + c * LANES:SWA_KV + (c + 1) * LANES].astype(F32)
        for variant in range(2):
            if variant == 1:
                kc_v = pltpu.roll(kc, SWA_HEAD_DIM, 1)
                vc_v = pltpu.roll(vc, SWA_HEAD_DIM, 1)
            else:
                kc_v, vc_v = kc, vc
            kc_b = kc_v.astype(BF16)
            vc_b = vc_v.astype(BF16)
            idx = variant * n_kc + c
            kband[variant].append(jnp.concatenate([kprev[idx], kc_b], axis=0))
            vband[variant].append(jnp.concatenate([vprev[idx], vc_b], axis=0))
            kprev[idx] = kc_b
            vprev[idx] = vc_b

    qi = lax.broadcasted_iota(jnp.int32, (BLK, 2 * BLK), 0)
    kj = lax.broadcasted_iota(jnp.int32, (BLK, 2 * BLK), 1)
    d = kj - qi
    kmin = jnp.where(n == 0, BLK, 0)
    valid = (d >= 1) & (d <= BLK) & (kj >= kmin)

    G = SWA_Q_HEADS // SWA_KV_HEADS
    cpk = G // 2
    hi = jnp.logical_not(lo)
    for j in range(SWA_KV_HEADS):
        cj, b = j // 2, j % 2
        chunks = range(j * cpk, (j + 1) * cpk)
        qn = [normrope(q_ref[:, c * LANES:(c + 1) * LANES].astype(F32), qw) * (SWA_HEAD_DIM ** -0.5)
              for c in chunks]
        outs = []
        for a in range(2):
            variant = 0 if a == b else 1
            msk = lo if a == 0 else hi
            lhs = jnp.concatenate([jnp.where(msk, x, 0.0).astype(BF16) for x in qn], axis=0)
            s = lax.dot_general(lhs, kband[variant][cj], (((1,), (1,)), ((), ())),
                                preferred_element_type=F32)
            ps, inv = [], []
            for r, c in enumerate(chunks):
                sr = jnp.where(valid, s[r * BLK:(r + 1) * BLK], -jnp.inf)
                sink = sink_ref[2 * c + a]
                m = jnp.maximum(jnp.max(sr, axis=-1, keepdims=True), sink)
                p = jnp.exp(sr - m)
                inv.append(1.0 / (jnp.sum(p, axis=-1, keepdims=True) + jnp.exp(sink - m)))
                ps.append(p.astype(BF16))
            o = jnp.dot(jnp.concatenate(ps, axis=0), vband[variant][cj], preferred_element_type=F32)
            outs.append([o[r * BLK:(r + 1) * BLK] * inv[r] for r in range(cpk)])
        for r, c in enumerate(chunks):
            o_pair = jnp.where(lo, outs[0][r], outs[1][r])
            gate = g_ref[:, c * LANES:(c + 1) * LANES].astype(F32)
            o_ref[:, c * LANES:(c + 1) * LANES] = (o_pair * gate).astype(BF16)


def _swa(p_qkv, sag, posf, q_norm_w, k_norm_w, sinks, B, T):
    N = B * T
    nb = T // SWA_BLOCK
    invf = jnp.asarray(_rope_inv_freq_lanes())
    qw = jnp.tile(q_norm_w.reshape(1, SWA_HEAD_DIM), (1, LANES // SWA_HEAD_DIM))
    kw = jnp.tile(k_norm_w.reshape(1, SWA_HEAD_DIM), (1, LANES // SWA_HEAD_DIM))
    kvw = 2 * SWA_KV
    n_kc = SWA_KV // LANES
    row = lambda b, n: b * nb + n
    return pl.pallas_call(
        _swa_body,
        grid=(B, nb),
        in_specs=[pl.BlockSpec(memory_space=pltpu.SMEM),
                  pl.BlockSpec((SWA_BLOCK, SWA_Q), lambda b, n: (row(b, n), 0)),
                  pl.BlockSpec((SWA_BLOCK, kvw), lambda b, n: (row(b, n), SWA_Q // kvw)),
                  pl.BlockSpec((SWA_BLOCK, SWA_Q), lambda b, n: (row(b, n), 0)),
                  pl.BlockSpec((SWA_BLOCK, 1), lambda b, n: (row(b, n), 0)),
                  pl.BlockSpec((1, LANES), lambda b, n: (0, 0)),
                  pl.BlockSpec((1, LANES), lambda b, n: (0, 0)),
                  pl.BlockSpec((1, LANES), lambda b, n: (0, 0))],
        out_specs=pl.BlockSpec((SWA_BLOCK, SWA_Q), lambda b, n: (row(b, n), 0)),
        out_shape=jax.ShapeDtypeStruct((N, SWA_Q), BF16),
        scratch_shapes=[pltpu.VMEM((2 * n_kc, SWA_BLOCK, LANES), BF16),
                        pltpu.VMEM((2 * n_kc, SWA_BLOCK, LANES), BF16)],
        compiler_params=_cparams(("arbitrary", "arbitrary")),
        name="swa_attn",
    )(sinks, p_qkv, p_qkv, sag, posf, invf, qw, kw)


def _delta_body(q_ref, k_ref, v_ref, z_ref, bc_ref, gc_ref, gr_ref, nw_ref, o_ref, S_ref,
                *, KH, TT):
    grp = pl.program_id(1)
    t = pl.program_id(2)
    C = DN_CHUNK
    NC = TT // C
    scale = DN_HEAD ** -0.5
    NT = (((1,), (1,)), ((), ()))

    @pl.when(t == 0)
    def _():
        S_ref[...] = jnp.zeros_like(S_ref)

    NV = 2 * KH
    row = lax.broadcasted_iota(jnp.int32, (TT, TT), 0)
    col = lax.broadcasted_iota(jnp.int32, (TT, TT), 1)
    same = (row // C) == (col // C)
    ri = lax.broadcasted_iota(jnp.int32, (C, TT), 0)
    cj = lax.broadcasted_iota(jnp.int32, (C, TT), 1)
    cjc = cj // C
    cjj = cj & (C - 1)
    tril_c = ri >= cjj
    strict_c = ri > cjj
    eye_c = (ri == cjj).astype(F32)
    lane = lax.broadcasted_iota(jnp.int32, (TT, LANES), 1)
    lchunk = lax.broadcasted_iota(jnp.int32, (1, TT), 1) // C
    bcall = bc_ref[...]
    gcall = gc_ref[...]
    nw = nw_ref[...]

    def compress(x):
        out = x[(NC - 1) * C:NC * C]
        for c in range(NC - 2, -1, -1):
            out = jnp.where(cjc == c, x[c * C:(c + 1) * C], out)
        return out

    def by_row(colvec):
        out = jnp.broadcast_to(colvec[(NC - 1) * C:NC * C], (C, TT))
        for c in range(NC - 2, -1, -1):
            out = jnp.where(cjc == c, colvec[c * C:(c + 1) * C], out)
        return out

    def bd(xc):
        return jnp.where(same, jnp.concatenate([xc] * NC, axis=0), 0.0).astype(BF16)

    kf, qf, kT, kkc, qkc = [], [], [], [], []
    for kh in range(KH):
        q = q_ref[:, kh * DN_HEAD:(kh + 1) * DN_HEAD]
        k = k_ref[:, kh * DN_HEAD:(kh + 1) * DN_HEAD]
        kkc.append(compress(lax.dot_general(k, k, NT, preferred_element_type=F32)))
        qkc.append(compress(lax.dot_general(q, k, NT, preferred_element_type=F32)))
        kf.append(k.astype(F32))
        qf.append(q.astype(F32))
        kT.append(kf[-1].T)

    bcol, gcol, grow, Lc, qkm, P = [], [], [], [], [], []
    for hh in range(NV):
        hglob = grp * NV + hh
        sel = lane == hglob
        bcol.append(jnp.sum(jnp.where(sel, bcall, 0.0), axis=-1, keepdims=True))
        gcol.append(jnp.sum(jnp.where(sel, gcall, 0.0), axis=-1, keepdims=True))
        grow.append(gr_ref[0, pl.ds(hglob, 1), :])
        dec = jnp.exp(jnp.where(tril_c, by_row(gcol[hh]) - grow[hh], -jnp.inf))
        Lc.append(jnp.where(strict_c, by_row(bcol[hh]) * kkc[hh // 2] * dec, 0.0))
        qkm.append(qkc[hh // 2] * dec * scale)
        P.append(eye_c - Lc[hh])

    Lp = [jnp.dot(Lc[hh].astype(BF16), bd(Lc[hh]), preferred_element_type=F32) for hh in range(NV)]
    step = 4
    while step < C:
        res = [jnp.dot(jnp.concatenate([Lp[hh], P[hh]], axis=0).astype(BF16), bd(Lp[hh]),
                       preferred_element_type=F32) for hh in range(NV)]
        Lp = [r[:C] for r in res]
        P = [P[hh] + res[hh][C:] for hh in range(NV)]
        step *= 2
    P = [P[hh] + jnp.dot(P[hh].astype(BF16), bd(Lp[hh]), preferred_element_type=F32) for hh in range(NV)]

    uw, qd, kdT, eg_last = [], [], [], []
    for hh in range(NV):
        eg = jnp.exp(gcol[hh])
        vf = v_ref[:, hh * DN_HEAD:(hh + 1) * DN_HEAD].astype(F32)
        rhs = jnp.concatenate([vf * bcol[hh], kf[hh // 2] * (bcol[hh] * eg)], axis=1).astype(BF16)
        uw.append(jnp.dot(bd(P[hh]), rhs, preferred_element_type=F32))
        qd.append((qf[hh // 2] * (eg * scale)).astype(BF16))
        glrow = jnp.zeros((1, TT), F32)
        for c in range(NC):
            glrow = jnp.where(lchunk == c, gcol[hh][c * C + C - 1:c * C + C, :], glrow)
        kdT.append(kT[hh // 2] * jnp.exp(glrow - grow[hh]))
        eg_last.append([eg[c * C + C - 1:c * C + C, :] for c in range(NC)])

    S = [S_ref[hh] for hh in range(NV)]
    vn = [[] for _ in range(NV)]
    o_inter = [[] for _ in range(NV)]
    for c in range(NC):
        r0, r1 = c * C, (c + 1) * C
        for hh in range(NV):
            w_c = uw[hh][r0:r1, DN_HEAD:].astype(BF16)
            lhs = jnp.concatenate([w_c, qd[hh][r0:r1]], axis=0)
            res = jnp.dot(lhs, S[hh].astype(BF16), preferred_element_type=F32)
            vn_c = (uw[hh][r0:r1, :DN_HEAD] - res[:C]).astype(BF16)
            vn[hh].append(vn_c)
            o_inter[hh].append(res[C:])
            kd_c = kdT[hh][:, r0:r1].astype(BF16)
            S[hh] = S[hh] * eg_last[hh][c] + jnp.dot(kd_c, vn_c, preferred_element_type=F32)

    for hh in range(NV):
        S_ref[hh] = S[hh]
        vn_all = jnp.concatenate(vn[hh], axis=0)
        o = jnp.concatenate(o_inter[hh], axis=0) + jnp.dot(bd(qkm[hh]), vn_all, preferred_element_type=F32)
        ms = jnp.mean(o * o, axis=-1, keepdims=True)
        z = z_ref[:, hh * DN_HEAD:(hh + 1) * DN_HEAD].astype(F32)
        o_ref[:, hh * DN_HEAD:(hh + 1) * DN_HEAD] = (o * lax.rsqrt(ms + EPS) * nw * z).astype(BF16)


def _delta(qk_n, vv, sz, beta, gc, gcT, dn_norm_w, B, T, KH=4, TT=256):
    N = B * T
    nt = T // TT
    ng = DN_K_HEADS // KH
    row = lambda b, g, t: b * nt + t
    kw = KH * DN_HEAD
    return pl.pallas_call(
        functools.partial(_delta_body, KH=KH, TT=TT),
        grid=(B, ng, nt),
        in_specs=[pl.BlockSpec((TT, kw), lambda b, g, t: (row(b, g, t), g)),
                  pl.BlockSpec((TT, kw), lambda b, g, t: (row(b, g, t), DN_KEY // kw + g)),
                  pl.BlockSpec((TT, 2 * kw), lambda b, g, t: (row(b, g, t), g)),
                  pl.BlockSpec((TT, 2 * kw), lambda b, g, t: (row(b, g, t), g)),
                  pl.BlockSpec((TT, LANES), lambda b, g, t: (row(b, g, t), 0)),
                  pl.BlockSpec((TT, LANES), lambda b, g, t: (row(b, g, t), 0)),
                  pl.BlockSpec((1, DN_V_HEADS, TT), lambda b, g, t: (b, 0, t)),
                  pl.BlockSpec((1, DN_HEAD), lambda b, g, t: (0, 0))],
        out_specs=pl.BlockSpec((TT, 2 * kw), lambda b, g, t: (row(b, g, t), g)),
        out_shape=jax.ShapeDtypeStruct((N, DN_VAL), BF16),
        scratch_shapes=[pltpu.VMEM((2 * KH, DN_HEAD, DN_HEAD), F32)],
        compiler_params=_cparams(("arbitrary", "arbitrary", "arbitrary")),
        name="delta_rule",
    )(qk_n, qk_n, vv, sz, beta, gc, gcT, dn_norm_w.reshape(1, DN_HEAD))


def _merge_body(a_ref, o_ref, ga_ref, gb_ref, wa_ref, wb_ref, y_ref):
    ya = jnp.dot(a_ref[...], wa_ref[...], preferred_element_type=F32)
    yb = jnp.dot(o_ref[...], wb_ref[...], preferred_element_type=F32)
    y = ga_ref[...].astype(F32) * ya + gb_ref[...].astype(F32) * yb
    y_ref[...] = y.astype(BF16)


def _merge(a_g, o_g, p_m, wa, wb, tm=512, tn=512):
    N = a_g.shape[0]
    D = wa.shape[1]
    nj = D // tn
    return pl.pallas_call(
        _merge_body,
        grid=(N // tm, nj),
        in_specs=[pl.BlockSpec((tm, a_g.shape[1]), lambda i, j: (i, 0)),
                  pl.BlockSpec((tm, o_g.shape[1]), lambda i, j: (i, 0)),
                  pl.BlockSpec((tm, tn), lambda i, j: (i, j)),
                  pl.BlockSpec((tm, tn), lambda i, j: (i, nj + j)),
                  pl.BlockSpec((wa.shape[0], tn), lambda i, j: (0, j)),
                  pl.BlockSpec((wb.shape[0], tn), lambda i, j: (0, j))],
        out_specs=pl.BlockSpec((tm, tn), lambda i, j: (i, j)),
        out_shape=jax.ShapeDtypeStruct((N, D), BF16),
        compiler_params=_cparams(("arbitrary", "arbitrary")),
        name="merge",
    )(a_g, o_g, p_m, p_m, wa, wb)


def _out_body(y_ref, w_ref, x_ref, g_ref, o_ref):
    acc = jnp.dot(y_ref[...], w_ref[...], preferred_element_type=F32)
    o_ref[...] = x_ref[...] + g_ref[0] * acc


def _outproj(y, w, x2d, gate, T, tm=512, tn=512):
    N, D = x2d.shape
    B = gate.shape[0]
    tps = T // tm
    return pl.pallas_call(
        _out_body,
        grid=(N // tm, D // tn),
        in_specs=[pl.BlockSpec((tm, y.shape[1]), lambda i, j: (i, 0)),
                  pl.BlockSpec((w.shape[0], tn), lambda i, j: (0, j)),
                  pl.BlockSpec((tm, tn), lambda i, j: (i, j)),
                  pl.BlockSpec((1, 1, tn), lambda i, j: (i // tps, 0, j))],
        out_specs=pl.BlockSpec((tm, tn), lambda i, j: (i, j)),
        out_shape=jax.ShapeDtypeStruct((N, D), F32),
        compiler_params=_cparams(("arbitrary", "arbitrary")),
        name="out_proj",
    )(y, w, x2d, gate.reshape(B, 1, D))


def _layer(l, x, c, positions, w_ada, b_ada, norm_w, w_in, q_norm_w, k_norm_w, sinks,
           conv_w, a_log, dt_bias, dn_norm_w, w_o_swa, w_o_dn, w_out):
    B, T, D = x.shape
    N = B * T
    x2d = x.reshape(N, D)

    mod = _adaln(c, w_ada, b_ada)
    shift, scale, gate = mod[:, :D], mod[:, D:2 * D], mod[:, 2 * D:]
    h = _normmod(x2d, norm_w, scale, shift, T)

    o_q, o_g = 0, SWA_Q + 2 * SWA_KV
    o_d = o_g + SWA_Q
    o_z = o_d + DN_CONV_CH
    o_b = o_z + DN_VAL
    o_a = o_b + DN_V_HEADS
    o_m = o_a + DN_V_HEADS
    assert o_a == o_b + DN_V_HEADS and o_b % LANES == 0
    p_qkv = _proj(h, w_in, l, o_q, o_g - o_q, None, T, nblk=1, name="proj_qkv")
    sag = _proj(h, w_in, l, o_g, o_d - o_g, "silu", T, name="proj_swa_gate")
    qk_n = _proj(h, w_in, l, o_d, 2 * DN_KEY, "conv", T, extras=(conv_w, 0, True), name="proj_dn_qk")
    vv = _proj(h, w_in, l, o_d + 2 * DN_KEY, DN_VAL, "conv", T, extras=(conv_w, 2 * DN_KEY, False),
               name="proj_dn_v")
    sz = _proj(h, w_in, l, o_z, o_b - o_z, "silu", T, name="proj_dn_z")
    al = jnp.zeros((1, LANES), F32).at[0, :DN_V_HEADS].set(a_log)
    dt = jnp.zeros((1, LANES), F32).at[0, :DN_V_HEADS].set(dt_bias)
    beta, gc = _proj(h, w_in, l, o_b, 2 * DN_V_HEADS, "gates", T, extras=(al, dt), name="proj_dn_gates")
    p_m = _proj(h, w_in, l, o_m, 2 * D, "sigmoid", T, name="proj_merge_gates")

    posf = positions.astype(F32).reshape(N, 1)
    a_g = _swa(p_qkv, sag, posf, q_norm_w, k_norm_w, sinks, B, T)

    gcT = gc[:, :DN_V_HEADS].reshape(B, T, DN_V_HEADS).transpose(0, 2, 1)
    o_g2 = _delta(qk_n, vv, sz, beta, gc, gcT, dn_norm_w, B, T)

    y = _merge(a_g, o_g2, p_m, w_o_swa.astype(BF16), w_o_dn.astype(BF16))
    out = _outproj(y, w_out.astype(BF16), x2d, gate, T)
    return out.reshape(B, T, D)


def kernel(x, c, positions, w_ada, b_ada, norm_w, w_in, q_norm_w, k_norm_w, sinks, conv_w,
           a_log, dt_bias, dn_norm_w, w_o_swa, w_o_dn, w_out):
    depth = w_ada.shape[0]
    for l in range(depth):
        x = _layer(l, x, c, positions, w_ada[l], b_ada[l], norm_w[l], w_in, q_norm_w[l],
                   k_norm_w[l], sinks[l], conv_w[l], a_log[l], dt_bias[l], dn_norm_w[l],
                   w_o_swa[l], w_o_dn[l], w_out[l])
    return x
```

```python
import functools

import numpy as np
import jax
import jax.numpy as jnp
from jax import lax
from jax.experimental import pallas as pl
from jax.experimental.pallas import tpu as pltpu

F32 = jnp.float32
BF16 = jnp.bfloat16

D_MODEL = 2048
SWA_Q_HEADS = 32
SWA_KV_HEADS = 4
SWA_HEAD_DIM = 64
SWA_BLOCK = 128
ROPE_THETA = 500000.0
ROPE_DIM = SWA_HEAD_DIM // 4
DN_K_HEADS = 16
DN_V_HEADS = 32
DN_HEAD = 128
DN_CONV = 4
DN_CHUNK = 64
EPS = 1e-6

SWA_Q = SWA_Q_HEADS * SWA_HEAD_DIM
SWA_KV = SWA_KV_HEADS * SWA_HEAD_DIM
DN_KEY = DN_K_HEADS * DN_HEAD
DN_VAL = DN_V_HEADS * DN_HEAD
DN_CONV_CH = 2 * DN_KEY + DN_VAL

LANES = 128
SUBLANES = 8
VMEM_LIMIT = 56 * 1024 * 1024


def _cparams(sem):
    return pltpu.CompilerParams(dimension_semantics=sem, vmem_limit_bytes=VMEM_LIMIT)


def _sigmoid(x):
    return 0.5 * jnp.tanh(0.5 * x) + 0.5


def _silu(x):
    return x * _sigmoid(x)


def _adaln_body(c_ref, w_ref, b_ref, o_ref):
    s = _silu(c_ref[...]).astype(BF16)
    o_ref[...] = jnp.dot(s, w_ref[...].astype(BF16), preferred_element_type=F32) + b_ref[...]


def _adaln(c, w_ada, b_ada, tn=512):
    B, D = c.shape
    n_out = w_ada.shape[1]
    c8 = jnp.zeros((SUBLANES, D), F32).at[:B].set(c)
    out = pl.pallas_call(
        _adaln_body,
        grid=(n_out // tn,),
        in_specs=[pl.BlockSpec((SUBLANES, D), lambda j: (0, 0)),
                  pl.BlockSpec((D, tn), lambda j: (0, j)),
                  pl.BlockSpec((1, tn), lambda j: (0, j))],
        out_specs=pl.BlockSpec((SUBLANES, tn), lambda j: (0, j)),
        out_shape=jax.ShapeDtypeStruct((SUBLANES, n_out), F32),
        compiler_params=_cparams(("arbitrary",)),
        name="adaln_mod",
    )(c8, w_ada, b_ada.reshape(1, n_out))
    return out[:B]


def _normmod_body(x_ref, nw_ref, sc_ref, sh_ref, o_ref):
    x = x_ref[...]
    ms = jnp.mean(x * x, axis=-1, keepdims=True)
    y = x * lax.rsqrt(ms + EPS) * nw_ref[...]
    o_ref[...] = (y * (1.0 + sc_ref[0]) + sh_ref[0]).astype(BF16)


def _normmod(x2d, norm_w, scale, shift, T, tm=512):
    N, D = x2d.shape
    B = scale.shape[0]
    tps = T // tm
    return pl.pallas_call(
        _normmod_body,
        grid=(N // tm,),
        in_specs=[pl.BlockSpec((tm, D), lambda i: (i, 0)),
                  pl.BlockSpec((1, D), lambda i: (0, 0)),
                  pl.BlockSpec((1, 1, D), lambda i: (i // tps, 0, 0)),
                  pl.BlockSpec((1, 1, D), lambda i: (i // tps, 0, 0))],
        out_specs=pl.BlockSpec((tm, D), lambda i: (i, 0)),
        out_shape=jax.ShapeDtypeStruct((N, D), BF16),
        compiler_params=_cparams(("arbitrary",)),
        name="norm_mod",
    )(x2d, norm_w.reshape(1, D), scale.reshape(B, 1, D), shift.reshape(B, 1, D))


HALF = LANES // 2


NT_DIMS = (((1,), (1,)), ((), ()))


def _cast_weights(w_refs, w2_ref, wbf):
    wblk = w_refs[0].shape[0]
    off = 0 if w2_ref is None else HALF
    for b, w_ref in enumerate(w_refs):
        lo = max(b * wblk - off, 0)
        wbf[lo:(b + 1) * wblk - off, :] = w_ref[lo + off - b * wblk:, :].astype(BF16)
    if w2_ref is not None:
        n = len(w_refs) * wblk
        wbf[n - off:n, :] = w2_ref[...].astype(BF16)


def _cast_once(w_refs, w2_ref, wbf):
    @pl.when(pl.program_id(1) == 0)
    def _():
        _cast_weights(w_refs, w2_ref, wbf)


def _proj_act_body(*refs, act, shifted, sub, nblk):
    h_ref, w_refs = refs[0], refs[1:1 + nblk]
    w2_ref = refs[1 + nblk] if shifted else None
    o_ref, wbf = refs[-2:]
    _cast_once(w_refs, w2_ref, wbf)
    for r0 in range(0, h_ref.shape[0], sub):
        acc = lax.dot_general(h_ref[r0:r0 + sub, :], wbf[...], NT_DIMS, preferred_element_type=F32)
        if act == "silu":
            acc = _silu(acc)
        elif act == "sigmoid":
            acc = _sigmoid(acc)
        o_ref[r0:r0 + sub, :] = acc.astype(o_ref.dtype)


def _proj_conv_body(*refs, tm, tps, norm, sub, nblk):
    h_ref, w_refs = refs[0], refs[1:1 + nblk]
    cw_ref, o_ref, wbf, buf = refs[1 + nblk:]
    i = pl.program_id(1)
    first = (i % tps) == 0
    _cast_once(w_refs, None, wbf)

    @pl.when(first)
    def _():
        buf[0:SUBLANES, :] = jnp.zeros((SUBLANES, buf.shape[1]), F32)

    @pl.when(jnp.logical_not(first))
    def _():
        buf[0:SUBLANES, :] = buf[tm:tm + SUBLANES, :]

    cw = cw_ref[...]
    for r0 in range(0, tm, sub):
        acc = lax.dot_general(h_ref[r0:r0 + sub, :], wbf[...], NT_DIMS, preferred_element_type=F32)
        b0 = SUBLANES + r0
        buf[b0:b0 + sub, :] = acc
        y = acc * cw[DN_CONV - 1:DN_CONV, :]
        win = buf[b0 - SUBLANES:b0 + sub, :]
        for d in range(1, DN_CONV):
            y = y + pltpu.roll(win, d, 0)[SUBLANES:, :] * cw[DN_CONV - 1 - d:DN_CONV - d, :]
        y = _silu(y)
        for c in range(y.shape[1] // DN_HEAD):
            yc = y[:, c * DN_HEAD:(c + 1) * DN_HEAD]
            if norm:
                yc = yc * lax.rsqrt(jnp.sum(yc * yc, axis=-1, keepdims=True) + EPS)
            o_ref[r0:r0 + sub, c * DN_HEAD:(c + 1) * DN_HEAD] = yc.astype(o_ref.dtype)


def _proj_gates_body(h_ref, w_ref, al_ref, dt_ref, beta_ref, gc_ref, wbf):
    _cast_once([w_ref], None, wbf)
    acc = lax.dot_general(h_ref[...], wbf[...], NT_DIMS, preferred_element_type=F32)
    beta_ref[...] = _sigmoid(acc)
    x = pltpu.roll(acc, LANES - DN_V_HEADS, 1) + dt_ref[...]
    sp = jnp.maximum(x, 0.0) + jnp.log(1.0 + jnp.exp(-jnp.abs(x)))
    g = -jnp.exp(al_ref[...]) * sp
    r64 = lax.broadcasted_iota(jnp.int32, g.shape, 0) & (DN_CHUNK - 1)
    s = 1
    while s < DN_CHUNK:
        g = g + jnp.where(r64 >= s, pltpu.roll(g, s, 0), 0.0)
        s *= 2
    gc_ref[...] = g


def _proj(h, w_t, layer, col0, width, mode, T, extras=(), tm=1024, wblk=512, nblk=2, sub=256,
          name="proj"):
    N, K = h.shape
    out_dtype = BF16
    tm = min(tm, T)
    sub = min(sub, tm)
    off = col0 % LANES
    base = col0 - off
    shifted = off != 0
    if mode == "gates":
        wblk, nblk = LANES, 1
    tn = wblk * nblk
    assert off in (0, HALF) and base % wblk == 0
    assert mode == "gates" or width % tn == 0
    jb = base // wblk
    grid = (max(width // tn, 1), N // tm)
    in_specs = [pl.BlockSpec((tm, K), lambda j, i: (i, 0))]
    in_specs += [pl.BlockSpec((None, wblk, K), lambda j, i, b=b: (layer, jb + j * nblk + b, 0))
                 for b in range(nblk)]
    args = [h] + [w_t] * nblk
    scratch = [pltpu.VMEM((tn, K), BF16)]
    out_spec = pl.BlockSpec((tm, tn), lambda j, i: (i, j))
    out_shape = jax.ShapeDtypeStruct((N, max(width, tn)), out_dtype)
    if shifted:
        in_specs.append(pl.BlockSpec((None, HALF, K),
                                     lambda j, i: (layer, (base + (j + 1) * tn) // HALF, 0)))
        args.append(w_t)
    if mode == "conv":
        cw, cw_col0, norm = extras
        in_specs.append(pl.BlockSpec((DN_CONV, tn), lambda j, i: (0, cw_col0 // tn + j)))
        args.append(cw)
        scratch.append(pltpu.VMEM((tm + SUBLANES, tn), F32))
        body = functools.partial(_proj_conv_body, tm=tm, tps=T // tm, norm=norm, sub=sub, nblk=nblk)
    elif mode == "gates":
        in_specs += [pl.BlockSpec((1, LANES), lambda j, i: (0, 0))] * 2
        args += list(extras)
        body = _proj_gates_body
        out_spec = [out_spec, out_spec]
        out_shape = [jax.ShapeDtypeStruct((N, LANES), F32)] * 2
    else:
        body = functools.partial(_proj_act_body, act=mode, shifted=shifted, sub=sub if mode else tm,
                                 nblk=nblk)
    return pl.pallas_call(
        body, grid=grid, in_specs=in_specs, out_specs=out_spec, out_shape=out_shape,
        scratch_shapes=scratch,
        compiler_params=_cparams(("arbitrary", "arbitrary")),
        name=name,
    )(*args)


def _rope_inv_freq_lanes():
    half = ROPE_DIM // 2
    inv = (np.float32(ROPE_THETA) ** (-np.arange(half, dtype=np.float32) * np.float32(2.0 / ROPE_DIM))).astype(np.float32)
    lanes = np.zeros((1, LANES), np.float32)
    for l in range(LANES):
        m = l % SWA_HEAD_DIM
        if m < ROPE_DIM:
            lanes[0, l] = inv[m % half]
    return lanes


def _swa_body(sink_ref, q_ref, kv_ref, g_ref, pos_ref, invf_ref, qw_ref, kw_ref, o_ref,
              kprev, vprev):
    n = pl.program_id(1)
    BLK = SWA_BLOCK
    half = ROPE_DIM // 2

    @pl.when(n == 0)
    def _():
        kprev[...] = jnp.zeros_like(kprev)
        vprev[...] = jnp.zeros_like(vprev)

    lane = lax.broadcasted_iota(jnp.int32, (1, LANES), 1)
    l64 = lane & (SWA_HEAD_DIM - 1)
    is_a = l64 < half
    is_b = (l64 >= half) & (l64 < ROPE_DIM)
    lo = lane < SWA_HEAD_DIM

    ang = pos_ref[...] * invf_ref[...]
    cs = jnp.cos(ang)
    sn = jnp.sin(ang)
    s1 = jnp.where(is_b, sn, 0.0)
    s2 = jnp.where(is_a, -sn, 0.0)

    def normrope(x, w):
        x2 = x * x
        s_lo = jnp.sum(jnp.where(lo, x2, 0.0), axis=-1, keepdims=True)
        s_hi = jnp.sum(jnp.where(lo, 0.0, x2), axis=-1, keepdims=True)
        r = jnp.where(lo, lax.rsqrt(s_lo * (1.0 / SWA_HEAD_DIM) + EPS),
                      lax.rsqrt(s_hi * (1.0 / SWA_HEAD_DIM) + EPS))
        xn = x * r * w
        return xn * cs + pltpu.roll(xn, half, 1) * s1 + pltpu.roll(xn, LANES - half, 1) * s2

    kw = kw_ref[...]
    qw = qw_ref[...]
    n_kc = SWA_KV // LANES
    kv = kv_ref[...]
    kband, vband = [], []
    for variant in range(2):
        kband.append([])
        vband.append([])
    for c in range(n_kc):
        kc = normrope(kv[:, c * LANES:(c + 1) * LANES].astype(F32), kw)
        vc = kv[:, SWA_KV + c * LANES:SWA_KV + (c + 1) * LANES].astype(F32)
        for variant in range(2):
            if variant == 1:
                kc_v = pltpu.roll(kc, SWA_HEAD_DIM, 1)
                vc_v = pltpu.roll(vc, SWA_HEAD_DIM, 1)
            else:
                kc_v, vc_v = kc, vc
            kc_b = kc_v.astype(BF16)
            vc_b = vc_v.astype(BF16)
            idx = variant * n_kc + c
            kband[variant].append(jnp.concatenate([kprev[idx], kc_b], axis=0))
            vband[variant].append(jnp.concatenate([vprev[idx], vc_b], axis=0))
            kprev[idx] = kc_b
            vprev[idx] = vc_b

    qi = lax.broadcasted_iota(jnp.int32, (BLK, 2 * BLK), 0)
    kj = lax.broadcasted_iota(jnp.int32, (BLK, 2 * BLK), 1)
    d = kj - qi
    kmin = jnp.where(n == 0, BLK, 0)
    valid = (d >= 1) & (d <= BLK) & (kj >= kmin)

    G = SWA_Q_HEADS // SWA_KV_HEADS
    cpk = G // 2
    hi = jnp.logical_not(lo)
    for j in range(SWA_KV_HEADS):
        cj, b = j // 2, j % 2
        chunks = range(j * cpk, (j + 1) * cpk)
        qn = [normrope(q_ref[:, c * LANES:(c + 1) * LANES].astype(F32), qw) * (SWA_HEAD_DIM ** -0.5)
              for c in chunks]
        outs = []
        for a in range(2):
            variant = 0 if a == b else 1
            msk = lo if a == 0 else hi
            lhs = jnp.concatenate([jnp.where(msk, x, 0.0).astype(BF16) for x in qn], axis=0)
            s = lax.dot_general(lhs, kband[variant][cj], (((1,), (1,)), ((), ())),
                                preferred_element_type=F32)
            ps, inv = [], []
            for r, c in enumerate(chunks):
                sr = jnp.where(valid, s[r * BLK:(r + 1) * BLK], -jnp.inf)
                sink = sink_ref[2 * c + a]
                m = jnp.maximum(jnp.max(sr, axis=-1, keepdims=True), sink)
                p = jnp.exp(sr - m)
                inv.append(1.0 / (jnp.sum(p, axis=-1, keepdims=True) + jnp.exp(sink - m)))
                ps.append(p.astype(BF16))
            o = jnp.dot(jnp.concatenate(ps, axis=0), vband[variant][cj], preferred_element_type=F32)
            outs.append([o[r * BLK:(r + 1) * BLK] * inv[r] for r in range(cpk)])
        for r, c in enumerate(chunks):
            o_pair = jnp.where(lo, outs[0][r], outs[1][r])
            gate = g_ref[:, c * LANES:(c + 1) * LANES].astype(F32)
            o_ref[:, c * LANES:(c + 1) * LANES] = (o_pair * gate).astype(BF16)


def _swa(p_qkv, sag, posf, q_norm_w, k_norm_w, sinks, B, T):
    N = B * T
    nb = T // SWA_BLOCK
    invf = jnp.asarray(_rope_inv_freq_lanes())
    qw = jnp.tile(q_norm_w.reshape(1, SWA_HEAD_DIM), (1, LANES // SWA_HEAD_DIM))
    kw = jnp.tile(k_norm_w.reshape(1, SWA_HEAD_DIM), (1, LANES // SWA_HEAD_DIM))
    kvw = 2 * SWA_KV
    n_kc = SWA_KV // LANES
    row = lambda b, n: b * nb + n
    return pl.pallas_call(
        _swa_body,
        grid=(B, nb),
        in_specs=[pl.BlockSpec(memory_space=pltpu.SMEM),
                  pl.BlockSpec((SWA_BLOCK, SWA_Q), lambda b, n: (row(b, n), 0)),
                  pl.BlockSpec((SWA_BLOCK, kvw), lambda b, n: (row(b, n), SWA_Q // kvw)),
                  pl.BlockSpec((SWA_BLOCK, SWA_Q), lambda b, n: (row(b, n), 0)),
                  pl.BlockSpec((SWA_BLOCK, 1), lambda b, n: (row(b, n), 0)),
                  pl.BlockSpec((1, LANES), lambda b, n: (0, 0)),
                  pl.BlockSpec((1, LANES), lambda b, n: (0, 0)),
                  pl.BlockSpec((1, LANES), lambda b, n: (0, 0))],
        out_specs=pl.BlockSpec((SWA_BLOCK, SWA_Q), lambda b, n: (row(b, n), 0)),
        out_shape=jax.ShapeDtypeStruct((N, SWA_Q), BF16),
        scratch_shapes=[pltpu.VMEM((2 * n_kc, SWA_BLOCK, LANES), BF16),
                        pltpu.VMEM((2 * n_kc, SWA_BLOCK, LANES), BF16)],
        compiler_params=_cparams(("arbitrary", "arbitrary")),
        name="swa_attn",
    )(sinks, p_qkv, p_qkv, sag, posf, invf, qw, kw)


def _delta_body(q_ref, k_ref, v_ref, z_ref, bc_ref, gc_ref, gr_ref, nw_ref, o_ref, S_ref,
                *, KH, TT):
    grp = pl.program_id(1)
    t = pl.program_id(2)
    C = DN_CHUNK
    NC = TT // C
    scale = DN_HEAD ** -0.5
    NT = (((1,), (1,)), ((), ()))

    @pl.when(t == 0)
    def _():
        S_ref[...] = jnp.zeros_like(S_ref)

    NV = 2 * KH
    row = lax.broadcasted_iota(jnp.int32, (TT, TT), 0)
    col = lax.broadcasted_iota(jnp.int32, (TT, TT), 1)
    same = (row // C) == (col // C)
    ri = lax.broadcasted_iota(jnp.int32, (C, TT), 0)
    cj = lax.broadcasted_iota(jnp.int32, (C, TT), 1)
    cjc = cj // C
    cjj = cj & (C - 1)
    tril_c = ri >= cjj
    strict_c = ri > cjj
    eye_c = (ri == cjj).astype(F32)
    lane = lax.broadcasted_iota(jnp.int32, (TT, LANES), 1)
    lchunk = lax.broadcasted_iota(jnp.int32, (1, TT), 1) // C
    bcall = bc_ref[...]
    gcall = gc_ref[...]
    nw = nw_ref[...]

    def compress(x):
        out = x[(NC - 1) * C:NC * C]
        for c in range(NC - 2, -1, -1):
            out = jnp.where(cjc == c, x[c * C:(c + 1) * C], out)
        return out

    def by_row(colvec):
        out = jnp.broadcast_to(colvec[(NC - 1) * C:NC * C], (C, TT))
        for c in range(NC - 2, -1, -1):
            out = jnp.where(cjc == c, colvec[c * C:(c + 1) * C], out)
        return out

    def bd(xc):
        return jnp.where(same, jnp.concatenate([xc] * NC, axis=0), 0.0).astype(BF16)

    kf, qf, kT, kkc, qkc = [], [], [], [], []
    for kh in range(KH):
        q = q_ref[:, kh * DN_HEAD:(kh + 1) * DN_HEAD]
        k = k_ref[:, kh * DN_HEAD:(kh + 1) * DN_HEAD]
        kkc.append(compress(lax.dot_general(k, k, NT, preferred_element_type=F32)))
        qkc.append(compress(lax.dot_general(q, k, NT, preferred_element_type=F32)))
        kf.append(k.astype(F32))
        qf.append(q.astype(F32))
        kT.append(kf[-1].T)

    bcol, gcol, grow, Lc, qkm, P = [], [], [], [], [], []
    for hh in range(NV):
        hglob = grp * NV + hh
        sel = lane == hglob
        bcol.append(jnp.sum(jnp.where(sel, bcall, 0.0), axis=-1, keepdims=True))
        gcol.append(jnp.sum(jnp.where(sel, gcall, 0.0), axis=-1, keepdims=True))
        grow.append(gr_ref[0, pl.ds(hglob, 1), :])
        dec = jnp.exp(jnp.where(tril_c, by_row(gcol[hh]) - grow[hh], -jnp.inf))
        Lc.append(jnp.where(strict_c, by_row(bcol[hh]) * kkc[hh // 2] * dec, 0.0))
        qkm.append(qkc[hh // 2] * dec * scale)
        P.append(eye_c - Lc[hh])

    Lp = [jnp.dot(Lc[hh].astype(BF16), bd(Lc[hh]), preferred_element_type=F32) for hh in range(NV)]
    step = 4
    while step < C:
        res = [jnp.dot(jnp.concatenate([Lp[hh], P[hh]], axis=0).astype(BF16), bd(Lp[hh]),
                       preferred_element_type=F32) for hh in range(NV)]
        Lp = [r[:C] for r in res]
        P = [P[hh] + res[hh][C:] for hh in range(NV)]
        step *= 2
    P = [P[hh] + jnp.dot(P[hh].astype(BF16), bd(Lp[hh]), preferred_element_type=F32) for hh in range(NV)]

    uw, qd, kdT, eg_last = [], [], [], []
    for hh in range(NV):
        eg = jnp.exp(gcol[hh])
        vf = v_ref[:, hh * DN_HEAD:(hh + 1) * DN_HEAD].astype(F32)
        rhs = jnp.concatenate([vf * bcol[hh], kf[hh // 2] * (bcol[hh] * eg)], axis=1).astype(BF16)
        uw.append(jnp.dot(bd(P[hh]), rhs, preferred_element_type=F32))
        qd.append((qf[hh // 2] * (eg * scale)).astype(BF16))
        glrow = jnp.zeros((1, TT), F32)
        for c in range(NC):
            glrow = jnp.where(lchunk == c, gcol[hh][c * C + C - 1:c * C + C, :], glrow)
        kdT.append(kT[hh // 2] * jnp.exp(glrow - grow[hh]))
        eg_last.append([eg[c * C + C - 1:c * C + C, :] for c in range(NC)])

    S = [S_ref[hh] for hh in range(NV)]
    vn = [[] for _ in range(NV)]
    o_inter = [[] for _ in range(NV)]
    for c in range(NC):
        r0, r1 = c * C, (c + 1) * C
        for hh in range(NV):
            w_c = uw[hh][r0:r1, DN_HEAD:].astype(BF16)
            lhs = jnp.concatenate([w_c, qd[hh][r0:r1]], axis=0)
            res = jnp.dot(lhs, S[hh].astype(BF16), preferred_element_type=F32)
            vn_c = (uw[hh][r0:r1, :DN_HEAD] - res[:C]).astype(BF16)
            vn[hh].append(vn_c)
            o_inter[hh].append(res[C:])
            kd_c = kdT[hh][:, r0:r1].astype(BF16)
            S[hh] = S[hh] * eg_last[hh][c] + jnp.dot(kd_c, vn_c, preferred_element_type=F32)

    for hh in range(NV):
        S_ref[hh] = S[hh]
        vn_all = jnp.concatenate(vn[hh], axis=0)
        o = jnp.concatenate(o_inter[hh], axis=0) + jnp.dot(bd(qkm[hh]), vn_all, preferred_element_type=F32)
        ms = jnp.mean(o * o, axis=-1, keepdims=True)
        z = z_ref[:, hh * DN_HEAD:(hh + 1) * DN_HEAD].astype(F32)
        o_ref[:, hh * DN_HEAD:(hh + 1) * DN_HEAD] = (o * lax.rsqrt(ms + EPS) * nw * z).astype(BF16)


def _delta(qk_n, vv, sz, beta, gc, gcT, dn_norm_w, B, T, KH=4, TT=256):
    N = B * T
    nt = T // TT
    ng = DN_K_HEADS // KH
    row = lambda b, g, t: b * nt + t
    kw = KH * DN_HEAD
    return pl.pallas_call(
        functools.partial(_delta_body, KH=KH, TT=TT),
        grid=(B, ng, nt),
        in_specs=[pl.BlockSpec((TT, kw), lambda b, g, t: (row(b, g, t), g)),
                  pl.BlockSpec((TT, kw), lambda b, g, t: (row(b, g, t), DN_KEY // kw + g)),
                  pl.BlockSpec((TT, 2 * kw), lambda b, g, t: (row(b, g, t), g)),
                  pl.BlockSpec((TT, 2 * kw), lambda b, g, t: (row(b, g, t), g)),
                  pl.BlockSpec((TT, LANES), lambda b, g, t: (row(b, g, t), 0)),
                  pl.BlockSpec((TT, LANES), lambda b, g, t: (row(b, g, t), 0)),
                  pl.BlockSpec((1, DN_V_HEADS, TT), lambda b, g, t: (b, 0, t)),
                  pl.BlockSpec((1, DN_HEAD), lambda b, g, t: (0, 0))],
        out_specs=pl.BlockSpec((TT, 2 * kw), lambda b, g, t: (row(b, g, t), g)),
        out_shape=jax.ShapeDtypeStruct((N, DN_VAL), BF16),
        scratch_shapes=[pltpu.VMEM((2 * KH, DN_HEAD, DN_HEAD), F32)],
        compiler_params=_cparams(("arbitrary", "arbitrary", "arbitrary")),
        name="delta_rule",
    )(qk_n, qk_n, vv, sz, beta, gc, gcT, dn_norm_w.reshape(1, DN_HEAD))


def _merge_body(a_ref, o_ref, ga_ref, gb_ref, wa_ref, wb_ref, y_ref):
    ya = jnp.dot(a_ref[...], wa_ref[...], preferred_element_type=F32)
    yb = jnp.dot(o_ref[...], wb_ref[...], preferred_element_type=F32)
    y = ga_ref[...].astype(F32) * ya + gb_ref[...].astype(F32) * yb
    y_ref[...] = y.astype(BF16)


def _merge(a_g, o_g, p_m, wa, wb, tm=512, tn=512):
    N = a_g.shape[0]
    D = wa.shape[1]
    nj = D // tn
    return pl.pallas_call(
        _merge_body,
        grid=(N // tm, nj),
        in_specs=[pl.BlockSpec((tm, a_g.shape[1]), lambda i, j: (i, 0)),
                  pl.BlockSpec((tm, o_g.shape[1]), lambda i, j: (i, 0)),
                  pl.BlockSpec((tm, tn), lambda i, j: (i, j)),
                  pl.BlockSpec((tm, tn), lambda i, j: (i, nj + j)),
                  pl.BlockSpec((wa.shape[0], tn), lambda i, j: (0, j)),
                  pl.BlockSpec((wb.shape[0], tn), lambda i, j: (0, j))],
        out_specs=pl.BlockSpec((tm, tn), lambda i, j: (i, j)),
        out_shape=jax.ShapeDtypeStruct((N, D), BF16),
        compiler_params=_cparams(("arbitrary", "arbitrary")),
        name="merge",
    )(a_g, o_g, p_m, p_m, wa, wb)


def _out_body(y_ref, w_ref, x_ref, g_ref, o_ref):
    acc = jnp.dot(y_ref[...], w_ref[...], preferred_element_type=F32)
    o_ref[...] = x_ref[...] + g_ref[0] * acc


def _outproj(y, w, x2d, gate, T, tm=512, tn=512):
    N, D = x2d.shape
    B = gate.shape[0]
    tps = T // tm
    return pl.pallas_call(
        _out_body,
        grid=(N // tm, D // tn),
        in_specs=[pl.BlockSpec((tm, y.shape[1]), lambda i, j: (i, 0)),
                  pl.BlockSpec((w.shape[0], tn), lambda i, j: (0, j)),
                  pl.BlockSpec((tm, tn), lambda i, j: (i, j)),
                  pl.BlockSpec((1, 1, tn), lambda i, j: (i // tps, 0, j))],
        out_specs=pl.BlockSpec((tm, tn), lambda i, j: (i, j)),
        out_shape=jax.ShapeDtypeStruct((N, D), F32),
        compiler_params=_cparams(("arbitrary", "arbitrary")),
        name="out_proj",
    )(y, w, x2d, gate.reshape(B, 1, D))


def _layer(l, x, c, positions, w_ada, b_ada, norm_w, w_in, q_norm_w, k_norm_w, sinks,
           conv_w, a_log, dt_bias, dn_norm_w, w_o_swa, w_o_dn, w_out):
    B, T, D = x.shape
    N = B * T
    x2d = x.reshape(N, D)

    mod = _adaln(c, w_ada, b_ada)
    shift, scale, gate = mod[:, :D], mod[:, D:2 * D], mod[:, 2 * D:]
    h = _normmod(x2d, norm_w, scale, shift, T)

    o_q, o_g = 0, SWA_Q + 2 * SWA_KV
    o_d = o_g + SWA_Q
    o_z = o_d + DN_CONV_CH
    o_b = o_z + DN_VAL
    o_a = o_b + DN_V_HEADS
    o_m = o_a + DN_V_HEADS
    assert o_a == o_b + DN_V_HEADS and o_b % LANES == 0
    w_in = jnp.swapaxes(w_in, 1, 2)
    p_qkv = _proj(h, w_in, l, o_q, o_g - o_q, None, T, nblk=1, name="proj_qkv")
    sag = _proj(h, w_in, l, o_g, o_d - o_g, "silu", T, name="proj_swa_gate")
    qk_n = _proj(h, w_in, l, o_d, 2 * DN_KEY, "conv", T, extras=(conv_w, 0, True), name="proj_dn_qk")
    vv = _proj(h, w_in, l, o_d + 2 * DN_KEY, DN_VAL, "conv", T, extras=(conv_w, 2 * DN_KEY, False),
               name="proj_dn_v")
    sz = _proj(h, w_in, l, o_z, o_b - o_z, "silu", T, name="proj_dn_z")
    al = jnp.zeros((1, LANES), F32).at[0, :DN_V_HEADS].set(a_log)
    dt = jnp.zeros((1, LANES), F32).at[0, :DN_V_HEADS].set(dt_bias)
    beta, gc = _proj(h, w_in, l, o_b, 2 * DN_V_HEADS, "gates", T, extras=(al, dt), name="proj_dn_gates")
    p_m = _proj(h, w_in, l, o_m, 2 * D, "sigmoid", T, name="proj_merge_gates")

    posf = positions.astype(F32).reshape(N, 1)
    a_g = _swa(p_qkv, sag, posf, q_norm_w, k_norm_w, sinks, B, T)

    gcT = gc[:, :DN_V_HEADS].reshape(B, T, DN_V_HEADS).transpose(0, 2, 1)
    o_g2 = _delta(qk_n, vv, sz, beta, gc, gcT, dn_norm_w, B, T)

    y = _merge(a_g, o_g2, p_m, w_o_swa.astype(BF16), w_o_dn.astype(BF16))
    out = _outproj(y, w_out.astype(BF16), x2d, gate, T)
    return out.reshape(B, T, D)


def kernel(x, c, positions, w_ada, b_ada, norm_w, w_in, q_norm_w, k_norm_w, sinks, conv_w,
           a_log, dt_bias, dn_norm_w, w_o_swa, w_o_dn, w_out):
    depth = w_ada.shape[0]
    for l in range(depth):
        x = _layer(l, x, c, positions, w_ada[l], b_ada[l], norm_w[l], w_in, q_norm_w[l],
                   k_norm_w[l], sinks[l], conv_w[l], a_log[l], dt_bias[l], dn_norm_w[l],
                   w_o_swa[l], w_o_dn[l], w_out[l])
    return x
```

```python
import functools

import numpy as np
import jax
import jax.numpy as jnp
from jax import lax
from jax.experimental import pallas as pl
from jax.experimental.pallas import tpu as pltpu

F32 = jnp.float32
BF16 = jnp.bfloat16

D_MODEL = 2048
SWA_Q_HEADS = 32
SWA_KV_HEADS = 4
SWA_HEAD_DIM = 64
SWA_BLOCK = 128
ROPE_THETA = 500000.0
ROPE_DIM = SWA_HEAD_DIM // 4
DN_K_HEADS = 16
DN_V_HEADS = 32
DN_HEAD = 128
DN_CONV = 4
DN_CHUNK = 64
EPS = 1e-6

SWA_Q = SWA_Q_HEADS * SWA_HEAD_DIM
SWA_KV = SWA_KV_HEADS * SWA_HEAD_DIM
DN_KEY = DN_K_HEADS * DN_HEAD
DN_VAL = DN_V_HEADS * DN_HEAD
DN_CONV_CH = 2 * DN_KEY + DN_VAL

LANES = 128
SUBLANES = 8
VMEM_LIMIT = 56 * 1024 * 1024


def _cparams(sem):
    return pltpu.CompilerParams(dimension_semantics=sem, vmem_limit_bytes=VMEM_LIMIT)


def _sigmoid(x):
    return 0.5 * jnp.tanh(0.5 * x) + 0.5


def _silu(x):
    return x * _sigmoid(x)


def _adaln_body(c_ref, w_ref, b_ref, o_ref):
    s = _silu(c_ref[...]).astype(BF16)
    o_ref[...] = jnp.dot(s, w_ref[...].astype(BF16), preferred_element_type=F32) + b_ref[...]


def _adaln(c, w_ada, b_ada, tn=512):
    B, D = c.shape
    n_out = w_ada.shape[1]
    c8 = jnp.zeros((SUBLANES, D), F32).at[:B].set(c)
    out = pl.pallas_call(
        _adaln_body,
        grid=(n_out // tn,),
        in_specs=[pl.BlockSpec((SUBLANES, D), lambda j: (0, 0)),
                  pl.BlockSpec((D, tn), lambda j: (0, j)),
                  pl.BlockSpec((1, tn), lambda j: (0, j))],
        out_specs=pl.BlockSpec((SUBLANES, tn), lambda j: (0, j)),
        out_shape=jax.ShapeDtypeStruct((SUBLANES, n_out), F32),
        compiler_params=_cparams(("arbitrary",)),
        name="adaln_mod",
    )(c8, w_ada, b_ada.reshape(1, n_out))
    return out[:B]


def _normmod_body(x_ref, nw_ref, sc_ref, sh_ref, o_ref):
    x = x_ref[...]
    ms = jnp.mean(x * x, axis=-1, keepdims=True)
    y = x * lax.rsqrt(ms + EPS) * nw_ref[...]
    o_ref[...] = (y * (1.0 + sc_ref[0]) + sh_ref[0]).astype(BF16)


def _normmod(x2d, norm_w, scale, shift, T, tm=512):
    N, D = x2d.shape
    B = scale.shape[0]
    tps = T // tm
    return pl.pallas_call(
        _normmod_body,
        grid=(N // tm,),
        in_specs=[pl.BlockSpec((tm, D), lambda i: (i, 0)),
                  pl.BlockSpec((1, D), lambda i: (0, 0)),
                  pl.BlockSpec((1, 1, D), lambda i: (i // tps, 0, 0)),
                  pl.BlockSpec((1, 1, D), lambda i: (i // tps, 0, 0))],
        out_specs=pl.BlockSpec((tm, D), lambda i: (i, 0)),
        out_shape=jax.ShapeDtypeStruct((N, D), BF16),
        compiler_params=_cparams(("arbitrary",)),
        name="norm_mod",
    )(x2d, norm_w.reshape(1, D), scale.reshape(B, 1, D), shift.reshape(B, 1, D))


HALF = LANES // 2


NT_DIMS = (((1,), (1,)), ((), ()))


def _cast_weights(w_refs, w2_ref, wbf):
    wblk = w_refs[0].shape[0]
    off = 0 if w2_ref is None else HALF
    for b, w_ref in enumerate(w_refs):
        lo = max(b * wblk - off, 0)
        wbf[lo:(b + 1) * wblk - off, :] = w_ref[lo + off - b * wblk:, :].astype(BF16)
    if w2_ref is not None:
        n = len(w_refs) * wblk
        wbf[n - off:n, :] = w2_ref[...].astype(BF16)


def _cast_once(w_refs, w2_ref, wbf):
    @pl.when(pl.program_id(1) == 0)
    def _():
        _cast_weights(w_refs, w2_ref, wbf)


def _proj_act_body(*refs, act, shifted, sub, nblk):
    h_ref, w_refs = refs[0], refs[1:1 + nblk]
    w2_ref = refs[1 + nblk] if shifted else None
    o_ref, wbf = refs[-2:]
    _cast_once(w_refs, w2_ref, wbf)
    for r0 in range(0, h_ref.shape[0], sub):
        acc = lax.dot_general(h_ref[r0:r0 + sub, :], wbf[...], NT_DIMS, preferred_element_type=F32)
        if act == "silu":
            acc = _silu(acc)
        elif act == "sigmoid":
            acc = _sigmoid(acc)
        o_ref[r0:r0 + sub, :] = acc.astype(o_ref.dtype)


def _proj_conv_body(*refs, tm, tps, norm, sub, nblk):
    h_ref, w_refs = refs[0], refs[1:1 + nblk]
    cw_ref, o_ref, wbf, buf = refs[1 + nblk:]
    i = pl.program_id(1)
    first = (i % tps) == 0
    _cast_once(w_refs, None, wbf)

    @pl.when(first)
    def _():
        buf[0:SUBLANES, :] = jnp.zeros((SUBLANES, buf.shape[1]), F32)

    @pl.when(jnp.logical_not(first))
    def _():
        buf[0:SUBLANES, :] = buf[tm:tm + SUBLANES, :]

    cw = cw_ref[...]
    mm = lambda r0: lax.dot_general(h_ref[r0:r0 + sub, :], wbf[...], NT_DIMS, preferred_element_type=F32)
    nxt = mm(0)
    for r0 in range(0, tm, sub):
        acc = nxt
        if r0 + sub < tm:
            nxt = mm(r0 + sub)
        b0 = SUBLANES + r0
        buf[b0:b0 + sub, :] = acc
        y = acc * cw[DN_CONV - 1:DN_CONV, :]
        win = buf[b0 - SUBLANES:b0 + sub, :]
        for d in range(1, DN_CONV):
            y = y + pltpu.roll(win, d, 0)[SUBLANES:, :] * cw[DN_CONV - 1 - d:DN_CONV - d, :]
        y = _silu(y)
        for c in range(y.shape[1] // DN_HEAD):
            yc = y[:, c * DN_HEAD:(c + 1) * DN_HEAD]
            if norm:
                yc = yc * lax.rsqrt(jnp.sum(yc * yc, axis=-1, keepdims=True) + EPS)
            o_ref[r0:r0 + sub, c * DN_HEAD:(c + 1) * DN_HEAD] = yc.astype(o_ref.dtype)


def _norm_rope(x, nw, cs, s1, s2, lo):
    half = ROPE_DIM // 2
    x2 = x * x
    s_lo = jnp.sum(jnp.where(lo, x2, 0.0), axis=-1, keepdims=True)
    s_hi = jnp.sum(jnp.where(lo, 0.0, x2), axis=-1, keepdims=True)
    r = jnp.where(lo, lax.rsqrt(s_lo * (1.0 / SWA_HEAD_DIM) + EPS),
                  lax.rsqrt(s_hi * (1.0 / SWA_HEAD_DIM) + EPS))
    xn = x * r * nw
    return xn * cs + pltpu.roll(xn, half, 1) * s1 + pltpu.roll(xn, LANES - half, 1) * s2


def _proj_qknorm_body(*refs, sub, nblk, n_norm):
    h_ref, w_refs = refs[0], refs[1:1 + nblk]
    c_ref, s1_ref, s2_ref, nw_ref, o_ref, wbf = refs[1 + nblk:]
    _cast_once(w_refs, None, wbf)
    lo = lax.broadcasted_iota(jnp.int32, (1, LANES), 1) < SWA_HEAD_DIM
    nw = nw_ref[...]
    for r0 in range(0, h_ref.shape[0], sub):
        acc = lax.dot_general(h_ref[r0:r0 + sub, :], wbf[...], NT_DIMS, preferred_element_type=F32)
        cs, s1, s2 = c_ref[r0:r0 + sub, :], s1_ref[r0:r0 + sub, :], s2_ref[r0:r0 + sub, :]
        for c in range(acc.shape[1] // LANES):
            x = acc[:, c * LANES:(c + 1) * LANES]
            if c < n_norm:
                x = _norm_rope(x, nw, cs, s1, s2, lo)
            o_ref[r0:r0 + sub, c * LANES:(c + 1) * LANES] = x.astype(o_ref.dtype)


def _proj_gates_body(h_ref, w_ref, al_ref, dt_ref, beta_ref, gc_ref, wbf):
    _cast_once([w_ref], None, wbf)
    acc = lax.dot_general(h_ref[...], wbf[...], NT_DIMS, preferred_element_type=F32)
    beta_ref[...] = _sigmoid(acc)
    x = pltpu.roll(acc, LANES - DN_V_HEADS, 1) + dt_ref[...]
    sp = jnp.maximum(x, 0.0) + jnp.log(1.0 + jnp.exp(-jnp.abs(x)))
    g = -jnp.exp(al_ref[...]) * sp
    r64 = lax.broadcasted_iota(jnp.int32, g.shape, 0) & (DN_CHUNK - 1)
    s = 1
    while s < DN_CHUNK:
        g = g + jnp.where(r64 >= s, pltpu.roll(g, s, 0), 0.0)
        s *= 2
    gc_ref[...] = g


def _proj(h, w_t, layer, col0, width, mode, T, extras=(), tm=1024, wblk=512, nblk=2, sub=256,
          name="proj"):
    N, K = h.shape
    out_dtype = BF16
    tm = min(tm, T)
    sub = min(sub, tm)
    off = col0 % LANES
    base = col0 - off
    shifted = off != 0
    if mode == "gates":
        wblk, nblk = LANES, 1
    tn = wblk * nblk
    assert off in (0, HALF) and base % wblk == 0
    assert mode == "gates" or width % tn == 0
    jb = base // wblk
    grid = (max(width // tn, 1), N // tm)
    in_specs = [pl.BlockSpec((tm, K), lambda j, i: (i, 0))]
    in_specs += [pl.BlockSpec((None, wblk, K), lambda j, i, b=b: (layer, jb + j * nblk + b, 0))
                 for b in range(nblk)]
    args = [h] + [w_t] * nblk
    scratch = [pltpu.VMEM((tn, K), BF16)]
    out_spec = pl.BlockSpec((tm, tn), lambda j, i: (i, j))
    out_shape = jax.ShapeDtypeStruct((N, max(width, tn)), out_dtype)
    if shifted:
        in_specs.append(pl.BlockSpec((None, HALF, K),
                                     lambda j, i: (layer, (base + (j + 1) * tn) // HALF, 0)))
        args.append(w_t)
    if mode == "conv":
        cw, cw_col0, norm = extras
        in_specs.append(pl.BlockSpec((DN_CONV, tn), lambda j, i: (0, cw_col0 // tn + j)))
        args.append(cw)
        scratch.append(pltpu.VMEM((tm + SUBLANES, tn), F32))
        body = functools.partial(_proj_conv_body, tm=tm, tps=T // tm, norm=norm, sub=sub, nblk=nblk)
    elif mode == "qknorm":
        tabs, nw, n_norm = extras
        in_specs += [pl.BlockSpec((tm, LANES), lambda j, i: (i, 0))] * 3
        in_specs.append(pl.BlockSpec((1, LANES), lambda j, i: (0, 0)))
        args += list(tabs) + [nw]
        body = functools.partial(_proj_qknorm_body, sub=sub, nblk=nblk, n_norm=n_norm)
    elif mode == "gates":
        in_specs += [pl.BlockSpec((1, LANES), lambda j, i: (0, 0))] * 2
        args += list(extras)
        body = _proj_gates_body
        out_spec = [out_spec, out_spec]
        out_shape = [jax.ShapeDtypeStruct((N, LANES), F32)] * 2
    else:
        body = functools.partial(_proj_act_body, act=mode, shifted=shifted, sub=sub if mode else tm,
                                 nblk=nblk)
    return pl.pallas_call(
        body, grid=grid, in_specs=in_specs, out_specs=out_spec, out_shape=out_shape,
        scratch_shapes=scratch,
        compiler_params=_cparams(("arbitrary", "arbitrary")),
        name=name,
    )(*args)


def _rope_inv_freq_lanes():
    half = ROPE_DIM // 2
    inv = (np.float32(ROPE_THETA) ** (-np.arange(half, dtype=np.float32) * np.float32(2.0 / ROPE_DIM))).astype(np.float32)
    lanes = np.zeros((1, LANES), np.float32)
    for l in range(LANES):
        m = l % SWA_HEAD_DIM
        if m < ROPE_DIM:
            lanes[0, l] = inv[m % half]
    return lanes


def _rope_tab_body(pos_ref, invf_ref, c_ref, s1_ref, s2_ref):
    half = ROPE_DIM // 2
    l64 = lax.broadcasted_iota(jnp.int32, (1, LANES), 1) & (SWA_HEAD_DIM - 1)
    ang = pos_ref[...] * invf_ref[...]
    sn = jnp.sin(ang)
    c_ref[...] = jnp.cos(ang)
    s1_ref[...] = jnp.where((l64 >= half) & (l64 < ROPE_DIM), sn, 0.0)
    s2_ref[...] = jnp.where(l64 < half, -sn, 0.0)


def _rope_tables(posf, tm=1024):
    N = posf.shape[0]
    tm = min(tm, N)
    spec = pl.BlockSpec((tm, LANES), lambda i: (i, 0))
    return pl.pallas_call(
        _rope_tab_body,
        grid=(N // tm,),
        in_specs=[pl.BlockSpec((tm, 1), lambda i: (i, 0)), pl.BlockSpec((1, LANES), lambda i: (0, 0))],
        out_specs=[spec] * 3,
        out_shape=[jax.ShapeDtypeStruct((N, LANES), F32)] * 3,
        compiler_params=_cparams(("arbitrary",)),
        name="rope_tables",
    )(posf, jnp.asarray(_rope_inv_freq_lanes()))


def _swa_body(sink_ref, q_ref, kv_ref, g_ref, o_ref, kprev, vprev):
    n = pl.program_id(1)
    BLK = SWA_BLOCK
    HD = SWA_HEAD_DIM

    @pl.when(n == 0)
    def _():
        kprev[...] = jnp.zeros_like(kprev)
        vprev[...] = jnp.zeros_like(vprev)

    lo = lax.broadcasted_iota(jnp.int32, (1, LANES), 1) < HD
    kv = kv_ref[...]
    n_kc = SWA_KV // LANES
    kc = [kv[:, c * LANES:(c + 1) * LANES].astype(F32) for c in range(n_kc)]
    vT = [kv[:, SWA_KV + c * LANES:SWA_KV + (c + 1) * LANES].astype(F32).T for c in range(n_kc)]
    zrows = jnp.zeros((HD, BLK), F32)

    kband, vbandT = {}, {}
    for j in range(SWA_KV_HEADS):
        cj, b = j // 2, j % 2
        for a in range(2):
            src = kc[cj] if a == b else pltpu.roll(kc[cj], HD, 1)
            k_cur = jnp.where(lo if a == 0 else jnp.logical_not(lo), src, 0.0).astype(BF16)
            rows = vT[cj][b * HD:(b + 1) * HD]
            v_cur = jnp.concatenate([rows, zrows] if a == 0 else [zrows, rows], axis=0).astype(BF16)
            idx = 2 * j + a
            kband[(j, a)] = jnp.concatenate([kprev[idx], k_cur], axis=0)
            vbandT[(j, a)] = jnp.concatenate([vprev[idx], v_cur], axis=1)
            kprev[idx] = k_cur
            vprev[idx] = v_cur

    kj = lax.broadcasted_iota(jnp.int32, (2 * BLK, 2 * BLK), 0)
    qi = lax.broadcasted_iota(jnp.int32, (2 * BLK, 2 * BLK), 1) & (BLK - 1)
    d = kj - qi
    kmin = jnp.where(n == 0, BLK, 0)
    valid = (d >= 1) & (d <= BLK) & (kj >= kmin)
    left = lax.broadcasted_iota(jnp.int32, (1, 2 * BLK), 1) < BLK

    G = SWA_Q_HEADS // SWA_KV_HEADS
    cpk = G // 2
    items = [(j, a, j * cpk + 2 * p) for j in range(SWA_KV_HEADS) for a in range(2)
             for p in range(cpk // 2)]

    def scores(item):
        j, a, c1 = item
        rhs = jnp.concatenate([q_ref[:, c1 * LANES:(c1 + 1) * LANES],
                               q_ref[:, (c1 + 1) * LANES:(c1 + 2) * LANES]], axis=0)
        return lax.dot_general(kband[(j, a)], rhs, NT_DIMS, preferred_element_type=F32)

    acc = {}

    def consume(item, oT, inv):
        j, a, c1 = item
        oT = oT * inv
        for c, part in ((c1, oT[:, :BLK]), (c1 + 1, oT[:, BLK:])):
            if a == 0:
                acc[c] = part
            else:
                gate = g_ref[:, c * LANES:(c + 1) * LANES].astype(F32)
                o_ref[:, c * LANES:(c + 1) * LANES] = ((acc.pop(c) + part).T * gate).astype(BF16)

    s_cur = scores(items[0])
    pending = None
    for i, item in enumerate(items):
        j, a, c1 = item
        s_next = scores(items[i + 1]) if i + 1 < len(items) else None
        s = jnp.where(valid, s_cur, -jnp.inf)
        sink = jnp.where(left, sink_ref[2 * c1 + a], sink_ref[2 * c1 + 2 + a])
        m = jnp.maximum(jnp.max(s, axis=0, keepdims=True), sink)
        e = jnp.exp(s - m)
        inv = 1.0 / (jnp.sum(e, axis=0, keepdims=True) + jnp.exp(sink - m))
        oT = jnp.dot(vbandT[(j, a)], e.astype(BF16), preferred_element_type=F32)
        if pending is not None:
            consume(*pending)
        pending = (item, oT, inv)
        s_cur = s_next
    consume(*pending)


def _swa(q_n, kv_n, sag, sinks, B, T):
    N = B * T
    nb = T // SWA_BLOCK
    row = lambda b, n: b * nb + n
    return pl.pallas_call(
        _swa_body,
        grid=(B, nb),
        in_specs=[pl.BlockSpec(memory_space=pltpu.SMEM),
                  pl.BlockSpec((SWA_BLOCK, SWA_Q), lambda b, n: (row(b, n), 0)),
                  pl.BlockSpec((SWA_BLOCK, 2 * SWA_KV), lambda b, n: (row(b, n), 0)),
                  pl.BlockSpec((SWA_BLOCK, SWA_Q), lambda b, n: (row(b, n), 0))],
        out_specs=pl.BlockSpec((SWA_BLOCK, SWA_Q), lambda b, n: (row(b, n), 0)),
        out_shape=jax.ShapeDtypeStruct((N, SWA_Q), BF16),
        scratch_shapes=[pltpu.VMEM((2 * SWA_KV_HEADS, SWA_BLOCK, LANES), BF16),
                        pltpu.VMEM((2 * SWA_KV_HEADS, LANES, SWA_BLOCK), BF16)],
        compiler_params=_cparams(("arbitrary", "arbitrary")),
        name="swa_attn",
    )(sinks, q_n, kv_n, sag)


def _delta_body(q_ref, k_ref, v_ref, z_ref, bc_ref, gc_ref, gr_ref, nw_ref, o_ref, S_ref,
                *, KH, TT):
    grp = pl.program_id(1)
    t = pl.program_id(2)
    C = DN_CHUNK
    NC = TT // C
    scale = DN_HEAD ** -0.5
    NT = (((1,), (1,)), ((), ()))

    @pl.when(t == 0)
    def _():
        S_ref[...] = jnp.zeros_like(S_ref)

    NV = 2 * KH
    row = lax.broadcasted_iota(jnp.int32, (TT, TT), 0)
    col = lax.broadcasted_iota(jnp.int32, (TT, TT), 1)
    same = (row // C) == (col // C)
    ri = lax.broadcasted_iota(jnp.int32, (C, TT), 0)
    cj = lax.broadcasted_iota(jnp.int32, (C, TT), 1)
    cjc = cj // C
    cjj = cj & (C - 1)
    tril_c = ri >= cjj
    strict_c = ri > cjj
    eye_c = (ri == cjj).astype(F32)
    lane = lax.broadcasted_iota(jnp.int32, (TT, LANES), 1)
    lchunk = lax.broadcasted_iota(jnp.int32, (1, TT), 1) // C
    bcall = bc_ref[...]
    gcall = gc_ref[...]
    nw = nw_ref[...]

    def compress(x):
        out = x[(NC - 1) * C:NC * C]
        for c in range(NC - 2, -1, -1):
            out = jnp.where(cjc == c, x[c * C:(c + 1) * C], out)
        return out

    def by_row(colvec):
        out = jnp.broadcast_to(colvec[(NC - 1) * C:NC * C], (C, TT))
        for c in range(NC - 2, -1, -1):
            out = jnp.where(cjc == c, colvec[c * C:(c + 1) * C], out)
        return out

    def bd(xc):
        return jnp.where(same, jnp.concatenate([xc] * NC, axis=0), 0.0).astype(BF16)

    kf, qf, kT, kkc, qkc = [], [], [], [], []
    for kh in range(KH):
        q = q_ref[:, kh * DN_HEAD:(kh + 1) * DN_HEAD]
        k = k_ref[:, kh * DN_HEAD:(kh + 1) * DN_HEAD]
        kkc.append(compress(lax.dot_general(k, k, NT, preferred_element_type=F32)))
        qkc.append(compress(lax.dot_general(q, k, NT, preferred_element_type=F32)))
        kf.append(k.astype(F32))
        qf.append(q.astype(F32))
        kT.append(kf[-1].T)

    bcol, gcol, grow, Lc, qkm, P = [], [], [], [], [], []
    for hh in range(NV):
        hglob = grp * NV + hh
        sel = lane == hglob
        bcol.append(jnp.sum(jnp.where(sel, bcall, 0.0), axis=-1, keepdims=True))
        gcol.append(jnp.sum(jnp.where(sel, gcall, 0.0), axis=-1, keepdims=True))
        grow.append(gr_ref[0, pl.ds(hglob, 1), :])
        dec = jnp.exp(jnp.where(tril_c, by_row(gcol[hh]) - grow[hh], -jnp.inf))
        Lc.append(jnp.where(strict_c, by_row(bcol[hh]) * kkc[hh // 2] * dec, 0.0))
        qkm.append(qkc[hh // 2] * dec * scale)
        P.append(eye_c - Lc[hh])

    Lp = [jnp.dot(Lc[hh].astype(BF16), bd(Lc[hh]), preferred_element_type=F32) for hh in range(NV)]
    step = 4
    while step < C:
        res = [jnp.dot(jnp.concatenate([Lp[hh], P[hh]], axis=0).astype(BF16), bd(Lp[hh]),
                       preferred_element_type=F32) for hh in range(NV)]
        Lp = [r[:C] for r in res]
        P = [P[hh] + res[hh][C:] for hh in range(NV)]
        step *= 2
    P = [P[hh] + jnp.dot(P[hh].astype(BF16), bd(Lp[hh]), preferred_element_type=F32) for hh in range(NV)]

    uw, qd, kdT, eg_last = [], [], [], []
    for hh in range(NV):
        eg = jnp.exp(gcol[hh])
        vf = v_ref[:, hh * DN_HEAD:(hh + 1) * DN_HEAD].astype(F32)
        rhs = jnp.concatenate([vf * bcol[hh], kf[hh // 2] * (bcol[hh] * eg)], axis=1).astype(BF16)
        uw.append(jnp.dot(bd(P[hh]), rhs, preferred_element_type=F32))
        qd.append((qf[hh // 2] * (eg * scale)).astype(BF16))
        glrow = jnp.zeros((1, TT), F32)
        for c in range(NC):
            glrow = jnp.where(lchunk == c, gcol[hh][c * C + C - 1:c * C + C, :], glrow)
        kdT.append(kT[hh // 2] * jnp.exp(glrow - grow[hh]))
        eg_last.append([eg[c * C + C - 1:c * C + C, :] for c in range(NC)])

    S = [S_ref[hh] for hh in range(NV)]
    vn = [[] for _ in range(NV)]
    o_inter = [[] for _ in range(NV)]
    for c in range(NC):
        r0, r1 = c * C, (c + 1) * C
        for hh in range(NV):
            w_c = uw[hh][r0:r1, DN_HEAD:].astype(BF16)
            lhs = jnp.concatenate([w_c, qd[hh][r0:r1]], axis=0)
            res = jnp.dot(lhs, S[hh].astype(BF16), preferred_element_type=F32)
            vn_c = (uw[hh][r0:r1, :DN_HEAD] - res[:C]).astype(BF16)
            vn[hh].append(vn_c)
            o_inter[hh].append(res[C:])
            kd_c = kdT[hh][:, r0:r1].astype(BF16)
            S[hh] = S[hh] * eg_last[hh][c] + jnp.dot(kd_c, vn_c, preferred_element_type=F32)

    for hh in range(NV):
        S_ref[hh] = S[hh]
        vn_all = jnp.concatenate(vn[hh], axis=0)
        o = jnp.concatenate(o_inter[hh], axis=0) + jnp.dot(bd(qkm[hh]), vn_all, preferred_element_type=F32)
        ms = jnp.mean(o * o, axis=-1, keepdims=True)
        z = z_ref[:, hh * DN_HEAD:(hh + 1) * DN_HEAD].astype(F32)
        o_ref[:, hh * DN_HEAD:(hh + 1) * DN_HEAD] = (o * lax.rsqrt(ms + EPS) * nw * z).astype(BF16)


def _delta(qk_n, vv, sz, beta, gc, gcT, dn_norm_w, B, T, KH=4, TT=256):
    N = B * T
    nt = T // TT
    ng = DN_K_HEADS // KH
    row = lambda b, g, t: b * nt + t
    kw = KH * DN_HEAD
    return pl.pallas_call(
        functools.partial(_delta_body, KH=KH, TT=TT),
        grid=(B, ng, nt),
        in_specs=[pl.BlockSpec((TT, kw), lambda b, g, t: (row(b, g, t), g)),
                  pl.BlockSpec((TT, kw), lambda b, g, t: (row(b, g, t), DN_KEY // kw + g)),
                  pl.BlockSpec((TT, 2 * kw), lambda b, g, t: (row(b, g, t), g)),
                  pl.BlockSpec((TT, 2 * kw), lambda b, g, t: (row(b, g, t), g)),
                  pl.BlockSpec((TT, LANES), lambda b, g, t: (row(b, g, t), 0)),
                  pl.BlockSpec((TT, LANES), lambda b, g, t: (row(b, g, t), 0)),
                  pl.BlockSpec((1, DN_V_HEADS, TT), lambda b, g, t: (b, 0, t)),
                  pl.BlockSpec((1, DN_HEAD), lambda b, g, t: (0, 0))],
        out_specs=pl.BlockSpec((TT, 2 * kw), lambda b, g, t: (row(b, g, t), g)),
        out_shape=jax.ShapeDtypeStruct((N, DN_VAL), BF16),
        scratch_shapes=[pltpu.VMEM((2 * KH, DN_HEAD, DN_HEAD), F32)],
        compiler_params=_cparams(("arbitrary", "arbitrary", "arbitrary")),
        name="delta_rule",
    )(qk_n, qk_n, vv, sz, beta, gc, gcT, dn_norm_w.reshape(1, DN_HEAD))


def _merge_body(a_ref, o_ref, ga_ref, gb_ref, wa_ref, wb_ref, y_ref):
    ya = jnp.dot(a_ref[...], wa_ref[...], preferred_element_type=F32)
    yb = jnp.dot(o_ref[...], wb_ref[...], preferred_element_type=F32)
    y = ga_ref[...].astype(F32) * ya + gb_ref[...].astype(F32) * yb
    y_ref[...] = y.astype(BF16)


def _merge(a_g, o_g, p_m, wa, wb, tm=512, tn=512):
    N = a_g.shape[0]
    D = wa.shape[1]
    nj = D // tn
    return pl.pallas_call(
        _merge_body,
        grid=(N // tm, nj),
        in_specs=[pl.BlockSpec((tm, a_g.shape[1]), lambda i, j: (i, 0)),
                  pl.BlockSpec((tm, o_g.shape[1]), lambda i, j: (i, 0)),
                  pl.BlockSpec((tm, tn), lambda i, j: (i, j)),
                  pl.BlockSpec((tm, tn), lambda i, j: (i, nj + j)),
                  pl.BlockSpec((wa.shape[0], tn), lambda i, j: (0, j)),
                  pl.BlockSpec((wb.shape[0], tn), lambda i, j: (0, j))],
        out_specs=pl.BlockSpec((tm, tn), lambda i, j: (i, j)),
        out_shape=jax.ShapeDtypeStruct((N, D), BF16),
        compiler_params=_cparams(("arbitrary", "arbitrary")),
        name="merge",
    )(a_g, o_g, p_m, p_m, wa, wb)


def _out_body(y_ref, w_ref, x_ref, g_ref, o_ref):
    acc = jnp.dot(y_ref[...], w_ref[...], preferred_element_type=F32)
    o_ref[...] = x_ref[...] + g_ref[0] * acc


def _outproj(y, w, x2d, gate, T, tm=512, tn=512):
    N, D = x2d.shape
    B = gate.shape[0]
    tps = T // tm
    return pl.pallas_call(
        _out_body,
        grid=(N // tm, D // tn),
        in_specs=[pl.BlockSpec((tm, y.shape[1]), lambda i, j: (i, 0)),
                  pl.BlockSpec((w.shape[0], tn), lambda i, j: (0, j)),
                  pl.BlockSpec((tm, tn), lambda i, j: (i, j)),
                  pl.BlockSpec((1, 1, tn), lambda i, j: (i // tps, 0, j))],
        out_specs=pl.BlockSpec((tm, tn), lambda i, j: (i, j)),
        out_shape=jax.ShapeDtypeStruct((N, D), F32),
        compiler_params=_cparams(("arbitrary", "arbitrary")),
        name="out_proj",
    )(y, w, x2d, gate.reshape(B, 1, D))


def _layer(l, x, c, positions, w_ada, b_ada, norm_w, w_in, q_norm_w, k_norm_w, sinks,
           conv_w, a_log, dt_bias, dn_norm_w, w_o_swa, w_o_dn, w_out):
    B, T, D = x.shape
    N = B * T
    x2d = x.reshape(N, D)

    mod = _adaln(c, w_ada, b_ada)
    shift, scale, gate = mod[:, :D], mod[:, D:2 * D], mod[:, 2 * D:]
    h = _normmod(x2d, norm_w, scale, shift, T)

    o_q, o_g = 0, SWA_Q + 2 * SWA_KV
    o_d = o_g + SWA_Q
    o_z = o_d + DN_CONV_CH
    o_b = o_z + DN_VAL
    o_a = o_b + DN_V_HEADS
    o_m = o_a + DN_V_HEADS
    assert o_a == o_b + DN_V_HEADS and o_b % LANES == 0
    w_in = jnp.swapaxes(w_in, 1, 2)
    posf = positions.astype(F32).reshape(N, 1)
    tabs = _rope_tables(posf)
    rep = LANES // SWA_HEAD_DIM
    qw = jnp.tile(q_norm_w.reshape(1, SWA_HEAD_DIM), (1, rep)) * (SWA_HEAD_DIM ** -0.5)
    kw = jnp.tile(k_norm_w.reshape(1, SWA_HEAD_DIM), (1, rep))
    q_n = _proj(h, w_in, l, o_q, SWA_Q, "qknorm", T, extras=(tabs, qw, 2 * 512 // LANES), name="proj_q")
    kv_n = _proj(h, w_in, l, o_q + SWA_Q, 2 * SWA_KV, "qknorm", T, nblk=1,
                 extras=(tabs, kw, SWA_KV // LANES), name="proj_kv")
    sag = _proj(h, w_in, l, o_g, o_d - o_g, "silu", T, name="proj_swa_gate")
    qk_n = _proj(h, w_in, l, o_d, 2 * DN_KEY, "conv", T, extras=(conv_w, 0, True), name="proj_dn_qk")
    vv = _proj(h, w_in, l, o_d + 2 * DN_KEY, DN_VAL, "conv", T, extras=(conv_w, 2 * DN_KEY, False),
               name="proj_dn_v")
    sz = _proj(h, w_in, l, o_z, o_b - o_z, "silu", T, name="proj_dn_z")
    al = jnp.zeros((1, LANES), F32).at[0, :DN_V_HEADS].set(a_log)
    dt = jnp.zeros((1, LANES), F32).at[0, :DN_V_HEADS].set(dt_bias)
    beta, gc = _proj(h, w_in, l, o_b, 2 * DN_V_HEADS, "gates", T, extras=(al, dt), name="proj_dn_gates")
    p_m = _proj(h, w_in, l, o_m, 2 * D, "sigmoid", T, name="proj_merge_gates")

    a_g = _swa(q_n, kv_n, sag, sinks, B, T)

    gcT = gc[:, :DN_V_HEADS].reshape(B, T, DN_V_HEADS).transpose(0, 2, 1)
    o_g2 = _delta(qk_n, vv, sz, beta, gc, gcT, dn_norm_w, B, T)

    y = _merge(a_g, o_g2, p_m, w_o_swa.astype(BF16), w_o_dn.astype(BF16))
    out = _outproj(y, w_out.astype(BF16), x2d, gate, T)
    return out.reshape(B, T, D)


def kernel(x, c, positions, w_ada, b_ada, norm_w, w_in, q_norm_w, k_norm_w, sinks, conv_w,
           a_log, dt_bias, dn_norm_w, w_o_swa, w_o_dn, w_out):
    depth = w_ada.shape[0]
    for l in range(depth):
        x = _layer(l, x, c, positions, w_ada[l], b_ada[l], norm_w[l], w_in, q_norm_w[l],
                   k_norm_w[l], sinks[l], conv_w[l], a_log[l], dt_bias[l], dn_norm_w[l],
                   w_o_swa[l], w_o_dn[l], w_out[l])
    return x
```

```python
import functools

import numpy as np
import jax
import jax.numpy as jnp
from jax import lax
from jax.experimental import pallas as pl
from jax.experimental.pallas import tpu as pltpu

F32 = jnp.float32
BF16 = jnp.bfloat16

D_MODEL = 2048
SWA_Q_HEADS = 32
SWA_KV_HEADS = 4
SWA_HEAD_DIM = 64
SWA_BLOCK = 128
ROPE_THETA = 500000.0
ROPE_DIM = SWA_HEAD_DIM // 4
DN_K_HEADS = 16
DN_V_HEADS = 32
DN_HEAD = 128
DN_CONV = 4
DN_CHUNK = 64
EPS = 1e-6

SWA_Q = SWA_Q_HEADS * SWA_HEAD_DIM
SWA_KV = SWA_KV_HEADS * SWA_HEAD_DIM
DN_KEY = DN_K_HEADS * DN_HEAD
DN_VAL = DN_V_HEADS * DN_HEAD
DN_CONV_CH = 2 * DN_KEY + DN_VAL

MXU_N = 256
LANES = 128
SUBLANES = 8
VMEM_LIMIT = 56 * 1024 * 1024


def _cparams(sem, flags=None):
    return pltpu.CompilerParams(dimension_semantics=sem, vmem_limit_bytes=VMEM_LIMIT, flags=flags)


def _sigmoid(x):
    return 0.5 * jnp.tanh(0.5 * x) + 0.5


def _silu(x):
    h = 0.5 * x
    return h + h * jnp.tanh(h)


def _adaln_body(c_ref, w_ref, b_ref, o_ref):
    s = _silu(c_ref[...]).astype(BF16)
    o_ref[...] = jnp.dot(s, w_ref[...].astype(BF16), preferred_element_type=F32) + b_ref[...]


def _adaln(c, w_ada, b_ada, tn=512):
    B, D = c.shape
    n_out = w_ada.shape[1]
    c8 = jnp.zeros((SUBLANES, D), F32).at[:B].set(c)
    out = pl.pallas_call(
        _adaln_body,
        grid=(n_out // tn,),
        in_specs=[pl.BlockSpec((SUBLANES, D), lambda j: (0, 0)),
                  pl.BlockSpec((D, tn), lambda j: (0, j)),
                  pl.BlockSpec((1, tn), lambda j: (0, j))],
        out_specs=pl.BlockSpec((SUBLANES, tn), lambda j: (0, j)),
        out_shape=jax.ShapeDtypeStruct((SUBLANES, n_out), F32),
        compiler_params=_cparams(("arbitrary",)),
        name="adaln_mod",
    )(c8, w_ada, b_ada.reshape(1, n_out))
    return out[:B]


def _normmod_body(x_ref, nw_ref, sc_ref, sh_ref, o_ref):
    x = x_ref[...]
    ms = jnp.mean(x * x, axis=-1, keepdims=True)
    y = x * lax.rsqrt(ms + EPS) * nw_ref[...]
    o_ref[...] = (y * (1.0 + sc_ref[0]) + sh_ref[0]).astype(BF16)


def _normmod(x2d, norm_w, scale, shift, T, tm=512):
    N, D = x2d.shape
    B = scale.shape[0]
    tps = T // tm
    return pl.pallas_call(
        _normmod_body,
        grid=(N // tm,),
        in_specs=[pl.BlockSpec((tm, D), lambda i: (i, 0)),
                  pl.BlockSpec((1, D), lambda i: (0, 0)),
                  pl.BlockSpec((1, 1, D), lambda i: (i // tps, 0, 0)),
                  pl.BlockSpec((1, 1, D), lambda i: (i // tps, 0, 0))],
        out_specs=pl.BlockSpec((tm, D), lambda i: (i, 0)),
        out_shape=jax.ShapeDtypeStruct((N, D), BF16),
        compiler_params=_cparams(("arbitrary",)),
        name="norm_mod",
    )(x2d, norm_w.reshape(1, D), scale.reshape(B, 1, D), shift.reshape(B, 1, D))


HALF = LANES // 2


NT_DIMS = (((1,), (1,)), ((), ()))


def _cast_weights(w_refs, w2_ref, wbf):
    wblk = w_refs[0].shape[0]
    off = 0 if w2_ref is None else HALF
    for b, w_ref in enumerate(w_refs):
        lo = max(b * wblk - off, 0)
        wbf[lo:(b + 1) * wblk - off, :] = w_ref[lo + off - b * wblk:, :].astype(BF16)
    if w2_ref is not None:
        n = len(w_refs) * wblk
        wbf[n - off:n, :] = w2_ref[...].astype(BF16)


def _cast_once(w_refs, w2_ref, wbf):
    @pl.when(pl.program_id(1) == 0)
    def _():
        _cast_weights(w_refs, w2_ref, wbf)


def _proj_act_body(*refs, act, shifted, sub, nblk):
    h_ref, w_refs = refs[0], refs[1:1 + nblk]
    w2_ref = refs[1 + nblk] if shifted else None
    o_ref, wbf = refs[-2:]
    _cast_once(w_refs, w2_ref, wbf)
    for r0 in range(0, h_ref.shape[0], sub):
        acc = lax.dot_general(h_ref[r0:r0 + sub, :], wbf[...], NT_DIMS, preferred_element_type=F32)
        if act == "silu":
            acc = _silu(acc)
        elif act == "sigmoid":
            acc = _sigmoid(acc)
        o_ref[r0:r0 + sub, :] = acc.astype(o_ref.dtype)


def _proj_conv_body(*refs, tm, tps, norm, sub, nblk):
    h_ref, w_refs = refs[0], refs[1:1 + nblk]
    cw_ref, o_ref, wbf, buf = refs[1 + nblk:]
    i = pl.program_id(1)
    first = (i % tps) == 0
    _cast_once(w_refs, None, wbf)

    @pl.when(first)
    def _():
        buf[0:SUBLANES, :] = jnp.zeros((SUBLANES, buf.shape[1]), F32)

    @pl.when(jnp.logical_not(first))
    def _():
        buf[0:SUBLANES, :] = buf[tm:tm + SUBLANES, :]

    cw = cw_ref[...]
    mm = lambda r0: lax.dot_general(h_ref[r0:r0 + sub, :], wbf[...], NT_DIMS, preferred_element_type=F32)
    nxt = mm(0)
    for r0 in range(0, tm, sub):
        acc = nxt
        if r0 + sub < tm:
            nxt = mm(r0 + sub)
        b0 = SUBLANES + r0
        buf[b0:b0 + sub, :] = acc
        y = acc * cw[DN_CONV - 1:DN_CONV, :]
        win = buf[b0 - SUBLANES:b0 + sub, :]
        for d in range(1, DN_CONV):
            y = y + pltpu.roll(win, d, 0)[SUBLANES:, :] * cw[DN_CONV - 1 - d:DN_CONV - d, :]
        y = _silu(y)
        for c in range(y.shape[1] // DN_HEAD):
            yc = y[:, c * DN_HEAD:(c + 1) * DN_HEAD]
            if norm:
                yc = yc * lax.rsqrt(jnp.sum(yc * yc, axis=-1, keepdims=True) + EPS)
            o_ref[r0:r0 + sub, c * DN_HEAD:(c + 1) * DN_HEAD] = yc.astype(o_ref.dtype)


def _norm_rope(x, nw, cs, s1, s2, lo):
    half = ROPE_DIM // 2
    x2 = x * x
    s_lo = jnp.sum(jnp.where(lo, x2, 0.0), axis=-1, keepdims=True)
    s_hi = jnp.sum(jnp.where(lo, 0.0, x2), axis=-1, keepdims=True)
    r = jnp.where(lo, lax.rsqrt(s_lo * (1.0 / SWA_HEAD_DIM) + EPS),
                  lax.rsqrt(s_hi * (1.0 / SWA_HEAD_DIM) + EPS))
    xn = x * r * nw
    return xn * cs + pltpu.roll(xn, half, 1) * s1 + pltpu.roll(xn, LANES - half, 1) * s2


def _proj_qknorm_body(*refs, sub, nblk, n_norm):
    h_ref, w_refs = refs[0], refs[1:1 + nblk]
    c_ref, s1_ref, s2_ref, nw_ref, o_ref, wbf = refs[1 + nblk:]
    _cast_once(w_refs, None, wbf)
    lo = lax.broadcasted_iota(jnp.int32, (1, LANES), 1) < SWA_HEAD_DIM
    nw = nw_ref[...]
    for r0 in range(0, h_ref.shape[0], sub):
        acc = lax.dot_general(h_ref[r0:r0 + sub, :], wbf[...], NT_DIMS, preferred_element_type=F32)
        cs, s1, s2 = c_ref[r0:r0 + sub, :], s1_ref[r0:r0 + sub, :], s2_ref[r0:r0 + sub, :]
        for c in range(acc.shape[1] // LANES):
            x = acc[:, c * LANES:(c + 1) * LANES]
            if c < n_norm:
                x = _norm_rope(x, nw, cs, s1, s2, lo)
            o_ref[r0:r0 + sub, c * LANES:(c + 1) * LANES] = x.astype(o_ref.dtype)


def _proj_gates_body(h_ref, w_ref, al_ref, dt_ref, beta_ref, gc_ref, wbf):
    _cast_once([w_ref], None, wbf)
    acc = lax.dot_general(h_ref[...], wbf[...], NT_DIMS, preferred_element_type=F32)
    beta_ref[...] = _sigmoid(acc)
    x = pltpu.roll(acc, LANES - DN_V_HEADS, 1) + dt_ref[...]
    sp = jnp.maximum(x, 0.0) + jnp.log(1.0 + jnp.exp(-jnp.abs(x)))
    g = -jnp.exp(al_ref[...]) * sp
    r64 = lax.broadcasted_iota(jnp.int32, g.shape, 0) & (DN_CHUNK - 1)
    s = 1
    while s < DN_CHUNK:
        g = g + jnp.where(r64 >= s, pltpu.roll(g, s, 0), 0.0)
        s *= 2
    gc_ref[...] = g


def _proj(h, w_t, layer, col0, width, mode, T, extras=(), tm=1024, wblk=512, nblk=2, sub=256,
          name="proj"):
    N, K = h.shape
    out_dtype = BF16
    tm = min(tm, T)
    sub = min(sub, tm)
    off = col0 % LANES
    base = col0 - off
    shifted = off != 0
    if mode == "gates":
        wblk, nblk = LANES, 1
    tn = wblk * nblk
    assert off in (0, HALF) and base % wblk == 0
    assert mode == "gates" or width % tn == 0
    jb = base // wblk
    grid = (max(width // tn, 1), N // tm)
    in_specs = [pl.BlockSpec((tm, K), lambda j, i: (i, 0))]
    in_specs += [pl.BlockSpec((None, wblk, K), lambda j, i, b=b: (layer, jb + j * nblk + b, 0))
                 for b in range(nblk)]
    args = [h] + [w_t] * nblk
    scratch = [pltpu.VMEM((tn, K), BF16)]
    out_spec = pl.BlockSpec((tm, tn), lambda j, i: (i, j))
    out_shape = jax.ShapeDtypeStruct((N, max(width, tn)), out_dtype)
    if shifted:
        in_specs.append(pl.BlockSpec((None, HALF, K),
                                     lambda j, i: (layer, (base + (j + 1) * tn) // HALF, 0)))
        args.append(w_t)
    if mode == "conv":
        cw, cw_col0, norm = extras
        in_specs.append(pl.BlockSpec((DN_CONV, tn), lambda j, i: (0, cw_col0 // tn + j)))
        args.append(cw)
        scratch.append(pltpu.VMEM((tm + SUBLANES, tn), F32))
        body = functools.partial(_proj_conv_body, tm=tm, tps=T // tm, norm=norm, sub=sub, nblk=nblk)
    elif mode == "qknorm":
        tabs, nw, n_norm = extras
        in_specs += [pl.BlockSpec((tm, LANES), lambda j, i: (i, 0))] * 3
        in_specs.append(pl.BlockSpec((1, LANES), lambda j, i: (0, 0)))
        args += list(tabs) + [nw]
        body = functools.partial(_proj_qknorm_body, sub=sub, nblk=nblk, n_norm=n_norm)
    elif mode == "gates":
        in_specs += [pl.BlockSpec((1, LANES), lambda j, i: (0, 0))] * 2
        args += list(extras)
        body = _proj_gates_body
        out_spec = [out_spec, out_spec]
        out_shape = [jax.ShapeDtypeStruct((N, LANES), F32)] * 2
    else:
        body = functools.partial(_proj_act_body, act=mode, shifted=shifted, sub=sub if mode else tm,
                                 nblk=nblk)
    return pl.pallas_call(
        body, grid=grid, in_specs=in_specs, out_specs=out_spec, out_shape=out_shape,
        scratch_shapes=scratch,
        compiler_params=_cparams(("arbitrary", "arbitrary")),
        name=name,
    )(*args)


def _rope_inv_freq_lanes():
    half = ROPE_DIM // 2
    inv = (np.float32(ROPE_THETA) ** (-np.arange(half, dtype=np.float32) * np.float32(2.0 / ROPE_DIM))).astype(np.float32)
    lanes = np.zeros((1, LANES), np.float32)
    for l in range(LANES):
        m = l % SWA_HEAD_DIM
        if m < ROPE_DIM:
            lanes[0, l] = inv[m % half]
    return lanes


def _rope_tab_body(pos_ref, invf_ref, c_ref, s1_ref, s2_ref):
    half = ROPE_DIM // 2
    l64 = lax.broadcasted_iota(jnp.int32, (1, LANES), 1) & (SWA_HEAD_DIM - 1)
    ang = pos_ref[...] * invf_ref[...]
    sn = jnp.sin(ang)
    c_ref[...] = jnp.cos(ang)
    s1_ref[...] = jnp.where((l64 >= half) & (l64 < ROPE_DIM), sn, 0.0)
    s2_ref[...] = jnp.where(l64 < half, -sn, 0.0)


def _rope_tables(posf, tm=1024):
    N = posf.shape[0]
    tm = min(tm, N)
    spec = pl.BlockSpec((tm, LANES), lambda i: (i, 0))
    return pl.pallas_call(
        _rope_tab_body,
        grid=(N // tm,),
        in_specs=[pl.BlockSpec((tm, 1), lambda i: (i, 0)), pl.BlockSpec((1, LANES), lambda i: (0, 0))],
        out_specs=[spec] * 3,
        out_shape=[jax.ShapeDtypeStruct((N, LANES), F32)] * 3,
        compiler_params=_cparams(("arbitrary",)),
        name="rope_tables",
    )(posf, jnp.asarray(_rope_inv_freq_lanes()))


def _swa_body(sink_ref, q_ref, kv_ref, g_ref, o_ref, kprev, vprev):
    n = pl.program_id(1)
    BLK = SWA_BLOCK
    HD = SWA_HEAD_DIM

    @pl.when(n == 0)
    def _():
        kprev[...] = jnp.zeros_like(kprev)
        vprev[...] = jnp.zeros_like(vprev)

    lo = lax.broadcasted_iota(jnp.int32, (1, LANES), 1) < HD
    kv = kv_ref[...]
    n_kc = SWA_KV // LANES
    kc = [kv[:, c * LANES:(c + 1) * LANES].astype(F32) for c in range(n_kc)]
    vT = [kv[:, SWA_KV + c * LANES:SWA_KV + (c + 1) * LANES].astype(F32).T for c in range(n_kc)]
    zrows = jnp.zeros((HD, BLK), F32)

    kband, vbandT = {}, {}
    for j in range(SWA_KV_HEADS):
        cj, b = j // 2, j % 2
        for a in range(2):
            src = kc[cj] if a == b else pltpu.roll(kc[cj], HD, 1)
            k_cur = jnp.where(lo if a == 0 else jnp.logical_not(lo), src, 0.0).astype(BF16)
            rows = vT[cj][b * HD:(b + 1) * HD]
            v_cur = jnp.concatenate([rows, zrows] if a == 0 else [zrows, rows], axis=0).astype(BF16)
            idx = 2 * j + a
            kband[(j, a)] = jnp.concatenate([kprev[idx], k_cur], axis=0)
            vbandT[(j, a)] = jnp.concatenate([vprev[idx], v_cur], axis=1)
            kprev[idx] = k_cur
            vprev[idx] = v_cur

    kj = lax.broadcasted_iota(jnp.int32, (2 * BLK, 2 * BLK), 0)
    qi = lax.broadcasted_iota(jnp.int32, (2 * BLK, 2 * BLK), 1) & (BLK - 1)
    d = kj - qi
    kmin = jnp.where(n == 0, BLK, 0)
    valid = (d >= 1) & (d <= BLK) & (kj >= kmin)
    left = lax.broadcasted_iota(jnp.int32, (1, 2 * BLK), 1) < BLK

    G = SWA_Q_HEADS // SWA_KV_HEADS
    cpk = G // 2
    items = [(j, a, j * cpk + 2 * p) for j in range(SWA_KV_HEADS) for a in range(2)
             for p in range(cpk // 2)]

    def scores(item):
        j, a, c1 = item
        rhs = jnp.concatenate([q_ref[:, c1 * LANES:(c1 + 1) * LANES],
                               q_ref[:, (c1 + 1) * LANES:(c1 + 2) * LANES]], axis=0)
        return lax.dot_general(kband[(j, a)], rhs, NT_DIMS, preferred_element_type=F32)

    acc = {}

    def consume(item, oT, inv):
        j, a, c1 = item
        oT = oT * inv
        for c, part in ((c1, oT[:, :BLK]), (c1 + 1, oT[:, BLK:])):
            if a == 0:
                acc[c] = part
            else:
                gate = g_ref[:, c * LANES:(c + 1) * LANES].astype(F32)
                o_ref[:, c * LANES:(c + 1) * LANES] = ((acc.pop(c) + part).T * gate).astype(BF16)

    s_cur = scores(items[0])
    pending = None
    for i, item in enumerate(items):
        j, a, c1 = item
        s_next = scores(items[i + 1]) if i + 1 < len(items) else None
        s = jnp.where(valid, s_cur, -jnp.inf)
        sink = jnp.where(left, sink_ref[2 * c1 + a], sink_ref[2 * c1 + 2 + a])
        m = jnp.maximum(jnp.max(s, axis=0, keepdims=True), sink)
        e = jnp.exp(s - m)
        inv = 1.0 / (jnp.sum(e, axis=0, keepdims=True) + jnp.exp(sink - m))
        oT = jnp.dot(vbandT[(j, a)], e.astype(BF16), preferred_element_type=F32)
        if pending is not None:
            consume(*pending)
        pending = (item, oT, inv)
        s_cur = s_next
    consume(*pending)


def _swa(q_n, kv_n, sag, sinks, B, T):
    N = B * T
    nb = T // SWA_BLOCK
    row = lambda b, n: b * nb + n
    return pl.pallas_call(
        _swa_body,
        grid=(B, nb),
        in_specs=[pl.BlockSpec(memory_space=pltpu.SMEM),
                  pl.BlockSpec((SWA_BLOCK, SWA_Q), lambda b, n: (row(b, n), 0)),
                  pl.BlockSpec((SWA_BLOCK, 2 * SWA_KV), lambda b, n: (row(b, n), 0)),
                  pl.BlockSpec((SWA_BLOCK, SWA_Q), lambda b, n: (row(b, n), 0))],
        out_specs=pl.BlockSpec((SWA_BLOCK, SWA_Q), lambda b, n: (row(b, n), 0)),
        out_shape=jax.ShapeDtypeStruct((N, SWA_Q), BF16),
        scratch_shapes=[pltpu.VMEM((2 * SWA_KV_HEADS, SWA_BLOCK, LANES), BF16),
                        pltpu.VMEM((2 * SWA_KV_HEADS, LANES, SWA_BLOCK), BF16)],
        compiler_params=_cparams(("arbitrary", "arbitrary")),
        name="swa_attn",
    )(sinks, q_n, kv_n, sag)


def _delta_body(q_ref, k_ref, v_ref, z_ref, bc_ref, gc_ref, gr_ref, nw_ref, o_ref, S_ref,
                *, KH, TT):
    grp = pl.program_id(1)
    t = pl.program_id(2)
    C = DN_CHUNK
    NC = TT // C
    scale = DN_HEAD ** -0.5
    NT = (((1,), (1,)), ((), ()))

    @pl.when(t == 0)
    def _():
        S_ref[...] = jnp.zeros_like(S_ref)

    NV = 2 * KH
    row = lax.broadcasted_iota(jnp.int32, (TT, TT), 0)
    col = lax.broadcasted_iota(jnp.int32, (TT, TT), 1)
    same = (row // C) == (col // C)
    ri = lax.broadcasted_iota(jnp.int32, (C, TT), 0)
    cj = lax.broadcasted_iota(jnp.int32, (C, TT), 1)
    cjc = cj // C
    cjj = cj & (C - 1)
    tril_c = ri >= cjj
    strict_c = ri > cjj
    eye_c = (ri == cjj).astype(F32)
    lane = lax.broadcasted_iota(jnp.int32, (TT, LANES), 1)
    lchunk = lax.broadcasted_iota(jnp.int32, (1, TT), 1) // C
    bcall = bc_ref[...]
    gcall = gc_ref[...]
    nw = nw_ref[...]

    def compress(x):
        out = x[(NC - 1) * C:NC * C]
        for c in range(NC - 2, -1, -1):
            out = jnp.where(cjc == c, x[c * C:(c + 1) * C], out)
        return out

    def by_row(colvec):
        out = jnp.broadcast_to(colvec[(NC - 1) * C:NC * C], (C, TT))
        for c in range(NC - 2, -1, -1):
            out = jnp.where(cjc == c, colvec[c * C:(c + 1) * C], out)
        return out

    def bd(xc):
        return jnp.where(same, jnp.concatenate([xc] * NC, axis=0), 0.0).astype(BF16)

    kf, qf, kT, kkc, qkc = [], [], [], [], []
    for kh in range(KH):
        q = q_ref[:, kh * DN_HEAD:(kh + 1) * DN_HEAD]
        k = k_ref[:, kh * DN_HEAD:(kh + 1) * DN_HEAD]
        kkc.append(compress(lax.dot_general(k, k, NT, preferred_element_type=F32)))
        qkc.append(compress(lax.dot_general(q, k, NT, preferred_element_type=F32)))
        kf.append(k.astype(F32))
        qf.append(q.astype(F32))
        kT.append(kf[-1].T)

    bcol, gcol, grow, Lc, qkm, P = [], [], [], [], [], []
    for hh in range(NV):
        hglob = grp * NV + hh
        sel = lane == hglob
        bcol.append(jnp.sum(jnp.where(sel, bcall, 0.0), axis=-1, keepdims=True))
        gcol.append(jnp.sum(jnp.where(sel, gcall, 0.0), axis=-1, keepdims=True))
        grow.append(gr_ref[0, pl.ds(hglob, 1), :])
        dec = jnp.exp(jnp.where(tril_c, by_row(gcol[hh]) - grow[hh], -jnp.inf))
        Lc.append(jnp.where(strict_c, by_row(bcol[hh]) * kkc[hh // 2] * dec, 0.0))
        qkm.append(qkc[hh // 2] * dec * scale)
        P.append(eye_c - Lc[hh])

    Lp = [jnp.dot(Lc[hh].astype(BF16), bd(Lc[hh]), preferred_element_type=F32) for hh in range(NV)]
    step = 4
    while step < C:
        res = [jnp.dot(jnp.concatenate([Lp[hh], P[hh]], axis=0).astype(BF16), bd(Lp[hh]),
                       preferred_element_type=F32) for hh in range(NV)]
        Lp = [r[:C] for r in res]
        P = [P[hh] + res[hh][C:] for hh in range(NV)]
        step *= 2
    P = [P[hh] + jnp.dot(P[hh].astype(BF16), bd(Lp[hh]), preferred_element_type=F32) for hh in range(NV)]

    uw, qd, kdT, eg_last = [], [], [], []
    for hh in range(NV):
        eg = jnp.exp(gcol[hh])
        vf = v_ref[:, hh * DN_HEAD:(hh + 1) * DN_HEAD].astype(F32)
        rhs = jnp.concatenate([vf * bcol[hh], kf[hh // 2] * (bcol[hh] * eg)], axis=1).astype(BF16)
        uw.append(jnp.dot(bd(P[hh]), rhs, preferred_element_type=F32))
        qd.append((qf[hh // 2] * (eg * scale)).astype(BF16))
        glrow = jnp.zeros((1, TT), F32)
        for c in range(NC):
            glrow = jnp.where(lchunk == c, gcol[hh][c * C + C - 1:c * C + C, :], glrow)
        kdT.append(kT[hh // 2] * jnp.exp(glrow - grow[hh]))
        eg_last.append([eg[c * C + C - 1:c * C + C, :] for c in range(NC)])

    S = [S_ref[hh] for hh in range(NV)]
    vn = [[] for _ in range(NV)]
    o_inter = [[] for _ in range(NV)]
    for c in range(NC):
        r0, r1 = c * C, (c + 1) * C
        for hh in range(NV):
            w_c = uw[hh][r0:r1, DN_HEAD:].astype(BF16)
            lhs = jnp.concatenate([w_c, qd[hh][r0:r1]], axis=0)
            res = jnp.dot(lhs, S[hh].astype(BF16), preferred_element_type=F32)
            vn_c = (uw[hh][r0:r1, :DN_HEAD] - res[:C]).astype(BF16)
            vn[hh].append(vn_c)
            o_inter[hh].append(res[C:])
            kd_c = kdT[hh][:, r0:r1].astype(BF16)
            S[hh] = S[hh] * eg_last[hh][c] + jnp.dot(kd_c, vn_c, preferred_element_type=F32)

    for hh in range(NV):
        S_ref[hh] = S[hh]
        vn_all = jnp.concatenate(vn[hh], axis=0)
        o = jnp.concatenate(o_inter[hh], axis=0) + jnp.dot(bd(qkm[hh]), vn_all, preferred_element_type=F32)
        ms = jnp.mean(o * o, axis=-1, keepdims=True)
        z = z_ref[:, hh * DN_HEAD:(hh + 1) * DN_HEAD].astype(F32)
        o_ref[:, hh * DN_HEAD:(hh + 1) * DN_HEAD] = (o * lax.rsqrt(ms + EPS) * nw * z).astype(BF16)


def _delta(qk_n, vv, sz, beta, gc, gcT, dn_norm_w, B, T, KH=8, TT=256):
    N = B * T
    nt = T // TT
    ng = DN_K_HEADS // KH
    row = lambda b, g, t: b * nt + t
    kw = KH * DN_HEAD
    return pl.pallas_call(
        functools.partial(_delta_body, KH=KH, TT=TT),
        grid=(B, ng, nt),
        in_specs=[pl.BlockSpec((TT, kw), lambda b, g, t: (row(b, g, t), g)),
                  pl.BlockSpec((TT, kw), lambda b, g, t: (row(b, g, t), DN_KEY // kw + g)),
                  pl.BlockSpec((TT, 2 * kw), lambda b, g, t: (row(b, g, t), g)),
                  pl.BlockSpec((TT, 2 * kw), lambda b, g, t: (row(b, g, t), g)),
                  pl.BlockSpec((TT, LANES), lambda b, g, t: (row(b, g, t), 0)),
                  pl.BlockSpec((TT, LANES), lambda b, g, t: (row(b, g, t), 0)),
                  pl.BlockSpec((1, DN_V_HEADS, TT), lambda b, g, t: (b, 0, t)),
                  pl.BlockSpec((1, DN_HEAD), lambda b, g, t: (0, 0))],
        out_specs=pl.BlockSpec((TT, 2 * kw), lambda b, g, t: (row(b, g, t), g)),
        out_shape=jax.ShapeDtypeStruct((N, DN_VAL), BF16),
        scratch_shapes=[pltpu.VMEM((2 * KH, DN_HEAD, DN_HEAD), F32)],
        compiler_params=_cparams(("arbitrary", "arbitrary", "arbitrary")),
        name="delta_rule",
    )(qk_n, qk_n, vv, sz, beta, gc, gcT, dn_norm_w.reshape(1, DN_HEAD))


def _merge_body(a_ref, o_ref, ga_ref, gb_ref, wa_ref, wb_ref, y_ref):
    ya = jnp.dot(a_ref[...], wa_ref[...], preferred_element_type=F32)
    yb = jnp.dot(o_ref[...], wb_ref[...], preferred_element_type=F32)
    y = ga_ref[...].astype(F32) * ya + gb_ref[...].astype(F32) * yb
    y_ref[...] = y.astype(BF16)


def _merge(a_g, o_g, p_m, wa, wb, tm=512, tn=512):
    N = a_g.shape[0]
    D = wa.shape[1]
    nj = D // tn
    return pl.pallas_call(
        _merge_body,
        grid=(N // tm, nj),
        in_specs=[pl.BlockSpec((tm, a_g.shape[1]), lambda i, j: (i, 0)),
                  pl.BlockSpec((tm, o_g.shape[1]), lambda i, j: (i, 0)),
                  pl.BlockSpec((tm, tn), lambda i, j: (i, j)),
                  pl.BlockSpec((tm, tn), lambda i, j: (i, nj + j)),
                  pl.BlockSpec((wa.shape[0], tn), lambda i, j: (0, j)),
                  pl.BlockSpec((wb.shape[0], tn), lambda i, j: (0, j))],
        out_specs=pl.BlockSpec((tm, tn), lambda i, j: (i, j)),
        out_shape=jax.ShapeDtypeStruct((N, D), BF16),
        compiler_params=_cparams(("arbitrary", "arbitrary")),
        name="merge",
    )(a_g, o_g, p_m, p_m, wa, wb)


def _out_body(y_ref, w_ref, x_ref, g_ref, o_ref):
    acc = jnp.dot(y_ref[...], w_ref[...], preferred_element_type=F32)
    o_ref[...] = x_ref[...] + g_ref[0] * acc


def _outproj(y, w, x2d, gate, T, tm=512, tn=512):
    N, D = x2d.shape
    B = gate.shape[0]
    tps = T // tm
    return pl.pallas_call(
        _out_body,
        grid=(N // tm, D // tn),
        in_specs=[pl.BlockSpec((tm, y.shape[1]), lambda i, j: (i, 0)),
                  pl.BlockSpec((w.shape[0], tn), lambda i, j: (0, j)),
                  pl.BlockSpec((tm, tn), lambda i, j: (i, j)),
                  pl.BlockSpec((1, 1, tn), lambda i, j: (i // tps, 0, j))],
        out_specs=pl.BlockSpec((tm, tn), lambda i, j: (i, j)),
        out_shape=jax.ShapeDtypeStruct((N, D), F32),
        compiler_params=_cparams(("arbitrary", "arbitrary")),
        name="out_proj",
    )(y, w, x2d, gate.reshape(B, 1, D))


def _layer(l, x, c, positions, w_ada, b_ada, norm_w, w_in, q_norm_w, k_norm_w, sinks,
           conv_w, a_log, dt_bias, dn_norm_w, w_o_swa, w_o_dn, w_out):
    B, T, D = x.shape
    N = B * T
    x2d = x.reshape(N, D)

    mod = _adaln(c, w_ada, b_ada)
    shift, scale, gate = mod[:, :D], mod[:, D:2 * D], mod[:, 2 * D:]
    h = _normmod(x2d, norm_w, scale, shift, T)

    o_q, o_g = 0, SWA_Q + 2 * SWA_KV
    o_d = o_g + SWA_Q
    o_z = o_d + DN_CONV_CH
    o_b = o_z + DN_VAL
    o_a = o_b + DN_V_HEADS
    o_m = o_a + DN_V_HEADS
    assert o_a == o_b + DN_V_HEADS and o_b % LANES == 0
    w_in = jnp.swapaxes(w_in, 1, 2)
    posf = positions.astype(F32).reshape(N, 1)
    tabs = _rope_tables(posf)
    rep = LANES // SWA_HEAD_DIM
    qw = jnp.tile(q_norm_w.reshape(1, SWA_HEAD_DIM), (1, rep)) * (SWA_HEAD_DIM ** -0.5)
    kw = jnp.tile(k_norm_w.reshape(1, SWA_HEAD_DIM), (1, rep))
    q_n = _proj(h, w_in, l, o_q, SWA_Q, "qknorm", T, extras=(tabs, qw, 2 * 512 // LANES), name="proj_q")
    kv_n = _proj(h, w_in, l, o_q + SWA_Q, 2 * SWA_KV, "qknorm", T, nblk=1,
                 extras=(tabs, kw, SWA_KV // LANES), name="proj_kv")
    sag = _proj(h, w_in, l, o_g, o_d - o_g, "silu", T, name="proj_swa_gate")
    qk_n = _proj(h, w_in, l, o_d, 2 * DN_KEY, "conv", T, extras=(conv_w, 0, True), name="proj_dn_qk")
    vv = _proj(h, w_in, l, o_d + 2 * DN_KEY, DN_VAL, "conv", T, extras=(conv_w, 2 * DN_KEY, False),
               name="proj_dn_v")
    sz = _proj(h, w_in, l, o_z, o_b - o_z, "silu", T, name="proj_dn_z")
    al = jnp.zeros((1, LANES), F32).at[0, :DN_V_HEADS].set(a_log)
    dt = jnp.zeros((1, LANES), F32).at[0, :DN_V_HEADS].set(dt_bias)
    beta, gc = _proj(h, w_in, l, o_b, 2 * DN_V_HEADS, "gates", T, extras=(al, dt), name="proj_dn_gates")
    p_m = _proj(h, w_in, l, o_m, 2 * D, "sigmoid", T, name="proj_merge_gates")

    a_g = _swa(q_n, kv_n, sag, sinks, B, T)

    gcT = gc[:, :DN_V_HEADS].reshape(B, T, DN_V_HEADS).transpose(0, 2, 1)
    o_g2 = _delta(qk_n, vv, sz, beta, gc, gcT, dn_norm_w, B, T)

    y = _merge(a_g, o_g2, p_m, w_o_swa.astype(BF16), w_o_dn.astype(BF16))
    out = _outproj(y, w_out.astype(BF16), x2d, gate, T)
    return out.reshape(B, T, D)


def kernel(x, c, positions, w_ada, b_ada, norm_w, w_in, q_norm_w, k_norm_w, sinks, conv_w,
           a_log, dt_bias, dn_norm_w, w_o_swa, w_o_dn, w_out):
    depth = w_ada.shape[0]
    for l in range(depth):
        x = _layer(l, x, c, positions, w_ada[l], b_ada[l], norm_w[l], w_in, q_norm_w[l],
                   k_norm_w[l], sinks[l], conv_w[l], a_log[l], dt_bias[l], dn_norm_w[l],
                   w_o_swa[l], w_o_dn[l], w_out[l])
    return x
```

```python
import functools

import numpy as np
import jax
import jax.numpy as jnp
from jax import lax
from jax.experimental import pallas as pl
from jax.experimental.pallas import tpu as pltpu

F32 = jnp.float32
BF16 = jnp.bfloat16

D_MODEL = 2048
SWA_Q_HEADS = 32
SWA_KV_HEADS = 4
SWA_HEAD_DIM = 64
SWA_BLOCK = 128
ROPE_THETA = 500000.0
ROPE_DIM = SWA_HEAD_DIM // 4
DN_K_HEADS = 16
DN_V_HEADS = 32
DN_HEAD = 128
DN_CONV = 4
DN_CHUNK = 64
EPS = 1e-6

SWA_Q = SWA_Q_HEADS * SWA_HEAD_DIM
SWA_KV = SWA_KV_HEADS * SWA_HEAD_DIM
DN_KEY = DN_K_HEADS * DN_HEAD
DN_VAL = DN_V_HEADS * DN_HEAD
DN_CONV_CH = 2 * DN_KEY + DN_VAL

MXU_N = 256
LANES = 128
SUBLANES = 8
VMEM_LIMIT = 56 * 1024 * 1024


def _cparams(sem, flags=None):
    return pltpu.CompilerParams(dimension_semantics=sem, vmem_limit_bytes=VMEM_LIMIT, flags=flags)


def _sigmoid(x):
    return 0.5 * jnp.tanh(0.5 * x) + 0.5


def _silu(x):
    h = 0.5 * x
    return h + h * jnp.tanh(h)


def _adaln_body(c_ref, w_ref, b_ref, o_ref):
    s = _silu(c_ref[...]).astype(BF16)
    o_ref[...] = jnp.dot(s, w_ref[...].astype(BF16), preferred_element_type=F32) + b_ref[...]


def _adaln(c, w_ada, b_ada, tn=512):
    B, D = c.shape
    n_out = w_ada.shape[1]
    c8 = jnp.zeros((SUBLANES, D), F32).at[:B].set(c)
    out = pl.pallas_call(
        _adaln_body,
        grid=(n_out // tn,),
        in_specs=[pl.BlockSpec((SUBLANES, D), lambda j: (0, 0)),
                  pl.BlockSpec((D, tn), lambda j: (0, j)),
                  pl.BlockSpec((1, tn), lambda j: (0, j))],
        out_specs=pl.BlockSpec((SUBLANES, tn), lambda j: (0, j)),
        out_shape=jax.ShapeDtypeStruct((SUBLANES, n_out), F32),
        compiler_params=_cparams(("arbitrary",)),
        name="adaln_mod",
    )(c8, w_ada, b_ada.reshape(1, n_out))
    return out[:B]


def _normmod_body(x_ref, nw_ref, sc_ref, sh_ref, o_ref):
    x = x_ref[...]
    ms = jnp.mean(x * x, axis=-1, keepdims=True)
    y = x * lax.rsqrt(ms + EPS) * nw_ref[...]
    o_ref[...] = (y * (1.0 + sc_ref[0]) + sh_ref[0]).astype(BF16)


def _normmod(x2d, norm_w, scale, shift, T, tm=512):
    N, D = x2d.shape
    B = scale.shape[0]
    tps = T // tm
    return pl.pallas_call(
        _normmod_body,
        grid=(N // tm,),
        in_specs=[pl.BlockSpec((tm, D), lambda i: (i, 0)),
                  pl.BlockSpec((1, D), lambda i: (0, 0)),
                  pl.BlockSpec((1, 1, D), lambda i: (i // tps, 0, 0)),
                  pl.BlockSpec((1, 1, D), lambda i: (i // tps, 0, 0))],
        out_specs=pl.BlockSpec((tm, D), lambda i: (i, 0)),
        out_shape=jax.ShapeDtypeStruct((N, D), BF16),
        compiler_params=_cparams(("arbitrary",)),
        name="norm_mod",
    )(x2d, norm_w.reshape(1, D), scale.reshape(B, 1, D), shift.reshape(B, 1, D))


HALF = LANES // 2


NT_DIMS = (((1,), (1,)), ((), ()))


def _cast_weights(w_refs, w2_ref, wbf):
    wblk = w_refs[0].shape[0]
    off = 0 if w2_ref is None else HALF
    for b, w_ref in enumerate(w_refs):
        lo = max(b * wblk - off, 0)
        wbf[lo:(b + 1) * wblk - off, :] = w_ref[lo + off - b * wblk:, :].astype(BF16)
    if w2_ref is not None:
        n = len(w_refs) * wblk
        wbf[n - off:n, :] = w2_ref[...].astype(BF16)


def _cast_once(w_refs, w2_ref, wbf):
    @pl.when(pl.program_id(1) == 0)
    def _():
        _cast_weights(w_refs, w2_ref, wbf)


def _proj_act_body(*refs, act, shifted, sub, nblk):
    h_ref, w_refs = refs[0], refs[1:1 + nblk]
    w2_ref = refs[1 + nblk] if shifted else None
    o_ref, wbf = refs[-2:]
    _cast_once(w_refs, w2_ref, wbf)
    for r0 in range(0, h_ref.shape[0], sub):
        acc = lax.dot_general(h_ref[r0:r0 + sub, :], wbf[...], NT_DIMS, preferred_element_type=F32)
        if act == "silu":
            acc = _silu(acc)
        elif act == "sigmoid":
            acc = _sigmoid(acc)
        o_ref[r0:r0 + sub, :] = acc.astype(o_ref.dtype)


def _proj_conv_body(*refs, tm, tps, norm, sub, nblk):
    h_ref, w_refs = refs[0], refs[1:1 + nblk]
    cw_ref, o_ref, wbf, buf = refs[1 + nblk:]
    i = pl.program_id(1)
    first = (i % tps) == 0
    _cast_once(w_refs, None, wbf)

    @pl.when(first)
    def _():
        buf[0:SUBLANES, :] = jnp.zeros((SUBLANES, buf.shape[1]), F32)

    @pl.when(jnp.logical_not(first))
    def _():
        buf[0:SUBLANES, :] = buf[tm:tm + SUBLANES, :]

    cw = cw_ref[...]
    mm = lambda r0: lax.dot_general(h_ref[r0:r0 + sub, :], wbf[...], NT_DIMS, preferred_element_type=F32)
    nxt = mm(0)
    for r0 in range(0, tm, sub):
        acc = nxt
        if r0 + sub < tm:
            nxt = mm(r0 + sub)
        b0 = SUBLANES + r0
        buf[b0:b0 + sub, :] = acc
        y = acc * cw[DN_CONV - 1:DN_CONV, :]
        win = buf[b0 - SUBLANES:b0 + sub, :]
        for d in range(1, DN_CONV):
            y = y + pltpu.roll(win, d, 0)[SUBLANES:, :] * cw[DN_CONV - 1 - d:DN_CONV - d, :]
        y = _silu(y)
        for c in range(y.shape[1] // DN_HEAD):
            yc = y[:, c * DN_HEAD:(c + 1) * DN_HEAD]
            if norm:
                yc = yc * lax.rsqrt(jnp.sum(yc * yc, axis=-1, keepdims=True) + EPS)
            o_ref[r0:r0 + sub, c * DN_HEAD:(c + 1) * DN_HEAD] = yc.astype(o_ref.dtype)


def _norm_rope(x, nw, cs, s1, s2, lo):
    half = ROPE_DIM // 2
    x2 = x * x
    s_lo = jnp.sum(jnp.where(lo, x2, 0.0), axis=-1, keepdims=True)
    s_hi = jnp.sum(jnp.where(lo, 0.0, x2), axis=-1, keepdims=True)
    r = jnp.where(lo, lax.rsqrt(s_lo * (1.0 / SWA_HEAD_DIM) + EPS),
                  lax.rsqrt(s_hi * (1.0 / SWA_HEAD_DIM) + EPS))
    xn = x * r * nw
    return xn * cs + pltpu.roll(xn, half, 1) * s1 + pltpu.roll(xn, LANES - half, 1) * s2


def _proj_qknorm_body(*refs, sub, nblk, n_norm):
    h_ref, w_refs = refs[0], refs[1:1 + nblk]
    c_ref, s1_ref, s2_ref, nw_ref, o_ref, wbf = refs[1 + nblk:]
    _cast_once(w_refs, None, wbf)
    lo = lax.broadcasted_iota(jnp.int32, (1, LANES), 1) < SWA_HEAD_DIM
    nw = nw_ref[...]
    for r0 in range(0, h_ref.shape[0], sub):
        acc = lax.dot_general(h_ref[r0:r0 + sub, :], wbf[...], NT_DIMS, preferred_element_type=F32)
        cs, s1, s2 = c_ref[r0:r0 + sub, :], s1_ref[r0:r0 + sub, :], s2_ref[r0:r0 + sub, :]
        for c in range(acc.shape[1] // LANES):
            x = acc[:, c * LANES:(c + 1) * LANES]
            if c < n_norm:
                x = _norm_rope(x, nw, cs, s1, s2, lo)
            o_ref[r0:r0 + sub, c * LANES:(c + 1) * LANES] = x.astype(o_ref.dtype)


def _proj_gates_body(h_ref, w_ref, al_ref, dt_ref, beta_ref, gc_ref, wbf):
    _cast_once([w_ref], None, wbf)
    acc = lax.dot_general(h_ref[...], wbf[...], NT_DIMS, preferred_element_type=F32)
    beta_ref[...] = _sigmoid(acc)
    x = pltpu.roll(acc, LANES - DN_V_HEADS, 1) + dt_ref[...]
    sp = jnp.maximum(x, 0.0) + jnp.log(1.0 + jnp.exp(-jnp.abs(x)))
    g = -jnp.exp(al_ref[...]) * sp
    r64 = lax.broadcasted_iota(jnp.int32, g.shape, 0) & (DN_CHUNK - 1)
    s = 1
    while s < DN_CHUNK:
        g = g + jnp.where(r64 >= s, pltpu.roll(g, s, 0), 0.0)
        s *= 2
    gc_ref[...] = g


def _proj(h, w_t, layer, col0, width, mode, T, extras=(), tm=1024, wblk=512, nblk=2, sub=256,
          name="proj"):
    N, K = h.shape
    out_dtype = BF16
    tm = min(tm, T)
    sub = min(sub, tm)
    off = col0 % LANES
    base = col0 - off
    shifted = off != 0
    if mode == "gates":
        wblk, nblk = LANES, 1
    tn = wblk * nblk
    assert off in (0, HALF) and base % wblk == 0
    assert mode == "gates" or width % tn == 0
    jb = base // wblk
    grid = (max(width // tn, 1), N // tm)
    in_specs = [pl.BlockSpec((tm, K), lambda j, i: (i, 0))]
    in_specs += [pl.BlockSpec((None, wblk, K), lambda j, i, b=b: (layer, jb + j * nblk + b, 0))
                 for b in range(nblk)]
    args = [h] + [w_t] * nblk
    scratch = [pltpu.VMEM((tn, K), BF16)]
    out_spec = pl.BlockSpec((tm, tn), lambda j, i: (i, j))
    out_shape = jax.ShapeDtypeStruct((N, max(width, tn)), out_dtype)
    if shifted:
        in_specs.append(pl.BlockSpec((None, HALF, K),
                                     lambda j, i: (layer, (base + (j + 1) * tn) // HALF, 0)))
        args.append(w_t)
    if mode == "conv":
        cw, cw_col0, norm = extras
        in_specs.append(pl.BlockSpec((DN_CONV, tn), lambda j, i: (0, cw_col0 // tn + j)))
        args.append(cw)
        scratch.append(pltpu.VMEM((tm + SUBLANES, tn), F32))
        body = functools.partial(_proj_conv_body, tm=tm, tps=T // tm, norm=norm, sub=sub, nblk=nblk)
    elif mode == "qknorm":
        tabs, nw, n_norm = extras
        in_specs += [pl.BlockSpec((tm, LANES), lambda j, i: (i, 0))] * 3
        in_specs.append(pl.BlockSpec((1, LANES), lambda j, i: (0, 0)))
        args += list(tabs) + [nw]
        body = functools.partial(_proj_qknorm_body, sub=sub, nblk=nblk, n_norm=n_norm)
    elif mode == "gates":
        in_specs += [pl.BlockSpec((1, LANES), lambda j, i: (0, 0))] * 2
        args += list(extras)
        body = _proj_gates_body
        out_spec = [out_spec, out_spec]
        out_shape = [jax.ShapeDtypeStruct((N, LANES), F32)] * 2
    else:
        body = functools.partial(_proj_act_body, act=mode, shifted=shifted, sub=sub if mode else tm,
                                 nblk=nblk)
    return pl.pallas_call(
        body, grid=grid, in_specs=in_specs, out_specs=out_spec, out_shape=out_shape,
        scratch_shapes=scratch,
        compiler_params=_cparams(("arbitrary", "arbitrary")),
        name=name,
    )(*args)


def _rope_inv_freq_lanes():
    half = ROPE_DIM // 2
    inv = (np.float32(ROPE_THETA) ** (-np.arange(half, dtype=np.float32) * np.float32(2.0 / ROPE_DIM))).astype(np.float32)
    lanes = np.zeros((1, LANES), np.float32)
    for l in range(LANES):
        m = l % SWA_HEAD_DIM
        if m < ROPE_DIM:
            lanes[0, l] = inv[m % half]
    return lanes


def _rope_tab_body(pos_ref, invf_ref, c_ref, s1_ref, s2_ref):
    half = ROPE_DIM // 2
    l64 = lax.broadcasted_iota(jnp.int32, (1, LANES), 1) & (SWA_HEAD_DIM - 1)
    ang = pos_ref[...] * invf_ref[...]
    sn = jnp.sin(ang)
    c_ref[...] = jnp.cos(ang)
    s1_ref[...] = jnp.where((l64 >= half) & (l64 < ROPE_DIM), sn, 0.0)
    s2_ref[...] = jnp.where(l64 < half, -sn, 0.0)


def _rope_tables(posf, tm=1024):
    N = posf.shape[0]
    tm = min(tm, N)
    spec = pl.BlockSpec((tm, LANES), lambda i: (i, 0))
    return pl.pallas_call(
        _rope_tab_body,
        grid=(N // tm,),
        in_specs=[pl.BlockSpec((tm, 1), lambda i: (i, 0)), pl.BlockSpec((1, LANES), lambda i: (0, 0))],
        out_specs=[spec] * 3,
        out_shape=[jax.ShapeDtypeStruct((N, LANES), F32)] * 3,
        compiler_params=_cparams(("arbitrary",)),
        name="rope_tables",
    )(posf, jnp.asarray(_rope_inv_freq_lanes()))


def _swa_body(sink_ref, q_ref, kv_ref, g_ref, o_ref, kprev, vprev):
    n = pl.program_id(1)
    BLK = SWA_BLOCK
    HD = SWA_HEAD_DIM

    @pl.when(n == 0)
    def _():
        kprev[...] = jnp.zeros_like(kprev)
        vprev[...] = jnp.zeros_like(vprev)

    lo = lax.broadcasted_iota(jnp.int32, (1, LANES), 1) < HD
    kv = kv_ref[...]
    n_kc = SWA_KV // LANES
    kc = [kv[:, c * LANES:(c + 1) * LANES].astype(F32) for c in range(n_kc)]
    vT = [kv[:, SWA_KV + c * LANES:SWA_KV + (c + 1) * LANES].astype(F32).T for c in range(n_kc)]
    zrows = jnp.zeros((HD, BLK), F32)

    kband, vbandT = {}, {}
    for j in range(SWA_KV_HEADS):
        cj, b = j // 2, j % 2
        for a in range(2):
            src = kc[cj] if a == b else pltpu.roll(kc[cj], HD, 1)
            k_cur = jnp.where(lo if a == 0 else jnp.logical_not(lo), src, 0.0).astype(BF16)
            rows = vT[cj][b * HD:(b + 1) * HD]
            v_cur = jnp.concatenate([rows, zrows] if a == 0 else [zrows, rows], axis=0).astype(BF16)
            idx = 2 * j + a
            kband[(j, a)] = jnp.concatenate([kprev[idx], k_cur], axis=0)
            vbandT[(j, a)] = jnp.concatenate([vprev[idx], v_cur], axis=1)
            kprev[idx] = k_cur
            vprev[idx] = v_cur

    kj = lax.broadcasted_iota(jnp.int32, (2 * BLK, 2 * BLK), 0)
    qi = lax.broadcasted_iota(jnp.int32, (2 * BLK, 2 * BLK), 1) & (BLK - 1)
    d = kj - qi
    kmin = jnp.where(n == 0, BLK, 0)
    valid = (d >= 1) & (d <= BLK) & (kj >= kmin)
    left = lax.broadcasted_iota(jnp.int32, (1, 2 * BLK), 1) < BLK

    G = SWA_Q_HEADS // SWA_KV_HEADS
    cpk = G // 2
    items = [(j, a, j * cpk + 2 * p) for j in range(SWA_KV_HEADS) for a in range(2)
             for p in range(cpk // 2)]

    def scores(item):
        j, a, c1 = item
        rhs = jnp.concatenate([q_ref[:, c1 * LANES:(c1 + 1) * LANES],
                               q_ref[:, (c1 + 1) * LANES:(c1 + 2) * LANES]], axis=0)
        return lax.dot_general(kband[(j, a)], rhs, NT_DIMS, preferred_element_type=F32)

    acc = {}

    def consume(item, oT, inv):
        j, a, c1 = item
        oT = oT * inv
        for c, part in ((c1, oT[:, :BLK]), (c1 + 1, oT[:, BLK:])):
            if a == 0:
                acc[c] = part
            else:
                gate = g_ref[:, c * LANES:(c + 1) * LANES].astype(F32)
                o_ref[:, c * LANES:(c + 1) * LANES] = ((acc.pop(c) + part).T * gate).astype(BF16)

    s_cur = scores(items[0])
    pending = None
    for i, item in enumerate(items):
        j, a, c1 = item
        s_next = scores(items[i + 1]) if i + 1 < len(items) else None
        s = jnp.where(valid, s_cur, -jnp.inf)
        sink = jnp.where(left, sink_ref[2 * c1 + a], sink_ref[2 * c1 + 2 + a])
        m = jnp.maximum(jnp.max(s, axis=0, keepdims=True), sink)
        e = jnp.exp(s - m)
        inv = 1.0 / (jnp.sum(e, axis=0, keepdims=True) + jnp.exp(sink - m))
        oT = jnp.dot(vbandT[(j, a)], e.astype(BF16), preferred_element_type=F32)
        if pending is not None:
            consume(*pending)
        pending = (item, oT, inv)
        s_cur = s_next
    consume(*pending)


def _swa(q_n, kv_n, sag, sinks, B, T):
    N = B * T
    nb = T // SWA_BLOCK
    row = lambda b, n: b * nb + n
    return pl.pallas_call(
        _swa_body,
        grid=(B, nb),
        in_specs=[pl.BlockSpec(memory_space=pltpu.SMEM),
                  pl.BlockSpec((SWA_BLOCK, SWA_Q), lambda b, n: (row(b, n), 0)),
                  pl.BlockSpec((SWA_BLOCK, 2 * SWA_KV), lambda b, n: (row(b, n), 0)),
                  pl.BlockSpec((SWA_BLOCK, SWA_Q), lambda b, n: (row(b, n), 0))],
        out_specs=pl.BlockSpec((SWA_BLOCK, SWA_Q), lambda b, n: (row(b, n), 0)),
        out_shape=jax.ShapeDtypeStruct((N, SWA_Q), BF16),
        scratch_shapes=[pltpu.VMEM((2 * SWA_KV_HEADS, SWA_BLOCK, LANES), BF16),
                        pltpu.VMEM((2 * SWA_KV_HEADS, LANES, SWA_BLOCK), BF16)],
        compiler_params=_cparams(("arbitrary", "arbitrary")),
        name="swa_attn",
    )(sinks, q_n, kv_n, sag)


def _delta_body(q_ref, k_ref, v_ref, z_ref, bc_ref, gc_ref, gr_ref, nw_ref, o_ref, S_ref,
                *, KH, TT):
    grp = pl.program_id(1)
    t = pl.program_id(2)
    C = DN_CHUNK
    NC = TT // C
    scale = DN_HEAD ** -0.5
    NT = (((1,), (1,)), ((), ()))

    @pl.when(t == 0)
    def _():
        S_ref[...] = jnp.zeros_like(S_ref)

    NV = 2 * KH
    row = lax.broadcasted_iota(jnp.int32, (TT, TT), 0)
    col = lax.broadcasted_iota(jnp.int32, (TT, TT), 1)
    same = (row // C) == (col // C)
    ri = lax.broadcasted_iota(jnp.int32, (C, TT), 0)
    cj = lax.broadcasted_iota(jnp.int32, (C, TT), 1)
    cjc = cj // C
    cjj = cj & (C - 1)
    tril_c = ri >= cjj
    strict_c = ri > cjj
    eye_c = (ri == cjj).astype(F32)
    lane = lax.broadcasted_iota(jnp.int32, (TT, LANES), 1)
    lchunk = lax.broadcasted_iota(jnp.int32, (1, TT), 1) // C
    bcall = bc_ref[...]
    gcall = gc_ref[...]
    nw = nw_ref[...]

    def compress(x):
        out = x[(NC - 1) * C:NC * C]
        for c in range(NC - 2, -1, -1):
            out = jnp.where(cjc == c, x[c * C:(c + 1) * C], out)
        return out

    def by_row(colvec):
        out = jnp.broadcast_to(colvec[(NC - 1) * C:NC * C], (C, TT))
        for c in range(NC - 2, -1, -1):
            out = jnp.where(cjc == c, colvec[c * C:(c + 1) * C], out)
        return out

    def bd(xc):
        return jnp.where(same, jnp.concatenate([xc] * NC, axis=0), 0.0).astype(BF16)

    kf, qf, kT, kkc, qkc = [], [], [], [], []
    for kh in range(KH):
        q = q_ref[:, kh * DN_HEAD:(kh + 1) * DN_HEAD]
        k = k_ref[:, kh * DN_HEAD:(kh + 1) * DN_HEAD]
        kkc.append(compress(lax.dot_general(k, k, NT, preferred_element_type=F32)))
        qkc.append(compress(lax.dot_general(q, k, NT, preferred_element_type=F32)))
        kf.append(k.astype(F32))
        qf.append(q.astype(F32))
        kT.append(kf[-1].T)

    bcol, gcol, grow, Lc, qkm, P = [], [], [], [], [], []
    for hh in range(NV):
        hglob = grp * NV + hh
        sel = lane == hglob
        bcol.append(jnp.sum(jnp.where(sel, bcall, 0.0), axis=-1, keepdims=True))
        gcol.append(jnp.sum(jnp.where(sel, gcall, 0.0), axis=-1, keepdims=True))
        grow.append(gr_ref[0, pl.ds(hglob, 1), :])
        dec = jnp.exp(jnp.where(tril_c, by_row(gcol[hh]) - grow[hh], -jnp.inf))
        Lc.append(jnp.where(strict_c, by_row(bcol[hh]) * kkc[hh // 2] * dec, 0.0))
        qkm.append(qkc[hh // 2] * dec * scale)
        P.append(eye_c - Lc[hh])

    Lp = [jnp.dot(Lc[hh].astype(BF16), bd(Lc[hh]), preferred_element_type=F32) for hh in range(NV)]
    step = 4
    while step < C:
        res = [jnp.dot(jnp.concatenate([Lp[hh], P[hh]], axis=0).astype(BF16), bd(Lp[hh]),
                       preferred_element_type=F32) for hh in range(NV)]
        Lp = [r[:C] for r in res]
        P = [P[hh] + res[hh][C:] for hh in range(NV)]
        step *= 2
    P = [P[hh] + jnp.dot(P[hh].astype(BF16), bd(Lp[hh]), preferred_element_type=F32) for hh in range(NV)]

    uw, qd, kdT, eg_last = [], [], [], []
    for hh in range(NV):
        eg = jnp.exp(gcol[hh])
        vf = v_ref[:, hh * DN_HEAD:(hh + 1) * DN_HEAD].astype(F32)
        rhs = jnp.concatenate([vf * bcol[hh], kf[hh // 2] * (bcol[hh] * eg)], axis=1).astype(BF16)
        uw.append(jnp.dot(bd(P[hh]), rhs, preferred_element_type=F32))
        qd.append((qf[hh // 2] * (eg * scale)).astype(BF16))
        glrow = jnp.zeros((1, TT), F32)
        for c in range(NC):
            glrow = jnp.where(lchunk == c, gcol[hh][c * C + C - 1:c * C + C, :], glrow)
        kdT.append(kT[hh // 2] * jnp.exp(glrow - grow[hh]))
        eg_last.append([eg[c * C + C - 1:c * C + C, :] for c in range(NC)])

    S = [S_ref[hh] for hh in range(NV)]
    vn = [[] for _ in range(NV)]
    o_inter = [[] for _ in range(NV)]
    for c in range(NC):
        r0, r1 = c * C, (c + 1) * C
        for hh in range(NV):
            w_c = uw[hh][r0:r1, DN_HEAD:].astype(BF16)
            lhs = jnp.concatenate([w_c, qd[hh][r0:r1]], axis=0)
            res = jnp.dot(lhs, S[hh].astype(BF16), preferred_element_type=F32)
            vn_c = (uw[hh][r0:r1, :DN_HEAD] - res[:C]).astype(BF16)
            vn[hh].append(vn_c)
            o_inter[hh].append(res[C:])
            kd_c = kdT[hh][:, r0:r1].astype(BF16)
            S[hh] = S[hh] * eg_last[hh][c] + jnp.dot(kd_c, vn_c, preferred_element_type=F32)

    for hh in range(NV):
        S_ref[hh] = S[hh]
        vn_all = jnp.concatenate(vn[hh], axis=0)
        o = jnp.concatenate(o_inter[hh], axis=0) + jnp.dot(bd(qkm[hh]), vn_all, preferred_element_type=F32)
        ms = jnp.mean(o * o, axis=-1, keepdims=True)
        z = z_ref[:, hh * DN_HEAD:(hh + 1) * DN_HEAD].astype(F32)
        o_ref[:, hh * DN_HEAD:(hh + 1) * DN_HEAD] = (o * lax.rsqrt(ms + EPS) * nw * z).astype(BF16)


def _delta(qk_n, vv, sz, beta, gc, gcT, dn_norm_w, B, T, KH=8, TT=256):
    N = B * T
    nt = T // TT
    ng = DN_K_HEADS // KH
    row = lambda b, g, t: b * nt + t
    kw = KH * DN_HEAD
    return pl.pallas_call(
        functools.partial(_delta_body, KH=KH, TT=TT),
        grid=(B, ng, nt),
        in_specs=[pl.BlockSpec((TT, kw), lambda b, g, t: (row(b, g, t), g)),
                  pl.BlockSpec((TT, kw), lambda b, g, t: (row(b, g, t), DN_KEY // kw + g)),
                  pl.BlockSpec((TT, 2 * kw), lambda b, g, t: (row(b, g, t), g)),
                  pl.BlockSpec((TT, 2 * kw), lambda b, g, t: (row(b, g, t), g)),
                  pl.BlockSpec((TT, LANES), lambda b, g, t: (row(b, g, t), 0)),
                  pl.BlockSpec((TT, LANES), lambda b, g, t: (row(b, g, t), 0)),
                  pl.BlockSpec((1, DN_V_HEADS, TT), lambda b, g, t: (b, 0, t)),
                  pl.BlockSpec((1, DN_HEAD), lambda b, g, t: (0, 0))],
        out_specs=pl.BlockSpec((TT, 2 * kw), lambda b, g, t: (row(b, g, t), g)),
        out_shape=jax.ShapeDtypeStruct((N, DN_VAL), BF16),
        scratch_shapes=[pltpu.VMEM((2 * KH, DN_HEAD, DN_HEAD), F32)],
        compiler_params=_cparams(("arbitrary", "arbitrary", "arbitrary")),
        name="delta_rule",
    )(qk_n, qk_n, vv, sz, beta, gc, gcT, dn_norm_w.reshape(1, DN_HEAD))


def _merge_body(a_ref, o_ref, ga_ref, gb_ref, wa_ref, wb_ref, y_ref):
    ya = jnp.dot(a_ref[...], wa_ref[...], preferred_element_type=F32)
    yb = jnp.dot(o_ref[...], wb_ref[...], preferred_element_type=F32)
    y = ga_ref[...].astype(F32) * ya + gb_ref[...].astype(F32) * yb
    y_ref[...] = y.astype(BF16)


def _merge(a_g, o_g, p_m, wa, wb, tm=512, tn=512):
    N = a_g.shape[0]
    D = wa.shape[1]
    nj = D // tn
    return pl.pallas_call(
        _merge_body,
        grid=(N // tm, nj),
        in_specs=[pl.BlockSpec((tm, a_g.shape[1]), lambda i, j: (i, 0)),
                  pl.BlockSpec((tm, o_g.shape[1]), lambda i, j: (i, 0)),
                  pl.BlockSpec((tm, tn), lambda i, j: (i, j)),
                  pl.BlockSpec((tm, tn), lambda i, j: (i, nj + j)),
                  pl.BlockSpec((wa.shape[0], tn), lambda i, j: (0, j)),
                  pl.BlockSpec((wb.shape[0], tn), lambda i, j: (0, j))],
        out_specs=pl.BlockSpec((tm, tn), lambda i, j: (i, j)),
        out_shape=jax.ShapeDtypeStruct((N, D), BF16),
        compiler_params=_cparams(("arbitrary", "arbitrary")),
        name="merge",
    )(a_g, o_g, p_m, p_m, wa, wb)


def _out_body(y_ref, w_ref, x_ref, g_ref, o_ref):
    acc = jnp.dot(y_ref[...], w_ref[...], preferred_element_type=F32)
    o_ref[...] = x_ref[...] + g_ref[0] * acc


def _outproj(y, w, x2d, gate, T, tm=512, tn=512):
    N, D = x2d.shape
    B = gate.shape[0]
    tps = T // tm
    return pl.pallas_call(
        _out_body,
        grid=(N // tm, D // tn),
        in_specs=[pl.BlockSpec((tm, y.shape[1]), lambda i, j: (i, 0)),
                  pl.BlockSpec((w.shape[0], tn), lambda i, j: (0, j)),
                  pl.BlockSpec((tm, tn), lambda i, j: (i, j)),
                  pl.BlockSpec((1, 1, tn), lambda i, j: (i // tps, 0, j))],
        out_specs=pl.BlockSpec((tm, tn), lambda i, j: (i, j)),
        out_shape=jax.ShapeDtypeStruct((N, D), F32),
        compiler_params=_cparams(("arbitrary", "arbitrary")),
        name="out_proj",
    )(y, w, x2d, gate.reshape(B, 1, D))


def _merge_out_body(a_ref, o_ref, ga_ref, gb_ref, wa_ref, wb_ref, wo_ref, x_ref, g_ref, out_ref):
    @pl.when(pl.program_id(1) == 0)
    def _():
        out_ref[...] = x_ref[...]

    ya = jnp.dot(a_ref[...], wa_ref[...], preferred_element_type=F32)
    yb = jnp.dot(o_ref[...], wb_ref[...], preferred_element_type=F32)
    y = (ga_ref[...].astype(F32) * ya + gb_ref[...].astype(F32) * yb).astype(BF16)
    out_ref[...] += g_ref[0] * jnp.dot(y, wo_ref[...], preferred_element_type=F32)


def _merge_out(a_g, o_g, p_m, wa, wb, wo, x2d, gate, T, tm=512, tn=512):
    N, D = x2d.shape
    B = gate.shape[0]
    nj = D // tn
    tps = T // tm
    return pl.pallas_call(
        _merge_out_body,
        grid=(N // tm, nj),
        in_specs=[pl.BlockSpec((tm, a_g.shape[1]), lambda i, j: (i, 0)),
                  pl.BlockSpec((tm, o_g.shape[1]), lambda i, j: (i, 0)),
                  pl.BlockSpec((tm, tn), lambda i, j: (i, j)),
                  pl.BlockSpec((tm, tn), lambda i, j: (i, nj + j)),
                  pl.BlockSpec((wa.shape[0], tn), lambda i, j: (0, j)),
                  pl.BlockSpec((wb.shape[0], tn), lambda i, j: (0, j)),
                  pl.BlockSpec((tn, D), lambda i, j: (j, 0)),
                  pl.BlockSpec((tm, D), lambda i, j: (i, 0)),
                  pl.BlockSpec((1, 1, D), lambda i, j: (i // tps, 0, 0))],
        out_specs=pl.BlockSpec((tm, D), lambda i, j: (i, 0)),
        out_shape=jax.ShapeDtypeStruct((N, D), F32),
        compiler_params=pltpu.CompilerParams(dimension_semantics=("arbitrary", "arbitrary"),
                                             vmem_limit_bytes=60 * 1024 * 1024),
        name="merge_out",
    )(a_g, o_g, p_m, p_m, wa, wb, wo, x2d, gate.reshape(B, 1, D))


def _layer(l, x, c, positions, w_ada, b_ada, norm_w, w_in, q_norm_w, k_norm_w, sinks,
           conv_w, a_log, dt_bias, dn_norm_w, w_o_swa, w_o_dn, w_out):
    B, T, D = x.shape
    N = B * T
    x2d = x.reshape(N, D)

    mod = _adaln(c, w_ada, b_ada)
    shift, scale, gate = mod[:, :D], mod[:, D:2 * D], mod[:, 2 * D:]
    h = _normmod(x2d, norm_w, scale, shift, T)

    o_q, o_g = 0, SWA_Q + 2 * SWA_KV
    o_d = o_g + SWA_Q
    o_z = o_d + DN_CONV_CH
    o_b = o_z + DN_VAL
    o_a = o_b + DN_V_HEADS
    o_m = o_a + DN_V_HEADS
    assert o_a == o_b + DN_V_HEADS and o_b % LANES == 0
    w_in = jnp.swapaxes(w_in, 1, 2)
    posf = positions.astype(F32).reshape(N, 1)
    tabs = _rope_tables(posf)
    rep = LANES // SWA_HEAD_DIM
    qw = jnp.tile(q_norm_w.reshape(1, SWA_HEAD_DIM), (1, rep)) * (SWA_HEAD_DIM ** -0.5)
    kw = jnp.tile(k_norm_w.reshape(1, SWA_HEAD_DIM), (1, rep))
    q_n = _proj(h, w_in, l, o_q, SWA_Q, "qknorm", T, extras=(tabs, qw, 2 * 512 // LANES), name="proj_q")
    kv_n = _proj(h, w_in, l, o_q + SWA_Q, 2 * SWA_KV, "qknorm", T, nblk=1,
                 extras=(tabs, kw, SWA_KV // LANES), name="proj_kv")
    sag = _proj(h, w_in, l, o_g, o_d - o_g, "silu", T, name="proj_swa_gate")
    qk_n = _proj(h, w_in, l, o_d, 2 * DN_KEY, "conv", T, extras=(conv_w, 0, True), name="proj_dn_qk")
    vv = _proj(h, w_in, l, o_d + 2 * DN_KEY, DN_VAL, "conv", T, extras=(conv_w, 2 * DN_KEY, False),
               name="proj_dn_v")
    sz = _proj(h, w_in, l, o_z, o_b - o_z, "silu", T, name="proj_dn_z")
    al = jnp.zeros((1, LANES), F32).at[0, :DN_V_HEADS].set(a_log)
    dt = jnp.zeros((1, LANES), F32).at[0, :DN_V_HEADS].set(dt_bias)
    beta, gc = _proj(h, w_in, l, o_b, 2 * DN_V_HEADS, "gates", T, extras=(al, dt), name="proj_dn_gates")
    p_m = _proj(h, w_in, l, o_m, 2 * D, "sigmoid", T, name="proj_merge_gates")

    a_g = _swa(q_n, kv_n, sag, sinks, B, T)

    gcT = gc[:, :DN_V_HEADS].reshape(B, T, DN_V_HEADS).transpose(0, 2, 1)
    o_g2 = _delta(qk_n, vv, sz, beta, gc, gcT, dn_norm_w, B, T)

    out = _merge_out(a_g, o_g2, p_m, w_o_swa.astype(BF16), w_o_dn.astype(BF16), w_out.astype(BF16),
                     x2d, gate, T)
    return out.reshape(B, T, D)


def kernel(x, c, positions, w_ada, b_ada, norm_w, w_in, q_norm_w, k_norm_w, sinks, conv_w,
           a_log, dt_bias, dn_norm_w, w_o_swa, w_o_dn, w_out):
    depth = w_ada.shape[0]
    for l in range(depth):
        x = _layer(l, x, c, positions, w_ada[l], b_ada[l], norm_w[l], w_in, q_norm_w[l],
                   k_norm_w[l], sinks[l], conv_w[l], a_log[l], dt_bias[l], dn_norm_w[l],
                   w_o_swa[l], w_o_dn[l], w_out[l])
    return x
```

```python
import functools

import numpy as np
import jax
import jax.numpy as jnp
from jax import lax
from jax.experimental import pallas as pl
from jax.experimental.pallas import tpu as pltpu

F32 = jnp.float32
BF16 = jnp.bfloat16

D_MODEL = 2048
SWA_Q_HEADS = 32
SWA_KV_HEADS = 4
SWA_HEAD_DIM = 64
SWA_BLOCK = 128
ROPE_THETA = 500000.0
ROPE_DIM = SWA_HEAD_DIM // 4
DN_K_HEADS = 16
DN_V_HEADS = 32
DN_HEAD = 128
DN_CONV = 4
DN_CHUNK = 64
EPS = 1e-6

SWA_Q = SWA_Q_HEADS * SWA_HEAD_DIM
SWA_KV = SWA_KV_HEADS * SWA_HEAD_DIM
DN_KEY = DN_K_HEADS * DN_HEAD
DN_VAL = DN_V_HEADS * DN_HEAD
DN_CONV_CH = 2 * DN_KEY + DN_VAL

MXU_N = 256
LANES = 128
SUBLANES = 8
VMEM_LIMIT = 56 * 1024 * 1024


def _cparams(sem, flags=None):
    return pltpu.CompilerParams(dimension_semantics=sem, vmem_limit_bytes=VMEM_LIMIT, flags=flags)


def _sigmoid(x):
    return 0.5 * jnp.tanh(0.5 * x) + 0.5


def _silu(x):
    h = 0.5 * x
    return h + h * jnp.tanh(h)


def _adaln_body(c_ref, w_ref, b_ref, o_ref):
    s = _silu(c_ref[...]).astype(BF16)
    o_ref[...] = jnp.dot(s, w_ref[...].astype(BF16), preferred_element_type=F32) + b_ref[...]


def _adaln(c, w_ada, b_ada, tn=512):
    B, D = c.shape
    n_out = w_ada.shape[1]
    c8 = jnp.zeros((SUBLANES, D), F32).at[:B].set(c)
    out = pl.pallas_call(
        _adaln_body,
        grid=(n_out // tn,),
        in_specs=[pl.BlockSpec((SUBLANES, D), lambda j: (0, 0)),
                  pl.BlockSpec((D, tn), lambda j: (0, j)),
                  pl.BlockSpec((1, tn), lambda j: (0, j))],
        out_specs=pl.BlockSpec((SUBLANES, tn), lambda j: (0, j)),
        out_shape=jax.ShapeDtypeStruct((SUBLANES, n_out), F32),
        compiler_params=_cparams(("arbitrary",)),
        name="adaln_mod",
    )(c8, w_ada, b_ada.reshape(1, n_out))
    return out[:B]


def _normmod_body(x_ref, nw_ref, sc_ref, sh_ref, o_ref):
    x = x_ref[...]
    ms = jnp.mean(x * x, axis=-1, keepdims=True)
    y = x * lax.rsqrt(ms + EPS) * nw_ref[...]
    o_ref[...] = (y * (1.0 + sc_ref[0]) + sh_ref[0]).astype(BF16)


def _normmod(x2d, norm_w, scale, shift, T, tm=512):
    N, D = x2d.shape
    B = scale.shape[0]
    tps = T // tm
    return pl.pallas_call(
        _normmod_body,
        grid=(N // tm,),
        in_specs=[pl.BlockSpec((tm, D), lambda i: (i, 0)),
                  pl.BlockSpec((1, D), lambda i: (0, 0)),
                  pl.BlockSpec((1, 1, D), lambda i: (i // tps, 0, 0)),
                  pl.BlockSpec((1, 1, D), lambda i: (i // tps, 0, 0))],
        out_specs=pl.BlockSpec((tm, D), lambda i: (i, 0)),
        out_shape=jax.ShapeDtypeStruct((N, D), BF16),
        compiler_params=_cparams(("arbitrary",)),
        name="norm_mod",
    )(x2d, norm_w.reshape(1, D), scale.reshape(B, 1, D), shift.reshape(B, 1, D))


HALF = LANES // 2


NT_DIMS = (((1,), (1,)), ((), ()))


def _cast_weights(w_refs, w2_ref, wbf):
    wblk = w_refs[0].shape[0]
    off = 0 if w2_ref is None else HALF
    for b, w_ref in enumerate(w_refs):
        lo = max(b * wblk - off, 0)
        wbf[lo:(b + 1) * wblk - off, :] = w_ref[lo + off - b * wblk:, :].astype(BF16)
    if w2_ref is not None:
        n = len(w_refs) * wblk
        wbf[n - off:n, :] = w2_ref[...].astype(BF16)


def _cast_once(w_refs, w2_ref, wbf):
    @pl.when(pl.program_id(1) == 0)
    def _():
        _cast_weights(w_refs, w2_ref, wbf)


def _proj_act_body(*refs, act, shifted, sub, nblk):
    h_ref, w_refs = refs[0], refs[1:1 + nblk]
    w2_ref = refs[1 + nblk] if shifted else None
    o_ref, wbf = refs[-2:]
    _cast_once(w_refs, w2_ref, wbf)
    for r0 in range(0, h_ref.shape[0], sub):
        acc = lax.dot_general(h_ref[r0:r0 + sub, :], wbf[...], NT_DIMS, preferred_element_type=F32)
        if act == "silu":
            acc = _silu(acc)
        elif act == "sigmoid":
            acc = _sigmoid(acc)
        o_ref[r0:r0 + sub, :] = acc.astype(o_ref.dtype)


def _proj_conv_body(*refs, tm, tps, norm, sub, nblk):
    h_ref, w_refs = refs[0], refs[1:1 + nblk]
    cw_ref, o_ref, wbf, buf = refs[1 + nblk:]
    i = pl.program_id(1)
    first = (i % tps) == 0
    _cast_once(w_refs, None, wbf)

    @pl.when(first)
    def _():
        buf[0:SUBLANES, :] = jnp.zeros((SUBLANES, buf.shape[1]), F32)

    @pl.when(jnp.logical_not(first))
    def _():
        buf[0:SUBLANES, :] = buf[tm:tm + SUBLANES, :]

    cw = cw_ref[...]
    mm = lambda r0: lax.dot_general(h_ref[r0:r0 + sub, :], wbf[...], NT_DIMS, preferred_element_type=F32)
    nxt = mm(0)
    for r0 in range(0, tm, sub):
        acc = nxt
        if r0 + sub < tm:
            nxt = mm(r0 + sub)
        b0 = SUBLANES + r0
        buf[b0:b0 + sub, :] = acc
        win = buf[b0 - SUBLANES:b0 + sub, :]
        y = win[SUBLANES:, :] * cw[DN_CONV - 1:DN_CONV, :]
        for d in range(1, DN_CONV):
            y = y + pltpu.roll(win, d, 0)[SUBLANES:, :] * cw[DN_CONV - 1 - d:DN_CONV - d, :]
        y = _silu(y)
        for c in range(y.shape[1] // DN_HEAD):
            yc = y[:, c * DN_HEAD:(c + 1) * DN_HEAD]
            if norm:
                yc = yc * lax.rsqrt(jnp.sum(yc * yc, axis=-1, keepdims=True) + EPS)
            o_ref[r0:r0 + sub, c * DN_HEAD:(c + 1) * DN_HEAD] = yc.astype(o_ref.dtype)


def _norm_rope(x, nw, cs, s1, s2, lo):
    half = ROPE_DIM // 2
    x2 = x * x
    s_lo = jnp.sum(jnp.where(lo, x2, 0.0), axis=-1, keepdims=True)
    s_hi = jnp.sum(jnp.where(lo, 0.0, x2), axis=-1, keepdims=True)
    r = jnp.where(lo, lax.rsqrt(s_lo * (1.0 / SWA_HEAD_DIM) + EPS),
                  lax.rsqrt(s_hi * (1.0 / SWA_HEAD_DIM) + EPS))
    xn = x * r * nw
    return xn * cs + pltpu.roll(xn, half, 1) * s1 + pltpu.roll(xn, LANES - half, 1) * s2


def _proj_qknorm_body(*refs, sub, nblk, n_norm):
    h_ref, w_refs = refs[0], refs[1:1 + nblk]
    c_ref, s1_ref, s2_ref, nw_ref, o_ref, wbf = refs[1 + nblk:]
    _cast_once(w_refs, None, wbf)
    lo = lax.broadcasted_iota(jnp.int32, (1, LANES), 1) < SWA_HEAD_DIM
    nw = nw_ref[...]
    for r0 in range(0, h_ref.shape[0], sub):
        acc = lax.dot_general(h_ref[r0:r0 + sub, :], wbf[...], NT_DIMS, preferred_element_type=F32)
        cs, s1, s2 = c_ref[r0:r0 + sub, :], s1_ref[r0:r0 + sub, :], s2_ref[r0:r0 + sub, :]
        for c in range(acc.shape[1] // LANES):
            x = acc[:, c * LANES:(c + 1) * LANES]
            if c < n_norm:
                x = _norm_rope(x, nw, cs, s1, s2, lo)
            o_ref[r0:r0 + sub, c * LANES:(c + 1) * LANES] = x.astype(o_ref.dtype)


def _proj_gates_body(h_ref, w_ref, al_ref, dt_ref, beta_ref, gc_ref, wbf):
    _cast_once([w_ref], None, wbf)
    acc = lax.dot_general(h_ref[...], wbf[...], NT_DIMS, preferred_element_type=F32)
    beta_ref[...] = _sigmoid(acc)
    x = pltpu.roll(acc, LANES - DN_V_HEADS, 1) + dt_ref[...]
    sp = jnp.maximum(x, 0.0) + jnp.log(1.0 + jnp.exp(-jnp.abs(x)))
    g = -jnp.exp(al_ref[...]) * sp
    r64 = lax.broadcasted_iota(jnp.int32, g.shape, 0) & (DN_CHUNK - 1)
    s = 1
    while s < DN_CHUNK:
        g = g + jnp.where(r64 >= s, pltpu.roll(g, s, 0), 0.0)
        s *= 2
    gc_ref[...] = g


def _proj(h, w_t, layer, col0, width, mode, T, extras=(), tm=1024, wblk=512, nblk=2, sub=256,
          name="proj"):
    N, K = h.shape
    out_dtype = BF16
    tm = min(tm, T)
    sub = min(sub, tm)
    off = col0 % LANES
    base = col0 - off
    shifted = off != 0
    if mode == "gates":
        wblk, nblk = LANES, 1
    tn = wblk * nblk
    assert off in (0, HALF) and base % wblk == 0
    assert mode == "gates" or width % tn == 0
    jb = base // wblk
    grid = (max(width // tn, 1), N // tm)
    in_specs = [pl.BlockSpec((tm, K), lambda j, i: (i, 0))]
    in_specs += [pl.BlockSpec((None, wblk, K), lambda j, i, b=b: (layer, jb + j * nblk + b, 0))
                 for b in range(nblk)]
    args = [h] + [w_t] * nblk
    scratch = [pltpu.VMEM((tn, K), BF16)]
    out_spec = pl.BlockSpec((tm, tn), lambda j, i: (i, j))
    out_shape = jax.ShapeDtypeStruct((N, max(width, tn)), out_dtype)
    if shifted:
        in_specs.append(pl.BlockSpec((None, HALF, K),
                                     lambda j, i: (layer, (base + (j + 1) * tn) // HALF, 0)))
        args.append(w_t)
    if mode == "conv":
        cw, cw_col0, norm = extras
        in_specs.append(pl.BlockSpec((DN_CONV, tn), lambda j, i: (0, cw_col0 // tn + j)))
        args.append(cw)
        scratch.append(pltpu.VMEM((tm + SUBLANES, tn), F32))
        body = functools.partial(_proj_conv_body, tm=tm, tps=T // tm, norm=norm, sub=sub, nblk=nblk)
    elif mode == "qknorm":
        tabs, nw, n_norm = extras
        in_specs += [pl.BlockSpec((tm, LANES), lambda j, i: (i, 0))] * 3
        in_specs.append(pl.BlockSpec((1, LANES), lambda j, i: (0, 0)))
        args += list(tabs) + [nw]
        body = functools.partial(_proj_qknorm_body, sub=sub, nblk=nblk, n_norm=n_norm)
    elif mode == "gates":
        in_specs += [pl.BlockSpec((1, LANES), lambda j, i: (0, 0))] * 2
        args += list(extras)
        body = _proj_gates_body
        out_spec = [out_spec, out_spec]
        out_shape = [jax.ShapeDtypeStruct((N, LANES), F32)] * 2
    else:
        body = functools.partial(_proj_act_body, act=mode, shifted=shifted, sub=sub if mode else tm,
                                 nblk=nblk)
    return pl.pallas_call(
        body, grid=grid, in_specs=in_specs, out_specs=out_spec, out_shape=out_shape,
        scratch_shapes=scratch,
        compiler_params=_cparams(("arbitrary", "arbitrary")),
        name=name,
    )(*args)


def _rope_inv_freq_lanes():
    half = ROPE_DIM // 2
    inv = (np.float32(ROPE_THETA) ** (-np.arange(half, dtype=np.float32) * np.float32(2.0 / ROPE_DIM))).astype(np.float32)
    lanes = np.zeros((1, LANES), np.float32)
    for l in range(LANES):
        m = l % SWA_HEAD_DIM
        if m < ROPE_DIM:
            lanes[0, l] = inv[m % half]
    return lanes


def _rope_tab_body(pos_ref, invf_ref, c_ref, s1_ref, s2_ref):
    half = ROPE_DIM // 2
    l64 = lax.broadcasted_iota(jnp.int32, (1, LANES), 1) & (SWA_HEAD_DIM - 1)
    ang = pos_ref[...] * invf_ref[...]
    sn = jnp.sin(ang)
    c_ref[...] = jnp.cos(ang)
    s1_ref[...] = jnp.where((l64 >= half) & (l64 < ROPE_DIM), sn, 0.0)
    s2_ref[...] = jnp.where(l64 < half, -sn, 0.0)


def _rope_tables(posf, tm=1024):
    N = posf.shape[0]
    tm = min(tm, N)
    spec = pl.BlockSpec((tm, LANES), lambda i: (i, 0))
    return pl.pallas_call(
        _rope_tab_body,
        grid=(N // tm,),
        in_specs=[pl.BlockSpec((tm, 1), lambda i: (i, 0)), pl.BlockSpec((1, LANES), lambda i: (0, 0))],
        out_specs=[spec] * 3,
        out_shape=[jax.ShapeDtypeStruct((N, LANES), F32)] * 3,
        compiler_params=_cparams(("arbitrary",)),
        name="rope_tables",
    )(posf, jnp.asarray(_rope_inv_freq_lanes()))


def _swa_body(sink_ref, q_ref, kv_ref, g_ref, o_ref, kprev, vprev):
    n = pl.program_id(1)
    BLK = SWA_BLOCK
    HD = SWA_HEAD_DIM

    @pl.when(n == 0)
    def _():
        kprev[...] = jnp.zeros_like(kprev)
        vprev[...] = jnp.zeros_like(vprev)

    lo = lax.broadcasted_iota(jnp.int32, (1, LANES), 1) < HD
    kv = kv_ref[...]
    n_kc = SWA_KV // LANES
    kc = [kv[:, c * LANES:(c + 1) * LANES].astype(F32) for c in range(n_kc)]
    vT = [kv[:, SWA_KV + c * LANES:SWA_KV + (c + 1) * LANES].astype(F32).T for c in range(n_kc)]
    zrows = jnp.zeros((HD, BLK), F32)

    kband, vbandT = {}, {}
    for j in range(SWA_KV_HEADS):
        cj, b = j // 2, j % 2
        for a in range(2):
            src = kc[cj] if a == b else pltpu.roll(kc[cj], HD, 1)
            k_cur = jnp.where(lo if a == 0 else jnp.logical_not(lo), src, 0.0).astype(BF16)
            rows = vT[cj][b * HD:(b + 1) * HD]
            v_cur = jnp.concatenate([rows, zrows] if a == 0 else [zrows, rows], axis=0).astype(BF16)
            idx = 2 * j + a
            kband[(j, a)] = jnp.concatenate([kprev[idx], k_cur], axis=0)
            vbandT[(j, a)] = jnp.concatenate([vprev[idx], v_cur], axis=1)
            kprev[idx] = k_cur
            vprev[idx] = v_cur

    kj = lax.broadcasted_iota(jnp.int32, (2 * BLK, 2 * BLK), 0)
    qi = lax.broadcasted_iota(jnp.int32, (2 * BLK, 2 * BLK), 1) & (BLK - 1)
    d = kj - qi
    kmin = jnp.where(n == 0, BLK, 0)
    valid = (d >= 1) & (d <= BLK) & (kj >= kmin)
    left = lax.broadcasted_iota(jnp.int32, (1, 2 * BLK), 1) < BLK

    G = SWA_Q_HEADS // SWA_KV_HEADS
    cpk = G // 2
    items = [(j, a, j * cpk + 2 * p) for j in range(SWA_KV_HEADS) for a in range(2)
             for p in range(cpk // 2)]

    def scores(item):
        j, a, c1 = item
        rhs = jnp.concatenate([q_ref[:, c1 * LANES:(c1 + 1) * LANES],
                               q_ref[:, (c1 + 1) * LANES:(c1 + 2) * LANES]], axis=0)
        return lax.dot_general(kband[(j, a)], rhs, NT_DIMS, preferred_element_type=F32)

    acc = {}

    def consume(item, oT, inv):
        j, a, c1 = item
        oT = oT * inv
        for c, part in ((c1, oT[:, :BLK]), (c1 + 1, oT[:, BLK:])):
            if a == 0:
                acc[c] = part
            else:
                gate = g_ref[:, c * LANES:(c + 1) * LANES].astype(F32)
                o_ref[:, c * LANES:(c + 1) * LANES] = ((acc.pop(c) + part).T * gate).astype(BF16)

    s_cur = scores(items[0])
    pending = None
    for i, item in enumerate(items):
        j, a, c1 = item
        s_next = scores(items[i + 1]) if i + 1 < len(items) else None
        s = jnp.where(valid, s_cur, -jnp.inf)
        sink = jnp.where(left, sink_ref[2 * c1 + a], sink_ref[2 * c1 + 2 + a])
        m = jnp.maximum(jnp.max(s, axis=0, keepdims=True), sink)
        e = jnp.exp(s - m)
        inv = 1.0 / (jnp.sum(e, axis=0, keepdims=True) + jnp.exp(sink - m))
        oT = jnp.dot(vbandT[(j, a)], e.astype(BF16), preferred_element_type=F32)
        if pending is not None:
            consume(*pending)
        pending = (item, oT, inv)
        s_cur = s_next
    consume(*pending)


def _swa(q_n, kv_n, sag, sinks, B, T):
    N = B * T
    nb = T // SWA_BLOCK
    row = lambda b, n: b * nb + n
    return pl.pallas_call(
        _swa_body,
        grid=(B, nb),
        in_specs=[pl.BlockSpec(memory_space=pltpu.SMEM),
                  pl.BlockSpec((SWA_BLOCK, SWA_Q), lambda b, n: (row(b, n), 0)),
                  pl.BlockSpec((SWA_BLOCK, 2 * SWA_KV), lambda b, n: (row(b, n), 0)),
                  pl.BlockSpec((SWA_BLOCK, SWA_Q), lambda b, n: (row(b, n), 0))],
        out_specs=pl.BlockSpec((SWA_BLOCK, SWA_Q), lambda b, n: (row(b, n), 0)),
        out_shape=jax.ShapeDtypeStruct((N, SWA_Q), BF16),
        scratch_shapes=[pltpu.VMEM((2 * SWA_KV_HEADS, SWA_BLOCK, LANES), BF16),
                        pltpu.VMEM((2 * SWA_KV_HEADS, LANES, SWA_BLOCK), BF16)],
        compiler_params=_cparams(("arbitrary", "arbitrary")),
        name="swa_attn",
    )(sinks, q_n, kv_n, sag)


def _delta_body(q_ref, k_ref, v_ref, z_ref, bc_ref, gc_ref, gr_ref, nw_ref, o_ref, S_ref,
                *, KH, TT):
    grp = pl.program_id(1)
    t = pl.program_id(2)
    C = DN_CHUNK
    NC = TT // C
    scale = DN_HEAD ** -0.5
    NT = (((1,), (1,)), ((), ()))

    @pl.when(t == 0)
    def _():
        S_ref[...] = jnp.zeros_like(S_ref)

    NV = 2 * KH
    row = lax.broadcasted_iota(jnp.int32, (TT, TT), 0)
    col = lax.broadcasted_iota(jnp.int32, (TT, TT), 1)
    same = (row // C) == (col // C)
    ri = lax.broadcasted_iota(jnp.int32, (C, TT), 0)
    cj = lax.broadcasted_iota(jnp.int32, (C, TT), 1)
    cjc = cj // C
    cjj = cj & (C - 1)
    tril_c = ri >= cjj
    strict_c = ri > cjj
    eye_c = (ri == cjj).astype(F32)
    lane = lax.broadcasted_iota(jnp.int32, (TT, LANES), 1)
    lchunk = lax.broadcasted_iota(jnp.int32, (1, TT), 1) // C
    bcall = bc_ref[...]
    gcall = gc_ref[...]
    nw = nw_ref[...]

    def compress(x):
        out = x[(NC - 1) * C:NC * C]
        for c in range(NC - 2, -1, -1):
            out = jnp.where(cjc == c, x[c * C:(c + 1) * C], out)
        return out

    def by_row(colvec):
        out = jnp.broadcast_to(colvec[(NC - 1) * C:NC * C], (C, TT))
        for c in range(NC - 2, -1, -1):
            out = jnp.where(cjc == c, colvec[c * C:(c + 1) * C], out)
        return out

    def bd(xc):
        return jnp.where(same, jnp.concatenate([xc] * NC, axis=0), 0.0).astype(BF16)

    kf, qf, kT, kkc, qkc = [], [], [], [], []
    for kh in range(KH):
        q = q_ref[:, kh * DN_HEAD:(kh + 1) * DN_HEAD]
        k = k_ref[:, kh * DN_HEAD:(kh + 1) * DN_HEAD]
        kq = lax.dot_general(jnp.concatenate([k, q], axis=0), k, NT, preferred_element_type=F32)
        kkc.append(compress(kq[:TT]))
        qkc.append(compress(kq[TT:]))
        kf.append(k.astype(F32))
        qf.append(q.astype(F32))
        kT.append(kf[-1].T)

    bcol, gcol, grow, Lc, qkm, P = [], [], [], [], [], []
    for hh in range(NV):
        hglob = grp * NV + hh
        sel = lane == hglob
        bcol.append(jnp.sum(jnp.where(sel, bcall, 0.0), axis=-1, keepdims=True))
        gcol.append(jnp.sum(jnp.where(sel, gcall, 0.0), axis=-1, keepdims=True))
        grow.append(gr_ref[0, pl.ds(hglob, 1), :])
        dec = jnp.exp(jnp.where(tril_c, by_row(gcol[hh]) - grow[hh], -jnp.inf))
        Lc.append(jnp.where(strict_c, by_row(bcol[hh]) * kkc[hh // 2] * dec, 0.0))
        qkm.append(qkc[hh // 2] * dec * scale)
        P.append(eye_c - Lc[hh])

    Lp = [jnp.dot(Lc[hh].astype(BF16), bd(Lc[hh]), preferred_element_type=F32) for hh in range(NV)]
    step = 4
    while step < C:
        res = [jnp.dot(jnp.concatenate([Lp[hh], P[hh]], axis=0).astype(BF16), bd(Lp[hh]),
                       preferred_element_type=F32) for hh in range(NV)]
        Lp = [r[:C] for r in res]
        P = [P[hh] + res[hh][C:] for hh in range(NV)]
        step *= 2
    P = [P[hh] + jnp.dot(P[hh].astype(BF16), bd(Lp[hh]), preferred_element_type=F32) for hh in range(NV)]

    uw, qd, kdT, eg_last = [], [], [], []
    for hh in range(NV):
        eg = jnp.exp(gcol[hh])
        vf = v_ref[:, hh * DN_HEAD:(hh + 1) * DN_HEAD].astype(F32)
        rhs = jnp.concatenate([vf * bcol[hh], kf[hh // 2] * (bcol[hh] * eg)], axis=1).astype(BF16)
        uw.append(jnp.dot(bd(P[hh]), rhs, preferred_element_type=F32))
        qd.append((qf[hh // 2] * (eg * scale)).astype(BF16))
        glrow = jnp.zeros((1, TT), F32)
        for c in range(NC):
            glrow = jnp.where(lchunk == c, gcol[hh][c * C + C - 1:c * C + C, :], glrow)
        kdT.append(kT[hh // 2] * jnp.exp(glrow - grow[hh]))
        eg_last.append([eg[c * C + C - 1:c * C + C, :] for c in range(NC)])

    S = [S_ref[hh] for hh in range(NV)]
    vn = [[] for _ in range(NV)]
    o_inter = [[] for _ in range(NV)]
    for c in range(NC):
        r0, r1 = c * C, (c + 1) * C
        for hh in range(NV):
            w_c = uw[hh][r0:r1, DN_HEAD:].astype(BF16)
            lhs = jnp.concatenate([w_c, qd[hh][r0:r1]], axis=0)
            res = jnp.dot(lhs, S[hh].astype(BF16), preferred_element_type=F32)
            vn_c = (uw[hh][r0:r1, :DN_HEAD] - res[:C]).astype(BF16)
            vn[hh].append(vn_c)
            o_inter[hh].append(res[C:])
            kd_c = kdT[hh][:, r0:r1].astype(BF16)
            S[hh] = S[hh] * eg_last[hh][c] + jnp.dot(kd_c, vn_c, preferred_element_type=F32)

    for hh in range(NV):
        S_ref[hh] = S[hh]
        vn_all = jnp.concatenate(vn[hh], axis=0)
        o = jnp.concatenate(o_inter[hh], axis=0) + jnp.dot(bd(qkm[hh]), vn_all, preferred_element_type=F32)
        ms = jnp.mean(o * o, axis=-1, keepdims=True)
        z = z_ref[:, hh * DN_HEAD:(hh + 1) * DN_HEAD].astype(F32)
        o_ref[:, hh * DN_HEAD:(hh + 1) * DN_HEAD] = (o * lax.rsqrt(ms + EPS) * nw * z).astype(BF16)


def _delta(qk_n, vv, sz, beta, gc, gcT, dn_norm_w, B, T, KH=8, TT=256):
    N = B * T
    nt = T // TT
    ng = DN_K_HEADS // KH
    row = lambda b, g, t: b * nt + t
    kw = KH * DN_HEAD
    return pl.pallas_call(
        functools.partial(_delta_body, KH=KH, TT=TT),
        grid=(B, ng, nt),
        in_specs=[pl.BlockSpec((TT, kw), lambda b, g, t: (row(b, g, t), g)),
                  pl.BlockSpec((TT, kw), lambda b, g, t: (row(b, g, t), DN_KEY // kw + g)),
                  pl.BlockSpec((TT, 2 * kw), lambda b, g, t: (row(b, g, t), g)),
                  pl.BlockSpec((TT, 2 * kw), lambda b, g, t: (row(b, g, t), g)),
                  pl.BlockSpec((TT, LANES), lambda b, g, t: (row(b, g, t), 0)),
                  pl.BlockSpec((TT, LANES), lambda b, g, t: (row(b, g, t), 0)),
                  pl.BlockSpec((1, DN_V_HEADS, TT), lambda b, g, t: (b, 0, t)),
                  pl.BlockSpec((1, DN_HEAD), lambda b, g, t: (0, 0))],
        out_specs=pl.BlockSpec((TT, 2 * kw), lambda b, g, t: (row(b, g, t), g)),
        out_shape=jax.ShapeDtypeStruct((N, DN_VAL), BF16),
        scratch_shapes=[pltpu.VMEM((2 * KH, DN_HEAD, DN_HEAD), F32)],
        compiler_params=_cparams(("arbitrary", "arbitrary", "arbitrary")),
        name="delta_rule",
    )(qk_n, qk_n, vv, sz, beta, gc, gcT, dn_norm_w.reshape(1, DN_HEAD))


def _merge_body(a_ref, o_ref, ga_ref, gb_ref, wa_ref, wb_ref, y_ref):
    ya = jnp.dot(a_ref[...], wa_ref[...], preferred_element_type=F32)
    yb = jnp.dot(o_ref[...], wb_ref[...], preferred_element_type=F32)
    y = ga_ref[...].astype(F32) * ya + gb_ref[...].astype(F32) * yb
    y_ref[...] = y.astype(BF16)


def _merge(a_g, o_g, p_m, wa, wb, tm=512, tn=512):
    N = a_g.shape[0]
    D = wa.shape[1]
    nj = D // tn
    return pl.pallas_call(
        _merge_body,
        grid=(N // tm, nj),
        in_specs=[pl.BlockSpec((tm, a_g.shape[1]), lambda i, j: (i, 0)),
                  pl.BlockSpec((tm, o_g.shape[1]), lambda i, j: (i, 0)),
                  pl.BlockSpec((tm, tn), lambda i, j: (i, j)),
                  pl.BlockSpec((tm, tn), lambda i, j: (i, nj + j)),
                  pl.BlockSpec((wa.shape[0], tn), lambda i, j: (0, j)),
                  pl.BlockSpec((wb.shape[0], tn), lambda i, j: (0, j))],
        out_specs=pl.BlockSpec((tm, tn), lambda i, j: (i, j)),
        out_shape=jax.ShapeDtypeStruct((N, D), BF16),
        compiler_params=_cparams(("arbitrary", "arbitrary")),
        name="merge",
    )(a_g, o_g, p_m, p_m, wa, wb)


def _out_body(y_ref, w_ref, x_ref, g_ref, o_ref):
    acc = jnp.dot(y_ref[...], w_ref[...], preferred_element_type=F32)
    o_ref[...] = x_ref[...] + g_ref[0] * acc


def _outproj(y, w, x2d, gate, T, tm=512, tn=512):
    N, D = x2d.shape
    B = gate.shape[0]
    tps = T // tm
    return pl.pallas_call(
        _out_body,
        grid=(N // tm, D // tn),
        in_specs=[pl.BlockSpec((tm, y.shape[1]), lambda i, j: (i, 0)),
                  pl.BlockSpec((w.shape[0], tn), lambda i, j: (0, j)),
                  pl.BlockSpec((tm, tn), lambda i, j: (i, j)),
                  pl.BlockSpec((1, 1, tn), lambda i, j: (i // tps, 0, j))],
        out_specs=pl.BlockSpec((tm, tn), lambda i, j: (i, j)),
        out_shape=jax.ShapeDtypeStruct((N, D), F32),
        compiler_params=_cparams(("arbitrary", "arbitrary")),
        name="out_proj",
    )(y, w, x2d, gate.reshape(B, 1, D))


def _merge_out_body(a_ref, o_ref, ga_ref, gb_ref, wa_ref, wb_ref, wo_ref, x_ref, g_ref, out_ref):
    @pl.when(pl.program_id(1) == 0)
    def _():
        out_ref[...] = x_ref[...]

    ya = jnp.dot(a_ref[...], wa_ref[...], preferred_element_type=F32)
    yb = jnp.dot(o_ref[...], wb_ref[...], preferred_element_type=F32)
    y = (ga_ref[...].astype(F32) * ya + gb_ref[...].astype(F32) * yb).astype(BF16)
    out_ref[...] += g_ref[0] * jnp.dot(y, wo_ref[...], preferred_element_type=F32)


def _merge_out(a_g, o_g, p_m, wa, wb, wo, x2d, gate, T, tm=512, tn=512):
    N, D = x2d.shape
    B = gate.shape[0]
    nj = D // tn
    tps = T // tm
    return pl.pallas_call(
        _merge_out_body,
        grid=(N // tm, nj),
        in_specs=[pl.BlockSpec((tm, a_g.shape[1]), lambda i, j: (i, 0)),
                  pl.BlockSpec((tm, o_g.shape[1]), lambda i, j: (i, 0)),
                  pl.BlockSpec((tm, tn), lambda i, j: (i, j)),
                  pl.BlockSpec((tm, tn), lambda i, j: (i, nj + j)),
                  pl.BlockSpec((wa.shape[0], tn), lambda i, j: (0, j)),
                  pl.BlockSpec((wb.shape[0], tn), lambda i, j: (0, j)),
                  pl.BlockSpec((tn, D), lambda i, j: (j, 0)),
                  pl.BlockSpec((tm, D), lambda i, j: (i, 0)),
                  pl.BlockSpec((1, 1, D), lambda i, j: (i // tps, 0, 0))],
        out_specs=pl.BlockSpec((tm, D), lambda i, j: (i, 0)),
        out_shape=jax.ShapeDtypeStruct((N, D), F32),
        compiler_params=pltpu.CompilerParams(dimension_semantics=("arbitrary", "arbitrary"),
                                             vmem_limit_bytes=60 * 1024 * 1024),
        name="merge_out",
    )(a_g, o_g, p_m, p_m, wa, wb, wo, x2d, gate.reshape(B, 1, D))


def _layer(l, x, c, positions, w_ada, b_ada, norm_w, w_in, q_norm_w, k_norm_w, sinks,
           conv_w, a_log, dt_bias, dn_norm_w, w_o_swa, w_o_dn, w_out):
    B, T, D = x.shape
    N = B * T
    x2d = x.reshape(N, D)

    mod = _adaln(c, w_ada, b_ada)
    shift, scale, gate = mod[:, :D], mod[:, D:2 * D], mod[:, 2 * D:]
    h = _normmod(x2d, norm_w, scale, shift, T)

    o_q, o_g = 0, SWA_Q + 2 * SWA_KV
    o_d = o_g + SWA_Q
    o_z = o_d + DN_CONV_CH
    o_b = o_z + DN_VAL
    o_a = o_b + DN_V_HEADS
    o_m = o_a + DN_V_HEADS
    assert o_a == o_b + DN_V_HEADS and o_b % LANES == 0
    w_in = jnp.swapaxes(w_in, 1, 2)
    posf = positions.astype(F32).reshape(N, 1)
    tabs = _rope_tables(posf)
    rep = LANES // SWA_HEAD_DIM
    qw = jnp.tile(q_norm_w.reshape(1, SWA_HEAD_DIM), (1, rep)) * (SWA_HEAD_DIM ** -0.5)
    kw = jnp.tile(k_norm_w.reshape(1, SWA_HEAD_DIM), (1, rep))
    q_n = _proj(h, w_in, l, o_q, SWA_Q, "qknorm", T, extras=(tabs, qw, 2 * 512 // LANES), name="proj_q")
    kv_n = _proj(h, w_in, l, o_q + SWA_Q, 2 * SWA_KV, "qknorm", T, nblk=1,
                 extras=(tabs, kw, SWA_KV // LANES), name="proj_kv")
    sag = _proj(h, w_in, l, o_g, o_d - o_g, "silu", T, name="proj_swa_gate")
    qk_n = _proj(h, w_in, l, o_d, 2 * DN_KEY, "conv", T, extras=(conv_w, 0, True), name="proj_dn_qk")
    vv = _proj(h, w_in, l, o_d + 2 * DN_KEY, DN_VAL, "conv", T, extras=(conv_w, 2 * DN_KEY, False),
               name="proj_dn_v")
    sz = _proj(h, w_in, l, o_z, o_b - o_z, "silu", T, name="proj_dn_z")
    al = jnp.zeros((1, LANES), F32).at[0, :DN_V_HEADS].set(a_log)
    dt = jnp.zeros((1, LANES), F32).at[0, :DN_V_HEADS].set(dt_bias)
    beta, gc = _proj(h, w_in, l, o_b, 2 * DN_V_HEADS, "gates", T, extras=(al, dt), name="proj_dn_gates")
    p_m = _proj(h, w_in, l, o_m, 2 * D, "sigmoid", T, name="proj_merge_gates")

    a_g = _swa(q_n, kv_n, sag, sinks, B, T)

    gcT = gc[:, :DN_V_HEADS].reshape(B, T, DN_V_HEADS).transpose(0, 2, 1)
    o_g2 = _delta(qk_n, vv, sz, beta, gc, gcT, dn_norm_w, B, T)

    out = _merge_out(a_g, o_g2, p_m, w_o_swa.astype(BF16), w_o_dn.astype(BF16), w_out.astype(BF16),
                     x2d, gate, T)
    return out.reshape(B, T, D)


def kernel(x, c, positions, w_ada, b_ada, norm_w, w_in, q_norm_w, k_norm_w, sinks, conv_w,
           a_log, dt_bias, dn_norm_w, w_o_swa, w_o_dn, w_out):
    depth = w_ada.shape[0]
    for l in range(depth):
        x = _layer(l, x, c, positions, w_ada[l], b_ada[l], norm_w[l], w_in, q_norm_w[l],
                   k_norm_w[l], sinks[l], conv_w[l], a_log[l], dt_bias[l], dn_norm_w[l],
                   w_o_swa[l], w_o_dn[l], w_out[l])
    return x
```

```python
import functools

import numpy as np
import jax
import jax.numpy as jnp
from jax import lax
from jax.experimental import pallas as pl
from jax.experimental.pallas import tpu as pltpu

F32 = jnp.float32
BF16 = jnp.bfloat16

D_MODEL = 2048
SWA_Q_HEADS = 32
SWA_KV_HEADS = 4
SWA_HEAD_DIM = 64
SWA_BLOCK = 128
ROPE_THETA = 500000.0
ROPE_DIM = SWA_HEAD_DIM // 4
DN_K_HEADS = 16
DN_V_HEADS = 32
DN_HEAD = 128
DN_CONV = 4
DN_CHUNK = 64
EPS = 1e-6

SWA_Q = SWA_Q_HEADS * SWA_HEAD_DIM
SWA_KV = SWA_KV_HEADS * SWA_HEAD_DIM
DN_KEY = DN_K_HEADS * DN_HEAD
DN_VAL = DN_V_HEADS * DN_HEAD
DN_CONV_CH = 2 * DN_KEY + DN_VAL

LANES = 128
SUBLANES = 8
VMEM_LIMIT = 56 * 1024 * 1024


def _cparams(sem, vmem=VMEM_LIMIT):
    return pltpu.CompilerParams(dimension_semantics=sem, vmem_limit_bytes=vmem)


def _sigmoid(x):
    return 0.5 * jnp.tanh(0.5 * x) + 0.5


def _silu(x):
    h = 0.5 * x
    return h + h * jnp.tanh(h)


def _adaln_body(c_ref, w_ref, b_ref, o_ref):
    s = _silu(c_ref[...]).astype(BF16)
    o_ref[...] = jnp.dot(s, w_ref[...].astype(BF16), preferred_element_type=F32) + b_ref[...]


def _adaln(c, w_ada, b_ada, tn=512):
    B, D = c.shape
    n_out = w_ada.shape[1]
    c8 = jnp.zeros((SUBLANES, D), F32).at[:B].set(c)
    out = pl.pallas_call(
        _adaln_body,
        grid=(n_out // tn,),
        in_specs=[pl.BlockSpec((SUBLANES, D), lambda j: (0, 0)),
                  pl.BlockSpec((D, tn), lambda j: (0, j)),
                  pl.BlockSpec((1, tn), lambda j: (0, j))],
        out_specs=pl.BlockSpec((SUBLANES, tn), lambda j: (0, j)),
        out_shape=jax.ShapeDtypeStruct((SUBLANES, n_out), F32),
        compiler_params=_cparams(("arbitrary",)),
        name="adaln_mod",
    )(c8, w_ada, b_ada.reshape(1, n_out))
    return out[:B]


def _normmod_body(x_ref, nw_ref, sc_ref, sh_ref, o_ref):
    x = x_ref[...]
    ms = jnp.mean(x * x, axis=-1, keepdims=True)
    y = x * lax.rsqrt(ms + EPS) * nw_ref[...]
    o_ref[...] = (y * (1.0 + sc_ref[0]) + sh_ref[0]).astype(BF16)


def _normmod(x2d, norm_w, scale, shift, T, tm=512):
    N, D = x2d.shape
    B = scale.shape[0]
    tps = T // tm
    return pl.pallas_call(
        _normmod_body,
        grid=(N // tm,),
        in_specs=[pl.BlockSpec((tm, D), lambda i: (i, 0)),
                  pl.BlockSpec((1, D), lambda i: (0, 0)),
                  pl.BlockSpec((1, 1, D), lambda i: (i // tps, 0, 0)),
                  pl.BlockSpec((1, 1, D), lambda i: (i // tps, 0, 0))],
        out_specs=pl.BlockSpec((tm, D), lambda i: (i, 0)),
        out_shape=jax.ShapeDtypeStruct((N, D), BF16),
        compiler_params=_cparams(("arbitrary",)),
        name="norm_mod",
    )(x2d, norm_w.reshape(1, D), scale.reshape(B, 1, D), shift.reshape(B, 1, D))


HALF = LANES // 2


NT_DIMS = (((1,), (1,)), ((), ()))


def _cast_weights(w_refs, w2_ref, wbf):
    wblk = w_refs[0].shape[0]
    off = 0 if w2_ref is None else HALF
    for b, w_ref in enumerate(w_refs):
        lo = max(b * wblk - off, 0)
        wbf[lo:(b + 1) * wblk - off, :] = w_ref[lo + off - b * wblk:, :].astype(BF16)
    if w2_ref is not None:
        n = len(w_refs) * wblk
        wbf[n - off:n, :] = w2_ref[...].astype(BF16)


def _cast_once(w_refs, w2_ref, wbf):
    @pl.when(pl.program_id(1) == 0)
    def _():
        _cast_weights(w_refs, w2_ref, wbf)


def _proj_act_body(*refs, act, shifted, sub, nblk):
    h_ref, w_refs = refs[0], refs[1:1 + nblk]
    w2_ref = refs[1 + nblk] if shifted else None
    o_ref, wbf = refs[-2:]
    _cast_once(w_refs, w2_ref, wbf)
    for r0 in range(0, h_ref.shape[0], sub):
        acc = lax.dot_general(h_ref[r0:r0 + sub, :], wbf[...], NT_DIMS, preferred_element_type=F32)
        if act == "silu":
            acc = _silu(acc)
        elif act == "sigmoid":
            acc = _sigmoid(acc)
        o_ref[r0:r0 + sub, :] = acc.astype(o_ref.dtype)


def _proj_conv_body(*refs, tm, tps, norm, sub, nblk):
    h_ref, w_refs = refs[0], refs[1:1 + nblk]
    cw_ref, o_ref, wbf, buf = refs[1 + nblk:]
    i = pl.program_id(1)
    first = (i % tps) == 0
    _cast_once(w_refs, None, wbf)

    @pl.when(first)
    def _():
        buf[0:SUBLANES, :] = jnp.zeros((SUBLANES, buf.shape[1]), F32)

    @pl.when(jnp.logical_not(first))
    def _():
        buf[0:SUBLANES, :] = buf[tm:tm + SUBLANES, :]

    cw = cw_ref[...]
    mm = lambda r0: lax.dot_general(h_ref[r0:r0 + sub, :], wbf[...], NT_DIMS, preferred_element_type=F32)
    nxt = mm(0)
    for r0 in range(0, tm, sub):
        acc = nxt
        if r0 + sub < tm:
            nxt = mm(r0 + sub)
        b0 = SUBLANES + r0
        buf[b0:b0 + sub, :] = acc
        win = buf[b0 - SUBLANES:b0 + sub, :]
        y = win[SUBLANES:, :] * cw[DN_CONV - 1:DN_CONV, :]
        for d in range(1, DN_CONV):
            y = y + pltpu.roll(win, d, 0)[SUBLANES:, :] * cw[DN_CONV - 1 - d:DN_CONV - d, :]
        y = _silu(y)
        for c in range(y.shape[1] // DN_HEAD):
            yc = y[:, c * DN_HEAD:(c + 1) * DN_HEAD]
            if norm:
                yc = yc * lax.rsqrt(jnp.sum(yc * yc, axis=-1, keepdims=True) + EPS)
            o_ref[r0:r0 + sub, c * DN_HEAD:(c + 1) * DN_HEAD] = yc.astype(o_ref.dtype)


def _norm_rope(x, nw, cs, s1, s2, lo):
    half = ROPE_DIM // 2
    x2 = x * x
    s_lo = jnp.sum(jnp.where(lo, x2, 0.0), axis=-1, keepdims=True)
    s_hi = jnp.sum(jnp.where(lo, 0.0, x2), axis=-1, keepdims=True)
    r = jnp.where(lo, lax.rsqrt(s_lo * (1.0 / SWA_HEAD_DIM) + EPS),
                  lax.rsqrt(s_hi * (1.0 / SWA_HEAD_DIM) + EPS))
    xn = x * r * nw
    return xn * cs + pltpu.roll(xn, half, 1) * s1 + pltpu.roll(xn, LANES - half, 1) * s2


def _proj_qknorm_body(*refs, sub, nblk, n_norm):
    h_ref, w_refs = refs[0], refs[1:1 + nblk]
    c_ref, s1_ref, s2_ref, nw_ref, o_ref, wbf = refs[1 + nblk:]
    _cast_once(w_refs, None, wbf)
    lo = lax.broadcasted_iota(jnp.int32, (1, LANES), 1) < SWA_HEAD_DIM
    nw = nw_ref[...]
    for r0 in range(0, h_ref.shape[0], sub):
        acc = lax.dot_general(h_ref[r0:r0 + sub, :], wbf[...], NT_DIMS, preferred_element_type=F32)
        cs, s1, s2 = c_ref[r0:r0 + sub, :], s1_ref[r0:r0 + sub, :], s2_ref[r0:r0 + sub, :]
        for c in range(acc.shape[1] // LANES):
            x = acc[:, c * LANES:(c + 1) * LANES]
            if c < n_norm:
                x = _norm_rope(x, nw, cs, s1, s2, lo)
            o_ref[r0:r0 + sub, c * LANES:(c + 1) * LANES] = x.astype(o_ref.dtype)


def _proj_gates_body(h_ref, w_ref, al_ref, dt_ref, beta_ref, gc_ref, wbf):
    _cast_once([w_ref], None, wbf)
    acc = lax.dot_general(h_ref[...], wbf[...], NT_DIMS, preferred_element_type=F32)
    beta_ref[...] = _sigmoid(acc)
    x = pltpu.roll(acc, LANES - DN_V_HEADS, 1) + dt_ref[...]
    sp = jnp.maximum(x, 0.0) + jnp.log(1.0 + jnp.exp(-jnp.abs(x)))
    g = -jnp.exp(al_ref[...]) * sp
    r64 = lax.broadcasted_iota(jnp.int32, g.shape, 0) & (DN_CHUNK - 1)
    s = 1
    while s < DN_CHUNK:
        g = g + jnp.where(r64 >= s, pltpu.roll(g, s, 0), 0.0)
        s *= 2
    gc_ref[...] = g


def _proj(h, w_t, layer, col0, width, mode, T, extras=(), tm=1024, wblk=512, nblk=2, sub=256,
          name="proj"):
    N, K = h.shape
    out_dtype = BF16
    tm = min(tm, T)
    sub = min(sub, tm)
    off = col0 % LANES
    base = col0 - off
    shifted = off != 0
    if mode == "gates":
        wblk, nblk = LANES, 1
    tn = wblk * nblk
    assert off in (0, HALF) and base % wblk == 0
    assert mode == "gates" or width % tn == 0
    jb = base // wblk
    grid = (max(width // tn, 1), N // tm)
    in_specs = [pl.BlockSpec((tm, K), lambda j, i: (i, 0))]
    in_specs += [pl.BlockSpec((None, wblk, K), lambda j, i, b=b: (layer, jb + j * nblk + b, 0))
                 for b in range(nblk)]
    args = [h] + [w_t] * nblk
    scratch = [pltpu.VMEM((tn, K), BF16)]
    out_spec = pl.BlockSpec((tm, tn), lambda j, i: (i, j))
    out_shape = jax.ShapeDtypeStruct((N, max(width, tn)), out_dtype)
    if shifted:
        in_specs.append(pl.BlockSpec((None, HALF, K),
                                     lambda j, i: (layer, (base + (j + 1) * tn) // HALF, 0)))
        args.append(w_t)
    if mode == "conv":
        cw, cw_col0, norm = extras
        in_specs.append(pl.BlockSpec((DN_CONV, tn), lambda j, i: (0, cw_col0 // tn + j)))
        args.append(cw)
        scratch.append(pltpu.VMEM((tm + SUBLANES, tn), F32))
        body = functools.partial(_proj_conv_body, tm=tm, tps=T // tm, norm=norm, sub=sub, nblk=nblk)
    elif mode == "qknorm":
        tabs, nw, n_norm = extras
        in_specs += [pl.BlockSpec((tm, LANES), lambda j, i: (i, 0))] * 3
        in_specs.append(pl.BlockSpec((1, LANES), lambda j, i: (0, 0)))
        args += list(tabs) + [nw]
        body = functools.partial(_proj_qknorm_body, sub=sub, nblk=nblk, n_norm=n_norm)
    elif mode == "gates":
        in_specs += [pl.BlockSpec((1, LANES), lambda j, i: (0, 0))] * 2
        args += list(extras)
        body = _proj_gates_body
        out_spec = [out_spec, out_spec]
        out_shape = [jax.ShapeDtypeStruct((N, LANES), F32)] * 2
    else:
        body = functools.partial(_proj_act_body, act=mode, shifted=shifted, sub=sub if mode else tm,
                                 nblk=nblk)
    return pl.pallas_call(
        body, grid=grid, in_specs=in_specs, out_specs=out_spec, out_shape=out_shape,
        scratch_shapes=scratch,
        compiler_params=_cparams(("arbitrary", "arbitrary")),
        name=name,
    )(*args)


def _rope_inv_freq_lanes():
    half = ROPE_DIM // 2
    inv = (np.float32(ROPE_THETA) ** (-np.arange(half, dtype=np.float32) * np.float32(2.0 / ROPE_DIM))).astype(np.float32)
    lanes = np.zeros((1, LANES), np.float32)
    for l in range(LANES):
        m = l % SWA_HEAD_DIM
        if m < ROPE_DIM:
            lanes[0, l] = inv[m % half]
    return lanes


def _rope_tab_body(pos_ref, invf_ref, c_ref, s1_ref, s2_ref):
    half = ROPE_DIM // 2
    l64 = lax.broadcasted_iota(jnp.int32, (1, LANES), 1) & (SWA_HEAD_DIM - 1)
    ang = pos_ref[...] * invf_ref[...]
    sn = jnp.sin(ang)
    c_ref[...] = jnp.cos(ang)
    s1_ref[...] = jnp.where((l64 >= half) & (l64 < ROPE_DIM), sn, 0.0)
    s2_ref[...] = jnp.where(l64 < half, -sn, 0.0)


def _rope_tables(posf, tm=1024):
    N = posf.shape[0]
    tm = min(tm, N)
    spec = pl.BlockSpec((tm, LANES), lambda i: (i, 0))
    return pl.pallas_call(
        _rope_tab_body,
        grid=(N // tm,),
        in_specs=[pl.BlockSpec((tm, 1), lambda i: (i, 0)), pl.BlockSpec((1, LANES), lambda i: (0, 0))],
        out_specs=[spec] * 3,
        out_shape=[jax.ShapeDtypeStruct((N, LANES), F32)] * 3,
        compiler_params=_cparams(("arbitrary",)),
        name="rope_tables",
    )(posf, jnp.asarray(_rope_inv_freq_lanes()))


def _swa_body(sink_ref, q_ref, kv_ref, g_ref, o_ref, kprev, vprev):
    n = pl.program_id(1)
    BLK = SWA_BLOCK
    HD = SWA_HEAD_DIM

    @pl.when(n == 0)
    def _():
        kprev[...] = jnp.zeros_like(kprev)
        vprev[...] = jnp.zeros_like(vprev)

    lo = lax.broadcasted_iota(jnp.int32, (1, LANES), 1) < HD
    n_kc = SWA_KV // LANES
    nblocks = q_ref.shape[0] // BLK
    zrows = jnp.zeros((HD, BLK), F32)

    kband, vbandT = {}, {}
    k_last = {idx: kprev[idx] for idx in range(2 * SWA_KV_HEADS)}
    v_last = {idx: vprev[idx] for idx in range(2 * SWA_KV_HEADS)}
    for bi in range(nblocks):
        kv = kv_ref[bi * BLK:(bi + 1) * BLK, :]
        kc = [kv[:, c * LANES:(c + 1) * LANES].astype(F32) for c in range(n_kc)]
        vT = [kv[:, SWA_KV + c * LANES:SWA_KV + (c + 1) * LANES].astype(F32).T for c in range(n_kc)]
        for j in range(SWA_KV_HEADS):
            cj, b = j // 2, j % 2
            for a in range(2):
                src = kc[cj] if a == b else pltpu.roll(kc[cj], HD, 1)
                k_cur = jnp.where(lo if a == 0 else jnp.logical_not(lo), src, 0.0).astype(BF16)
                rows = vT[cj][b * HD:(b + 1) * HD]
                v_cur = jnp.concatenate([rows, zrows] if a == 0 else [zrows, rows], axis=0).astype(BF16)
                idx = 2 * j + a
                kband[(bi, j, a)] = jnp.concatenate([k_last[idx], k_cur], axis=0)
                vbandT[(bi, j, a)] = jnp.concatenate([v_last[idx], v_cur], axis=1)
                k_last[idx], v_last[idx] = k_cur, v_cur
    for idx in range(2 * SWA_KV_HEADS):
        kprev[idx] = k_last[idx]
        vprev[idx] = v_last[idx]

    kj = lax.broadcasted_iota(jnp.int32, (2 * BLK, 2 * BLK), 0)
    qi = lax.broadcasted_iota(jnp.int32, (2 * BLK, 2 * BLK), 1) & (BLK - 1)
    d = kj - qi
    in_window = (d >= 1) & (d <= BLK)
    kmin = jnp.where(n == 0, BLK, 0)
    valid = [in_window & (kj >= kmin)] + [in_window] * (nblocks - 1)
    left = lax.broadcasted_iota(jnp.int32, (1, 2 * BLK), 1) < BLK

    G = SWA_Q_HEADS // SWA_KV_HEADS
    cpk = G // 2
    items = [(bi, j, a, j * cpk + 2 * p) for bi in range(nblocks) for j in range(SWA_KV_HEADS)
             for a in range(2) for p in range(cpk // 2)]

    def scores(item):
        bi, j, a, c1 = item
        rows = slice(bi * BLK, (bi + 1) * BLK)
        rhs = jnp.concatenate([q_ref[rows, c1 * LANES:(c1 + 1) * LANES],
                               q_ref[rows, (c1 + 1) * LANES:(c1 + 2) * LANES]], axis=0)
        return lax.dot_general(kband[(bi, j, a)], rhs, NT_DIMS, preferred_element_type=F32)

    acc = {}

    def consume(item, oT, inv):
        bi, j, a, c1 = item
        rows = slice(bi * BLK, (bi + 1) * BLK)
        oT = oT * inv
        for c, part in ((c1, oT[:, :BLK]), (c1 + 1, oT[:, BLK:])):
            if a == 0:
                acc[c] = part
            else:
                gate = g_ref[rows, c * LANES:(c + 1) * LANES].astype(F32)
                o_ref[rows, c * LANES:(c + 1) * LANES] = ((acc.pop(c) + part).T * gate).astype(BF16)

    s_cur = scores(items[0])
    pending = None
    for i, item in enumerate(items):
        bi, j, a, c1 = item
        s_next = scores(items[i + 1]) if i + 1 < len(items) else None
        s = jnp.where(valid[bi], s_cur, -jnp.inf)
        sink = jnp.where(left, sink_ref[2 * c1 + a], sink_ref[2 * c1 + 2 + a])
        m = jnp.maximum(jnp.max(s, axis=0, keepdims=True), sink)
        e = jnp.exp(s - m)
        inv = 1.0 / (jnp.sum(e, axis=0, keepdims=True) + jnp.exp(sink - m))
        oT = jnp.dot(vbandT[(bi, j, a)], e.astype(BF16), preferred_element_type=F32)
        if pending is not None:
            consume(*pending)
        pending = (item, oT, inv)
        s_cur = s_next
    consume(*pending)


def _swa(q_n, kv_n, sag, sinks, B, T, blocks_per_step=2):
    N = B * T
    rows = blocks_per_step * SWA_BLOCK
    nb = T // rows
    row = lambda b, n: b * nb + n
    return pl.pallas_call(
        _swa_body,
        grid=(B, nb),
        in_specs=[pl.BlockSpec(memory_space=pltpu.SMEM),
                  pl.BlockSpec((rows, SWA_Q), lambda b, n: (row(b, n), 0)),
                  pl.BlockSpec((rows, 2 * SWA_KV), lambda b, n: (row(b, n), 0)),
                  pl.BlockSpec((rows, SWA_Q), lambda b, n: (row(b, n), 0))],
        out_specs=pl.BlockSpec((rows, SWA_Q), lambda b, n: (row(b, n), 0)),
        out_shape=jax.ShapeDtypeStruct((N, SWA_Q), BF16),
        scratch_shapes=[pltpu.VMEM((2 * SWA_KV_HEADS, SWA_BLOCK, LANES), BF16),
                        pltpu.VMEM((2 * SWA_KV_HEADS, LANES, SWA_BLOCK), BF16)],
        compiler_params=_cparams(("arbitrary", "arbitrary")),
        name="swa_attn",
    )(sinks, q_n, kv_n, sag)


def _delta_body(q_ref, k_ref, v_ref, z_ref, bc_ref, gc_ref, gr_ref, nw_ref, o_ref, S_ref,
                *, KH, TT):
    grp = pl.program_id(1)
    t = pl.program_id(2)
    C = DN_CHUNK
    NC = TT // C
    scale = DN_HEAD ** -0.5
    NT = (((1,), (1,)), ((), ()))

    @pl.when(t == 0)
    def _():
        S_ref[...] = jnp.zeros_like(S_ref)

    NV = 2 * KH
    row = lax.broadcasted_iota(jnp.int32, (TT, TT), 0)
    col = lax.broadcasted_iota(jnp.int32, (TT, TT), 1)
    same = (row // C) == (col // C)
    ri = lax.broadcasted_iota(jnp.int32, (C, TT), 0)
    cj = lax.broadcasted_iota(jnp.int32, (C, TT), 1)
    cjc = cj // C
    cjj = cj & (C - 1)
    tril_c = ri >= cjj
    strict_c = ri > cjj
    eye_c = (ri == cjj).astype(F32)
    lane = lax.broadcasted_iota(jnp.int32, (TT, LANES), 1)
    lchunk = lax.broadcasted_iota(jnp.int32, (1, TT), 1) // C
    bcall = bc_ref[...]
    gcall = gc_ref[...]
    nw = nw_ref[...]

    def compress(x):
        out = x[(NC - 1) * C:NC * C]
        for c in range(NC - 2, -1, -1):
            out = jnp.where(cjc == c, x[c * C:(c + 1) * C], out)
        return out

    def by_row(colvec):
        out = jnp.broadcast_to(colvec[(NC - 1) * C:NC * C], (C, TT))
        for c in range(NC - 2, -1, -1):
            out = jnp.where(cjc == c, colvec[c * C:(c + 1) * C], out)
        return out

    def bd(xc):
        return jnp.where(same, jnp.concatenate([xc] * NC, axis=0), 0.0).astype(BF16)

    kf, qf, kT, kkc, qkc = [], [], [], [], []
    for kh in range(KH):
        q = q_ref[:, kh * DN_HEAD:(kh + 1) * DN_HEAD]
        k = k_ref[:, kh * DN_HEAD:(kh + 1) * DN_HEAD]
        kq = lax.dot_general(jnp.concatenate([k, q], axis=0), k, NT, preferred_element_type=F32)
        kkc.append(compress(kq[:TT]))
        qkc.append(compress(kq[TT:]))
        kf.append(k.astype(F32))
        qf.append(q.astype(F32))
        kT.append(kf[-1].T)

    bcol, gcol, grow, Lc, qkm, P = [], [], [], [], [], []
    for hh in range(NV):
        hglob = grp * NV + hh
        sel = lane == hglob
        bcol.append(jnp.sum(jnp.where(sel, bcall, 0.0), axis=-1, keepdims=True))
        gcol.append(jnp.sum(jnp.where(sel, gcall, 0.0), axis=-1, keepdims=True))
        grow.append(gr_ref[0, pl.ds(hglob, 1), :])
        dec = jnp.exp(jnp.where(tril_c, by_row(gcol[hh]) - grow[hh], -jnp.inf))
        Lc.append(jnp.where(strict_c, by_row(bcol[hh]) * kkc[hh // 2] * dec, 0.0))
        qkm.append(qkc[hh // 2] * dec * scale)
        P.append(eye_c - Lc[hh])

    Lp = [jnp.dot(Lc[hh].astype(BF16), bd(Lc[hh]), preferred_element_type=F32) for hh in range(NV)]
    step = 4
    while step < C:
        res = [jnp.dot(jnp.concatenate([Lp[hh], P[hh]], axis=0).astype(BF16), bd(Lp[hh]),
                       preferred_element_type=F32) for hh in range(NV)]
        Lp = [r[:C] for r in res]
        P = [P[hh] + res[hh][C:] for hh in range(NV)]
        step *= 2
    P = [P[hh] + jnp.dot(P[hh].astype(BF16), bd(Lp[hh]), preferred_element_type=F32) for hh in range(NV)]

    uw, qd, kdT, eg_last = [], [], [], []
    for hh in range(NV):
        eg = jnp.exp(gcol[hh])
        vf = v_ref[:, hh * DN_HEAD:(hh + 1) * DN_HEAD].astype(F32)
        rhs = jnp.concatenate([vf * bcol[hh], kf[hh // 2] * (bcol[hh] * eg)], axis=1).astype(BF16)
        uw.append(jnp.dot(bd(P[hh]), rhs, preferred_element_type=F32))
        qd.append((qf[hh // 2] * (eg * scale)).astype(BF16))
        glrow = jnp.zeros((1, TT), F32)
        for c in range(NC):
            glrow = jnp.where(lchunk == c, gcol[hh][c * C + C - 1:c * C + C, :], glrow)
        kdT.append(kT[hh // 2] * jnp.exp(glrow - grow[hh]))
        eg_last.append([eg[c * C + C - 1:c * C + C, :] for c in range(NC)])

    S = [S_ref[hh] for hh in range(NV)]
    vn = [[] for _ in range(NV)]
    o_inter = [[] for _ in range(NV)]
    for c in range(NC):
        r0, r1 = c * C, (c + 1) * C
        for hh in range(NV):
            w_c = uw[hh][r0:r1, DN_HEAD:].astype(BF16)
            lhs = jnp.concatenate([w_c, qd[hh][r0:r1]], axis=0)
            res = jnp.dot(lhs, S[hh].astype(BF16), preferred_element_type=F32)
            vn_c = (uw[hh][r0:r1, :DN_HEAD] - res[:C]).astype(BF16)
            vn[hh].append(vn_c)
            o_inter[hh].append(res[C:])
            kd_c = kdT[hh][:, r0:r1].astype(BF16)
            S[hh] = S[hh] * eg_last[hh][c] + jnp.dot(kd_c, vn_c, preferred_element_type=F32)

    for hh in range(NV):
        S_ref[hh] = S[hh]
        vn_all = jnp.concatenate(vn[hh], axis=0)
        o = jnp.concatenate(o_inter[hh], axis=0) + jnp.dot(bd(qkm[hh]), vn_all, preferred_element_type=F32)
        ms = jnp.mean(o * o, axis=-1, keepdims=True)
        z = z_ref[:, hh * DN_HEAD:(hh + 1) * DN_HEAD].astype(F32)
        o_ref[:, hh * DN_HEAD:(hh + 1) * DN_HEAD] = (o * lax.rsqrt(ms + EPS) * nw * z).astype(BF16)


def _delta(qk_n, vv, sz, beta, gc, gcT, dn_norm_w, B, T, KH=8, TT=256):
    N = B * T
    nt = T // TT
    ng = DN_K_HEADS // KH
    row = lambda b, g, t: b * nt + t
    kw = KH * DN_HEAD
    return pl.pallas_call(
        functools.partial(_delta_body, KH=KH, TT=TT),
        grid=(B, ng, nt),
        in_specs=[pl.BlockSpec((TT, kw), lambda b, g, t: (row(b, g, t), g)),
                  pl.BlockSpec((TT, kw), lambda b, g, t: (row(b, g, t), DN_KEY // kw + g)),
                  pl.BlockSpec((TT, 2 * kw), lambda b, g, t: (row(b, g, t), g)),
                  pl.BlockSpec((TT, 2 * kw), lambda b, g, t: (row(b, g, t), g)),
                  pl.BlockSpec((TT, LANES), lambda b, g, t: (row(b, g, t), 0)),
                  pl.BlockSpec((TT, LANES), lambda b, g, t: (row(b, g, t), 0)),
                  pl.BlockSpec((1, DN_V_HEADS, TT), lambda b, g, t: (b, 0, t)),
                  pl.BlockSpec((1, DN_HEAD), lambda b, g, t: (0, 0))],
        out_specs=pl.BlockSpec((TT, 2 * kw), lambda b, g, t: (row(b, g, t), g)),
        out_shape=jax.ShapeDtypeStruct((N, DN_VAL), BF16),
        scratch_shapes=[pltpu.VMEM((2 * KH, DN_HEAD, DN_HEAD), F32)],
        compiler_params=_cparams(("arbitrary", "arbitrary", "arbitrary")),
        name="delta_rule",
    )(qk_n, qk_n, vv, sz, beta, gc, gcT, dn_norm_w.reshape(1, DN_HEAD))


def _merge_out_body(a_ref, o_ref, ga_ref, gb_ref, wa_ref, wb_ref, wo_ref, x_ref, g_ref, out_ref):
    @pl.when(pl.program_id(1) == 0)
    def _():
        out_ref[...] = x_ref[...]

    ya = jnp.dot(a_ref[...], wa_ref[...], preferred_element_type=F32)
    yb = jnp.dot(o_ref[...], wb_ref[...], preferred_element_type=F32)
    y = (ga_ref[...].astype(F32) * ya + gb_ref[...].astype(F32) * yb).astype(BF16)
    out_ref[...] += g_ref[0] * jnp.dot(y, wo_ref[...], preferred_element_type=F32)


def _merge_out(a_g, o_g, p_m, wa, wb, wo, x2d, gate, T, tm=512, tn=512):
    N, D = x2d.shape
    B = gate.shape[0]
    nj = D // tn
    tps = T // tm
    return pl.pallas_call(
        _merge_out_body,
        grid=(N // tm, nj),
        in_specs=[pl.BlockSpec((tm, a_g.shape[1]), lambda i, j: (i, 0)),
                  pl.BlockSpec((tm, o_g.shape[1]), lambda i, j: (i, 0)),
                  pl.BlockSpec((tm, tn), lambda i, j: (i, j)),
                  pl.BlockSpec((tm, tn), lambda i, j: (i, nj + j)),
                  pl.BlockSpec((wa.shape[0], tn), lambda i, j: (0, j)),
                  pl.BlockSpec((wb.shape[0], tn), lambda i, j: (0, j)),
                  pl.BlockSpec((tn, D), lambda i, j: (j, 0)),
                  pl.BlockSpec((tm, D), lambda i, j: (i, 0)),
                  pl.BlockSpec((1, 1, D), lambda i, j: (i // tps, 0, 0))],
        out_specs=pl.BlockSpec((tm, D), lambda i, j: (i, 0)),
        out_shape=jax.ShapeDtypeStruct((N, D), F32),
        compiler_params=_cparams(("arbitrary", "arbitrary"), vmem=60 * 1024 * 1024),
        name="merge_out",
    )(a_g, o_g, p_m, p_m, wa, wb, wo, x2d, gate.reshape(B, 1, D))


def _layer(l, x, c, positions, w_ada, b_ada, norm_w, w_in, q_norm_w, k_norm_w, sinks,
           conv_w, a_log, dt_bias, dn_norm_w, w_o_swa, w_o_dn, w_out):
    B, T, D = x.shape
    N = B * T
    x2d = x.reshape(N, D)

    mod = _adaln(c, w_ada, b_ada)
    shift, scale, gate = mod[:, :D], mod[:, D:2 * D], mod[:, 2 * D:]
    h = _normmod(x2d, norm_w, scale, shift, T)

    o_q, o_g = 0, SWA_Q + 2 * SWA_KV
    o_d = o_g + SWA_Q
    o_z = o_d + DN_CONV_CH
    o_b = o_z + DN_VAL
    o_a = o_b + DN_V_HEADS
    o_m = o_a + DN_V_HEADS
    assert o_a == o_b + DN_V_HEADS and o_b % LANES == 0
    w_in = jnp.swapaxes(w_in, 1, 2)
    posf = positions.astype(F32).reshape(N, 1)
    tabs = _rope_tables(posf)
    rep = LANES // SWA_HEAD_DIM
    qw = jnp.tile(q_norm_w.reshape(1, SWA_HEAD_DIM), (1, rep)) * (SWA_HEAD_DIM ** -0.5)
    kw = jnp.tile(k_norm_w.reshape(1, SWA_HEAD_DIM), (1, rep))
    q_n = _proj(h, w_in, l, o_q, SWA_Q, "qknorm", T, extras=(tabs, qw, 2 * 512 // LANES), name="proj_q")
    kv_n = _proj(h, w_in, l, o_q + SWA_Q, 2 * SWA_KV, "qknorm", T, nblk=1,
                 extras=(tabs, kw, SWA_KV // LANES), name="proj_kv")
    sag = _proj(h, w_in, l, o_g, o_d - o_g, "silu", T, tm=2048, name="proj_swa_gate")
    qk_n = _proj(h, w_in, l, o_d, 2 * DN_KEY, "conv", T, extras=(conv_w, 0, True), name="proj_dn_qk")
    vv = _proj(h, w_in, l, o_d + 2 * DN_KEY, DN_VAL, "conv", T, extras=(conv_w, 2 * DN_KEY, False),
               name="proj_dn_v")
    sz = _proj(h, w_in, l, o_z, o_b - o_z, "silu", T, tm=2048, name="proj_dn_z")
    al = jnp.zeros((1, LANES), F32).at[0, :DN_V_HEADS].set(a_log)
    dt = jnp.zeros((1, LANES), F32).at[0, :DN_V_HEADS].set(dt_bias)
    beta, gc = _proj(h, w_in, l, o_b, 2 * DN_V_HEADS, "gates", T, extras=(al, dt), name="proj_dn_gates")
    p_m = _proj(h, w_in, l, o_m, 2 * D, "sigmoid", T, tm=2048, name="proj_merge_gates")

    a_g = _swa(q_n, kv_n, sag, sinks, B, T)

    gcT = gc[:, :DN_V_HEADS].reshape(B, T, DN_V_HEADS).transpose(0, 2, 1)
    o_g2 = _delta(qk_n, vv, sz, beta, gc, gcT, dn_norm_w, B, T)

    out = _merge_out(a_g, o_g2, p_m, w_o_swa.astype(BF16), w_o_dn.astype(BF16), w_out.astype(BF16),
                     x2d, gate, T)
    return out.reshape(B, T, D)


def kernel(x, c, positions, w_ada, b_ada, norm_w, w_in, q_norm_w, k_norm_w, sinks, conv_w,
           a_log, dt_bias, dn_norm_w, w_o_swa, w_o_dn, w_out):
    depth = w_ada.shape[0]
    for l in range(depth):
        x = _layer(l, x, c, positions, w_ada[l], b_ada[l], norm_w[l], w_in, q_norm_w[l],
                   k_norm_w[l], sinks[l], conv_w[l], a_log[l], dt_bias[l], dn_norm_w[l],
                   w_o_swa[l], w_o_dn[l], w_out[l])
    return x
```

```python
import functools

import numpy as np
import jax
import jax.numpy as jnp
from jax import lax
from jax.experimental import pallas as pl
from jax.experimental.pallas import tpu as pltpu

F32 = jnp.float32
BF16 = jnp.bfloat16

D_MODEL = 2048
SWA_Q_HEADS = 32
SWA_KV_HEADS = 4
SWA_HEAD_DIM = 64
SWA_BLOCK = 128
ROPE_THETA = 500000.0
ROPE_DIM = SWA_HEAD_DIM // 4
DN_K_HEADS = 16
DN_V_HEADS = 32
DN_HEAD = 128
DN_CONV = 4
DN_CHUNK = 64
EPS = 1e-6

SWA_Q = SWA_Q_HEADS * SWA_HEAD_DIM
SWA_KV = SWA_KV_HEADS * SWA_HEAD_DIM
DN_KEY = DN_K_HEADS * DN_HEAD
DN_VAL = DN_V_HEADS * DN_HEAD
DN_CONV_CH = 2 * DN_KEY + DN_VAL

LANES = 128
SUBLANES = 8
VMEM_LIMIT = 56 * 1024 * 1024


def _cparams(sem, vmem=VMEM_LIMIT):
    return pltpu.CompilerParams(dimension_semantics=sem, vmem_limit_bytes=vmem)


def _sigmoid(x):
    return 0.5 * jnp.tanh(0.5 * x) + 0.5


def _silu(x):
    h = 0.5 * x
    return h + h * jnp.tanh(h)


def _adaln_body(c_ref, w_ref, b_ref, o_ref):
    s = _silu(c_ref[...]).astype(BF16)
    o_ref[...] = jnp.dot(s, w_ref[...].astype(BF16), preferred_element_type=F32) + b_ref[...]


def _adaln(c, w_ada, b_ada, tn=1024):
    B, D = c.shape
    n_out = w_ada.shape[1]
    c8 = jnp.zeros((SUBLANES, D), F32).at[:B].set(c)
    out = pl.pallas_call(
        _adaln_body,
        grid=(n_out // tn,),
        in_specs=[pl.BlockSpec((SUBLANES, D), lambda j: (0, 0)),
                  pl.BlockSpec((D, tn), lambda j: (0, j)),
                  pl.BlockSpec((1, tn), lambda j: (0, j))],
        out_specs=pl.BlockSpec((SUBLANES, tn), lambda j: (0, j)),
        out_shape=jax.ShapeDtypeStruct((SUBLANES, n_out), F32),
        compiler_params=_cparams(("arbitrary",)),
        name="adaln_mod",
    )(c8, w_ada, b_ada.reshape(1, n_out))
    return out[:B]


def _normmod_body(x_ref, nw_ref, sc_ref, sh_ref, o_ref):
    x = x_ref[...]
    ms = jnp.mean(x * x, axis=-1, keepdims=True)
    y = x * lax.rsqrt(ms + EPS) * nw_ref[...]
    o_ref[...] = (y * (1.0 + sc_ref[0]) + sh_ref[0]).astype(BF16)


def _normmod(x2d, norm_w, scale, shift, T, tm=1024):
    N, D = x2d.shape
    B = scale.shape[0]
    tps = T // tm
    return pl.pallas_call(
        _normmod_body,
        grid=(N // tm,),
        in_specs=[pl.BlockSpec((tm, D), lambda i: (i, 0)),
                  pl.BlockSpec((1, D), lambda i: (0, 0)),
                  pl.BlockSpec((1, 1, D), lambda i: (i // tps, 0, 0)),
                  pl.BlockSpec((1, 1, D), lambda i: (i // tps, 0, 0))],
        out_specs=pl.BlockSpec((tm, D), lambda i: (i, 0)),
        out_shape=jax.ShapeDtypeStruct((N, D), BF16),
        compiler_params=_cparams(("arbitrary",)),
        name="norm_mod",
    )(x2d, norm_w.reshape(1, D), scale.reshape(B, 1, D), shift.reshape(B, 1, D))


HALF = LANES // 2


NT_DIMS = (((1,), (1,)), ((), ()))


def _cast_weights(w_refs, w2_ref, wbf):
    wblk = w_refs[0].shape[0]
    off = 0 if w2_ref is None else HALF
    for b, w_ref in enumerate(w_refs):
        lo = max(b * wblk - off, 0)
        wbf[lo:(b + 1) * wblk - off, :] = w_ref[lo + off - b * wblk:, :].astype(BF16)
    if w2_ref is not None:
        n = len(w_refs) * wblk
        wbf[n - off:n, :] = w2_ref[...].astype(BF16)


def _cast_once(w_refs, w2_ref, wbf):
    @pl.when(pl.program_id(1) == 0)
    def _():
        _cast_weights(w_refs, w2_ref, wbf)


def _proj_act_body(*refs, act, shifted, sub, nblk):
    h_ref, w_refs = refs[0], refs[1:1 + nblk]
    w2_ref = refs[1 + nblk] if shifted else None
    o_ref, wbf = refs[-2:]
    _cast_once(w_refs, w2_ref, wbf)
    for r0 in range(0, h_ref.shape[0], sub):
        acc = lax.dot_general(h_ref[r0:r0 + sub, :], wbf[...], NT_DIMS, preferred_element_type=F32)
        if act == "silu":
            acc = _silu(acc)
        elif act == "sigmoid":
            acc = _sigmoid(acc)
        o_ref[r0:r0 + sub, :] = acc.astype(o_ref.dtype)


def _proj_conv_body(*refs, tm, tps, norm, sub, nblk):
    h_ref, w_refs = refs[0], refs[1:1 + nblk]
    cw_ref, o_ref, wbf, buf = refs[1 + nblk:]
    i = pl.program_id(1)
    first = (i % tps) == 0
    _cast_once(w_refs, None, wbf)

    @pl.when(first)
    def _():
        buf[0:SUBLANES, :] = jnp.zeros((SUBLANES, buf.shape[1]), F32)

    @pl.when(jnp.logical_not(first))
    def _():
        buf[0:SUBLANES, :] = buf[tm:tm + SUBLANES, :]

    cw = cw_ref[...]
    mm = lambda r0: lax.dot_general(h_ref[r0:r0 + sub, :], wbf[...], NT_DIMS, preferred_element_type=F32)
    nxt = mm(0)
    for r0 in range(0, tm, sub):
        acc = nxt
        if r0 + sub < tm:
            nxt = mm(r0 + sub)
        b0 = SUBLANES + r0
        buf[b0:b0 + sub, :] = acc
        win = buf[b0 - SUBLANES:b0 + sub, :]
        y = win[SUBLANES:, :] * cw[DN_CONV - 1:DN_CONV, :]
        for d in range(1, DN_CONV):
            y = y + pltpu.roll(win, d, 0)[SUBLANES:, :] * cw[DN_CONV - 1 - d:DN_CONV - d, :]
        y = _silu(y)
        for c in range(y.shape[1] // DN_HEAD):
            yc = y[:, c * DN_HEAD:(c + 1) * DN_HEAD]
            if norm:
                yc = yc * lax.rsqrt(jnp.sum(yc * yc, axis=-1, keepdims=True) + EPS)
            o_ref[r0:r0 + sub, c * DN_HEAD:(c + 1) * DN_HEAD] = yc.astype(o_ref.dtype)


def _norm_rope(x, nw, cs, s1, s2, lo):
    half = ROPE_DIM // 2
    x2 = x * x
    s_lo = jnp.sum(jnp.where(lo, x2, 0.0), axis=-1, keepdims=True)
    s_hi = jnp.sum(jnp.where(lo, 0.0, x2), axis=-1, keepdims=True)
    r = jnp.where(lo, lax.rsqrt(s_lo * (1.0 / SWA_HEAD_DIM) + EPS),
                  lax.rsqrt(s_hi * (1.0 / SWA_HEAD_DIM) + EPS))
    xn = x * r * nw
    return xn * cs + pltpu.roll(xn, half, 1) * s1 + pltpu.roll(xn, LANES - half, 1) * s2


def _proj_qknorm_body(*refs, sub, nblk, n_norm):
    h_ref, w_refs = refs[0], refs[1:1 + nblk]
    c_ref, s1_ref, s2_ref, nw_ref, o_ref, wbf = refs[1 + nblk:]
    _cast_once(w_refs, None, wbf)
    lo = lax.broadcasted_iota(jnp.int32, (1, LANES), 1) < SWA_HEAD_DIM
    nw = nw_ref[...]
    for r0 in range(0, h_ref.shape[0], sub):
        acc = lax.dot_general(h_ref[r0:r0 + sub, :], wbf[...], NT_DIMS, preferred_element_type=F32)
        cs, s1, s2 = c_ref[r0:r0 + sub, :], s1_ref[r0:r0 + sub, :], s2_ref[r0:r0 + sub, :]
        for c in range(acc.shape[1] // LANES):
            x = acc[:, c * LANES:(c + 1) * LANES]
            if c < n_norm:
                x = _norm_rope(x, nw, cs, s1, s2, lo)
            o_ref[r0:r0 + sub, c * LANES:(c + 1) * LANES] = x.astype(o_ref.dtype)


def _proj_gates_body(h_ref, w_ref, al_ref, dt_ref, beta_ref, gc_ref, wbf):
    _cast_once([w_ref], None, wbf)
    acc = lax.dot_general(h_ref[...], wbf[...], NT_DIMS, preferred_element_type=F32)
    beta_ref[...] = _sigmoid(acc)
    x = pltpu.roll(acc, LANES - DN_V_HEADS, 1) + dt_ref[...]
    sp = jnp.maximum(x, 0.0) + jnp.log(1.0 + jnp.exp(-jnp.abs(x)))
    g = -jnp.exp(al_ref[...]) * sp
    r64 = lax.broadcasted_iota(jnp.int32, g.shape, 0) & (DN_CHUNK - 1)
    s = 1
    while s < DN_CHUNK:
        g = g + jnp.where(r64 >= s, pltpu.roll(g, s, 0), 0.0)
        s *= 2
    gc_ref[...] = g


def _proj(h, w_t, layer, col0, width, mode, T, extras=(), tm=1024, wblk=512, nblk=2, sub=256,
          name="proj"):
    N, K = h.shape
    out_dtype = BF16
    tm = min(tm, T)
    sub = min(sub, tm)
    off = col0 % LANES
    base = col0 - off
    shifted = off != 0
    if mode == "gates":
        wblk, nblk = LANES, 1
    tn = wblk * nblk
    assert off in (0, HALF) and base % wblk == 0
    assert mode == "gates" or width % tn == 0
    jb = base // wblk
    grid = (max(width // tn, 1), N // tm)
    in_specs = [pl.BlockSpec((tm, K), lambda j, i: (i, 0))]
    in_specs += [pl.BlockSpec((None, wblk, K), lambda j, i, b=b: (layer, jb + j * nblk + b, 0))
                 for b in range(nblk)]
    args = [h] + [w_t] * nblk
    scratch = [pltpu.VMEM((tn, K), BF16)]
    out_spec = pl.BlockSpec((tm, tn), lambda j, i: (i, j))
    out_shape = jax.ShapeDtypeStruct((N, max(width, tn)), out_dtype)
    if shifted:
        in_specs.append(pl.BlockSpec((None, HALF, K),
                                     lambda j, i: (layer, (base + (j + 1) * tn) // HALF, 0)))
        args.append(w_t)
    if mode == "conv":
        cw, cw_col0, norm = extras
        in_specs.append(pl.BlockSpec((DN_CONV, tn), lambda j, i: (0, cw_col0 // tn + j)))
        args.append(cw)
        scratch.append(pltpu.VMEM((tm + SUBLANES, tn), F32))
        body = functools.partial(_proj_conv_body, tm=tm, tps=T // tm, norm=norm, sub=sub, nblk=nblk)
    elif mode == "qknorm":
        tabs, nw, n_norm = extras
        in_specs += [pl.BlockSpec((tm, LANES), lambda j, i: (i, 0))] * 3
        in_specs.append(pl.BlockSpec((1, LANES), lambda j, i: (0, 0)))
        args += list(tabs) + [nw]
        body = functools.partial(_proj_qknorm_body, sub=sub, nblk=nblk, n_norm=n_norm)
    elif mode == "gates":
        in_specs += [pl.BlockSpec((1, LANES), lambda j, i: (0, 0))] * 2
        args += list(extras)
        body = _proj_gates_body
        out_spec = [out_spec, out_spec]
        out_shape = [jax.ShapeDtypeStruct((N, LANES), F32)] * 2
    else:
        body = functools.partial(_proj_act_body, act=mode, shifted=shifted, sub=sub if mode else tm,
                                 nblk=nblk)
    return pl.pallas_call(
        body, grid=grid, in_specs=in_specs, out_specs=out_spec, out_shape=out_shape,
        scratch_shapes=scratch,
        compiler_params=_cparams(("arbitrary", "arbitrary")),
        name=name,
    )(*args)


def _rope_inv_freq_lanes():
    half = ROPE_DIM // 2
    inv = (np.float32(ROPE_THETA) ** (-np.arange(half, dtype=np.float32) * np.float32(2.0 / ROPE_DIM))).astype(np.float32)
    lanes = np.zeros((1, LANES), np.float32)
    for l in range(LANES):
        m = l % SWA_HEAD_DIM
        if m < ROPE_DIM:
            lanes[0, l] = inv[m % half]
    return lanes


def _rope_tab_body(pos_ref, invf_ref, c_ref, s1_ref, s2_ref):
    half = ROPE_DIM // 2
    l64 = lax.broadcasted_iota(jnp.int32, (1, LANES), 1) & (SWA_HEAD_DIM - 1)
    ang = pos_ref[...] * invf_ref[...]
    sn = jnp.sin(ang)
    c_ref[...] = jnp.cos(ang)
    s1_ref[...] = jnp.where((l64 >= half) & (l64 < ROPE_DIM), sn, 0.0)
    s2_ref[...] = jnp.where(l64 < half, -sn, 0.0)


def _rope_tables(posf, tm=2048):
    N = posf.shape[0]
    tm = min(tm, N)
    spec = pl.BlockSpec((tm, LANES), lambda i: (i, 0))
    return pl.pallas_call(
        _rope_tab_body,
        grid=(N // tm,),
        in_specs=[pl.BlockSpec((tm, 1), lambda i: (i, 0)), pl.BlockSpec((1, LANES), lambda i: (0, 0))],
        out_specs=[spec] * 3,
        out_shape=[jax.ShapeDtypeStruct((N, LANES), F32)] * 3,
        compiler_params=_cparams(("arbitrary",)),
        name="rope_tables",
    )(posf, jnp.asarray(_rope_inv_freq_lanes()))


def _swa_body(sink_ref, q_ref, kv_ref, g_ref, o_ref, kprev, vprev):
    n = pl.program_id(1)
    BLK = SWA_BLOCK
    HD = SWA_HEAD_DIM

    @pl.when(n == 0)
    def _():
        kprev[...] = jnp.zeros_like(kprev)
        vprev[...] = jnp.zeros_like(vprev)

    lo = lax.broadcasted_iota(jnp.int32, (1, LANES), 1) < HD
    n_kc = SWA_KV // LANES
    nblocks = q_ref.shape[0] // BLK
    zrows = jnp.zeros((HD, BLK), F32)

    kband, vbandT = {}, {}
    k_last = {idx: kprev[idx] for idx in range(2 * SWA_KV_HEADS)}
    v_last = {idx: vprev[idx] for idx in range(2 * SWA_KV_HEADS)}
    for bi in range(nblocks):
        kv = kv_ref[bi * BLK:(bi + 1) * BLK, :]
        kc = [kv[:, c * LANES:(c + 1) * LANES].astype(F32) for c in range(n_kc)]
        vT = [kv[:, SWA_KV + c * LANES:SWA_KV + (c + 1) * LANES].astype(F32).T for c in range(n_kc)]
        for j in range(SWA_KV_HEADS):
            cj, b = j // 2, j % 2
            for a in range(2):
                src = kc[cj] if a == b else pltpu.roll(kc[cj], HD, 1)
                k_cur = jnp.where(lo if a == 0 else jnp.logical_not(lo), src, 0.0).astype(BF16)
                rows = vT[cj][b * HD:(b + 1) * HD]
                v_cur = jnp.concatenate([rows, zrows] if a == 0 else [zrows, rows], axis=0).astype(BF16)
                idx = 2 * j + a
                kband[(bi, j, a)] = jnp.concatenate([k_last[idx], k_cur], axis=0)
                vbandT[(bi, j, a)] = jnp.concatenate([v_last[idx], v_cur], axis=1)
                k_last[idx], v_last[idx] = k_cur, v_cur
    for idx in range(2 * SWA_KV_HEADS):
        kprev[idx] = k_last[idx]
        vprev[idx] = v_last[idx]

    kj = lax.broadcasted_iota(jnp.int32, (2 * BLK, 2 * BLK), 0)
    qi = lax.broadcasted_iota(jnp.int32, (2 * BLK, 2 * BLK), 1) & (BLK - 1)
    d = kj - qi
    in_window = (d >= 1) & (d <= BLK)
    kmin = jnp.where(n == 0, BLK, 0)
    valid = [in_window & (kj >= kmin)] + [in_window] * (nblocks - 1)
    left = lax.broadcasted_iota(jnp.int32, (1, 2 * BLK), 1) < BLK

    G = SWA_Q_HEADS // SWA_KV_HEADS
    cpk = G // 2
    items = [(bi, j, a, j * cpk + 2 * p) for bi in range(nblocks) for j in range(SWA_KV_HEADS)
             for a in range(2) for p in range(cpk // 2)]

    def scores(item):
        bi, j, a, c1 = item
        rows = slice(bi * BLK, (bi + 1) * BLK)
        rhs = jnp.concatenate([q_ref[rows, c1 * LANES:(c1 + 1) * LANES],
                               q_ref[rows, (c1 + 1) * LANES:(c1 + 2) * LANES]], axis=0)
        return lax.dot_general(kband[(bi, j, a)], rhs, NT_DIMS, preferred_element_type=F32)

    acc = {}

    def consume(item, oT, inv):
        bi, j, a, c1 = item
        rows = slice(bi * BLK, (bi + 1) * BLK)
        oT = oT * inv
        for c, part in ((c1, oT[:, :BLK]), (c1 + 1, oT[:, BLK:])):
            if a == 0:
                acc[c] = part
            else:
                gate = g_ref[rows, c * LANES:(c + 1) * LANES].astype(F32)
                o_ref[rows, c * LANES:(c + 1) * LANES] = ((acc.pop(c) + part).T * gate).astype(BF16)

    s_cur = scores(items[0])
    pending = None
    for i, item in enumerate(items):
        bi, j, a, c1 = item
        s_next = scores(items[i + 1]) if i + 1 < len(items) else None
        s = jnp.where(valid[bi], s_cur, -jnp.inf)
        sink = jnp.where(left, sink_ref[2 * c1 + a], sink_ref[2 * c1 + 2 + a])
        m = jnp.maximum(jnp.max(s, axis=0, keepdims=True), sink)
        e = jnp.exp(s - m)
        inv = 1.0 / (jnp.sum(e, axis=0, keepdims=True) + jnp.exp(sink - m))
        oT = jnp.dot(vbandT[(bi, j, a)], e.astype(BF16), preferred_element_type=F32)
        if pending is not None:
            consume(*pending)
        pending = (item, oT, inv)
        s_cur = s_next
    consume(*pending)


def _swa(q_n, kv_n, sag, sinks, B, T, blocks_per_step=2):
    N = B * T
    rows = blocks_per_step * SWA_BLOCK
    nb = T // rows
    row = lambda b, n: b * nb + n
    return pl.pallas_call(
        _swa_body,
        grid=(B, nb),
        in_specs=[pl.BlockSpec(memory_space=pltpu.SMEM),
                  pl.BlockSpec((rows, SWA_Q), lambda b, n: (row(b, n), 0)),
                  pl.BlockSpec((rows, 2 * SWA_KV), lambda b, n: (row(b, n), 0)),
                  pl.BlockSpec((rows, SWA_Q), lambda b, n: (row(b, n), 0))],
        out_specs=pl.BlockSpec((rows, SWA_Q), lambda b, n: (row(b, n), 0)),
        out_shape=jax.ShapeDtypeStruct((N, SWA_Q), BF16),
        scratch_shapes=[pltpu.VMEM((2 * SWA_KV_HEADS, SWA_BLOCK, LANES), BF16),
                        pltpu.VMEM((2 * SWA_KV_HEADS, LANES, SWA_BLOCK), BF16)],
        compiler_params=_cparams(("arbitrary", "arbitrary")),
        name="swa_attn",
    )(sinks, q_n, kv_n, sag)


def _delta_body(q_ref, k_ref, v_ref, z_ref, bc_ref, gc_ref, gr_ref, nw_ref, o_ref, S_ref,
                *, KH, TT):
    grp = pl.program_id(1)
    t = pl.program_id(2)
    C = DN_CHUNK
    NC = TT // C
    scale = DN_HEAD ** -0.5
    NT = (((1,), (1,)), ((), ()))

    @pl.when(t == 0)
    def _():
        S_ref[...] = jnp.zeros_like(S_ref)

    NV = 2 * KH
    row = lax.broadcasted_iota(jnp.int32, (TT, TT), 0)
    col = lax.broadcasted_iota(jnp.int32, (TT, TT), 1)
    same = (row // C) == (col // C)
    ri = lax.broadcasted_iota(jnp.int32, (C, TT), 0)
    cj = lax.broadcasted_iota(jnp.int32, (C, TT), 1)
    cjc = cj // C
    cjj = cj & (C - 1)
    tril_c = ri >= cjj
    strict_c = ri > cjj
    eye_c = (ri == cjj).astype(F32)
    lane = lax.broadcasted_iota(jnp.int32, (TT, LANES), 1)
    lchunk = lax.broadcasted_iota(jnp.int32, (1, TT), 1) // C
    bcall = bc_ref[...]
    gcall = gc_ref[...]
    nw = nw_ref[...]

    def compress(x):
        out = x[(NC - 1) * C:NC * C]
        for c in range(NC - 2, -1, -1):
            out = jnp.where(cjc == c, x[c * C:(c + 1) * C], out)
        return out

    def by_row(colvec):
        out = jnp.broadcast_to(colvec[(NC - 1) * C:NC * C], (C, TT))
        for c in range(NC - 2, -1, -1):
            out = jnp.where(cjc == c, colvec[c * C:(c + 1) * C], out)
        return out

    def bd(xc):
        return jnp.where(same, jnp.concatenate([xc] * NC, axis=0), 0.0).astype(BF16)

    kf, qf, kT, kkc, qkc = [], [], [], [], []
    for kh in range(KH):
        q = q_ref[:, kh * DN_HEAD:(kh + 1) * DN_HEAD]
        k = k_ref[:, kh * DN_HEAD:(kh + 1) * DN_HEAD]
        kq = lax.dot_general(jnp.concatenate([k, q], axis=0), k, NT, preferred_element_type=F32)
        kkc.append(compress(kq[:TT]))
        qkc.append(compress(kq[TT:]))
        kf.append(k.astype(F32))
        qf.append(q.astype(F32))
        kT.append(kf[-1].T)

    bcol, gcol, grow, Lc, qkm, P = [], [], [], [], [], []
    for hh in range(NV):
        hglob = grp * NV + hh
        sel = lane == hglob
        bcol.append(jnp.sum(jnp.where(sel, bcall, 0.0), axis=-1, keepdims=True))
        gcol.append(jnp.sum(jnp.where(sel, gcall, 0.0), axis=-1, keepdims=True))
        grow.append(gr_ref[0, pl.ds(hglob, 1), :])
        dec = jnp.exp(jnp.where(tril_c, by_row(gcol[hh]) - grow[hh], -jnp.inf))
        Lc.append(jnp.where(strict_c, by_row(bcol[hh]) * kkc[hh // 2] * dec, 0.0))
        qkm.append(qkc[hh // 2] * dec * scale)
        P.append(eye_c - Lc[hh])

    Lp = [jnp.dot(Lc[hh].astype(BF16), bd(Lc[hh]), preferred_element_type=F32) for hh in range(NV)]
    step = 4
    while step < C:
        res = [jnp.dot(jnp.concatenate([Lp[hh], P[hh]], axis=0).astype(BF16), bd(Lp[hh]),
                       preferred_element_type=F32) for hh in range(NV)]
        Lp = [r[:C] for r in res]
        P = [P[hh] + res[hh][C:] for hh in range(NV)]
        step *= 2
    P = [P[hh] + jnp.dot(P[hh].astype(BF16), bd(Lp[hh]), preferred_element_type=F32) for hh in range(NV)]

    uw, qd, kdT, eg_last = [], [], [], []
    for hh in range(NV):
        eg = jnp.exp(gcol[hh])
        vf = v_ref[:, hh * DN_HEAD:(hh + 1) * DN_HEAD].astype(F32)
        rhs = jnp.concatenate([vf * bcol[hh], kf[hh // 2] * (bcol[hh] * eg)], axis=1).astype(BF16)
        uw.append(jnp.dot(bd(P[hh]), rhs, preferred_element_type=F32))
        qd.append((qf[hh // 2] * (eg * scale)).astype(BF16))
        glrow = jnp.zeros((1, TT), F32)
        for c in range(NC):
            glrow = jnp.where(lchunk == c, gcol[hh][c * C + C - 1:c * C + C, :], glrow)
        kdT.append(kT[hh // 2] * jnp.exp(glrow - grow[hh]))
        eg_last.append([eg[c * C + C - 1:c * C + C, :] for c in range(NC)])

    S = [S_ref[hh] for hh in range(NV)]
    vn = [[] for _ in range(NV)]
    o_inter = [[] for _ in range(NV)]
    for c in range(NC):
        r0, r1 = c * C, (c + 1) * C
        for hh in range(NV):
            w_c = uw[hh][r0:r1, DN_HEAD:].astype(BF16)
            lhs = jnp.concatenate([w_c, qd[hh][r0:r1]], axis=0)
            res = jnp.dot(lhs, S[hh].astype(BF16), preferred_element_type=F32)
            vn_c = (uw[hh][r0:r1, :DN_HEAD] - res[:C]).astype(BF16)
            vn[hh].append(vn_c)
            o_inter[hh].append(res[C:])
            kd_c = kdT[hh][:, r0:r1].astype(BF16)
            S[hh] = S[hh] * eg_last[hh][c] + jnp.dot(kd_c, vn_c, preferred_element_type=F32)

    for hh in range(NV):
        S_ref[hh] = S[hh]
        vn_all = jnp.concatenate(vn[hh], axis=0)
        o = jnp.concatenate(o_inter[hh], axis=0) + jnp.dot(bd(qkm[hh]), vn_all, preferred_element_type=F32)
        ms = jnp.mean(o * o, axis=-1, keepdims=True)
        z = z_ref[:, hh * DN_HEAD:(hh + 1) * DN_HEAD].astype(F32)
        o_ref[:, hh * DN_HEAD:(hh + 1) * DN_HEAD] = (o * lax.rsqrt(ms + EPS) * nw * z).astype(BF16)


def _delta(qk_n, vv, sz, beta, gc, gcT, dn_norm_w, B, T, KH=8, TT=256):
    N = B * T
    nt = T // TT
    ng = DN_K_HEADS // KH
    row = lambda b, g, t: b * nt + t
    kw = KH * DN_HEAD
    return pl.pallas_call(
        functools.partial(_delta_body, KH=KH, TT=TT),
        grid=(B, ng, nt),
        in_specs=[pl.BlockSpec((TT, kw), lambda b, g, t: (row(b, g, t), g)),
                  pl.BlockSpec((TT, kw), lambda b, g, t: (row(b, g, t), DN_KEY // kw + g)),
                  pl.BlockSpec((TT, 2 * kw), lambda b, g, t: (row(b, g, t), g)),
                  pl.BlockSpec((TT, 2 * kw), lambda b, g, t: (row(b, g, t), g)),
                  pl.BlockSpec((TT, LANES), lambda b, g, t: (row(b, g, t), 0)),
                  pl.BlockSpec((TT, LANES), lambda b, g, t: (row(b, g, t), 0)),
                  pl.BlockSpec((1, DN_V_HEADS, TT), lambda b, g, t: (b, 0, t)),
                  pl.BlockSpec((1, DN_HEAD), lambda b, g, t: (0, 0))],
        out_specs=pl.BlockSpec((TT, 2 * kw), lambda b, g, t: (row(b, g, t), g)),
        out_shape=jax.ShapeDtypeStruct((N, DN_VAL), BF16),
        scratch_shapes=[pltpu.VMEM((2 * KH, DN_HEAD, DN_HEAD), F32)],
        compiler_params=_cparams(("arbitrary", "arbitrary", "arbitrary")),
        name="delta_rule",
    )(qk_n, qk_n, vv, sz, beta, gc, gcT, dn_norm_w.reshape(1, DN_HEAD))


def _merge_out_body(a_ref, o_ref, ga_ref, gb_ref, wa_ref, wb_ref, wo_ref, x_ref, g_ref, out_ref):
    @pl.when(pl.program_id(1) == 0)
    def _():
        out_ref[...] = x_ref[...]

    ya = jnp.dot(a_ref[...], wa_ref[...], preferred_element_type=F32)
    yb = jnp.dot(o_ref[...], wb_ref[...], preferred_element_type=F32)
    y = (ga_ref[...].astype(F32) * ya + gb_ref[...].astype(F32) * yb).astype(BF16)
    out_ref[...] += g_ref[0] * jnp.dot(y, wo_ref[...], preferred_element_type=F32)


def _merge_out(a_g, o_g, p_m, wa, wb, wo, x2d, gate, T, tm=512, tn=512):
    N, D = x2d.shape
    B = gate.shape[0]
    nj = D // tn
    tps = T // tm
    return pl.pallas_call(
        _merge_out_body,
        grid=(N // tm, nj),
        in_specs=[pl.BlockSpec((tm, a_g.shape[1]), lambda i, j: (i, 0)),
                  pl.BlockSpec((tm, o_g.shape[1]), lambda i, j: (i, 0)),
                  pl.BlockSpec((tm, tn), lambda i, j: (i, j)),
                  pl.BlockSpec((tm, tn), lambda i, j: (i, nj + j)),
                  pl.BlockSpec((wa.shape[0], tn), lambda i, j: (0, j)),
                  pl.BlockSpec((wb.shape[0], tn), lambda i, j: (0, j)),
                  pl.BlockSpec((tn, D), lambda i, j: (j, 0)),
                  pl.BlockSpec((tm, D), lambda i, j: (i, 0)),
                  pl.BlockSpec((1, 1, D), lambda i, j: (i // tps, 0, 0))],
        out_specs=pl.BlockSpec((tm, D), lambda i, j: (i, 0)),
        out_shape=jax.ShapeDtypeStruct((N, D), F32),
        compiler_params=_cparams(("arbitrary", "arbitrary"), vmem=60 * 1024 * 1024),
        name="merge_out",
    )(a_g, o_g, p_m, p_m, wa, wb, wo, x2d, gate.reshape(B, 1, D))


def _layer(l, x, c, positions, w_ada, b_ada, norm_w, w_in, q_norm_w, k_norm_w, sinks,
           conv_w, a_log, dt_bias, dn_norm_w, w_o_swa, w_o_dn, w_out):
    B, T, D = x.shape
    N = B * T
    x2d = x.reshape(N, D)

    mod = _adaln(c, w_ada, b_ada)
    shift, scale, gate = mod[:, :D], mod[:, D:2 * D], mod[:, 2 * D:]
    h = _normmod(x2d, norm_w, scale, shift, T)

    o_q, o_g = 0, SWA_Q + 2 * SWA_KV
    o_d = o_g + SWA_Q
    o_z = o_d + DN_CONV_CH
    o_b = o_z + DN_VAL
    o_a = o_b + DN_V_HEADS
    o_m = o_a + DN_V_HEADS
    assert o_a == o_b + DN_V_HEADS and o_b % LANES == 0
    w_in = jnp.swapaxes(w_in, 1, 2)
    posf = positions.astype(F32).reshape(N, 1)
    tabs = _rope_tables(posf)
    rep = LANES // SWA_HEAD_DIM
    qw = jnp.tile(q_norm_w.reshape(1, SWA_HEAD_DIM), (1, rep)) * (SWA_HEAD_DIM ** -0.5)
    kw = jnp.tile(k_norm_w.reshape(1, SWA_HEAD_DIM), (1, rep))
    q_n = _proj(h, w_in, l, o_q, SWA_Q, "qknorm", T, extras=(tabs, qw, 2 * 512 // LANES), tm=2048, name="proj_q")
    kv_n = _proj(h, w_in, l, o_q + SWA_Q, 2 * SWA_KV, "qknorm", T, nblk=1,
                 extras=(tabs, kw, SWA_KV // LANES), name="proj_kv")
    sag = _proj(h, w_in, l, o_g, o_d - o_g, "silu", T, tm=2048, name="proj_swa_gate")
    qk_n = _proj(h, w_in, l, o_d, 2 * DN_KEY, "conv", T, extras=(conv_w, 0, True), name="proj_dn_qk")
    vv = _proj(h, w_in, l, o_d + 2 * DN_KEY, DN_VAL, "conv", T, extras=(conv_w, 2 * DN_KEY, False),
               name="proj_dn_v")
    sz = _proj(h, w_in, l, o_z, o_b - o_z, "silu", T, tm=2048, name="proj_dn_z")
    al = jnp.zeros((1, LANES), F32).at[0, :DN_V_HEADS].set(a_log)
    dt = jnp.zeros((1, LANES), F32).at[0, :DN_V_HEADS].set(dt_bias)
    beta, gc = _proj(h, w_in, l, o_b, 2 * DN_V_HEADS, "gates", T, extras=(al, dt), name="proj_dn_gates")
    p_m = _proj(h, w_in, l, o_m, 2 * D, "sigmoid", T, tm=2048, name="proj_merge_gates")

    a_g = _swa(q_n, kv_n, sag, sinks, B, T)

    gcT = gc[:, :DN_V_HEADS].reshape(B, T, DN_V_HEADS).transpose(0, 2, 1)
    o_g2 = _delta(qk_n, vv, sz, beta, gc, gcT, dn_norm_w, B, T)

    out = _merge_out(a_g, o_g2, p_m, w_o_swa.astype(BF16), w_o_dn.astype(BF16), w_out.astype(BF16),
                     x2d, gate, T)
    return out.reshape(B, T, D)


def kernel(x, c, positions, w_ada, b_ada, norm_w, w_in, q_norm_w, k_norm_w, sinks, conv_w,
           a_log, dt_bias, dn_norm_w, w_o_swa, w_o_dn, w_out):
    depth = w_ada.shape[0]
    for l in range(depth):
        x = _layer(l, x, c, positions, w_ada[l], b_ada[l], norm_w[l], w_in, q_norm_w[l],
                   k_norm_w[l], sinks[l], conv_w[l], a_log[l], dt_bias[l], dn_norm_w[l],
                   w_o_swa[l], w_o_dn[l], w_out[l])
    return x
```

```python
import functools

import numpy as np
import jax
import jax.numpy as jnp
from jax import lax
from jax.experimental import pallas as pl
from jax.experimental.pallas import tpu as pltpu

F32 = jnp.float32
BF16 = jnp.bfloat16

D_MODEL = 2048
SWA_Q_HEADS = 32
SWA_KV_HEADS = 4
SWA_HEAD_DIM = 64
SWA_BLOCK = 128
ROPE_THETA = 500000.0
ROPE_DIM = SWA_HEAD_DIM // 4
DN_K_HEADS = 16
DN_V_HEADS = 32
DN_HEAD = 128
DN_CONV = 4
DN_CHUNK = 64
EPS = 1e-6

SWA_Q = SWA_Q_HEADS * SWA_HEAD_DIM
SWA_KV = SWA_KV_HEADS * SWA_HEAD_DIM
DN_KEY = DN_K_HEADS * DN_HEAD
DN_VAL = DN_V_HEADS * DN_HEAD
DN_CONV_CH = 2 * DN_KEY + DN_VAL

LANES = 128
SUBLANES = 8
VMEM_LIMIT = 56 * 1024 * 1024


def _cparams(sem, vmem=VMEM_LIMIT):
    return pltpu.CompilerParams(dimension_semantics=sem, vmem_limit_bytes=vmem)


def _sigmoid(x):
    return 0.5 * jnp.tanh(0.5 * x) + 0.5


def _silu(x):
    h = 0.5 * x
    return h + h * jnp.tanh(h)


def _adaln_body(c_ref, w_ref, b_ref, o_ref):
    s = _silu(c_ref[...]).astype(BF16)
    o_ref[...] = jnp.dot(s, w_ref[...].astype(BF16), preferred_element_type=F32) + b_ref[...]


def _adaln(c, w_ada, b_ada, tn=1024):
    B, D = c.shape
    n_out = w_ada.shape[1]
    c8 = jnp.zeros((SUBLANES, D), F32).at[:B].set(c)
    out = pl.pallas_call(
        _adaln_body,
        grid=(n_out // tn,),
        in_specs=[pl.BlockSpec((SUBLANES, D), lambda j: (0, 0)),
                  pl.BlockSpec((D, tn), lambda j: (0, j)),
                  pl.BlockSpec((1, tn), lambda j: (0, j))],
        out_specs=pl.BlockSpec((SUBLANES, tn), lambda j: (0, j)),
        out_shape=jax.ShapeDtypeStruct((SUBLANES, n_out), F32),
        compiler_params=_cparams(("arbitrary",)),
        name="adaln_mod",
    )(c8, w_ada, b_ada.reshape(1, n_out))
    return out[:B]


def _normmod_body(x_ref, nw_ref, sc_ref, sh_ref, pos_ref, invf_ref, o_ref, c_ref, s1_ref, s2_ref):
    x = x_ref[...]
    ms = jnp.mean(x * x, axis=-1, keepdims=True)
    y = x * lax.rsqrt(ms + EPS) * nw_ref[...]
    o_ref[...] = (y * (1.0 + sc_ref[0]) + sh_ref[0]).astype(BF16)

    half = ROPE_DIM // 2
    l64 = lax.broadcasted_iota(jnp.int32, (1, LANES), 1) & (SWA_HEAD_DIM - 1)
    ang = pos_ref[...] * invf_ref[...]
    sn = jnp.sin(ang)
    c_ref[...] = jnp.cos(ang)
    s1_ref[...] = jnp.where((l64 >= half) & (l64 < ROPE_DIM), sn, 0.0)
    s2_ref[...] = jnp.where(l64 < half, -sn, 0.0)


def _normmod(x2d, norm_w, scale, shift, posf, T, tm=1024):
    N, D = x2d.shape
    B = scale.shape[0]
    tm = min(tm, T)
    tps = T // tm
    tab = pl.BlockSpec((tm, LANES), lambda i: (i, 0))
    outs = pl.pallas_call(
        _normmod_body,
        grid=(N // tm,),
        in_specs=[pl.BlockSpec((tm, D), lambda i: (i, 0)),
                  pl.BlockSpec((1, D), lambda i: (0, 0)),
                  pl.BlockSpec((1, 1, D), lambda i: (i // tps, 0, 0)),
                  pl.BlockSpec((1, 1, D), lambda i: (i // tps, 0, 0)),
                  pl.BlockSpec((tm, 1), lambda i: (i, 0)),
                  pl.BlockSpec((1, LANES), lambda i: (0, 0))],
        out_specs=[pl.BlockSpec((tm, D), lambda i: (i, 0)), tab, tab, tab],
        out_shape=[jax.ShapeDtypeStruct((N, D), BF16)] + [jax.ShapeDtypeStruct((N, LANES), F32)] * 3,
        compiler_params=_cparams(("arbitrary",)),
        name="norm_mod",
    )(x2d, norm_w.reshape(1, D), scale.reshape(B, 1, D), shift.reshape(B, 1, D), posf,
      jnp.asarray(_rope_inv_freq_lanes()))
    return outs[0], tuple(outs[1:])


HALF = LANES // 2


NT_DIMS = (((1,), (1,)), ((), ()))


def _cast_weights(w_refs, w2_ref, wbf):
    wblk = w_refs[0].shape[0]
    off = 0 if w2_ref is None else HALF
    for b, w_ref in enumerate(w_refs):
        lo = max(b * wblk - off, 0)
        wbf[lo:(b + 1) * wblk - off, :] = w_ref[lo + off - b * wblk:, :].astype(BF16)
    if w2_ref is not None:
        n = len(w_refs) * wblk
        wbf[n - off:n, :] = w2_ref[...].astype(BF16)


def _cast_once(w_refs, w2_ref, wbf):
    @pl.when(pl.program_id(1) == 0)
    def _():
        _cast_weights(w_refs, w2_ref, wbf)


def _proj_act_body(*refs, act, shifted, sub, nblk):
    h_ref, w_refs = refs[0], refs[1:1 + nblk]
    w2_ref = refs[1 + nblk] if shifted else None
    o_ref, wbf = refs[-2:]
    _cast_once(w_refs, w2_ref, wbf)
    for r0 in range(0, h_ref.shape[0], sub):
        acc = lax.dot_general(h_ref[r0:r0 + sub, :], wbf[...], NT_DIMS, preferred_element_type=F32)
        if act == "silu":
            acc = _silu(acc)
        elif act == "sigmoid":
            acc = _sigmoid(acc)
        o_ref[r0:r0 + sub, :] = acc.astype(o_ref.dtype)


def _proj_conv_body(*refs, tm, tps, norm, sub, nblk):
    h_ref, w_refs = refs[0], refs[1:1 + nblk]
    cw_ref, o_ref, wbf, buf = refs[1 + nblk:]
    i = pl.program_id(1)
    first = (i % tps) == 0
    _cast_once(w_refs, None, wbf)

    @pl.when(first)
    def _():
        buf[0:SUBLANES, :] = jnp.zeros((SUBLANES, buf.shape[1]), F32)

    @pl.when(jnp.logical_not(first))
    def _():
        buf[0:SUBLANES, :] = buf[tm:tm + SUBLANES, :]

    cw = cw_ref[...]
    mm = lambda r0: lax.dot_general(h_ref[r0:r0 + sub, :], wbf[...], NT_DIMS, preferred_element_type=F32)
    nxt = mm(0)
    for r0 in range(0, tm, sub):
        acc = nxt
        if r0 + sub < tm:
            nxt = mm(r0 + sub)
        b0 = SUBLANES + r0
        buf[b0:b0 + sub, :] = acc
        win = buf[b0 - SUBLANES:b0 + sub, :]
        y = win[SUBLANES:, :] * cw[DN_CONV - 1:DN_CONV, :]
        for d in range(1, DN_CONV):
            y = y + pltpu.roll(win, d, 0)[SUBLANES:, :] * cw[DN_CONV - 1 - d:DN_CONV - d, :]
        y = _silu(y)
        for c in range(y.shape[1] // DN_HEAD):
            yc = y[:, c * DN_HEAD:(c + 1) * DN_HEAD]
            if norm:
                yc = yc * lax.rsqrt(jnp.sum(yc * yc, axis=-1, keepdims=True) + EPS)
            o_ref[r0:r0 + sub, c * DN_HEAD:(c + 1) * DN_HEAD] = yc.astype(o_ref.dtype)


def _norm_rope(x, nw, cs, s1, s2, lo):
    half = ROPE_DIM // 2
    x2 = x * x
    s_lo = jnp.sum(jnp.where(lo, x2, 0.0), axis=-1, keepdims=True)
    s_hi = jnp.sum(jnp.where(lo, 0.0, x2), axis=-1, keepdims=True)
    r = jnp.where(lo, lax.rsqrt(s_lo * (1.0 / SWA_HEAD_DIM) + EPS),
                  lax.rsqrt(s_hi * (1.0 / SWA_HEAD_DIM) + EPS))
    xn = x * r * nw
    return xn * cs + pltpu.roll(xn, half, 1) * s1 + pltpu.roll(xn, LANES - half, 1) * s2


def _proj_qknorm_body(*refs, sub, nblk, n_norm):
    h_ref, w_refs = refs[0], refs[1:1 + nblk]
    c_ref, s1_ref, s2_ref, nw_ref, o_ref, wbf = refs[1 + nblk:]
    _cast_once(w_refs, None, wbf)
    lo = lax.broadcasted_iota(jnp.int32, (1, LANES), 1) < SWA_HEAD_DIM
    nw = nw_ref[...]
    for r0 in range(0, h_ref.shape[0], sub):
        acc = lax.dot_general(h_ref[r0:r0 + sub, :], wbf[...], NT_DIMS, preferred_element_type=F32)
        cs, s1, s2 = c_ref[r0:r0 + sub, :], s1_ref[r0:r0 + sub, :], s2_ref[r0:r0 + sub, :]
        for c in range(acc.shape[1] // LANES):
            x = acc[:, c * LANES:(c + 1) * LANES]
            if c < n_norm:
                x = _norm_rope(x, nw, cs, s1, s2, lo)
            o_ref[r0:r0 + sub, c * LANES:(c + 1) * LANES] = x.astype(o_ref.dtype)


def _proj_gates_body(h_ref, w_ref, al_ref, dt_ref, beta_ref, gc_ref, wbf):
    _cast_once([w_ref], None, wbf)
    acc = lax.dot_general(h_ref[...], wbf[...], NT_DIMS, preferred_element_type=F32)
    beta_ref[...] = _sigmoid(acc)
    x = pltpu.roll(acc, LANES - DN_V_HEADS, 1) + dt_ref[...]
    sp = jnp.maximum(x, 0.0) + jnp.log(1.0 + jnp.exp(-jnp.abs(x)))
    g = -jnp.exp(al_ref[...]) * sp
    r64 = lax.broadcasted_iota(jnp.int32, g.shape, 0) & (DN_CHUNK - 1)
    s = 1
    while s < DN_CHUNK:
        g = g + jnp.where(r64 >= s, pltpu.roll(g, s, 0), 0.0)
        s *= 2
    gc_ref[...] = g


def _proj(h, w_t, layer, col0, width, mode, T, extras=(), tm=1024, wblk=512, nblk=2, sub=256,
          name="proj"):
    N, K = h.shape
    out_dtype = BF16
    tm = min(tm, T)
    sub = min(sub, tm)
    off = col0 % LANES
    base = col0 - off
    shifted = off != 0
    if mode == "gates":
        wblk, nblk = LANES, 1
    tn = wblk * nblk
    assert off in (0, HALF) and base % wblk == 0
    assert mode == "gates" or width % tn == 0
    jb = base // wblk
    grid = (max(width // tn, 1), N // tm)
    in_specs = [pl.BlockSpec((tm, K), lambda j, i: (i, 0))]
    in_specs += [pl.BlockSpec((None, wblk, K), lambda j, i, b=b: (layer, jb + j * nblk + b, 0))
                 for b in range(nblk)]
    args = [h] + [w_t] * nblk
    scratch = [pltpu.VMEM((tn, K), BF16)]
    out_spec = pl.BlockSpec((tm, tn), lambda j, i: (i, j))
    out_shape = jax.ShapeDtypeStruct((N, max(width, tn)), out_dtype)
    if shifted:
        in_specs.append(pl.BlockSpec((None, HALF, K),
                                     lambda j, i: (layer, (base + (j + 1) * tn) // HALF, 0)))
        args.append(w_t)
    if mode == "conv":
        cw, cw_col0, norm = extras
        in_specs.append(pl.BlockSpec((DN_CONV, tn), lambda j, i: (0, cw_col0 // tn + j)))
        args.append(cw)
        scratch.append(pltpu.VMEM((tm + SUBLANES, tn), F32))
        body = functools.partial(_proj_conv_body, tm=tm, tps=T // tm, norm=norm, sub=sub, nblk=nblk)
    elif mode == "qknorm":
        tabs, nw, n_norm = extras
        in_specs += [pl.BlockSpec((tm, LANES), lambda j, i: (i, 0))] * 3
        in_specs.append(pl.BlockSpec((1, LANES), lambda j, i: (0, 0)))
        args += list(tabs) + [nw]
        body = functools.partial(_proj_qknorm_body, sub=sub, nblk=nblk, n_norm=n_norm)
    elif mode == "gates":
        in_specs += [pl.BlockSpec((1, LANES), lambda j, i: (0, 0))] * 2
        args += list(extras)
        body = _proj_gates_body
        out_spec = [out_spec, out_spec]
        out_shape = [jax.ShapeDtypeStruct((N, LANES), F32)] * 2
    else:
        body = functools.partial(_proj_act_body, act=mode, shifted=shifted, sub=sub if mode else tm,
                                 nblk=nblk)
    return pl.pallas_call(
        body, grid=grid, in_specs=in_specs, out_specs=out_spec, out_shape=out_shape,
        scratch_shapes=scratch,
        compiler_params=_cparams(("arbitrary", "arbitrary")),
        name=name,
    )(*args)


def _rope_inv_freq_lanes():
    half = ROPE_DIM // 2
    inv = (np.float32(ROPE_THETA) ** (-np.arange(half, dtype=np.float32) * np.float32(2.0 / ROPE_DIM))).astype(np.float32)
    lanes = np.zeros((1, LANES), np.float32)
    for l in range(LANES):
        m = l % SWA_HEAD_DIM
        if m < ROPE_DIM:
            lanes[0, l] = inv[m % half]
    return lanes


def _swa_body(sink_ref, q_ref, kv_ref, g_ref, o_ref, kprev, vprev):
    n = pl.program_id(1)
    BLK = SWA_BLOCK
    HD = SWA_HEAD_DIM

    @pl.when(n == 0)
    def _():
        kprev[...] = jnp.zeros_like(kprev)
        vprev[...] = jnp.zeros_like(vprev)

    lo = lax.broadcasted_iota(jnp.int32, (1, LANES), 1) < HD
    n_kc = SWA_KV // LANES
    nblocks = q_ref.shape[0] // BLK
    zrows = jnp.zeros((HD, BLK), F32)

    kband, vbandT = {}, {}
    k_last = {idx: kprev[idx] for idx in range(2 * SWA_KV_HEADS)}
    v_last = {idx: vprev[idx] for idx in range(2 * SWA_KV_HEADS)}
    for bi in range(nblocks):
        kv = kv_ref[bi * BLK:(bi + 1) * BLK, :]
        kc = [kv[:, c * LANES:(c + 1) * LANES].astype(F32) for c in range(n_kc)]
        vT = [kv[:, SWA_KV + c * LANES:SWA_KV + (c + 1) * LANES].astype(F32).T for c in range(n_kc)]
        for j in range(SWA_KV_HEADS):
            cj, b = j // 2, j % 2
            for a in range(2):
                src = kc[cj] if a == b else pltpu.roll(kc[cj], HD, 1)
                k_cur = jnp.where(lo if a == 0 else jnp.logical_not(lo), src, 0.0).astype(BF16)
                rows = vT[cj][b * HD:(b + 1) * HD]
                v_cur = jnp.concatenate([rows, zrows] if a == 0 else [zrows, rows], axis=0).astype(BF16)
                idx = 2 * j + a
                kband[(bi, j, a)] = jnp.concatenate([k_last[idx], k_cur], axis=0)
                vbandT[(bi, j, a)] = jnp.concatenate([v_last[idx], v_cur], axis=1)
                k_last[idx], v_last[idx] = k_cur, v_cur
    for idx in range(2 * SWA_KV_HEADS):
        kprev[idx] = k_last[idx]
        vprev[idx] = v_last[idx]

    kj = lax.broadcasted_iota(jnp.int32, (2 * BLK, 2 * BLK), 0)
    qi = lax.broadcasted_iota(jnp.int32, (2 * BLK, 2 * BLK), 1) & (BLK - 1)
    d = kj - qi
    in_window = (d >= 1) & (d <= BLK)
    kmin = jnp.where(n == 0, BLK, 0)
    valid = [in_window & (kj >= kmin)] + [in_window] * (nblocks - 1)
    left = lax.broadcasted_iota(jnp.int32, (1, 2 * BLK), 1) < BLK

    G = SWA_Q_HEADS // SWA_KV_HEADS
    cpk = G // 2
    items = [(bi, j, a, j * cpk + 2 * p) for bi in range(nblocks) for j in range(SWA_KV_HEADS)
             for a in range(2) for p in range(cpk // 2)]

    def scores(item):
        bi, j, a, c1 = item
        rows = slice(bi * BLK, (bi + 1) * BLK)
        rhs = jnp.concatenate([q_ref[rows, c1 * LANES:(c1 + 1) * LANES],
                               q_ref[rows, (c1 + 1) * LANES:(c1 + 2) * LANES]], axis=0)
        return lax.dot_general(kband[(bi, j, a)], rhs, NT_DIMS, preferred_element_type=F32)

    acc = {}

    def consume(item, oT, inv):
        bi, j, a, c1 = item
        rows = slice(bi * BLK, (bi + 1) * BLK)
        oT = oT * inv
        for c, part in ((c1, oT[:, :BLK]), (c1 + 1, oT[:, BLK:])):
            if a == 0:
                acc[c] = part
            else:
                gate = g_ref[rows, c * LANES:(c + 1) * LANES].astype(F32)
                o_ref[rows, c * LANES:(c + 1) * LANES] = ((acc.pop(c) + part).T * gate).astype(BF16)

    s_cur = scores(items[0])
    pending = None
    for i, item in enumerate(items):
        bi, j, a, c1 = item
        s_next = scores(items[i + 1]) if i + 1 < len(items) else None
        s = jnp.where(valid[bi], s_cur, -jnp.inf)
        sink = jnp.where(left, sink_ref[2 * c1 + a], sink_ref[2 * c1 + 2 + a])
        m = jnp.maximum(jnp.max(s, axis=0, keepdims=True), sink)
        e = jnp.exp(s - m)
        inv = 1.0 / (jnp.sum(e, axis=0, keepdims=True) + jnp.exp(sink - m))
        oT = jnp.dot(vbandT[(bi, j, a)], e.astype(BF16), preferred_element_type=F32)
        if pending is not None:
            consume(*pending)
        pending = (item, oT, inv)
        s_cur = s_next
    consume(*pending)


def _swa(q_n, kv_n, sag, sinks, B, T, blocks_per_step=2):
    N = B * T
    rows = blocks_per_step * SWA_BLOCK
    nb = T // rows
    row = lambda b, n: b * nb + n
    return pl.pallas_call(
        _swa_body,
        grid=(B, nb),
        in_specs=[pl.BlockSpec(memory_space=pltpu.SMEM),
                  pl.BlockSpec((rows, SWA_Q), lambda b, n: (row(b, n), 0)),
                  pl.BlockSpec((rows, 2 * SWA_KV), lambda b, n: (row(b, n), 0)),
                  pl.BlockSpec((rows, SWA_Q), lambda b, n: (row(b, n), 0))],
        out_specs=pl.BlockSpec((rows, SWA_Q), lambda b, n: (row(b, n), 0)),
        out_shape=jax.ShapeDtypeStruct((N, SWA_Q), BF16),
        scratch_shapes=[pltpu.VMEM((2 * SWA_KV_HEADS, SWA_BLOCK, LANES), BF16),
                        pltpu.VMEM((2 * SWA_KV_HEADS, LANES, SWA_BLOCK), BF16)],
        compiler_params=_cparams(("arbitrary", "arbitrary")),
        name="swa_attn",
    )(sinks, q_n, kv_n, sag)


def _delta_body(q_ref, k_ref, v_ref, z_ref, bc_ref, gc_ref, gr_ref, nw_ref, o_ref, S_ref,
                *, KH, TT):
    grp = pl.program_id(1)
    t = pl.program_id(2)
    C = DN_CHUNK
    NC = TT // C
    scale = DN_HEAD ** -0.5
    NT = (((1,), (1,)), ((), ()))

    @pl.when(t == 0)
    def _():
        S_ref[...] = jnp.zeros_like(S_ref)

    NV = 2 * KH
    row = lax.broadcasted_iota(jnp.int32, (TT, TT), 0)
    col = lax.broadcasted_iota(jnp.int32, (TT, TT), 1)
    same = (row // C) == (col // C)
    ri = lax.broadcasted_iota(jnp.int32, (C, TT), 0)
    cj = lax.broadcasted_iota(jnp.int32, (C, TT), 1)
    cjc = cj // C
    cjj = cj & (C - 1)
    tril_c = ri >= cjj
    strict_c = ri > cjj
    eye_c = (ri == cjj).astype(F32)
    lane = lax.broadcasted_iota(jnp.int32, (TT, LANES), 1)
    lchunk = lax.broadcasted_iota(jnp.int32, (1, TT), 1) // C
    bcall = bc_ref[...]
    gcall = gc_ref[...]
    nw = nw_ref[...]

    def compress(x):
        out = x[(NC - 1) * C:NC * C]
        for c in range(NC - 2, -1, -1):
            out = jnp.where(cjc == c, x[c * C:(c + 1) * C], out)
        return out

    def by_row(colvec):
        out = jnp.broadcast_to(colvec[(NC - 1) * C:NC * C], (C, TT))
        for c in range(NC - 2, -1, -1):
            out = jnp.where(cjc == c, colvec[c * C:(c + 1) * C], out)
        return out

    def bd(xc):
        return jnp.where(same, jnp.concatenate([xc] * NC, axis=0), 0.0).astype(BF16)

    kf, qf, kT, kkc, qkc = [], [], [], [], []
    for kh in range(KH):
        q = q_ref[:, kh * DN_HEAD:(kh + 1) * DN_HEAD]
        k = k_ref[:, kh * DN_HEAD:(kh + 1) * DN_HEAD]
        kq = lax.dot_general(jnp.concatenate([k, q], axis=0), k, NT, preferred_element_type=F32)
        kkc.append(compress(kq[:TT]))
        qkc.append(compress(kq[TT:]))
        kf.append(k.astype(F32))
        qf.append(q.astype(F32))
        kT.append(kf[-1].T)

    bcol, gcol, grow, Lc, qkm, P = [], [], [], [], [], []
    for hh in range(NV):
        hglob = grp * NV + hh
        sel = lane == hglob
        bcol.append(jnp.sum(jnp.where(sel, bcall, 0.0), axis=-1, keepdims=True))
        gcol.append(jnp.sum(jnp.where(sel, gcall, 0.0), axis=-1, keepdims=True))
        grow.append(gr_ref[0, pl.ds(hglob, 1), :])
        dec = jnp.exp(jnp.where(tril_c, by_row(gcol[hh]) - grow[hh], -jnp.inf))
        Lc.append(jnp.where(strict_c, by_row(bcol[hh]) * kkc[hh // 2] * dec, 0.0))
        qkm.append(qkc[hh // 2] * dec * scale)
        P.append(eye_c - Lc[hh])

    Lp = [jnp.dot(Lc[hh].astype(BF16), bd(Lc[hh]), preferred_element_type=F32) for hh in range(NV)]
    step = 4
    while step < C:
        res = [jnp.dot(jnp.concatenate([Lp[hh], P[hh]], axis=0).astype(BF16), bd(Lp[hh]),
                       preferred_element_type=F32) for hh in range(NV)]
        Lp = [r[:C] for r in res]
        P = [P[hh] + res[hh][C:] for hh in range(NV)]
        step *= 2
    P = [P[hh] + jnp.dot(P[hh].astype(BF16), bd(Lp[hh]), preferred_element_type=F32) for hh in range(NV)]

    uw, qd, kdT, eg_last = [], [], [], []
    for hh in range(NV):
        eg = jnp.exp(gcol[hh])
        vf = v_ref[:, hh * DN_HEAD:(hh + 1) * DN_HEAD].astype(F32)
        rhs = jnp.concatenate([vf * bcol[hh], kf[hh // 2] * (bcol[hh] * eg)], axis=1).astype(BF16)
        uw.append(jnp.dot(bd(P[hh]), rhs, preferred_element_type=F32))
        qd.append((qf[hh // 2] * (eg * scale)).astype(BF16))
        glrow = jnp.zeros((1, TT), F32)
        for c in range(NC):
            glrow = jnp.where(lchunk == c, gcol[hh][c * C + C - 1:c * C + C, :], glrow)
        kdT.append(kT[hh // 2] * jnp.exp(glrow - grow[hh]))
        eg_last.append([eg[c * C + C - 1:c * C + C, :] for c in range(NC)])

    S = [S_ref[hh] for hh in range(NV)]
    vn = [[] for _ in range(NV)]
    o_inter = [[] for _ in range(NV)]
    for c in range(NC):
        r0, r1 = c * C, (c + 1) * C
        for hh in range(NV):
            w_c = uw[hh][r0:r1, DN_HEAD:].astype(BF16)
            lhs = jnp.concatenate([w_c, qd[hh][r0:r1]], axis=0)
            res = jnp.dot(lhs, S[hh].astype(BF16), preferred_element_type=F32)
            vn_c = (uw[hh][r0:r1, :DN_HEAD] - res[:C]).astype(BF16)
            vn[hh].append(vn_c)
            o_inter[hh].append(res[C:])
            kd_c = kdT[hh][:, r0:r1].astype(BF16)
            S[hh] = S[hh] * eg_last[hh][c] + jnp.dot(kd_c, vn_c, preferred_element_type=F32)

    for hh in range(NV):
        S_ref[hh] = S[hh]
        vn_all = jnp.concatenate(vn[hh], axis=0)
        o = jnp.concatenate(o_inter[hh], axis=0) + jnp.dot(bd(qkm[hh]), vn_all, preferred_element_type=F32)
        ms = jnp.mean(o * o, axis=-1, keepdims=True)
        z = z_ref[:, hh * DN_HEAD:(hh + 1) * DN_HEAD].astype(F32)
        o_ref[:, hh * DN_HEAD:(hh + 1) * DN_HEAD] = (o * lax.rsqrt(ms + EPS) * nw * z).astype(BF16)


def _delta(qk_n, vv, sz, beta, gc, gcT, dn_norm_w, B, T, KH=8, TT=256):
    N = B * T
    nt = T // TT
    ng = DN_K_HEADS // KH
    row = lambda b, g, t: b * nt + t
    kw = KH * DN_HEAD
    return pl.pallas_call(
        functools.partial(_delta_body, KH=KH, TT=TT),
        grid=(B, ng, nt),
        in_specs=[pl.BlockSpec((TT, kw), lambda b, g, t: (row(b, g, t), g)),
                  pl.BlockSpec((TT, kw), lambda b, g, t: (row(b, g, t), DN_KEY // kw + g)),
                  pl.BlockSpec((TT, 2 * kw), lambda b, g, t: (row(b, g, t), g)),
                  pl.BlockSpec((TT, 2 * kw), lambda b, g, t: (row(b, g, t), g)),
                  pl.BlockSpec((TT, LANES), lambda b, g, t: (row(b, g, t), 0)),
                  pl.BlockSpec((TT, LANES), lambda b, g, t: (row(b, g, t), 0)),
                  pl.BlockSpec((1, DN_V_HEADS, TT), lambda b, g, t: (b, 0, t)),
                  pl.BlockSpec((1, DN_HEAD), lambda b, g, t: (0, 0))],
        out_specs=pl.BlockSpec((TT, 2 * kw), lambda b, g, t: (row(b, g, t), g)),
        out_shape=jax.ShapeDtypeStruct((N, DN_VAL), BF16),
        scratch_shapes=[pltpu.VMEM((2 * KH, DN_HEAD, DN_HEAD), F32)],
        compiler_params=_cparams(("arbitrary", "arbitrary", "arbitrary")),
        name="delta_rule",
    )(qk_n, qk_n, vv, sz, beta, gc, gcT, dn_norm_w.reshape(1, DN_HEAD))


def _merge_out_body(a_ref, o_ref, ga_ref, gb_ref, wa_ref, wb_ref, wo_ref, x_ref, g_ref, out_ref):
    @pl.when(pl.program_id(1) == 0)
    def _():
        out_ref[...] = x_ref[...]

    ya = jnp.dot(a_ref[...], wa_ref[...], preferred_element_type=F32)
    yb = jnp.dot(o_ref[...], wb_ref[...], preferred_element_type=F32)
    y = (ga_ref[...].astype(F32) * ya + gb_ref[...].astype(F32) * yb).astype(BF16)
    out_ref[...] += g_ref[0] * jnp.dot(y, wo_ref[...], preferred_element_type=F32)


def _merge_out(a_g, o_g, p_m, wa, wb, wo, x2d, gate, T, tm=512, tn=512):
    N, D = x2d.shape
    B = gate.shape[0]
    nj = D // tn
    tps = T // tm
    return pl.pallas_call(
        _merge_out_body,
        grid=(N // tm, nj),
        in_specs=[pl.BlockSpec((tm, a_g.shape[1]), lambda i, j: (i, 0)),
                  pl.BlockSpec((tm, o_g.shape[1]), lambda i, j: (i, 0)),
                  pl.BlockSpec((tm, tn), lambda i, j: (i, j)),
                  pl.BlockSpec((tm, tn), lambda i, j: (i, nj + j)),
                  pl.BlockSpec((wa.shape[0], tn), lambda i, j: (0, j)),
                  pl.BlockSpec((wb.shape[0], tn), lambda i, j: (0, j)),
                  pl.BlockSpec((tn, D), lambda i, j: (j, 0)),
                  pl.BlockSpec((tm, D), lambda i, j: (i, 0)),
                  pl.BlockSpec((1, 1, D), lambda i, j: (i // tps, 0, 0))],
        out_specs=pl.BlockSpec((tm, D), lambda i, j: (i, 0)),
        out_shape=jax.ShapeDtypeStruct((N, D), F32),
        compiler_params=_cparams(("arbitrary", "arbitrary"), vmem=60 * 1024 * 1024),
        name="merge_out",
    )(a_g, o_g, p_m, p_m, wa, wb, wo, x2d, gate.reshape(B, 1, D))


def _layer(l, x, c, positions, w_ada, b_ada, norm_w, w_in, q_norm_w, k_norm_w, sinks,
           conv_w, a_log, dt_bias, dn_norm_w, w_o_swa, w_o_dn, w_out):
    B, T, D = x.shape
    N = B * T
    x2d = x.reshape(N, D)

    mod = _adaln(c, w_ada, b_ada)
    shift, scale, gate = mod[:, :D], mod[:, D:2 * D], mod[:, 2 * D:]
    posf = positions.astype(F32).reshape(N, 1)
    h, tabs = _normmod(x2d, norm_w, scale, shift, posf, T)

    o_q, o_g = 0, SWA_Q + 2 * SWA_KV
    o_d = o_g + SWA_Q
    o_z = o_d + DN_CONV_CH
    o_b = o_z + DN_VAL
    o_a = o_b + DN_V_HEADS
    o_m = o_a + DN_V_HEADS
    assert o_a == o_b + DN_V_HEADS and o_b % LANES == 0
    w_in = jnp.swapaxes(w_in, 1, 2)
    rep = LANES // SWA_HEAD_DIM
    qw = jnp.tile(q_norm_w.reshape(1, SWA_HEAD_DIM), (1, rep)) * (SWA_HEAD_DIM ** -0.5)
    kw = jnp.tile(k_norm_w.reshape(1, SWA_HEAD_DIM), (1, rep))
    q_n = _proj(h, w_in, l, o_q, SWA_Q, "qknorm", T, extras=(tabs, qw, 2 * 512 // LANES), name="proj_q")
    kv_n = _proj(h, w_in, l, o_q + SWA_Q, 2 * SWA_KV, "qknorm", T, nblk=1,
                 extras=(tabs, kw, SWA_KV // LANES), name="proj_kv")
    sag = _proj(h, w_in, l, o_g, o_d - o_g, "silu", T, tm=2048, name="proj_swa_gate")
    qk_n = _proj(h, w_in, l, o_d, 2 * DN_KEY, "conv", T, extras=(conv_w, 0, True), name="proj_dn_qk")
    vv = _proj(h, w_in, l, o_d + 2 * DN_KEY, DN_VAL, "conv", T, extras=(conv_w, 2 * DN_KEY, False),
               name="proj_dn_v")
    sz = _proj(h, w_in, l, o_z, o_b - o_z, "silu", T, tm=2048, name="proj_dn_z")
    al = jnp.zeros((1, LANES), F32).at[0, :DN_V_HEADS].set(a_log)
    dt = jnp.zeros((1, LANES), F32).at[0, :DN_V_HEADS].set(dt_bias)
    beta, gc = _proj(h, w_in, l, o_b, 2 * DN_V_HEADS, "gates", T, extras=(al, dt), name="proj_dn_gates")
    p_m = _proj(h, w_in, l, o_m, 2 * D, "sigmoid", T, tm=2048, name="proj_merge_gates")

    a_g = _swa(q_n, kv_n, sag, sinks, B, T)

    gcT = gc[:, :DN_V_HEADS].reshape(B, T, DN_V_HEADS).transpose(0, 2, 1)
    o_g2 = _delta(qk_n, vv, sz, beta, gc, gcT, dn_norm_w, B, T)

    out = _merge_out(a_g, o_g2, p_m, w_o_swa.astype(BF16), w_o_dn.astype(BF16), w_out.astype(BF16),
                     x2d, gate, T)
    return out.reshape(B, T, D)


def kernel(x, c, positions, w_ada, b_ada, norm_w, w_in, q_norm_w, k_norm_w, sinks, conv_w,
           a_log, dt_bias, dn_norm_w, w_o_swa, w_o_dn, w_out):
    depth = w_ada.shape[0]
    for l in range(depth):
        x = _layer(l, x, c, positions, w_ada[l], b_ada[l], norm_w[l], w_in, q_norm_w[l],
                   k_norm_w[l], sinks[l], conv_w[l], a_log[l], dt_bias[l], dn_norm_w[l],
                   w_o_swa[l], w_o_dn[l], w_out[l])
    return x
```

```python
import functools

import numpy as np
import jax
import jax.numpy as jnp
from jax import lax
from jax.experimental import pallas as pl
from jax.experimental.pallas import tpu as pltpu

F32 = jnp.float32
BF16 = jnp.bfloat16

D_MODEL = 2048
SWA_Q_HEADS = 32
SWA_KV_HEADS = 4
SWA_HEAD_DIM = 64
SWA_BLOCK = 128
ROPE_THETA = 500000.0
ROPE_DIM = SWA_HEAD_DIM // 4
DN_K_HEADS = 16
DN_V_HEADS = 32
DN_HEAD = 128
DN_CONV = 4
DN_CHUNK = 64
EPS = 1e-6
LOG2E = 1.4426950408889634

SWA_Q = SWA_Q_HEADS * SWA_HEAD_DIM
SWA_KV = SWA_KV_HEADS * SWA_HEAD_DIM
DN_KEY = DN_K_HEADS * DN_HEAD
DN_VAL = DN_V_HEADS * DN_HEAD
DN_CONV_CH = 2 * DN_KEY + DN_VAL

LANES = 128
SUBLANES = 8
VMEM_LIMIT = 56 * 1024 * 1024


def _cparams(sem, vmem=VMEM_LIMIT):
    return pltpu.CompilerParams(dimension_semantics=sem, vmem_limit_bytes=vmem)


def _sigmoid(x):
    return 0.5 * jnp.tanh(0.5 * x) + 0.5


def _silu(x):
    h = 0.5 * x
    return h + h * jnp.tanh(h)


def _adaln_body(c_ref, w_ref, b_ref, o_ref):
    s = _silu(c_ref[...]).astype(BF16)
    o_ref[...] = jnp.dot(s, w_ref[...].astype(BF16), preferred_element_type=F32) + b_ref[...]


def _adaln(c, w_ada, b_ada, tn=1024):
    B, D = c.shape
    n_out = w_ada.shape[1]
    c8 = jnp.zeros((SUBLANES, D), F32).at[:B].set(c)
    out = pl.pallas_call(
        _adaln_body,
        grid=(n_out // tn,),
        in_specs=[pl.BlockSpec((SUBLANES, D), lambda j: (0, 0)),
                  pl.BlockSpec((D, tn), lambda j: (0, j)),
                  pl.BlockSpec((1, tn), lambda j: (0, j))],
        out_specs=pl.BlockSpec((SUBLANES, tn), lambda j: (0, j)),
        out_shape=jax.ShapeDtypeStruct((SUBLANES, n_out), F32),
        compiler_params=_cparams(("arbitrary",)),
        name="adaln_mod",
    )(c8, w_ada, b_ada.reshape(1, n_out))
    return out[:B]


def _normmod_body(x_ref, nw_ref, sc_ref, sh_ref, pos_ref, invf_ref, o_ref, c_ref, s1_ref, s2_ref):
    x = x_ref[...]
    ms = jnp.mean(x * x, axis=-1, keepdims=True)
    y = x * lax.rsqrt(ms + EPS) * nw_ref[...]
    o_ref[...] = (y * (1.0 + sc_ref[0]) + sh_ref[0]).astype(BF16)

    half = ROPE_DIM // 2
    l64 = lax.broadcasted_iota(jnp.int32, (1, LANES), 1) & (SWA_HEAD_DIM - 1)
    ang = pos_ref[...] * invf_ref[...]
    sn = jnp.sin(ang)
    c_ref[...] = jnp.cos(ang)
    s1_ref[...] = jnp.where((l64 >= half) & (l64 < ROPE_DIM), sn, 0.0)
    s2_ref[...] = jnp.where(l64 < half, -sn, 0.0)


def _normmod(x2d, norm_w, scale, shift, posf, T, tm=1024):
    N, D = x2d.shape
    B = scale.shape[0]
    tm = min(tm, T)
    tps = T // tm
    tab = pl.BlockSpec((tm, LANES), lambda i: (i, 0))
    outs = pl.pallas_call(
        _normmod_body,
        grid=(N // tm,),
        in_specs=[pl.BlockSpec((tm, D), lambda i: (i, 0)),
                  pl.BlockSpec((1, D), lambda i: (0, 0)),
                  pl.BlockSpec((1, 1, D), lambda i: (i // tps, 0, 0)),
                  pl.BlockSpec((1, 1, D), lambda i: (i // tps, 0, 0)),
                  pl.BlockSpec((tm, 1), lambda i: (i, 0)),
                  pl.BlockSpec((1, LANES), lambda i: (0, 0))],
        out_specs=[pl.BlockSpec((tm, D), lambda i: (i, 0)), tab, tab, tab],
        out_shape=[jax.ShapeDtypeStruct((N, D), BF16)] + [jax.ShapeDtypeStruct((N, LANES), F32)] * 3,
        compiler_params=_cparams(("arbitrary",)),
        name="norm_mod",
    )(x2d, norm_w.reshape(1, D), scale.reshape(B, 1, D), shift.reshape(B, 1, D), posf,
      jnp.asarray(_rope_inv_freq_lanes()))
    return outs[0], tuple(outs[1:])


HALF = LANES // 2


NT_DIMS = (((1,), (1,)), ((), ()))


def _cast_weights(w_refs, w2_ref, wbf):
    wblk = w_refs[0].shape[0]
    off = 0 if w2_ref is None else HALF
    for b, w_ref in enumerate(w_refs):
        lo = max(b * wblk - off, 0)
        wbf[lo:(b + 1) * wblk - off, :] = w_ref[lo + off - b * wblk:, :].astype(BF16)
    if w2_ref is not None:
        n = len(w_refs) * wblk
        wbf[n - off:n, :] = w2_ref[...].astype(BF16)


def _cast_once(w_refs, w2_ref, wbf):
    @pl.when(pl.program_id(1) == 0)
    def _():
        _cast_weights(w_refs, w2_ref, wbf)


def _proj_act_body(*refs, act, shifted, sub, nblk):
    h_ref, w_refs = refs[0], refs[1:1 + nblk]
    w2_ref = refs[1 + nblk] if shifted else None
    o_ref, wbf = refs[-2:]
    _cast_once(w_refs, w2_ref, wbf)
    for r0 in range(0, h_ref.shape[0], sub):
        acc = lax.dot_general(h_ref[r0:r0 + sub, :], wbf[...], NT_DIMS, preferred_element_type=F32)
        if act == "silu":
            acc = _silu(acc)
        elif act == "sigmoid":
            acc = _sigmoid(acc)
        o_ref[r0:r0 + sub, :] = acc.astype(o_ref.dtype)


def _proj_conv_body(*refs, tm, tps, norm, sub, nblk):
    h_ref, w_refs = refs[0], refs[1:1 + nblk]
    cw_ref, o_ref, wbf, buf = refs[1 + nblk:]
    i = pl.program_id(1)
    first = (i % tps) == 0
    _cast_once(w_refs, None, wbf)

    @pl.when(first)
    def _():
        buf[0:SUBLANES, :] = jnp.zeros((SUBLANES, buf.shape[1]), F32)

    @pl.when(jnp.logical_not(first))
    def _():
        buf[0:SUBLANES, :] = buf[tm:tm + SUBLANES, :]

    cw = cw_ref[...]
    mm = lambda r0: lax.dot_general(h_ref[r0:r0 + sub, :], wbf[...], NT_DIMS, preferred_element_type=F32)
    nxt = mm(0)
    for r0 in range(0, tm, sub):
        acc = nxt
        if r0 + sub < tm:
            nxt = mm(r0 + sub)
        b0 = SUBLANES + r0
        buf[b0:b0 + sub, :] = acc
        win = buf[b0 - SUBLANES:b0 + sub, :]
        y = win[SUBLANES:, :] * cw[DN_CONV - 1:DN_CONV, :]
        for d in range(1, DN_CONV):
            y = y + pltpu.roll(win, d, 0)[SUBLANES:, :] * cw[DN_CONV - 1 - d:DN_CONV - d, :]
        y = _silu(y)
        for c in range(y.shape[1] // DN_HEAD):
            yc = y[:, c * DN_HEAD:(c + 1) * DN_HEAD]
            if norm:
                yc = yc * lax.rsqrt(jnp.sum(yc * yc, axis=-1, keepdims=True) + EPS)
            o_ref[r0:r0 + sub, c * DN_HEAD:(c + 1) * DN_HEAD] = yc.astype(o_ref.dtype)


def _norm_rope(x, nw, cs, s1, s2, lo):
    half = ROPE_DIM // 2
    x2 = x * x
    s_lo = jnp.sum(jnp.where(lo, x2, 0.0), axis=-1, keepdims=True)
    s_hi = jnp.sum(jnp.where(lo, 0.0, x2), axis=-1, keepdims=True)
    r = jnp.where(lo, lax.rsqrt(s_lo * (1.0 / SWA_HEAD_DIM) + EPS),
                  lax.rsqrt(s_hi * (1.0 / SWA_HEAD_DIM) + EPS))
    xn = x * r * nw
    return xn * cs + pltpu.roll(xn, half, 1) * s1 + pltpu.roll(xn, LANES - half, 1) * s2


def _proj_qknorm_body(*refs, sub, nblk, n_norm):
    h_ref, w_refs = refs[0], refs[1:1 + nblk]
    c_ref, s1_ref, s2_ref, nw_ref, o_ref, wbf = refs[1 + nblk:]
    _cast_once(w_refs, None, wbf)
    lo = lax.broadcasted_iota(jnp.int32, (1, LANES), 1) < SWA_HEAD_DIM
    nw = nw_ref[...]
    for r0 in range(0, h_ref.shape[0], sub):
        acc = lax.dot_general(h_ref[r0:r0 + sub, :], wbf[...], NT_DIMS, preferred_element_type=F32)
        cs, s1, s2 = c_ref[r0:r0 + sub, :], s1_ref[r0:r0 + sub, :], s2_ref[r0:r0 + sub, :]
        for c in range(acc.shape[1] // LANES):
            x = acc[:, c * LANES:(c + 1) * LANES]
            if c < n_norm:
                x = _norm_rope(x, nw, cs, s1, s2, lo)
            o_ref[r0:r0 + sub, c * LANES:(c + 1) * LANES] = x.astype(o_ref.dtype)


def _proj_gates_body(h_ref, w_ref, al_ref, dt_ref, beta_ref, gc_ref, wbf):
    _cast_once([w_ref], None, wbf)
    acc = lax.dot_general(h_ref[...], wbf[...], NT_DIMS, preferred_element_type=F32)
    beta_ref[...] = _sigmoid(acc)
    x = pltpu.roll(acc, LANES - DN_V_HEADS, 1) + dt_ref[...]
    sp = jnp.maximum(x, 0.0) + jnp.log(1.0 + jnp.exp(-jnp.abs(x)))
    g = -jnp.exp(al_ref[...]) * sp
    r64 = lax.broadcasted_iota(jnp.int32, g.shape, 0) & (DN_CHUNK - 1)
    s = 1
    while s < DN_CHUNK:
        g = g + jnp.where(r64 >= s, pltpu.roll(g, s, 0), 0.0)
        s *= 2
    gc_ref[...] = g


def _proj(h, w_t, layer, col0, width, mode, T, extras=(), tm=1024, wblk=512, nblk=2, sub=256,
          name="proj"):
    N, K = h.shape
    out_dtype = BF16
    tm = min(tm, T)
    sub = min(sub, tm)
    off = col0 % LANES
    base = col0 - off
    shifted = off != 0
    if mode == "gates":
        wblk, nblk = LANES, 1
    tn = wblk * nblk
    assert off in (0, HALF) and base % wblk == 0
    assert mode == "gates" or width % tn == 0
    jb = base // wblk
    grid = (max(width // tn, 1), N // tm)
    in_specs = [pl.BlockSpec((tm, K), lambda j, i: (i, 0))]
    in_specs += [pl.BlockSpec((None, wblk, K), lambda j, i, b=b: (layer, jb + j * nblk + b, 0))
                 for b in range(nblk)]
    args = [h] + [w_t] * nblk
    scratch = [pltpu.VMEM((tn, K), BF16)]
    out_spec = pl.BlockSpec((tm, tn), lambda j, i: (i, j))
    out_shape = jax.ShapeDtypeStruct((N, max(width, tn)), out_dtype)
    if shifted:
        in_specs.append(pl.BlockSpec((None, HALF, K),
                                     lambda j, i: (layer, (base + (j + 1) * tn) // HALF, 0)))
        args.append(w_t)
    if mode == "conv":
        cw, cw_col0, norm = extras
        in_specs.append(pl.BlockSpec((DN_CONV, tn), lambda j, i: (0, cw_col0 // tn + j)))
        args.append(cw)
        scratch.append(pltpu.VMEM((tm + SUBLANES, tn), F32))
        body = functools.partial(_proj_conv_body, tm=tm, tps=T // tm, norm=norm, sub=sub, nblk=nblk)
    elif mode == "qknorm":
        tabs, nw, n_norm = extras
        in_specs += [pl.BlockSpec((tm, LANES), lambda j, i: (i, 0))] * 3
        in_specs.append(pl.BlockSpec((1, LANES), lambda j, i: (0, 0)))
        args += list(tabs) + [nw]
        body = functools.partial(_proj_qknorm_body, sub=sub, nblk=nblk, n_norm=n_norm)
    elif mode == "gates":
        in_specs += [pl.BlockSpec((1, LANES), lambda j, i: (0, 0))] * 2
        args += list(extras)
        body = _proj_gates_body
        out_spec = [out_spec, out_spec]
        out_shape = [jax.ShapeDtypeStruct((N, LANES), F32)] * 2
    else:
        body = functools.partial(_proj_act_body, act=mode, shifted=shifted, sub=sub if mode else tm,
                                 nblk=nblk)
    return pl.pallas_call(
        body, grid=grid, in_specs=in_specs, out_specs=out_spec, out_shape=out_shape,
        scratch_shapes=scratch,
        compiler_params=_cparams(("arbitrary", "arbitrary")),
        name=name,
    )(*args)


def _rope_inv_freq_lanes():
    half = ROPE_DIM // 2
    inv = (np.float32(ROPE_THETA) ** (-np.arange(half, dtype=np.float32) * np.float32(2.0 / ROPE_DIM))).astype(np.float32)
    lanes = np.zeros((1, LANES), np.float32)
    for l in range(LANES):
        m = l % SWA_HEAD_DIM
        if m < ROPE_DIM:
            lanes[0, l] = inv[m % half]
    return lanes


def _swa_body(sink_ref, q_ref, kv_ref, g_ref, o_ref, kprev, vprev):
    n = pl.program_id(1)
    BLK = SWA_BLOCK
    HD = SWA_HEAD_DIM

    @pl.when(n == 0)
    def _():
        kprev[...] = jnp.zeros_like(kprev)
        vprev[...] = jnp.zeros_like(vprev)

    lo = lax.broadcasted_iota(jnp.int32, (1, LANES), 1) < HD
    n_kc = SWA_KV // LANES
    nblocks = q_ref.shape[0] // BLK
    zrows = (lax.broadcasted_iota(jnp.int32, (HD, BLK), 0) == 0).astype(F32)
    top = lax.broadcasted_iota(jnp.int32, (LANES, 1), 0) < HD

    kband, vbandT = {}, {}
    k_last = {idx: kprev[idx] for idx in range(2 * SWA_KV_HEADS)}
    v_last = {idx: vprev[idx] for idx in range(2 * SWA_KV_HEADS)}
    for bi in range(nblocks):
        kv = kv_ref[bi * BLK:(bi + 1) * BLK, :]
        kc = [kv[:, c * LANES:(c + 1) * LANES].astype(F32) for c in range(n_kc)]
        vT = [kv[:, SWA_KV + c * LANES:SWA_KV + (c + 1) * LANES].astype(F32).T for c in range(n_kc)]
        for j in range(SWA_KV_HEADS):
            cj, b = j // 2, j % 2
            for a in range(2):
                src = kc[cj] if a == b else pltpu.roll(kc[cj], HD, 1)
                k_cur = jnp.where(lo if a == 0 else jnp.logical_not(lo), src, 0.0).astype(BF16)
                rows = vT[cj][b * HD:(b + 1) * HD]
                v_cur = jnp.concatenate([rows, zrows] if a == 0 else [zrows, rows], axis=0).astype(BF16)
                idx = 2 * j + a
                kband[(bi, j, a)] = jnp.concatenate([k_last[idx], k_cur], axis=0)
                vbandT[(bi, j, a)] = jnp.concatenate([v_last[idx], v_cur], axis=1)
                k_last[idx], v_last[idx] = k_cur, v_cur
    for idx in range(2 * SWA_KV_HEADS):
        kprev[idx] = k_last[idx]
        vprev[idx] = v_last[idx]

    kj = lax.broadcasted_iota(jnp.int32, (2 * BLK, 2 * BLK), 0)
    qi = lax.broadcasted_iota(jnp.int32, (2 * BLK, 2 * BLK), 1) & (BLK - 1)
    d = kj - qi
    in_window = (d >= 1) & (d <= BLK)
    kmin = jnp.where(n == 0, BLK, 0)
    valid = [in_window & (kj >= kmin)] + [in_window] * (nblocks - 1)
    left = lax.broadcasted_iota(jnp.int32, (1, 2 * BLK), 1) < BLK

    G = SWA_Q_HEADS // SWA_KV_HEADS
    cpk = G // 2
    items = [(bi, j, a, j * cpk + 2 * p) for bi in range(nblocks) for j in range(SWA_KV_HEADS)
             for a in range(2) for p in range(cpk // 2)]

    def scores(item):
        bi, j, a, c1 = item
        rows = slice(bi * BLK, (bi + 1) * BLK)
        rhs = jnp.concatenate([q_ref[rows, c1 * LANES:(c1 + 1) * LANES],
                               q_ref[rows, (c1 + 1) * LANES:(c1 + 2) * LANES]], axis=0)
        return lax.dot_general(kband[(bi, j, a)], rhs, NT_DIMS, preferred_element_type=F32)

    acc = {}

    def consume(item, oT, sink_term):
        bi, j, a, c1 = item
        rows = slice(bi * BLK, (bi + 1) * BLK)
        r1 = (1 - a) * HD
        oT = oT * (1.0 / (oT[r1:r1 + 1, :] + sink_term))
        for c, part in ((c1, oT[:, :BLK]), (c1 + 1, oT[:, BLK:])):
            if a == 0:
                acc[c] = part
            else:
                gate = g_ref[rows, c * LANES:(c + 1) * LANES].astype(F32)
                o_pair = jnp.where(top, acc.pop(c), part)
                o_ref[rows, c * LANES:(c + 1) * LANES] = (o_pair.T * gate).astype(BF16)

    ahead = 4
    queue = [scores(it) for it in items[:ahead]]
    pending = None
    for i, item in enumerate(items):
        bi, j, a, c1 = item
        if i + ahead < len(items):
            queue.append(scores(items[i + ahead]))
        s_cur = queue.pop(0)
        s = jnp.where(valid[bi], s_cur, -jnp.inf)
        sink = jnp.where(left, sink_ref[2 * c1 + a], sink_ref[2 * c1 + 2 + a])
        m = jnp.maximum(jnp.max(s, axis=0, keepdims=True), sink)
        e = jnp.exp2(s - m)
        oT = jnp.dot(vbandT[(bi, j, a)], e.astype(BF16), preferred_element_type=F32)
        if pending is not None:
            consume(*pending)
        pending = (item, oT, jnp.exp2(sink - m))
    consume(*pending)


def _swa(q_n, kv_n, sag, sinks, B, T, blocks_per_step=2):
    N = B * T
    rows = blocks_per_step * SWA_BLOCK
    nb = T // rows
    row = lambda b, n: b * nb + n
    return pl.pallas_call(
        _swa_body,
        grid=(B, nb),
        in_specs=[pl.BlockSpec(memory_space=pltpu.SMEM),
                  pl.BlockSpec((rows, SWA_Q), lambda b, n: (row(b, n), 0)),
                  pl.BlockSpec((rows, 2 * SWA_KV), lambda b, n: (row(b, n), 0)),
                  pl.BlockSpec((rows, SWA_Q), lambda b, n: (row(b, n), 0))],
        out_specs=pl.BlockSpec((rows, SWA_Q), lambda b, n: (row(b, n), 0)),
        out_shape=jax.ShapeDtypeStruct((N, SWA_Q), BF16),
        scratch_shapes=[pltpu.VMEM((2 * SWA_KV_HEADS, SWA_BLOCK, LANES), BF16),
                        pltpu.VMEM((2 * SWA_KV_HEADS, LANES, SWA_BLOCK), BF16)],
        compiler_params=_cparams(("arbitrary", "arbitrary")),
        name="swa_attn",
    )(sinks, q_n, kv_n, sag)


def _delta_body(q_ref, k_ref, v_ref, z_ref, bc_ref, gc_ref, gr_ref, nw_ref, o_ref, S_ref,
                *, KH, TT):
    grp = pl.program_id(1)
    t = pl.program_id(2)
    C = DN_CHUNK
    NC = TT // C
    scale = DN_HEAD ** -0.5
    NT = (((1,), (1,)), ((), ()))

    @pl.when(t == 0)
    def _():
        S_ref[...] = jnp.zeros_like(S_ref)

    NV = 2 * KH
    row = lax.broadcasted_iota(jnp.int32, (TT, TT), 0)
    col = lax.broadcasted_iota(jnp.int32, (TT, TT), 1)
    same = (row // C) == (col // C)
    ri = lax.broadcasted_iota(jnp.int32, (C, TT), 0)
    cj = lax.broadcasted_iota(jnp.int32, (C, TT), 1)
    cjc = cj // C
    cjj = cj & (C - 1)
    tril_c = ri >= cjj
    strict_c = ri > cjj
    eye_c = (ri == cjj).astype(F32)
    lane = lax.broadcasted_iota(jnp.int32, (TT, LANES), 1)
    lchunk = lax.broadcasted_iota(jnp.int32, (1, TT), 1) // C
    bcall = bc_ref[...]
    gcall = gc_ref[...]
    nw = nw_ref[...]

    def compress(x):
        out = x[(NC - 1) * C:NC * C]
        for c in range(NC - 2, -1, -1):
            out = jnp.where(cjc == c, x[c * C:(c + 1) * C], out)
        return out

    def by_row(colvec):
        out = jnp.broadcast_to(colvec[(NC - 1) * C:NC * C], (C, TT))
        for c in range(NC - 2, -1, -1):
            out = jnp.where(cjc == c, colvec[c * C:(c + 1) * C], out)
        return out

    def bd(xc):
        return jnp.where(same, jnp.concatenate([xc] * NC, axis=0), 0.0).astype(BF16)

    kf, qf, kT, kkc, qkc = [], [], [], [], []
    for kh in range(KH):
        q = q_ref[:, kh * DN_HEAD:(kh + 1) * DN_HEAD]
        k = k_ref[:, kh * DN_HEAD:(kh + 1) * DN_HEAD]
        kq = lax.dot_general(jnp.concatenate([k, q], axis=0), k, NT, preferred_element_type=F32)
        kkc.append(compress(kq[:TT]))
        qkc.append(compress(kq[TT:]))
        kf.append(k.astype(F32))
        qf.append(q.astype(F32))
        kT.append(kf[-1].T)

    bcol, gcol, grow, Lc, qkm, P = [], [], [], [], [], []
    for hh in range(NV):
        hglob = grp * NV + hh
        sel = lane == hglob
        bcol.append(jnp.sum(jnp.where(sel, bcall, 0.0), axis=-1, keepdims=True))
        gcol.append(jnp.sum(jnp.where(sel, gcall, 0.0), axis=-1, keepdims=True))
        grow.append(gr_ref[0, pl.ds(hglob, 1), :])
        dec = jnp.exp(jnp.where(tril_c, by_row(gcol[hh]) - grow[hh], -jnp.inf))
        Lc.append(jnp.where(strict_c, by_row(bcol[hh]) * kkc[hh // 2] * dec, 0.0))
        qkm.append(qkc[hh // 2] * dec * scale)
        P.append(eye_c - Lc[hh])

    Lp = [jnp.dot(Lc[hh].astype(BF16), bd(Lc[hh]), preferred_element_type=F32) for hh in range(NV)]
    step = 4
    while step < C:
        res = [jnp.dot(jnp.concatenate([Lp[hh], P[hh]], axis=0).astype(BF16), bd(Lp[hh]),
                       preferred_element_type=F32) for hh in range(NV)]
        Lp = [r[:C] for r in res]
        P = [P[hh] + res[hh][C:] for hh in range(NV)]
        step *= 2
    P = [P[hh] + jnp.dot(P[hh].astype(BF16), bd(Lp[hh]), preferred_element_type=F32) for hh in range(NV)]

    uw, qd, kdT, eg_last = [], [], [], []
    for hh in range(NV):
        eg = jnp.exp(gcol[hh])
        vf = v_ref[:, hh * DN_HEAD:(hh + 1) * DN_HEAD].astype(F32)
        rhs = jnp.concatenate([vf * bcol[hh], kf[hh // 2] * (bcol[hh] * eg)], axis=1).astype(BF16)
        uw.append(jnp.dot(bd(P[hh]), rhs, preferred_element_type=F32))
        qd.append((qf[hh // 2] * (eg * scale)).astype(BF16))
        glrow = jnp.zeros((1, TT), F32)
        for c in range(NC):
            glrow = jnp.where(lchunk == c, gcol[hh][c * C + C - 1:c * C + C, :], glrow)
        kdT.append(kT[hh // 2] * jnp.exp(glrow - grow[hh]))
        eg_last.append([eg[c * C + C - 1:c * C + C, :] for c in range(NC)])

    S = [S_ref[hh] for hh in range(NV)]
    vn = [[] for _ in range(NV)]
    o_inter = [[] for _ in range(NV)]
    for c in range(NC):
        r0, r1 = c * C, (c + 1) * C
        for hh in range(NV):
            w_c = uw[hh][r0:r1, DN_HEAD:].astype(BF16)
            lhs = jnp.concatenate([w_c, qd[hh][r0:r1]], axis=0)
            res = jnp.dot(lhs, S[hh].astype(BF16), preferred_element_type=F32)
            vn_c = (uw[hh][r0:r1, :DN_HEAD] - res[:C]).astype(BF16)
            vn[hh].append(vn_c)
            o_inter[hh].append(res[C:])
            kd_c = kdT[hh][:, r0:r1].astype(BF16)
            S[hh] = S[hh] * eg_last[hh][c] + jnp.dot(kd_c, vn_c, preferred_element_type=F32)

    for hh in range(NV):
        S_ref[hh] = S[hh]
        vn_all = jnp.concatenate(vn[hh], axis=0)
        o = jnp.concatenate(o_inter[hh], axis=0) + jnp.dot(bd(qkm[hh]), vn_all, preferred_element_type=F32)
        ms = jnp.mean(o * o, axis=-1, keepdims=True)
        z = z_ref[:, hh * DN_HEAD:(hh + 1) * DN_HEAD].astype(F32)
        o_ref[:, hh * DN_HEAD:(hh + 1) * DN_HEAD] = (o * lax.rsqrt(ms + EPS) * nw * z).astype(BF16)


def _delta(qk_n, vv, sz, beta, gc, gcT, dn_norm_w, B, T, KH=8, TT=256):
    N = B * T
    nt = T // TT
    ng = DN_K_HEADS // KH
    row = lambda b, g, t: b * nt + t
    kw = KH * DN_HEAD
    return pl.pallas_call(
        functools.partial(_delta_body, KH=KH, TT=TT),
        grid=(B, ng, nt),
        in_specs=[pl.BlockSpec((TT, kw), lambda b, g, t: (row(b, g, t), g)),
                  pl.BlockSpec((TT, kw), lambda b, g, t: (row(b, g, t), DN_KEY // kw + g)),
                  pl.BlockSpec((TT, 2 * kw), lambda b, g, t: (row(b, g, t), g)),
                  pl.BlockSpec((TT, 2 * kw), lambda b, g, t: (row(b, g, t), g)),
                  pl.BlockSpec((TT, LANES), lambda b, g, t: (row(b, g, t), 0)),
                  pl.BlockSpec((TT, LANES), lambda b, g, t: (row(b, g, t), 0)),
                  pl.BlockSpec((1, DN_V_HEADS, TT), lambda b, g, t: (b, 0, t)),
                  pl.BlockSpec((1, DN_HEAD), lambda b, g, t: (0, 0))],
        out_specs=pl.BlockSpec((TT, 2 * kw), lambda b, g, t: (row(b, g, t), g)),
        out_shape=jax.ShapeDtypeStruct((N, DN_VAL), BF16),
        scratch_shapes=[pltpu.VMEM((2 * KH, DN_HEAD, DN_HEAD), F32)],
        compiler_params=_cparams(("arbitrary", "arbitrary", "arbitrary")),
        name="delta_rule",
    )(qk_n, qk_n, vv, sz, beta, gc, gcT, dn_norm_w.reshape(1, DN_HEAD))


def _merge_out_body(a_ref, o_ref, ga_ref, gb_ref, wa_ref, wb_ref, wo_ref, x_ref, g_ref, out_ref):
    @pl.when(pl.program_id(1) == 0)
    def _():
        out_ref[...] = x_ref[...]

    ya = jnp.dot(a_ref[...], wa_ref[...], preferred_element_type=F32)
    yb = jnp.dot(o_ref[...], wb_ref[...], preferred_element_type=F32)
    y = (ga_ref[...].astype(F32) * ya + gb_ref[...].astype(F32) * yb).astype(BF16)
    out_ref[...] += g_ref[0] * jnp.dot(y, wo_ref[...], preferred_element_type=F32)


def _merge_out(a_g, o_g, p_m, wa, wb, wo, x2d, gate, T, tm=512, tn=512):
    N, D = x2d.shape
    B = gate.shape[0]
    nj = D // tn
    tps = T // tm
    return pl.pallas_call(
        _merge_out_body,
        grid=(N // tm, nj),
        in_specs=[pl.BlockSpec((tm, a_g.shape[1]), lambda i, j: (i, 0)),
                  pl.BlockSpec((tm, o_g.shape[1]), lambda i, j: (i, 0)),
                  pl.BlockSpec((tm, tn), lambda i, j: (i, j)),
                  pl.BlockSpec((tm, tn), lambda i, j: (i, nj + j)),
                  pl.BlockSpec((wa.shape[0], tn), lambda i, j: (0, j)),
                  pl.BlockSpec((wb.shape[0], tn), lambda i, j: (0, j)),
                  pl.BlockSpec((tn, D), lambda i, j: (j, 0)),
                  pl.BlockSpec((tm, D), lambda i, j: (i, 0)),
                  pl.BlockSpec((1, 1, D), lambda i, j: (i // tps, 0, 0))],
        out_specs=pl.BlockSpec((tm, D), lambda i, j: (i, 0)),
        out_shape=jax.ShapeDtypeStruct((N, D), F32),
        compiler_params=_cparams(("arbitrary", "arbitrary"), vmem=60 * 1024 * 1024),
        name="merge_out",
    )(a_g, o_g, p_m, p_m, wa, wb, wo, x2d, gate.reshape(B, 1, D))


def _layer(l, x, c, positions, w_ada, b_ada, norm_w, w_in, q_norm_w, k_norm_w, sinks,
           conv_w, a_log, dt_bias, dn_norm_w, w_o_swa, w_o_dn, w_out):
    B, T, D = x.shape
    N = B * T
    x2d = x.reshape(N, D)

    mod = _adaln(c, w_ada, b_ada)
    shift, scale, gate = mod[:, :D], mod[:, D:2 * D], mod[:, 2 * D:]
    posf = positions.astype(F32).reshape(N, 1)
    h, tabs = _normmod(x2d, norm_w, scale, shift, posf, T)

    o_q, o_g = 0, SWA_Q + 2 * SWA_KV
    o_d = o_g + SWA_Q
    o_z = o_d + DN_CONV_CH
    o_b = o_z + DN_VAL
    o_a = o_b + DN_V_HEADS
    o_m = o_a + DN_V_HEADS
    assert o_a == o_b + DN_V_HEADS and o_b % LANES == 0
    w_in = jnp.swapaxes(w_in, 1, 2)
    rep = LANES // SWA_HEAD_DIM
    qw = jnp.tile(q_norm_w.reshape(1, SWA_HEAD_DIM), (1, rep)) * (SWA_HEAD_DIM ** -0.5 * LOG2E)
    kw = jnp.tile(k_norm_w.reshape(1, SWA_HEAD_DIM), (1, rep))
    q_n = _proj(h, w_in, l, o_q, SWA_Q, "qknorm", T, extras=(tabs, qw, 2 * 512 // LANES), name="proj_q")
    kv_n = _proj(h, w_in, l, o_q + SWA_Q, 2 * SWA_KV, "qknorm", T, nblk=1,
                 extras=(tabs, kw, SWA_KV // LANES), name="proj_kv")
    sag = _proj(h, w_in, l, o_g, o_d - o_g, "silu", T, tm=2048, name="proj_swa_gate")
    qk_n = _proj(h, w_in, l, o_d, 2 * DN_KEY, "conv", T, extras=(conv_w, 0, True), name="proj_dn_qk")
    vv = _proj(h, w_in, l, o_d + 2 * DN_KEY, DN_VAL, "conv", T, extras=(conv_w, 2 * DN_KEY, False),
               name="proj_dn_v")
    sz = _proj(h, w_in, l, o_z, o_b - o_z, "silu", T, tm=2048, name="proj_dn_z")
    al = jnp.zeros((1, LANES), F32).at[0, :DN_V_HEADS].set(a_log)
    dt = jnp.zeros((1, LANES), F32).at[0, :DN_V_HEADS].set(dt_bias)
    beta, gc = _proj(h, w_in, l, o_b, 2 * DN_V_HEADS, "gates", T, extras=(al, dt), name="proj_dn_gates")
    p_m = _proj(h, w_in, l, o_m, 2 * D, "sigmoid", T, tm=2048, name="proj_merge_gates")

    a_g = _swa(q_n, kv_n, sag, sinks * LOG2E, B, T)

    gcT = gc[:, :DN_V_HEADS].reshape(B, T, DN_V_HEADS).transpose(0, 2, 1)
    o_g2 = _delta(qk_n, vv, sz, beta, gc, gcT, dn_norm_w, B, T)

    out = _merge_out(a_g, o_g2, p_m, w_o_swa.astype(BF16), w_o_dn.astype(BF16), w_out.astype(BF16),
                     x2d, gate, T)
    return out.reshape(B, T, D)


def kernel(x, c, positions, w_ada, b_ada, norm_w, w_in, q_norm_w, k_norm_w, sinks, conv_w,
           a_log, dt_bias, dn_norm_w, w_o_swa, w_o_dn, w_out):
    depth = w_ada.shape[0]
    for l in range(depth):
        x = _layer(l, x, c, positions, w_ada[l], b_ada[l], norm_w[l], w_in, q_norm_w[l],
                   k_norm_w[l], sinks[l], conv_w[l], a_log[l], dt_bias[l], dn_norm_w[l],
                   w_o_swa[l], w_o_dn[l], w_out[l])
    return x
```

```python
import functools

import numpy as np
import jax
import jax.numpy as jnp
from jax import lax
from jax.experimental import pallas as pl
from jax.experimental.pallas import tpu as pltpu

F32 = jnp.float32
BF16 = jnp.bfloat16

D_MODEL = 2048
SWA_Q_HEADS = 32
SWA_KV_HEADS = 4
SWA_HEAD_DIM = 64
SWA_BLOCK = 128
ROPE_THETA = 500000.0
ROPE_DIM = SWA_HEAD_DIM // 4
DN_K_HEADS = 16
DN_V_HEADS = 32
DN_HEAD = 128
DN_CONV = 4
DN_CHUNK = 64
EPS = 1e-6
LOG2E = 1.4426950408889634

SWA_Q = SWA_Q_HEADS * SWA_HEAD_DIM
SWA_KV = SWA_KV_HEADS * SWA_HEAD_DIM
DN_KEY = DN_K_HEADS * DN_HEAD
DN_VAL = DN_V_HEADS * DN_HEAD
DN_CONV_CH = 2 * DN_KEY + DN_VAL

LANES = 128
SUBLANES = 8
VMEM_LIMIT = 56 * 1024 * 1024


def _cparams(sem, vmem=VMEM_LIMIT):
    return pltpu.CompilerParams(dimension_semantics=sem, vmem_limit_bytes=vmem)


def _sigmoid(x):
    return 0.5 * jnp.tanh(0.5 * x) + 0.5


def _silu(x):
    h = 0.5 * x
    return h + h * jnp.tanh(h)


def _adaln_body(c_ref, w_ref, b_ref, o_ref):
    s = _silu(c_ref[...]).astype(BF16)
    o_ref[...] = jnp.dot(s, w_ref[...].astype(BF16), preferred_element_type=F32) + b_ref[...]


def _adaln(c, w_ada, b_ada, tn=1024):
    B, D = c.shape
    n_out = w_ada.shape[1]
    c8 = jnp.zeros((SUBLANES, D), F32).at[:B].set(c)
    out = pl.pallas_call(
        _adaln_body,
        grid=(n_out // tn,),
        in_specs=[pl.BlockSpec((SUBLANES, D), lambda j: (0, 0)),
                  pl.BlockSpec((D, tn), lambda j: (0, j)),
                  pl.BlockSpec((1, tn), lambda j: (0, j))],
        out_specs=pl.BlockSpec((SUBLANES, tn), lambda j: (0, j)),
        out_shape=jax.ShapeDtypeStruct((SUBLANES, n_out), F32),
        compiler_params=_cparams(("arbitrary",)),
        name="adaln_mod",
    )(c8, w_ada, b_ada.reshape(1, n_out))
    return out[:B]


def _normmod_body(x_ref, nw_ref, sc_ref, sh_ref, pos_ref, invf_ref, o_ref, c_ref, s1_ref, s2_ref):
    x = x_ref[...]
    ms = jnp.mean(x * x, axis=-1, keepdims=True)
    y = x * lax.rsqrt(ms + EPS) * nw_ref[...]
    o_ref[...] = (y * (1.0 + sc_ref[0]) + sh_ref[0]).astype(BF16)

    half = ROPE_DIM // 2
    l64 = lax.broadcasted_iota(jnp.int32, (1, LANES), 1) & (SWA_HEAD_DIM - 1)
    ang = pos_ref[...] * invf_ref[...]
    sn = jnp.sin(ang)
    c_ref[...] = jnp.cos(ang)
    s1_ref[...] = jnp.where((l64 >= half) & (l64 < ROPE_DIM), sn, 0.0)
    s2_ref[...] = jnp.where(l64 < half, -sn, 0.0)


def _normmod(x2d, norm_w, scale, shift, posf, T, tm=1024):
    N, D = x2d.shape
    B = scale.shape[0]
    tm = min(tm, T)
    tps = T // tm
    tab = pl.BlockSpec((tm, LANES), lambda i: (i, 0))
    outs = pl.pallas_call(
        _normmod_body,
        grid=(N // tm,),
        in_specs=[pl.BlockSpec((tm, D), lambda i: (i, 0)),
                  pl.BlockSpec((1, D), lambda i: (0, 0)),
                  pl.BlockSpec((1, 1, D), lambda i: (i // tps, 0, 0)),
                  pl.BlockSpec((1, 1, D), lambda i: (i // tps, 0, 0)),
                  pl.BlockSpec((tm, 1), lambda i: (i, 0)),
                  pl.BlockSpec((1, LANES), lambda i: (0, 0))],
        out_specs=[pl.BlockSpec((tm, D), lambda i: (i, 0)), tab, tab, tab],
        out_shape=[jax.ShapeDtypeStruct((N, D), BF16)] + [jax.ShapeDtypeStruct((N, LANES), F32)] * 3,
        compiler_params=_cparams(("arbitrary",)),
        name="norm_mod",
    )(x2d, norm_w.reshape(1, D), scale.reshape(B, 1, D), shift.reshape(B, 1, D), posf,
      jnp.asarray(_rope_inv_freq_lanes()))
    return outs[0], tuple(outs[1:])


HALF = LANES // 2


NT_DIMS = (((1,), (1,)), ((), ()))


def _cast_weights(w_refs, w2_ref, wbf):
    wblk = w_refs[0].shape[0]
    off = 0 if w2_ref is None else HALF
    for b, w_ref in enumerate(w_refs):
        lo = max(b * wblk - off, 0)
        wbf[lo:(b + 1) * wblk - off, :] = w_ref[lo + off - b * wblk:, :].astype(BF16)
    if w2_ref is not None:
        n = len(w_refs) * wblk
        wbf[n - off:n, :] = w2_ref[...].astype(BF16)


def _cast_once(w_refs, w2_ref, wbf):
    @pl.when(pl.program_id(1) == 0)
    def _():
        _cast_weights(w_refs, w2_ref, wbf)


MM_AHEAD = 3


def _for_row_subtiles(h_ref, wbf, sub, epilogue):
    starts = list(range(0, h_ref.shape[0], sub))
    mm = lambda r0: lax.dot_general(h_ref[r0:r0 + sub, :], wbf[...], NT_DIMS, preferred_element_type=F32)
    queue = [mm(r0) for r0 in starts[:MM_AHEAD]]
    for idx, r0 in enumerate(starts):
        if idx + MM_AHEAD < len(starts):
            queue.append(mm(starts[idx + MM_AHEAD]))
        epilogue(r0, queue.pop(0))


def _proj_act_body(*refs, act, shifted, sub, nblk):
    h_ref, w_refs = refs[0], refs[1:1 + nblk]
    w2_ref = refs[1 + nblk] if shifted else None
    o_ref, wbf = refs[-2:]
    _cast_once(w_refs, w2_ref, wbf)

    def epilogue(r0, acc):
        if act == "silu":
            acc = _silu(acc)
        elif act == "sigmoid":
            acc = _sigmoid(acc)
        o_ref[r0:r0 + sub, :] = acc.astype(o_ref.dtype)

    _for_row_subtiles(h_ref, wbf, sub, epilogue)


def _proj_conv_body(*refs, tm, tps, norm, sub, nblk):
    h_ref, w_refs = refs[0], refs[1:1 + nblk]
    cw_ref, o_ref, wbf, buf = refs[1 + nblk:]
    i = pl.program_id(1)
    first = (i % tps) == 0
    _cast_once(w_refs, None, wbf)

    @pl.when(first)
    def _():
        buf[0:SUBLANES, :] = jnp.zeros((SUBLANES, buf.shape[1]), F32)

    @pl.when(jnp.logical_not(first))
    def _():
        buf[0:SUBLANES, :] = buf[tm:tm + SUBLANES, :]

    cw = cw_ref[...]

    def epilogue(r0, acc):
        b0 = SUBLANES + r0
        buf[b0:b0 + sub, :] = acc
        win = buf[b0 - SUBLANES:b0 + sub, :]
        y = win[SUBLANES:, :] * cw[DN_CONV - 1:DN_CONV, :]
        for d in range(1, DN_CONV):
            y = y + pltpu.roll(win, d, 0)[SUBLANES:, :] * cw[DN_CONV - 1 - d:DN_CONV - d, :]
        y = _silu(y)
        for c in range(y.shape[1] // DN_HEAD):
            yc = y[:, c * DN_HEAD:(c + 1) * DN_HEAD]
            if norm:
                yc = yc * lax.rsqrt(jnp.sum(yc * yc, axis=-1, keepdims=True) + EPS)
            o_ref[r0:r0 + sub, c * DN_HEAD:(c + 1) * DN_HEAD] = yc.astype(o_ref.dtype)

    _for_row_subtiles(h_ref, wbf, sub, epilogue)


def _norm_rope(x, nw, cs, s1, s2, lo):
    half = ROPE_DIM // 2
    x2 = x * x
    s_lo = jnp.sum(jnp.where(lo, x2, 0.0), axis=-1, keepdims=True)
    s_hi = jnp.sum(jnp.where(lo, 0.0, x2), axis=-1, keepdims=True)
    r = jnp.where(lo, lax.rsqrt(s_lo * (1.0 / SWA_HEAD_DIM) + EPS),
                  lax.rsqrt(s_hi * (1.0 / SWA_HEAD_DIM) + EPS))
    xn = x * r * nw
    return xn * cs + pltpu.roll(xn, half, 1) * s1 + pltpu.roll(xn, LANES - half, 1) * s2


def _proj_qknorm_body(*refs, sub, nblk, n_norm):
    h_ref, w_refs = refs[0], refs[1:1 + nblk]
    c_ref, s1_ref, s2_ref, nw_ref, o_ref, wbf = refs[1 + nblk:]
    _cast_once(w_refs, None, wbf)
    lo = lax.broadcasted_iota(jnp.int32, (1, LANES), 1) < SWA_HEAD_DIM
    nw = nw_ref[...]
    def epilogue(r0, acc):
        cs, s1, s2 = c_ref[r0:r0 + sub, :], s1_ref[r0:r0 + sub, :], s2_ref[r0:r0 + sub, :]
        for c in range(acc.shape[1] // LANES):
            x = acc[:, c * LANES:(c + 1) * LANES]
            if c < n_norm:
                x = _norm_rope(x, nw, cs, s1, s2, lo)
            o_ref[r0:r0 + sub, c * LANES:(c + 1) * LANES] = x.astype(o_ref.dtype)

    _for_row_subtiles(h_ref, wbf, sub, epilogue)


def _proj_gates_body(h_ref, w_ref, al_ref, dt_ref, beta_ref, gc_ref, wbf):
    _cast_once([w_ref], None, wbf)
    acc = lax.dot_general(h_ref[...], wbf[...], NT_DIMS, preferred_element_type=F32)
    beta_ref[...] = _sigmoid(acc)
    x = pltpu.roll(acc, LANES - DN_V_HEADS, 1) + dt_ref[...]
    sp = jnp.maximum(x, 0.0) + jnp.log(1.0 + jnp.exp(-jnp.abs(x)))
    g = -jnp.exp(al_ref[...]) * sp
    r64 = lax.broadcasted_iota(jnp.int32, g.shape, 0) & (DN_CHUNK - 1)
    s = 1
    while s < DN_CHUNK:
        g = g + jnp.where(r64 >= s, pltpu.roll(g, s, 0), 0.0)
        s *= 2
    gc_ref[...] = g


def _proj(h, w_t, layer, col0, width, mode, T, extras=(), tm=1024, wblk=512, nblk=2, sub=256,
          name="proj"):
    N, K = h.shape
    out_dtype = BF16
    tm = min(tm, T)
    sub = min(sub, tm)
    off = col0 % LANES
    base = col0 - off
    shifted = off != 0
    if mode == "gates":
        wblk, nblk = LANES, 1
    tn = wblk * nblk
    assert off in (0, HALF) and base % wblk == 0
    assert mode == "gates" or width % tn == 0
    jb = base // wblk
    grid = (max(width // tn, 1), N // tm)
    in_specs = [pl.BlockSpec((tm, K), lambda j, i: (i, 0))]
    in_specs += [pl.BlockSpec((None, wblk, K), lambda j, i, b=b: (layer, jb + j * nblk + b, 0))
                 for b in range(nblk)]
    args = [h] + [w_t] * nblk
    scratch = [pltpu.VMEM((tn, K), BF16)]
    out_spec = pl.BlockSpec((tm, tn), lambda j, i: (i, j))
    out_shape = jax.ShapeDtypeStruct((N, max(width, tn)), out_dtype)
    if shifted:
        in_specs.append(pl.BlockSpec((None, HALF, K),
                                     lambda j, i: (layer, (base + (j + 1) * tn) // HALF, 0)))
        args.append(w_t)
    if mode == "conv":
        cw, cw_col0, norm = extras
        in_specs.append(pl.BlockSpec((DN_CONV, tn), lambda j, i: (0, cw_col0 // tn + j)))
        args.append(cw)
        scratch.append(pltpu.VMEM((tm + SUBLANES, tn), F32))
        body = functools.partial(_proj_conv_body, tm=tm, tps=T // tm, norm=norm, sub=sub, nblk=nblk)
    elif mode == "qknorm":
        tabs, nw, n_norm = extras
        in_specs += [pl.BlockSpec((tm, LANES), lambda j, i: (i, 0))] * 3
        in_specs.append(pl.BlockSpec((1, LANES), lambda j, i: (0, 0)))
        args += list(tabs) + [nw]
        body = functools.partial(_proj_qknorm_body, sub=sub, nblk=nblk, n_norm=n_norm)
    elif mode == "gates":
        in_specs += [pl.BlockSpec((1, LANES), lambda j, i: (0, 0))] * 2
        args += list(extras)
        body = _proj_gates_body
        out_spec = [out_spec, out_spec]
        out_shape = [jax.ShapeDtypeStruct((N, LANES), F32)] * 2
    else:
        body = functools.partial(_proj_act_body, act=mode, shifted=shifted, sub=sub if mode else tm,
                                 nblk=nblk)
    return pl.pallas_call(
        body, grid=grid, in_specs=in_specs, out_specs=out_spec, out_shape=out_shape,
        scratch_shapes=scratch,
        compiler_params=_cparams(("arbitrary", "arbitrary")),
        name=name,
    )(*args)


def _rope_inv_freq_lanes():
    half = ROPE_DIM // 2
    inv = (np.float32(ROPE_THETA) ** (-np.arange(half, dtype=np.float32) * np.float32(2.0 / ROPE_DIM))).astype(np.float32)
    lanes = np.zeros((1, LANES), np.float32)
    for l in range(LANES):
        m = l % SWA_HEAD_DIM
        if m < ROPE_DIM:
            lanes[0, l] = inv[m % half]
    return lanes


def _swa_body(sink_ref, q_ref, kv_ref, g_ref, o_ref, kprev, vprev):
    n = pl.program_id(1)
    BLK = SWA_BLOCK
    HD = SWA_HEAD_DIM

    @pl.when(n == 0)
    def _():
        kprev[...] = jnp.zeros_like(kprev)
        vprev[...] = jnp.zeros_like(vprev)

    lo = lax.broadcasted_iota(jnp.int32, (1, LANES), 1) < HD
    n_kc = SWA_KV // LANES
    nblocks = q_ref.shape[0] // BLK
    zrows = (lax.broadcasted_iota(jnp.int32, (HD, BLK), 0) == 0).astype(F32)
    top = lax.broadcasted_iota(jnp.int32, (LANES, 1), 0) < HD

    kband, vbandT = {}, {}
    k_last = {idx: kprev[idx] for idx in range(2 * SWA_KV_HEADS)}
    v_last = {idx: vprev[idx] for idx in range(2 * SWA_KV_HEADS)}
    for bi in range(nblocks):
        kv = kv_ref[bi * BLK:(bi + 1) * BLK, :]
        kc = [kv[:, c * LANES:(c + 1) * LANES].astype(F32) for c in range(n_kc)]
        vT = [kv[:, SWA_KV + c * LANES:SWA_KV + (c + 1) * LANES].astype(F32).T for c in range(n_kc)]
        for j in range(SWA_KV_HEADS):
            cj, b = j // 2, j % 2
            for a in range(2):
                src = kc[cj] if a == b else pltpu.roll(kc[cj], HD, 1)
                k_cur = jnp.where(lo if a == 0 else jnp.logical_not(lo), src, 0.0).astype(BF16)
                rows = vT[cj][b * HD:(b + 1) * HD]
                v_cur = jnp.concatenate([rows, zrows] if a == 0 else [zrows, rows], axis=0).astype(BF16)
                idx = 2 * j + a
                kband[(bi, j, a)] = jnp.concatenate([k_last[idx], k_cur], axis=0)
                vbandT[(bi, j, a)] = jnp.concatenate([v_last[idx], v_cur], axis=1)
                k_last[idx], v_last[idx] = k_cur, v_cur
    for idx in range(2 * SWA_KV_HEADS):
        kprev[idx] = k_last[idx]
        vprev[idx] = v_last[idx]

    kj = lax.broadcasted_iota(jnp.int32, (2 * BLK, 2 * BLK), 0)
    qi = lax.broadcasted_iota(jnp.int32, (2 * BLK, 2 * BLK), 1) & (BLK - 1)
    d = kj - qi
    in_window = (d >= 1) & (d <= BLK)
    kmin = jnp.where(n == 0, BLK, 0)
    valid = [in_window & (kj >= kmin)] + [in_window] * (nblocks - 1)
    left = lax.broadcasted_iota(jnp.int32, (1, 2 * BLK), 1) < BLK

    G = SWA_Q_HEADS // SWA_KV_HEADS
    cpk = G // 2
    items = [(bi, j, a, j * cpk + 2 * p) for bi in range(nblocks) for j in range(SWA_KV_HEADS)
             for a in range(2) for p in range(cpk // 2)]

    def scores(item):
        bi, j, a, c1 = item
        rows = slice(bi * BLK, (bi + 1) * BLK)
        rhs = jnp.concatenate([q_ref[rows, c1 * LANES:(c1 + 1) * LANES],
                               q_ref[rows, (c1 + 1) * LANES:(c1 + 2) * LANES]], axis=0)
        return lax.dot_general(kband[(bi, j, a)], rhs, NT_DIMS, preferred_element_type=F32)

    acc = {}

    def consume(item, oT, sink_term):
        bi, j, a, c1 = item
        rows = slice(bi * BLK, (bi + 1) * BLK)
        r1 = (1 - a) * HD
        oT = oT * (1.0 / (oT[r1:r1 + 1, :] + sink_term))
        for c, part in ((c1, oT[:, :BLK]), (c1 + 1, oT[:, BLK:])):
            if a == 0:
                acc[c] = part
            else:
                gate = g_ref[rows, c * LANES:(c + 1) * LANES].astype(F32)
                o_pair = jnp.where(top, acc.pop(c), part)
                o_ref[rows, c * LANES:(c + 1) * LANES] = (o_pair.T * gate).astype(BF16)

    ahead = 4
    queue = [scores(it) for it in items[:ahead]]
    pending = None
    for i, item in enumerate(items):
        bi, j, a, c1 = item
        if i + ahead < len(items):
            queue.append(scores(items[i + ahead]))
        s_cur = queue.pop(0)
        s = jnp.where(valid[bi], s_cur, -jnp.inf)
        sink = jnp.where(left, sink_ref[2 * c1 + a], sink_ref[2 * c1 + 2 + a])
        m = jnp.maximum(jnp.max(s, axis=0, keepdims=True), sink)
        e = jnp.exp2(s - m)
        oT = jnp.dot(vbandT[(bi, j, a)], e.astype(BF16), preferred_element_type=F32)
        if pending is not None:
            consume(*pending)
        pending = (item, oT, jnp.exp2(sink - m))
    consume(*pending)


def _swa(q_n, kv_n, sag, sinks, B, T, blocks_per_step=2):
    N = B * T
    rows = blocks_per_step * SWA_BLOCK
    nb = T // rows
    row = lambda b, n: b * nb + n
    return pl.pallas_call(
        _swa_body,
        grid=(B, nb),
        in_specs=[pl.BlockSpec(memory_space=pltpu.SMEM),
                  pl.BlockSpec((rows, SWA_Q), lambda b, n: (row(b, n), 0)),
                  pl.BlockSpec((rows, 2 * SWA_KV), lambda b, n: (row(b, n), 0)),
                  pl.BlockSpec((rows, SWA_Q), lambda b, n: (row(b, n), 0))],
        out_specs=pl.BlockSpec((rows, SWA_Q), lambda b, n: (row(b, n), 0)),
        out_shape=jax.ShapeDtypeStruct((N, SWA_Q), BF16),
        scratch_shapes=[pltpu.VMEM((2 * SWA_KV_HEADS, SWA_BLOCK, LANES), BF16),
                        pltpu.VMEM((2 * SWA_KV_HEADS, LANES, SWA_BLOCK), BF16)],
        compiler_params=_cparams(("arbitrary", "arbitrary")),
        name="swa_attn",
    )(sinks, q_n, kv_n, sag)


def _delta_body(q_ref, k_ref, v_ref, z_ref, bc_ref, gc_ref, gr_ref, nw_ref, o_ref, S_ref,
                *, KH, TT):
    grp = pl.program_id(1)
    t = pl.program_id(2)
    C = DN_CHUNK
    NC = TT // C
    scale = DN_HEAD ** -0.5
    NT = (((1,), (1,)), ((), ()))

    @pl.when(t == 0)
    def _():
        S_ref[...] = jnp.zeros_like(S_ref)

    NV = 2 * KH
    row = lax.broadcasted_iota(jnp.int32, (TT, TT), 0)
    col = lax.broadcasted_iota(jnp.int32, (TT, TT), 1)
    same = (row // C) == (col // C)
    ri = lax.broadcasted_iota(jnp.int32, (C, TT), 0)
    cj = lax.broadcasted_iota(jnp.int32, (C, TT), 1)
    cjc = cj // C
    cjj = cj & (C - 1)
    tril_c = ri >= cjj
    strict_c = ri > cjj
    eye_c = (ri == cjj).astype(F32)
    lane = lax.broadcasted_iota(jnp.int32, (TT, LANES), 1)
    lchunk = lax.broadcasted_iota(jnp.int32, (1, TT), 1) // C
    bcall = bc_ref[...]
    gcall = gc_ref[...]
    nw = nw_ref[...]

    def compress(x):
        out = x[(NC - 1) * C:NC * C]
        for c in range(NC - 2, -1, -1):
            out = jnp.where(cjc == c, x[c * C:(c + 1) * C], out)
        return out

    def by_row(colvec):
        out = jnp.broadcast_to(colvec[(NC - 1) * C:NC * C], (C, TT))
        for c in range(NC - 2, -1, -1):
            out = jnp.where(cjc == c, colvec[c * C:(c + 1) * C], out)
        return out

    def bd(xc):
        return jnp.where(same, jnp.concatenate([xc] * NC, axis=0), 0.0).astype(BF16)

    kf, qf, kT, kkc, qkc = [], [], [], [], []
    for kh in range(KH):
        q = q_ref[:, kh * DN_HEAD:(kh + 1) * DN_HEAD]
        k = k_ref[:, kh * DN_HEAD:(kh + 1) * DN_HEAD]
        kq = lax.dot_general(jnp.concatenate([k, q], axis=0), k, NT, preferred_element_type=F32)
        kkc.append(compress(kq[:TT]))
        qkc.append(compress(kq[TT:]))
        kf.append(k.astype(F32))
        qf.append(q.astype(F32))
        kT.append(kf[-1].T)

    bcol, gcol, grow, Lc, qkm, P = [], [], [], [], [], []
    for hh in range(NV):
        hglob = grp * NV + hh
        sel = lane == hglob
        bcol.append(jnp.sum(jnp.where(sel, bcall, 0.0), axis=-1, keepdims=True))
        gcol.append(jnp.sum(jnp.where(sel, gcall, 0.0), axis=-1, keepdims=True))
        grow.append(gr_ref[0, pl.ds(hglob, 1), :])
        dec = jnp.exp(jnp.where(tril_c, by_row(gcol[hh]) - grow[hh], -jnp.inf))
        Lc.append(jnp.where(strict_c, by_row(bcol[hh]) * kkc[hh // 2] * dec, 0.0))
        qkm.append(qkc[hh // 2] * dec * scale)
        P.append(eye_c - Lc[hh])

    Lp = [jnp.dot(Lc[hh].astype(BF16), bd(Lc[hh]), preferred_element_type=F32) for hh in range(NV)]
    step = 4
    while step < C:
        res = [jnp.dot(jnp.concatenate([Lp[hh], P[hh]], axis=0).astype(BF16), bd(Lp[hh]),
                       preferred_element_type=F32) for hh in range(NV)]
        Lp = [r[:C] for r in res]
        P = [P[hh] + res[hh][C:] for hh in range(NV)]
        step *= 2
    P = [P[hh] + jnp.dot(P[hh].astype(BF16), bd(Lp[hh]), preferred_element_type=F32) for hh in range(NV)]

    uw, qd, kdT, eg_last = [], [], [], []
    for hh in range(NV):
        eg = jnp.exp(gcol[hh])
        vf = v_ref[:, hh * DN_HEAD:(hh + 1) * DN_HEAD].astype(F32)
        rhs = jnp.concatenate([vf * bcol[hh], kf[hh // 2] * (bcol[hh] * eg)], axis=1).astype(BF16)
        uw.append(jnp.dot(bd(P[hh]), rhs, preferred_element_type=F32))
        qd.append((qf[hh // 2] * (eg * scale)).astype(BF16))
        glrow = jnp.zeros((1, TT), F32)
        for c in range(NC):
            glrow = jnp.where(lchunk == c, gcol[hh][c * C + C - 1:c * C + C, :], glrow)
        kdT.append(kT[hh // 2] * jnp.exp(glrow - grow[hh]))
        eg_last.append([eg[c * C + C - 1:c * C + C, :] for c in range(NC)])

    S = [S_ref[hh] for hh in range(NV)]
    vn = [[] for _ in range(NV)]
    o_inter = [[] for _ in range(NV)]
    for c in range(NC):
        r0, r1 = c * C, (c + 1) * C
        res = []
        for hh in range(NV):
            w_c = uw[hh][r0:r1, DN_HEAD:].astype(BF16)
            lhs = jnp.concatenate([w_c, qd[hh][r0:r1]], axis=0)
            res.append(jnp.dot(lhs, S[hh].astype(BF16), preferred_element_type=F32))
        inc = []
        for hh in range(NV):
            vn_c = (uw[hh][r0:r1, :DN_HEAD] - res[hh][:C]).astype(BF16)
            vn[hh].append(vn_c)
            o_inter[hh].append(res[hh][C:])
            kd_c = kdT[hh][:, r0:r1].astype(BF16)
            inc.append(jnp.dot(kd_c, vn_c, preferred_element_type=F32))
        for hh in range(NV):
            S[hh] = S[hh] * eg_last[hh][c] + inc[hh]

    for hh in range(NV):
        S_ref[hh] = S[hh]
        vn_all = jnp.concatenate(vn[hh], axis=0)
        o = jnp.concatenate(o_inter[hh], axis=0) + jnp.dot(bd(qkm[hh]), vn_all, preferred_element_type=F32)
        ms = jnp.mean(o * o, axis=-1, keepdims=True)
        z = z_ref[:, hh * DN_HEAD:(hh + 1) * DN_HEAD].astype(F32)
        o_ref[:, hh * DN_HEAD:(hh + 1) * DN_HEAD] = (o * lax.rsqrt(ms + EPS) * nw * z).astype(BF16)


def _delta(qk_n, vv, sz, beta, gc, gcT, dn_norm_w, B, T, KH=8, TT=256):
    N = B * T
    nt = T // TT
    ng = DN_K_HEADS // KH
    row = lambda b, g, t: b * nt + t
    kw = KH * DN_HEAD
    return pl.pallas_call(
        functools.partial(_delta_body, KH=KH, TT=TT),
        grid=(B, ng, nt),
        in_specs=[pl.BlockSpec((TT, kw), lambda b, g, t: (row(b, g, t), g)),
                  pl.BlockSpec((TT, kw), lambda b, g, t: (row(b, g, t), DN_KEY // kw + g)),
                  pl.BlockSpec((TT, 2 * kw), lambda b, g, t: (row(b, g, t), g)),
                  pl.BlockSpec((TT, 2 * kw), lambda b, g, t: (row(b, g, t), g)),
                  pl.BlockSpec((TT, LANES), lambda b, g, t: (row(b, g, t), 0)),
                  pl.BlockSpec((TT, LANES), lambda b, g, t: (row(b, g, t), 0)),
                  pl.BlockSpec((1, DN_V_HEADS, TT), lambda b, g, t: (b, 0, t)),
                  pl.BlockSpec((1, DN_HEAD), lambda b, g, t: (0, 0))],
        out_specs=pl.BlockSpec((TT, 2 * kw), lambda b, g, t: (row(b, g, t), g)),
        out_shape=jax.ShapeDtypeStruct((N, DN_VAL), BF16),
        scratch_shapes=[pltpu.VMEM((2 * KH, DN_HEAD, DN_HEAD), F32)],
        compiler_params=_cparams(("arbitrary", "arbitrary", "arbitrary")),
        name="delta_rule",
    )(qk_n, qk_n, vv, sz, beta, gc, gcT, dn_norm_w.reshape(1, DN_HEAD))


def _merge_out_body(a_ref, o_ref, ga_ref, gb_ref, wa_ref, wb_ref, wo_ref, x_ref, g_ref, out_ref):
    @pl.when(pl.program_id(1) == 0)
    def _():
        out_ref[...] = x_ref[...]

    ya = jnp.dot(a_ref[...], wa_ref[...], preferred_element_type=F32)
    yb = jnp.dot(o_ref[...], wb_ref[...], preferred_element_type=F32)
    y = (ga_ref[...].astype(F32) * ya + gb_ref[...].astype(F32) * yb).astype(BF16)
    out_ref[...] += g_ref[0] * jnp.dot(y, wo_ref[...], preferred_element_type=F32)


def _merge_out(a_g, o_g, p_m, wa, wb, wo, x2d, gate, T, tm=512, tn=512):
    N, D = x2d.shape
    B = gate.shape[0]
    nj = D // tn
    tps = T // tm
    return pl.pallas_call(
        _merge_out_body,
        grid=(N // tm, nj),
        in_specs=[pl.BlockSpec((tm, a_g.shape[1]), lambda i, j: (i, 0)),
                  pl.BlockSpec((tm, o_g.shape[1]), lambda i, j: (i, 0)),
                  pl.BlockSpec((tm, tn), lambda i, j: (i, j)),
                  pl.BlockSpec((tm, tn), lambda i, j: (i, nj + j)),
                  pl.BlockSpec((wa.shape[0], tn), lambda i, j: (0, j)),
                  pl.BlockSpec((wb.shape[0], tn), lambda i, j: (0, j)),
                  pl.BlockSpec((tn, D), lambda i, j: (j, 0)),
                  pl.BlockSpec((tm, D), lambda i, j: (i, 0)),
                  pl.BlockSpec((1, 1, D), lambda i, j: (i // tps, 0, 0))],
        out_specs=pl.BlockSpec((tm, D), lambda i, j: (i, 0)),
        out_shape=jax.ShapeDtypeStruct((N, D), F32),
        compiler_params=_cparams(("arbitrary", "arbitrary"), vmem=60 * 1024 * 1024),
        name="merge_out",
    )(a_g, o_g, p_m, p_m, wa, wb, wo, x2d, gate.reshape(B, 1, D))


def _layer(l, x, c, positions, w_ada, b_ada, norm_w, w_in, q_norm_w, k_norm_w, sinks,
           conv_w, a_log, dt_bias, dn_norm_w, w_o_swa, w_o_dn, w_out):
    B, T, D = x.shape
    N = B * T
    x2d = x.reshape(N, D)

    mod = _adaln(c, w_ada, b_ada)
    shift, scale, gate = mod[:, :D], mod[:, D:2 * D], mod[:, 2 * D:]
    posf = positions.astype(F32).reshape(N, 1)
    h, tabs = _normmod(x2d, norm_w, scale, shift, posf, T)

    o_q, o_g = 0, SWA_Q + 2 * SWA_KV
    o_d = o_g + SWA_Q
    o_z = o_d + DN_CONV_CH
    o_b = o_z + DN_VAL
    o_a = o_b + DN_V_HEADS
    o_m = o_a + DN_V_HEADS
    assert o_a == o_b + DN_V_HEADS and o_b % LANES == 0
    w_in = jnp.swapaxes(w_in, 1, 2)
    rep = LANES // SWA_HEAD_DIM
    qw = jnp.tile(q_norm_w.reshape(1, SWA_HEAD_DIM), (1, rep)) * (SWA_HEAD_DIM ** -0.5 * LOG2E)
    kw = jnp.tile(k_norm_w.reshape(1, SWA_HEAD_DIM), (1, rep))
    q_n = _proj(h, w_in, l, o_q, SWA_Q, "qknorm", T, extras=(tabs, qw, SWA_Q // LANES), name="proj_q")
    kv_n = _proj(h, w_in, l, o_q + SWA_Q, 2 * SWA_KV, "qknorm", T, nblk=1,
                 extras=(tabs, kw, SWA_KV // LANES), name="proj_kv")
    sag = _proj(h, w_in, l, o_g, o_d - o_g, "silu", T, tm=2048, name="proj_swa_gate")
    qk_n = _proj(h, w_in, l, o_d, 2 * DN_KEY, "conv", T, extras=(conv_w, 0, True), name="proj_dn_qk")
    vv = _proj(h, w_in, l, o_d + 2 * DN_KEY, DN_VAL, "conv", T, extras=(conv_w, 2 * DN_KEY, False),
               name="proj_dn_v")
    sz = _proj(h, w_in, l, o_z, o_b - o_z, "silu", T, tm=2048, name="proj_dn_z")
    al = jnp.zeros((1, LANES), F32).at[0, :DN_V_HEADS].set(a_log)
    dt = jnp.zeros((1, LANES), F32).at[0, :DN_V_HEADS].set(dt_bias)
    beta, gc = _proj(h, w_in, l, o_b, 2 * DN_V_HEADS, "gates", T, extras=(al, dt), name="proj_dn_gates")
    p_m = _proj(h, w_in, l, o_m, 2 * D, "sigmoid", T, tm=2048, name="proj_merge_gates")

    a_g = _swa(q_n, kv_n, sag, sinks * LOG2E, B, T)

    gcT = gc[:, :DN_V_HEADS].reshape(B, T, DN_V_HEADS).transpose(0, 2, 1)
    o_g2 = _delta(qk_n, vv, sz, beta, gc, gcT, dn_norm_w, B, T)

    out = _merge_out(a_g, o_g2, p_m, w_o_swa.astype(BF16), w_o_dn.astype(BF16), w_out.astype(BF16),
                     x2d, gate, T)
    return out.reshape(B, T, D)


def kernel(x, c, positions, w_ada, b_ada, norm_w, w_in, q_norm_w, k_norm_w, sinks, conv_w,
           a_log, dt_bias, dn_norm_w, w_o_swa, w_o_dn, w_out):
    depth = w_ada.shape[0]
    for l in range(depth):
        x = _layer(l, x, c, positions, w_ada[l], b_ada[l], norm_w[l], w_in, q_norm_w[l],
                   k_norm_w[l], sinks[l], conv_w[l], a_log[l], dt_bias[l], dn_norm_w[l],
                   w_o_swa[l], w_o_dn[l], w_out[l])
    return x
```

```python
import functools

import numpy as np
import jax
import jax.numpy as jnp
from jax import lax
from jax.experimental import pallas as pl
from jax.experimental.pallas import tpu as pltpu

F32 = jnp.float32
BF16 = jnp.bfloat16

D_MODEL = 2048
SWA_Q_HEADS = 32
SWA_KV_HEADS = 4
SWA_HEAD_DIM = 64
SWA_BLOCK = 128
ROPE_THETA = 500000.0
ROPE_DIM = SWA_HEAD_DIM // 4
DN_K_HEADS = 16
DN_V_HEADS = 32
DN_HEAD = 128
DN_CONV = 4
DN_CHUNK = 64
EPS = 1e-6
LOG2E = 1.4426950408889634

SWA_Q = SWA_Q_HEADS * SWA_HEAD_DIM
SWA_KV = SWA_KV_HEADS * SWA_HEAD_DIM
DN_KEY = DN_K_HEADS * DN_HEAD
DN_VAL = DN_V_HEADS * DN_HEAD
DN_CONV_CH = 2 * DN_KEY + DN_VAL

LANES = 128
SUBLANES = 8
VMEM_LIMIT = 56 * 1024 * 1024


def _cparams(sem, vmem=VMEM_LIMIT):
    return pltpu.CompilerParams(dimension_semantics=sem, vmem_limit_bytes=vmem)


def _sigmoid(x):
    return 0.5 * jnp.tanh(0.5 * x) + 0.5


def _silu(x):
    h = 0.5 * x
    return h + h * jnp.tanh(h)


def _adaln_body(c_ref, w_ref, b_ref, o_ref):
    s = _silu(c_ref[...]).astype(BF16)
    o_ref[...] = jnp.dot(s, w_ref[...].astype(BF16), preferred_element_type=F32) + b_ref[...]


def _adaln(c, w_ada, b_ada, tn=1024):
    B, D = c.shape
    n_out = w_ada.shape[1]
    c8 = jnp.zeros((SUBLANES, D), F32).at[:B].set(c)
    out = pl.pallas_call(
        _adaln_body,
        grid=(n_out // tn,),
        in_specs=[pl.BlockSpec((SUBLANES, D), lambda j: (0, 0)),
                  pl.BlockSpec((D, tn), lambda j: (0, j)),
                  pl.BlockSpec((1, tn), lambda j: (0, j))],
        out_specs=pl.BlockSpec((SUBLANES, tn), lambda j: (0, j)),
        out_shape=jax.ShapeDtypeStruct((SUBLANES, n_out), F32),
        compiler_params=_cparams(("arbitrary",)),
        name="adaln_mod",
    )(c8, w_ada, b_ada.reshape(1, n_out))
    return out[:B]


def _normmod_body(x_ref, nw_ref, sc_ref, sh_ref, pos_ref, invf_ref, o_ref, c_ref, s1_ref, s2_ref):
    x = x_ref[...]
    ms = jnp.mean(x * x, axis=-1, keepdims=True)
    y = x * lax.rsqrt(ms + EPS) * nw_ref[...]
    o_ref[...] = (y * (1.0 + sc_ref[0]) + sh_ref[0]).astype(BF16)

    half = ROPE_DIM // 2
    l64 = lax.broadcasted_iota(jnp.int32, (1, LANES), 1) & (SWA_HEAD_DIM - 1)
    ang = pos_ref[...] * invf_ref[...]
    sn = jnp.sin(ang)
    c_ref[...] = jnp.cos(ang)
    s1_ref[...] = jnp.where((l64 >= half) & (l64 < ROPE_DIM), sn, 0.0)
    s2_ref[...] = jnp.where(l64 < half, -sn, 0.0)


def _normmod(x2d, norm_w, scale, shift, posf, T, tm=1024):
    N, D = x2d.shape
    B = scale.shape[0]
    tm = min(tm, T)
    tps = T // tm
    tab = pl.BlockSpec((tm, LANES), lambda i: (i, 0))
    outs = pl.pallas_call(
        _normmod_body,
        grid=(N // tm,),
        in_specs=[pl.BlockSpec((tm, D), lambda i: (i, 0)),
                  pl.BlockSpec((1, D), lambda i: (0, 0)),
                  pl.BlockSpec((1, 1, D), lambda i: (i // tps, 0, 0)),
                  pl.BlockSpec((1, 1, D), lambda i: (i // tps, 0, 0)),
                  pl.BlockSpec((tm, 1), lambda i: (i, 0)),
                  pl.BlockSpec((1, LANES), lambda i: (0, 0))],
        out_specs=[pl.BlockSpec((tm, D), lambda i: (i, 0)), tab, tab, tab],
        out_shape=[jax.ShapeDtypeStruct((N, D), BF16)] + [jax.ShapeDtypeStruct((N, LANES), F32)] * 3,
        compiler_params=_cparams(("arbitrary",)),
        name="norm_mod",
    )(x2d, norm_w.reshape(1, D), scale.reshape(B, 1, D), shift.reshape(B, 1, D), posf,
      jnp.asarray(_rope_inv_freq_lanes()))
    return outs[0], tuple(outs[1:])


HALF = LANES // 2


NT_DIMS = (((1,), (1,)), ((), ()))


def _cast_weights(w_refs, w2_ref, wbf):
    wblk = w_refs[0].shape[0]
    off = 0 if w2_ref is None else HALF
    for b, w_ref in enumerate(w_refs):
        lo = max(b * wblk - off, 0)
        wbf[lo:(b + 1) * wblk - off, :] = w_ref[lo + off - b * wblk:, :].astype(BF16)
    if w2_ref is not None:
        n = len(w_refs) * wblk
        wbf[n - off:n, :] = w2_ref[...].astype(BF16)


def _cast_once(w_refs, w2_ref, wbf):
    @pl.when(pl.program_id(1) == 0)
    def _():
        _cast_weights(w_refs, w2_ref, wbf)


MM_AHEAD = 3


def _for_row_subtiles(h_ref, wbf, sub, epilogue):
    starts = list(range(0, h_ref.shape[0], sub))
    mm = lambda r0: lax.dot_general(h_ref[r0:r0 + sub, :], wbf[...], NT_DIMS, preferred_element_type=F32)
    queue = [mm(r0) for r0 in starts[:MM_AHEAD]]
    for idx, r0 in enumerate(starts):
        if idx + MM_AHEAD < len(starts):
            queue.append(mm(starts[idx + MM_AHEAD]))
        epilogue(r0, queue.pop(0))


def _proj_act_body(*refs, act, shifted, sub, nblk):
    h_ref, w_refs = refs[0], refs[1:1 + nblk]
    w2_ref = refs[1 + nblk] if shifted else None
    o_ref, wbf = refs[-2:]
    _cast_once(w_refs, w2_ref, wbf)

    def epilogue(r0, acc):
        if act == "silu":
            acc = _silu(acc)
        elif act == "sigmoid":
            acc = _sigmoid(acc)
        o_ref[r0:r0 + sub, :] = acc.astype(o_ref.dtype)

    _for_row_subtiles(h_ref, wbf, sub, epilogue)


def _proj_conv_body(*refs, tm, tps, norm, sub, nblk):
    h_ref, w_refs = refs[0], refs[1:1 + nblk]
    cw_ref, o_ref, wbf, buf = refs[1 + nblk:]
    i = pl.program_id(1)
    first = (i % tps) == 0
    _cast_once(w_refs, None, wbf)

    @pl.when(first)
    def _():
        buf[0:SUBLANES, :] = jnp.zeros((SUBLANES, buf.shape[1]), F32)

    @pl.when(jnp.logical_not(first))
    def _():
        buf[0:SUBLANES, :] = buf[tm:tm + SUBLANES, :]

    cw = cw_ref[...]

    def epilogue(r0, acc):
        b0 = SUBLANES + r0
        buf[b0:b0 + sub, :] = acc
        win = buf[b0 - SUBLANES:b0 + sub, :]
        y = win[SUBLANES:, :] * cw[DN_CONV - 1:DN_CONV, :]
        for d in range(1, DN_CONV):
            y = y + pltpu.roll(win, d, 0)[SUBLANES:, :] * cw[DN_CONV - 1 - d:DN_CONV - d, :]
        y = _silu(y)
        for c in range(y.shape[1] // DN_HEAD):
            yc = y[:, c * DN_HEAD:(c + 1) * DN_HEAD]
            if norm:
                yc = yc * lax.rsqrt(jnp.sum(yc * yc, axis=-1, keepdims=True) + EPS)
            o_ref[r0:r0 + sub, c * DN_HEAD:(c + 1) * DN_HEAD] = yc.astype(o_ref.dtype)

    _for_row_subtiles(h_ref, wbf, sub, epilogue)


def _norm_rope(x, nw, cs, s1, s2, lo):
    half = ROPE_DIM // 2
    x2 = x * x
    s_lo = jnp.sum(jnp.where(lo, x2, 0.0), axis=-1, keepdims=True)
    s_hi = jnp.sum(jnp.where(lo, 0.0, x2), axis=-1, keepdims=True)
    r = jnp.where(lo, lax.rsqrt(s_lo * (1.0 / SWA_HEAD_DIM) + EPS),
                  lax.rsqrt(s_hi * (1.0 / SWA_HEAD_DIM) + EPS))
    xn = x * r * nw
    return xn * cs + pltpu.roll(xn, half, 1) * s1 + pltpu.roll(xn, LANES - half, 1) * s2


def _proj_qknorm_body(*refs, sub, nblk, n_norm):
    h_ref, w_refs = refs[0], refs[1:1 + nblk]
    c_ref, s1_ref, s2_ref, nw_ref, o_ref, wbf = refs[1 + nblk:]
    _cast_once(w_refs, None, wbf)
    lo = lax.broadcasted_iota(jnp.int32, (1, LANES), 1) < SWA_HEAD_DIM
    nw = nw_ref[...]
    def epilogue(r0, acc):
        cs, s1, s2 = c_ref[r0:r0 + sub, :], s1_ref[r0:r0 + sub, :], s2_ref[r0:r0 + sub, :]
        for c in range(acc.shape[1] // LANES):
            x = acc[:, c * LANES:(c + 1) * LANES]
            if c < n_norm:
                x = _norm_rope(x, nw, cs, s1, s2, lo)
            o_ref[r0:r0 + sub, c * LANES:(c + 1) * LANES] = x.astype(o_ref.dtype)

    _for_row_subtiles(h_ref, wbf, sub, epilogue)


def _proj_gates_body(h_ref, w_ref, al_ref, dt_ref, beta_ref, gc_ref, wbf):
    _cast_once([w_ref], None, wbf)
    acc = lax.dot_general(h_ref[...], wbf[...], NT_DIMS, preferred_element_type=F32)
    beta_ref[...] = _sigmoid(acc)
    x = pltpu.roll(acc, LANES - DN_V_HEADS, 1) + dt_ref[...]
    sp = jnp.maximum(x, 0.0) + jnp.log(1.0 + jnp.exp(-jnp.abs(x)))
    g = -jnp.exp(al_ref[...]) * sp
    r64 = lax.broadcasted_iota(jnp.int32, g.shape, 0) & (DN_CHUNK - 1)
    s = 1
    while s < DN_CHUNK:
        g = g + jnp.where(r64 >= s, pltpu.roll(g, s, 0), 0.0)
        s *= 2
    gc_ref[...] = g


def _proj(h, w_t, layer, col0, width, mode, T, extras=(), tm=1024, wblk=512, nblk=2, sub=256,
          name="proj"):
    N, K = h.shape
    out_dtype = BF16
    tm = min(tm, T)
    sub = min(sub, tm)
    off = col0 % LANES
    base = col0 - off
    shifted = off != 0
    if mode == "gates":
        wblk, nblk = LANES, 1
    tn = wblk * nblk
    assert off in (0, HALF) and base % wblk == 0
    assert mode == "gates" or width % tn == 0
    jb = base // wblk
    grid = (max(width // tn, 1), N // tm)
    in_specs = [pl.BlockSpec((tm, K), lambda j, i: (i, 0))]
    in_specs += [pl.BlockSpec((None, wblk, K), lambda j, i, b=b: (layer, jb + j * nblk + b, 0))
                 for b in range(nblk)]
    args = [h] + [w_t] * nblk
    scratch = [pltpu.VMEM((tn, K), BF16)]
    out_spec = pl.BlockSpec((tm, tn), lambda j, i: (i, j))
    out_shape = jax.ShapeDtypeStruct((N, max(width, tn)), out_dtype)
    if shifted:
        in_specs.append(pl.BlockSpec((None, HALF, K),
                                     lambda j, i: (layer, (base + (j + 1) * tn) // HALF, 0)))
        args.append(w_t)
    if mode == "conv":
        cw, cw_col0, norm = extras
        in_specs.append(pl.BlockSpec((DN_CONV, tn), lambda j, i: (0, cw_col0 // tn + j)))
        args.append(cw)
        scratch.append(pltpu.VMEM((tm + SUBLANES, tn), F32))
        body = functools.partial(_proj_conv_body, tm=tm, tps=T // tm, norm=norm, sub=sub, nblk=nblk)
    elif mode == "qknorm":
        tabs, nw, n_norm = extras
        in_specs += [pl.BlockSpec((tm, LANES), lambda j, i: (i, 0))] * 3
        in_specs.append(pl.BlockSpec((1, LANES), lambda j, i: (0, 0)))
        args += list(tabs) + [nw]
        body = functools.partial(_proj_qknorm_body, sub=sub, nblk=nblk, n_norm=n_norm)
    elif mode == "gates":
        in_specs += [pl.BlockSpec((1, LANES), lambda j, i: (0, 0))] * 2
        args += list(extras)
        body = _proj_gates_body
        out_spec = [out_spec, out_spec]
        out_shape = [jax.ShapeDtypeStruct((N, LANES), F32)] * 2
    else:
        body = functools.partial(_proj_act_body, act=mode, shifted=shifted, sub=sub if mode else tm,
                                 nblk=nblk)
    return pl.pallas_call(
        body, grid=grid, in_specs=in_specs, out_specs=out_spec, out_shape=out_shape,
        scratch_shapes=scratch,
        compiler_params=_cparams(("arbitrary", "arbitrary")),
        name=name,
    )(*args)


def _rope_inv_freq_lanes():
    half = ROPE_DIM // 2
    inv = (np.float32(ROPE_THETA) ** (-np.arange(half, dtype=np.float32) * np.float32(2.0 / ROPE_DIM))).astype(np.float32)
    lanes = np.zeros((1, LANES), np.float32)
    for l in range(LANES):
        m = l % SWA_HEAD_DIM
        if m < ROPE_DIM:
            lanes[0, l] = inv[m % half]
    return lanes


def _swa_body(sink_ref, q_ref, kv_ref, g_ref, o_ref, kprev, vprev):
    n = pl.program_id(1)
    BLK = SWA_BLOCK
    HD = SWA_HEAD_DIM

    @pl.when(n == 0)
    def _():
        kprev[...] = jnp.zeros_like(kprev)
        vprev[...] = jnp.zeros_like(vprev)

    lo = lax.broadcasted_iota(jnp.int32, (1, LANES), 1) < HD
    n_kc = SWA_KV // LANES
    nblocks = q_ref.shape[0] // BLK
    zrows = (lax.broadcasted_iota(jnp.int32, (HD, BLK), 0) == 0).astype(F32)
    top = lax.broadcasted_iota(jnp.int32, (LANES, 1), 0) < HD

    kband, vbandT = {}, {}
    k_last = {idx: kprev[idx] for idx in range(2 * SWA_KV_HEADS)}
    v_last = {idx: vprev[idx] for idx in range(2 * SWA_KV_HEADS)}
    for bi in range(nblocks):
        kv = kv_ref[bi * BLK:(bi + 1) * BLK, :]
        kc = [kv[:, c * LANES:(c + 1) * LANES].astype(F32) for c in range(n_kc)]
        vT = [kv[:, SWA_KV + c * LANES:SWA_KV + (c + 1) * LANES].astype(F32).T for c in range(n_kc)]
        for j in range(SWA_KV_HEADS):
            cj, b = j // 2, j % 2
            for a in range(2):
                src = kc[cj] if a == b else pltpu.roll(kc[cj], HD, 1)
                k_cur = jnp.where(lo if a == 0 else jnp.logical_not(lo), src, 0.0).astype(BF16)
                rows = vT[cj][b * HD:(b + 1) * HD]
                v_cur = jnp.concatenate([rows, zrows] if a == 0 else [zrows, rows], axis=0).astype(BF16)
                idx = 2 * j + a
                kband[(bi, j, a)] = jnp.concatenate([k_last[idx], k_cur], axis=0)
                vbandT[(bi, j, a)] = jnp.concatenate([v_last[idx], v_cur], axis=1)
                k_last[idx], v_last[idx] = k_cur, v_cur
    for idx in range(2 * SWA_KV_HEADS):
        kprev[idx] = k_last[idx]
        vprev[idx] = v_last[idx]

    kj = lax.broadcasted_iota(jnp.int32, (2 * BLK, 2 * BLK), 0)
    qi = lax.broadcasted_iota(jnp.int32, (2 * BLK, 2 * BLK), 1) & (BLK - 1)
    d = kj - qi
    in_window = (d >= 1) & (d <= BLK)
    kmin = jnp.where(n == 0, BLK, 0)
    valid = [in_window & (kj >= kmin)] + [in_window] * (nblocks - 1)
    left = lax.broadcasted_iota(jnp.int32, (1, 2 * BLK), 1) < BLK

    G = SWA_Q_HEADS // SWA_KV_HEADS
    cpk = G // 2
    items = [(bi, j, a, j * cpk + 2 * p) for bi in range(nblocks) for j in range(SWA_KV_HEADS)
             for a in range(2) for p in range(cpk // 2)]

    def scores(item):
        bi, j, a, c1 = item
        rows = slice(bi * BLK, (bi + 1) * BLK)
        rhs = jnp.concatenate([q_ref[rows, c1 * LANES:(c1 + 1) * LANES],
                               q_ref[rows, (c1 + 1) * LANES:(c1 + 2) * LANES]], axis=0)
        return lax.dot_general(kband[(bi, j, a)], rhs, NT_DIMS, preferred_element_type=F32)

    acc = {}

    def consume(item, oT, sink_term):
        bi, j, a, c1 = item
        rows = slice(bi * BLK, (bi + 1) * BLK)
        r1 = (1 - a) * HD
        oT = oT * (1.0 / (oT[r1:r1 + 1, :] + sink_term))
        for c, part in ((c1, oT[:, :BLK]), (c1 + 1, oT[:, BLK:])):
            if a == 0:
                acc[c] = part
            else:
                gate = g_ref[rows, c * LANES:(c + 1) * LANES].astype(F32)
                o_pair = jnp.where(top, acc.pop(c), part)
                o_ref[rows, c * LANES:(c + 1) * LANES] = (o_pair.T * gate).astype(BF16)

    ahead = 4
    queue = [scores(it) for it in items[:ahead]]
    pending = None
    for i, item in enumerate(items):
        bi, j, a, c1 = item
        if i + ahead < len(items):
            queue.append(scores(items[i + ahead]))
        s_cur = queue.pop(0)
        s = jnp.where(valid[bi], s_cur, -jnp.inf)
        sink = jnp.where(left, sink_ref[2 * c1 + a], sink_ref[2 * c1 + 2 + a])
        m = jnp.maximum(jnp.max(s, axis=0, keepdims=True), sink)
        e = jnp.exp2(s - m)
        oT = jnp.dot(vbandT[(bi, j, a)], e.astype(BF16), preferred_element_type=F32)
        if pending is not None:
            consume(*pending)
        pending = (item, oT, jnp.exp2(sink - m))
    consume(*pending)


def _swa(q_n, kv_n, sag, sinks, B, T, blocks_per_step=2):
    N = B * T
    rows = blocks_per_step * SWA_BLOCK
    nb = T // rows
    row = lambda b, n: b * nb + n
    return pl.pallas_call(
        _swa_body,
        grid=(B, nb),
        in_specs=[pl.BlockSpec(memory_space=pltpu.SMEM),
                  pl.BlockSpec((rows, SWA_Q), lambda b, n: (row(b, n), 0)),
                  pl.BlockSpec((rows, 2 * SWA_KV), lambda b, n: (row(b, n), 0)),
                  pl.BlockSpec((rows, SWA_Q), lambda b, n: (row(b, n), 0))],
        out_specs=pl.BlockSpec((rows, SWA_Q), lambda b, n: (row(b, n), 0)),
        out_shape=jax.ShapeDtypeStruct((N, SWA_Q), BF16),
        scratch_shapes=[pltpu.VMEM((2 * SWA_KV_HEADS, SWA_BLOCK, LANES), BF16),
                        pltpu.VMEM((2 * SWA_KV_HEADS, LANES, SWA_BLOCK), BF16)],
        compiler_params=_cparams(("arbitrary", "arbitrary")),
        name="swa_attn",
    )(sinks, q_n, kv_n, sag)


def _delta_body(q_ref, k_ref, v_ref, z_ref, bc_ref, gc_ref, gr_ref, nw_ref, o_ref, S_ref,
                *, KH, TT):
    grp = pl.program_id(1)
    t = pl.program_id(2)
    C = DN_CHUNK
    NC = TT // C
    scale = DN_HEAD ** -0.5
    NT = (((1,), (1,)), ((), ()))

    @pl.when(t == 0)
    def _():
        S_ref[...] = jnp.zeros_like(S_ref)

    NV = 2 * KH
    row = lax.broadcasted_iota(jnp.int32, (TT, TT), 0)
    col = lax.broadcasted_iota(jnp.int32, (TT, TT), 1)
    same = (row // C) == (col // C)
    ri = lax.broadcasted_iota(jnp.int32, (C, TT), 0)
    cj = lax.broadcasted_iota(jnp.int32, (C, TT), 1)
    cjc = cj // C
    cjj = cj & (C - 1)
    tril_c = ri >= cjj
    strict_c = ri > cjj
    eye_c = (ri == cjj).astype(F32)
    lane = lax.broadcasted_iota(jnp.int32, (TT, LANES), 1)
    lchunk = lax.broadcasted_iota(jnp.int32, (1, TT), 1) // C
    bcall = bc_ref[...]
    gcall = gc_ref[...]
    nw = nw_ref[...]

    def compress(x):
        out = x[(NC - 1) * C:NC * C]
        for c in range(NC - 2, -1, -1):
            out = jnp.where(cjc == c, x[c * C:(c + 1) * C], out)
        return out

    def by_row(colvec):
        out = jnp.broadcast_to(colvec[(NC - 1) * C:NC * C], (C, TT))
        for c in range(NC - 2, -1, -1):
            out = jnp.where(cjc == c, colvec[c * C:(c + 1) * C], out)
        return out

    def bd(xc):
        return jnp.where(same, jnp.concatenate([xc] * NC, axis=0), 0.0).astype(BF16)

    kf, qf, kT, kkc, qkc = [], [], [], [], []
    for kh in range(KH):
        q = q_ref[:, kh * DN_HEAD:(kh + 1) * DN_HEAD]
        k = k_ref[:, kh * DN_HEAD:(kh + 1) * DN_HEAD]
        kq = lax.dot_general(jnp.concatenate([k, q], axis=0), k, NT, preferred_element_type=F32)
        kkc.append(compress(kq[:TT]))
        qkc.append(compress(kq[TT:]))
        kf.append(k.astype(F32))
        qf.append(q.astype(F32))
        kT.append(kf[-1].T)

    bcol, gcol, grow, Lc, qkm, P = [], [], [], [], [], []
    for hh in range(NV):
        hglob = grp * NV + hh
        sel = lane == hglob
        bcol.append(jnp.sum(jnp.where(sel, bcall, 0.0), axis=-1, keepdims=True))
        gcol.append(jnp.sum(jnp.where(sel, gcall, 0.0), axis=-1, keepdims=True))
        grow.append(gr_ref[0, pl.ds(hglob, 1), :])
        dec = jnp.exp(jnp.where(tril_c, by_row(gcol[hh]) - grow[hh], -jnp.inf))
        Lc.append(jnp.where(strict_c, by_row(bcol[hh]) * kkc[hh // 2] * dec, 0.0))
        qkm.append(qkc[hh // 2] * dec * scale)
        P.append(eye_c - Lc[hh])

    Lp = [jnp.dot(Lc[hh].astype(BF16), bd(Lc[hh]), preferred_element_type=F32) for hh in range(NV)]
    step = 4
    while step < C:
        res = [jnp.dot(jnp.concatenate([Lp[hh], P[hh]], axis=0).astype(BF16), bd(Lp[hh]),
                       preferred_element_type=F32) for hh in range(NV)]
        Lp = [r[:C] for r in res]
        P = [P[hh] + res[hh][C:] for hh in range(NV)]
        step *= 2
    P = [P[hh] + jnp.dot(P[hh].astype(BF16), bd(Lp[hh]), preferred_element_type=F32) for hh in range(NV)]

    uw, qd, kdT, eg_last = [], [], [], []
    for hh in range(NV):
        eg = jnp.exp(gcol[hh])
        vf = v_ref[:, hh * DN_HEAD:(hh + 1) * DN_HEAD].astype(F32)
        rhs = jnp.concatenate([vf * bcol[hh], kf[hh // 2] * (bcol[hh] * eg)], axis=1).astype(BF16)
        uw.append(jnp.dot(bd(P[hh]), rhs, preferred_element_type=F32))
        qd.append((qf[hh // 2] * (eg * scale)).astype(BF16))
        glrow = jnp.zeros((1, TT), F32)
        for c in range(NC):
            glrow = jnp.where(lchunk == c, gcol[hh][c * C + C - 1:c * C + C, :], glrow)
        kdT.append(kT[hh // 2] * jnp.exp(glrow - grow[hh]))
        eg_last.append([eg[c * C + C - 1:c * C + C, :] for c in range(NC)])

    S = [S_ref[hh] for hh in range(NV)]
    vn = [[] for _ in range(NV)]
    o_inter = [[] for _ in range(NV)]
    for c in range(NC):
        r0, r1 = c * C, (c + 1) * C
        res = []
        for hh in range(NV):
            w_c = uw[hh][r0:r1, DN_HEAD:].astype(BF16)
            lhs = jnp.concatenate([w_c, qd[hh][r0:r1]], axis=0)
            res.append(jnp.dot(lhs, S[hh].astype(BF16), preferred_element_type=F32))
        inc = []
        for hh in range(NV):
            vn_c = (uw[hh][r0:r1, :DN_HEAD] - res[hh][:C]).astype(BF16)
            vn[hh].append(vn_c)
            o_inter[hh].append(res[hh][C:])
            kd_c = kdT[hh][:, r0:r1].astype(BF16)
            inc.append(jnp.dot(kd_c, vn_c, preferred_element_type=F32))
        for hh in range(NV):
            S[hh] = S[hh] * eg_last[hh][c] + inc[hh]

    for hh in range(NV):
        S_ref[hh] = S[hh]
        vn_all = jnp.concatenate(vn[hh], axis=0)
        o = jnp.concatenate(o_inter[hh], axis=0) + jnp.dot(bd(qkm[hh]), vn_all, preferred_element_type=F32)
        ms = jnp.mean(o * o, axis=-1, keepdims=True)
        z = z_ref[:, hh * DN_HEAD:(hh + 1) * DN_HEAD].astype(F32)
        o_ref[:, hh * DN_HEAD:(hh + 1) * DN_HEAD] = (o * lax.rsqrt(ms + EPS) * nw * z).astype(BF16)


def _delta(qk_n, vv, sz, beta, gc, gcT, dn_norm_w, B, T, KH=8, TT=256):
    N = B * T
    nt = T // TT
    ng = DN_K_HEADS // KH
    row = lambda b, g, t: b * nt + t
    kw = KH * DN_HEAD
    return pl.pallas_call(
        functools.partial(_delta_body, KH=KH, TT=TT),
        grid=(B, ng, nt),
        in_specs=[pl.BlockSpec((TT, kw), lambda b, g, t: (row(b, g, t), g)),
                  pl.BlockSpec((TT, kw), lambda b, g, t: (row(b, g, t), DN_KEY // kw + g)),
                  pl.BlockSpec((TT, 2 * kw), lambda b, g, t: (row(b, g, t), g)),
                  pl.BlockSpec((TT, 2 * kw), lambda b, g, t: (row(b, g, t), g)),
                  pl.BlockSpec((TT, LANES), lambda b, g, t: (row(b, g, t), 0)),
                  pl.BlockSpec((TT, LANES), lambda b, g, t: (row(b, g, t), 0)),
                  pl.BlockSpec((1, DN_V_HEADS, TT), lambda b, g, t: (b, 0, t)),
                  pl.BlockSpec((1, DN_HEAD), lambda b, g, t: (0, 0))],
        out_specs=pl.BlockSpec((TT, 2 * kw), lambda b, g, t: (row(b, g, t), g)),
        out_shape=jax.ShapeDtypeStruct((N, DN_VAL), BF16),
        scratch_shapes=[pltpu.VMEM((2 * KH, DN_HEAD, DN_HEAD), F32)],
        compiler_params=_cparams(("arbitrary", "arbitrary", "arbitrary")),
        name="delta_rule",
    )(qk_n, qk_n, vv, sz, beta, gc, gcT, dn_norm_w.reshape(1, DN_HEAD))


def _merge_out_body(a_ref, o_ref, gm_ref, wa_ref, wb_ref, wo_ref, x_ref, g_ref, out_ref):
    D = out_ref.shape[1]
    ya = jnp.dot(a_ref[...], wa_ref[...], preferred_element_type=F32)
    yb = jnp.dot(o_ref[...], wb_ref[...], preferred_element_type=F32)
    y = (gm_ref[:, :D].astype(F32) * ya + gm_ref[:, D:].astype(F32) * yb).astype(BF16)
    out_ref[...] = x_ref[...] + g_ref[0] * jnp.dot(y, wo_ref[...], preferred_element_type=F32)


def _merge_out(a_g, o_g, p_m, wa, wb, wo, x2d, gate, T, tm=256):
    N, D = x2d.shape
    B = gate.shape[0]
    tps = T // tm
    resident = lambda w: pl.BlockSpec(w.shape, lambda i: (0, 0), pipeline_mode=pl.Buffered(1))
    return pl.pallas_call(
        _merge_out_body,
        grid=(N // tm,),
        in_specs=[pl.BlockSpec((tm, a_g.shape[1]), lambda i: (i, 0)),
                  pl.BlockSpec((tm, o_g.shape[1]), lambda i: (i, 0)),
                  pl.BlockSpec((tm, p_m.shape[1]), lambda i: (i, 0)),
                  resident(wa), resident(wb), resident(wo),
                  pl.BlockSpec((tm, D), lambda i: (i, 0)),
                  pl.BlockSpec((1, 1, D), lambda i: (i // tps, 0, 0))],
        out_specs=pl.BlockSpec((tm, D), lambda i: (i, 0)),
        out_shape=jax.ShapeDtypeStruct((N, D), F32),
        compiler_params=_cparams(("arbitrary",), vmem=60 * 1024 * 1024),
        name="merge_out",
    )(a_g, o_g, p_m, wa, wb, wo, x2d, gate.reshape(B, 1, D))


def _layer(l, x, c, positions, w_ada, b_ada, norm_w, w_in, q_norm_w, k_norm_w, sinks,
           conv_w, a_log, dt_bias, dn_norm_w, w_o_swa, w_o_dn, w_out):
    B, T, D = x.shape
    N = B * T
    x2d = x.reshape(N, D)

    mod = _adaln(c, w_ada, b_ada)
    shift, scale, gate = mod[:, :D], mod[:, D:2 * D], mod[:, 2 * D:]
    posf = positions.astype(F32).reshape(N, 1)
    h, tabs = _normmod(x2d, norm_w, scale, shift, posf, T)

    o_q, o_g = 0, SWA_Q + 2 * SWA_KV
    o_d = o_g + SWA_Q
    o_z = o_d + DN_CONV_CH
    o_b = o_z + DN_VAL
    o_a = o_b + DN_V_HEADS
    o_m = o_a + DN_V_HEADS
    assert o_a == o_b + DN_V_HEADS and o_b % LANES == 0
    w_in = jnp.swapaxes(w_in, 1, 2)
    rep = LANES // SWA_HEAD_DIM
    qw = jnp.tile(q_norm_w.reshape(1, SWA_HEAD_DIM), (1, rep)) * (SWA_HEAD_DIM ** -0.5 * LOG2E)
    kw = jnp.tile(k_norm_w.reshape(1, SWA_HEAD_DIM), (1, rep))
    q_n = _proj(h, w_in, l, o_q, SWA_Q, "qknorm", T, extras=(tabs, qw, SWA_Q // LANES), name="proj_q")
    kv_n = _proj(h, w_in, l, o_q + SWA_Q, 2 * SWA_KV, "qknorm", T, nblk=1,
                 extras=(tabs, kw, SWA_KV // LANES), name="proj_kv")
    sag = _proj(h, w_in, l, o_g, o_d - o_g, "silu", T, tm=2048, name="proj_swa_gate")
    qk_n = _proj(h, w_in, l, o_d, 2 * DN_KEY, "conv", T, extras=(conv_w, 0, True), name="proj_dn_qk")
    vv = _proj(h, w_in, l, o_d + 2 * DN_KEY, DN_VAL, "conv", T, extras=(conv_w, 2 * DN_KEY, False),
               name="proj_dn_v")
    sz = _proj(h, w_in, l, o_z, o_b - o_z, "silu", T, tm=2048, name="proj_dn_z")
    al = jnp.zeros((1, LANES), F32).at[0, :DN_V_HEADS].set(a_log)
    dt = jnp.zeros((1, LANES), F32).at[0, :DN_V_HEADS].set(dt_bias)
    beta, gc = _proj(h, w_in, l, o_b, 2 * DN_V_HEADS, "gates", T, extras=(al, dt), name="proj_dn_gates")
    p_m = _proj(h, w_in, l, o_m, 2 * D, "sigmoid", T, tm=2048, name="proj_merge_gates")

    a_g = _swa(q_n, kv_n, sag, sinks * LOG2E, B, T)

    gcT = gc[:, :DN_V_HEADS].reshape(B, T, DN_V_HEADS).transpose(0, 2, 1)
    o_g2 = _delta(qk_n, vv, sz, beta, gc, gcT, dn_norm_w, B, T)

    out = _merge_out(a_g, o_g2, p_m, w_o_swa.astype(BF16), w_o_dn.astype(BF16), w_out.astype(BF16),
                     x2d, gate, T)
    return out.reshape(B, T, D)


def kernel(x, c, positions, w_ada, b_ada, norm_w, w_in, q_norm_w, k_norm_w, sinks, conv_w,
           a_log, dt_bias, dn_norm_w, w_o_swa, w_o_dn, w_out):
    depth = w_ada.shape[0]
    for l in range(depth):
        x = _layer(l, x, c, positions, w_ada[l], b_ada[l], norm_w[l], w_in, q_norm_w[l],
                   k_norm_w[l], sinks[l], conv_w[l], a_log[l], dt_bias[l], dn_norm_w[l],
                   w_o_swa[l], w_o_dn[l], w_out[l])
    return x
```

```python
import functools

import numpy as np
import jax
import jax.numpy as jnp
from jax import lax
from jax.experimental import pallas as pl
from jax.experimental.pallas import tpu as pltpu

F32 = jnp.float32
BF16 = jnp.bfloat16

D_MODEL = 2048
SWA_Q_HEADS = 32
SWA_KV_HEADS = 4
SWA_HEAD_DIM = 64
SWA_BLOCK = 128
ROPE_THETA = 500000.0
ROPE_DIM = SWA_HEAD_DIM // 4
DN_K_HEADS = 16
DN_V_HEADS = 32
DN_HEAD = 128
DN_CONV = 4
DN_CHUNK = 64
EPS = 1e-6
LOG2E = 1.4426950408889634

SWA_Q = SWA_Q_HEADS * SWA_HEAD_DIM
SWA_KV = SWA_KV_HEADS * SWA_HEAD_DIM
DN_KEY = DN_K_HEADS * DN_HEAD
DN_VAL = DN_V_HEADS * DN_HEAD
DN_CONV_CH = 2 * DN_KEY + DN_VAL

LANES = 128
SUBLANES = 8
VMEM_LIMIT = 56 * 1024 * 1024


def _cparams(sem, vmem=VMEM_LIMIT):
    return pltpu.CompilerParams(dimension_semantics=sem, vmem_limit_bytes=vmem)


def _sigmoid(x):
    return 0.5 * jnp.tanh(0.5 * x) + 0.5


def _silu(x):
    h = 0.5 * x
    return h + h * jnp.tanh(h)


def _adaln_body(c_ref, w_ref, b_ref, o_ref):
    s = _silu(c_ref[...]).astype(BF16)
    o_ref[...] = jnp.dot(s, w_ref[...].astype(BF16), preferred_element_type=F32) + b_ref[...]


def _adaln(c, w_ada, b_ada, tn=1024):
    B, D = c.shape
    n_out = w_ada.shape[1]
    c8 = jnp.zeros((SUBLANES, D), F32).at[:B].set(c)
    out = pl.pallas_call(
        _adaln_body,
        grid=(n_out // tn,),
        in_specs=[pl.BlockSpec((SUBLANES, D), lambda j: (0, 0)),
                  pl.BlockSpec((D, tn), lambda j: (0, j)),
                  pl.BlockSpec((1, tn), lambda j: (0, j))],
        out_specs=pl.BlockSpec((SUBLANES, tn), lambda j: (0, j)),
        out_shape=jax.ShapeDtypeStruct((SUBLANES, n_out), F32),
        compiler_params=_cparams(("arbitrary",)),
        name="adaln_mod",
    )(c8, w_ada, b_ada.reshape(1, n_out))
    return out[:B]


def _normmod_body(x_ref, nw_ref, sc_ref, sh_ref, pos_ref, invf_ref, o_ref, c_ref, s1_ref, s2_ref):
    x = x_ref[...]
    ms = jnp.mean(x * x, axis=-1, keepdims=True)
    y = x * lax.rsqrt(ms + EPS) * nw_ref[...]
    o_ref[...] = (y * (1.0 + sc_ref[0]) + sh_ref[0]).astype(BF16)

    half = ROPE_DIM // 2
    l64 = lax.broadcasted_iota(jnp.int32, (1, LANES), 1) & (SWA_HEAD_DIM - 1)
    ang = pos_ref[...] * invf_ref[...]
    sn = jnp.sin(ang)
    c_ref[...] = jnp.cos(ang)
    s1_ref[...] = jnp.where((l64 >= half) & (l64 < ROPE_DIM), sn, 0.0)
    s2_ref[...] = jnp.where(l64 < half, -sn, 0.0)


def _normmod(x2d, norm_w, scale, shift, posf, T, tm=1024):
    N, D = x2d.shape
    B = scale.shape[0]
    tm = min(tm, T)
    tps = T // tm
    tab = pl.BlockSpec((tm, LANES), lambda i: (i, 0))
    outs = pl.pallas_call(
        _normmod_body,
        grid=(N // tm,),
        in_specs=[pl.BlockSpec((tm, D), lambda i: (i, 0)),
                  pl.BlockSpec((1, D), lambda i: (0, 0)),
                  pl.BlockSpec((1, 1, D), lambda i: (i // tps, 0, 0)),
                  pl.BlockSpec((1, 1, D), lambda i: (i // tps, 0, 0)),
                  pl.BlockSpec((tm, 1), lambda i: (i, 0)),
                  pl.BlockSpec((1, LANES), lambda i: (0, 0))],
        out_specs=[pl.BlockSpec((tm, D), lambda i: (i, 0)), tab, tab, tab],
        out_shape=[jax.ShapeDtypeStruct((N, D), BF16)] + [jax.ShapeDtypeStruct((N, LANES), F32)] * 3,
        compiler_params=_cparams(("arbitrary",)),
        name="norm_mod",
    )(x2d, norm_w.reshape(1, D), scale.reshape(B, 1, D), shift.reshape(B, 1, D), posf,
      jnp.asarray(_rope_inv_freq_lanes()))
    return outs[0], tuple(outs[1:])


HALF = LANES // 2


NT_DIMS = (((1,), (1,)), ((), ()))


def _cast_weights(w_refs, w2_ref, wbf):
    wblk = w_refs[0].shape[0]
    off = 0 if w2_ref is None else HALF
    for b, w_ref in enumerate(w_refs):
        lo = max(b * wblk - off, 0)
        wbf[lo:(b + 1) * wblk - off, :] = w_ref[lo + off - b * wblk:, :].astype(BF16)
    if w2_ref is not None:
        n = len(w_refs) * wblk
        wbf[n - off:n, :] = w2_ref[...].astype(BF16)


def _cast_once(w_refs, w2_ref, wbf):
    @pl.when(pl.program_id(1) == 0)
    def _():
        _cast_weights(w_refs, w2_ref, wbf)


MM_AHEAD = 3


def _for_row_subtiles(h_ref, wbf, sub, epilogue):
    starts = list(range(0, h_ref.shape[0], sub))
    mm = lambda r0: lax.dot_general(h_ref[r0:r0 + sub, :], wbf[...], NT_DIMS, preferred_element_type=F32)
    queue = [mm(r0) for r0 in starts[:MM_AHEAD]]
    for idx, r0 in enumerate(starts):
        if idx + MM_AHEAD < len(starts):
            queue.append(mm(starts[idx + MM_AHEAD]))
        epilogue(r0, queue.pop(0))


def _proj_act_body(*refs, act, shifted, sub, nblk):
    h_ref, w_refs = refs[0], refs[1:1 + nblk]
    w2_ref = refs[1 + nblk] if shifted else None
    o_ref, wbf = refs[-2:]
    _cast_once(w_refs, w2_ref, wbf)

    def epilogue(r0, acc):
        if act == "silu":
            acc = _silu(acc)
        elif act == "sigmoid":
            acc = _sigmoid(acc)
        o_ref[r0:r0 + sub, :] = acc.astype(o_ref.dtype)

    _for_row_subtiles(h_ref, wbf, sub, epilogue)


def _proj_conv_body(*refs, tm, tps, norm, sub, nblk):
    h_ref, w_refs = refs[0], refs[1:1 + nblk]
    cw_ref, o_ref, wbf, buf = refs[1 + nblk:]
    i = pl.program_id(1)
    first = (i % tps) == 0
    _cast_once(w_refs, None, wbf)

    @pl.when(first)
    def _():
        buf[0:SUBLANES, :] = jnp.zeros((SUBLANES, buf.shape[1]), F32)

    @pl.when(jnp.logical_not(first))
    def _():
        buf[0:SUBLANES, :] = buf[tm:tm + SUBLANES, :]

    cw = cw_ref[...]

    def epilogue(r0, acc):
        b0 = SUBLANES + r0
        buf[b0:b0 + sub, :] = acc
        win = buf[b0 - SUBLANES:b0 + sub, :]
        y = win[SUBLANES:, :] * cw[DN_CONV - 1:DN_CONV, :]
        for d in range(1, DN_CONV):
            y = y + pltpu.roll(win, d, 0)[SUBLANES:, :] * cw[DN_CONV - 1 - d:DN_CONV - d, :]
        y = _silu(y)
        for c in range(y.shape[1] // DN_HEAD):
            yc = y[:, c * DN_HEAD:(c + 1) * DN_HEAD]
            if norm:
                yc = yc * lax.rsqrt(jnp.sum(yc * yc, axis=-1, keepdims=True) + EPS)
            o_ref[r0:r0 + sub, c * DN_HEAD:(c + 1) * DN_HEAD] = yc.astype(o_ref.dtype)

    _for_row_subtiles(h_ref, wbf, sub, epilogue)


def _norm_rope(x, nw, cs, s1, s2, lo):
    half = ROPE_DIM // 2
    x2 = x * x
    s_lo = jnp.sum(jnp.where(lo, x2, 0.0), axis=-1, keepdims=True)
    s_hi = jnp.sum(jnp.where(lo, 0.0, x2), axis=-1, keepdims=True)
    r = jnp.where(lo, lax.rsqrt(s_lo * (1.0 / SWA_HEAD_DIM) + EPS),
                  lax.rsqrt(s_hi * (1.0 / SWA_HEAD_DIM) + EPS))
    xn = x * r * nw
    return xn * cs + pltpu.roll(xn, half, 1) * s1 + pltpu.roll(xn, LANES - half, 1) * s2


def _proj_qknorm_body(*refs, sub, nblk, n_norm):
    h_ref, w_refs = refs[0], refs[1:1 + nblk]
    c_ref, s1_ref, s2_ref, nw_ref, o_ref, wbf = refs[1 + nblk:]
    _cast_once(w_refs, None, wbf)
    lo = lax.broadcasted_iota(jnp.int32, (1, LANES), 1) < SWA_HEAD_DIM
    nw = nw_ref[...]
    def epilogue(r0, acc):
        cs, s1, s2 = c_ref[r0:r0 + sub, :], s1_ref[r0:r0 + sub, :], s2_ref[r0:r0 + sub, :]
        for c in range(acc.shape[1] // LANES):
            x = acc[:, c * LANES:(c + 1) * LANES]
            if c < n_norm:
                x = _norm_rope(x, nw, cs, s1, s2, lo)
            o_ref[r0:r0 + sub, c * LANES:(c + 1) * LANES] = x.astype(o_ref.dtype)

    _for_row_subtiles(h_ref, wbf, sub, epilogue)


def _proj_gates_body(h_ref, w_ref, al_ref, dt_ref, beta_ref, gc_ref, wbf):
    _cast_once([w_ref], None, wbf)
    acc = lax.dot_general(h_ref[...], wbf[...], NT_DIMS, preferred_element_type=F32)
    beta_ref[...] = _sigmoid(acc)
    x = pltpu.roll(acc, LANES - DN_V_HEADS, 1) + dt_ref[...]
    sp = jnp.maximum(x, 0.0) + jnp.log(1.0 + jnp.exp(-jnp.abs(x)))
    g = -jnp.exp(al_ref[...]) * sp
    r64 = lax.broadcasted_iota(jnp.int32, g.shape, 0) & (DN_CHUNK - 1)
    s = 1
    while s < DN_CHUNK:
        g = g + jnp.where(r64 >= s, pltpu.roll(g, s, 0), 0.0)
        s *= 2
    gc_ref[...] = g


def _proj(h, w_t, layer, col0, width, mode, T, extras=(), tm=1024, wblk=512, nblk=2, sub=256,
          name="proj"):
    N, K = h.shape
    out_dtype = BF16
    tm = min(tm, T)
    sub = min(sub, tm)
    off = col0 % LANES
    base = col0 - off
    shifted = off != 0
    if mode == "gates":
        wblk, nblk = LANES, 1
    tn = wblk * nblk
    assert off in (0, HALF) and base % wblk == 0
    assert mode == "gates" or width % tn == 0
    jb = base // wblk
    grid = (max(width // tn, 1), N // tm)
    in_specs = [pl.BlockSpec((tm, K), lambda j, i: (i, 0))]
    in_specs += [pl.BlockSpec((None, wblk, K), lambda j, i, b=b: (layer, jb + j * nblk + b, 0))
                 for b in range(nblk)]
    args = [h] + [w_t] * nblk
    scratch = [pltpu.VMEM((tn, K), BF16)]
    out_spec = pl.BlockSpec((tm, tn), lambda j, i: (i, j))
    out_shape = jax.ShapeDtypeStruct((N, max(width, tn)), out_dtype)
    if shifted:
        in_specs.append(pl.BlockSpec((None, HALF, K),
                                     lambda j, i: (layer, (base + (j + 1) * tn) // HALF, 0)))
        args.append(w_t)
    if mode == "conv":
        cw, cw_col0, norm = extras
        in_specs.append(pl.BlockSpec((DN_CONV, tn), lambda j, i: (0, cw_col0 // tn + j)))
        args.append(cw)
        scratch.append(pltpu.VMEM((tm + SUBLANES, tn), F32))
        body = functools.partial(_proj_conv_body, tm=tm, tps=T // tm, norm=norm, sub=sub, nblk=nblk)
    elif mode == "qknorm":
        tabs, nw, n_norm = extras
        in_specs += [pl.BlockSpec((tm, LANES), lambda j, i: (i, 0))] * 3
        in_specs.append(pl.BlockSpec((1, LANES), lambda j, i: (0, 0)))
        args += list(tabs) + [nw]
        body = functools.partial(_proj_qknorm_body, sub=sub, nblk=nblk, n_norm=n_norm)
    elif mode == "gates":
        in_specs += [pl.BlockSpec((1, LANES), lambda j, i: (0, 0))] * 2
        args += list(extras)
        body = _proj_gates_body
        out_spec = [out_spec, out_spec]
        out_shape = [jax.ShapeDtypeStruct((N, LANES), F32)] * 2
    else:
        body = functools.partial(_proj_act_body, act=mode, shifted=shifted, sub=sub if mode else tm,
                                 nblk=nblk)
    return pl.pallas_call(
        body, grid=grid, in_specs=in_specs, out_specs=out_spec, out_shape=out_shape,
        scratch_shapes=scratch,
        compiler_params=_cparams(("arbitrary", "arbitrary")),
        name=name,
    )(*args)


def _rope_inv_freq_lanes():
    half = ROPE_DIM // 2
    inv = (np.float32(ROPE_THETA) ** (-np.arange(half, dtype=np.float32) * np.float32(2.0 / ROPE_DIM))).astype(np.float32)
    lanes = np.zeros((1, LANES), np.float32)
    for l in range(LANES):
        m = l % SWA_HEAD_DIM
        if m < ROPE_DIM:
            lanes[0, l] = inv[m % half]
    return lanes


def _swa_body(sink_ref, q_ref, kv_ref, g_ref, o_ref, kprev, vprev):
    n = pl.program_id(1)
    BLK = SWA_BLOCK
    HD = SWA_HEAD_DIM

    @pl.when(n == 0)
    def _():
        kprev[...] = jnp.zeros_like(kprev)
        vprev[...] = jnp.zeros_like(vprev)

    lo = lax.broadcasted_iota(jnp.int32, (1, LANES), 1) < HD
    n_kc = SWA_KV // LANES
    nblocks = q_ref.shape[0] // BLK
    zrows = (lax.broadcasted_iota(jnp.int32, (HD, BLK), 0) == 0).astype(F32)
    top = lax.broadcasted_iota(jnp.int32, (LANES, 1), 0) < HD

    kband, vbandT = {}, {}
    k_last = {idx: kprev[idx] for idx in range(2 * SWA_KV_HEADS)}
    v_last = {idx: vprev[idx] for idx in range(2 * SWA_KV_HEADS)}
    for bi in range(nblocks):
        kv = kv_ref[bi * BLK:(bi + 1) * BLK, :]
        kc = [kv[:, c * LANES:(c + 1) * LANES].astype(F32) for c in range(n_kc)]
        vT = [kv[:, SWA_KV + c * LANES:SWA_KV + (c + 1) * LANES].astype(F32).T for c in range(n_kc)]
        for j in range(SWA_KV_HEADS):
            cj, b = j // 2, j % 2
            for a in range(2):
                src = kc[cj] if a == b else pltpu.roll(kc[cj], HD, 1)
                k_cur = jnp.where(lo if a == 0 else jnp.logical_not(lo), src, 0.0).astype(BF16)
                rows = vT[cj][b * HD:(b + 1) * HD]
                v_cur = jnp.concatenate([rows, zrows] if a == 0 else [zrows, rows], axis=0).astype(BF16)
                idx = 2 * j + a
                kband[(bi, j, a)] = jnp.concatenate([k_last[idx], k_cur], axis=0)
                vbandT[(bi, j, a)] = jnp.concatenate([v_last[idx], v_cur], axis=1)
                k_last[idx], v_last[idx] = k_cur, v_cur
    for idx in range(2 * SWA_KV_HEADS):
        kprev[idx] = k_last[idx]
        vprev[idx] = v_last[idx]

    kj = lax.broadcasted_iota(jnp.int32, (2 * BLK, 2 * BLK), 0)
    qi = lax.broadcasted_iota(jnp.int32, (2 * BLK, 2 * BLK), 1) & (BLK - 1)
    d = kj - qi
    in_window = (d >= 1) & (d <= BLK)
    kmin = jnp.where(n == 0, BLK, 0)
    valid = [in_window & (kj >= kmin)] + [in_window] * (nblocks - 1)
    left = lax.broadcasted_iota(jnp.int32, (1, 2 * BLK), 1) < BLK

    G = SWA_Q_HEADS // SWA_KV_HEADS
    cpk = G // 2
    items = [(bi, j, a, j * cpk + 2 * p) for bi in range(nblocks) for j in range(SWA_KV_HEADS)
             for a in range(2) for p in range(cpk // 2)]

    def scores(item):
        bi, j, a, c1 = item
        rows = slice(bi * BLK, (bi + 1) * BLK)
        rhs = jnp.concatenate([q_ref[rows, c1 * LANES:(c1 + 1) * LANES],
                               q_ref[rows, (c1 + 1) * LANES:(c1 + 2) * LANES]], axis=0)
        return lax.dot_general(kband[(bi, j, a)], rhs, NT_DIMS, preferred_element_type=F32)

    acc = {}

    def consume(item, oT, sink_term):
        bi, j, a, c1 = item
        rows = slice(bi * BLK, (bi + 1) * BLK)
        r1 = (1 - a) * HD
        oT = oT * (1.0 / (oT[r1:r1 + 1, :] + sink_term))
        for c, part in ((c1, oT[:, :BLK]), (c1 + 1, oT[:, BLK:])):
            if a == 0:
                acc[c] = part
            else:
                gate = g_ref[rows, c * LANES:(c + 1) * LANES].astype(F32)
                o_pair = jnp.where(top, acc.pop(c), part)
                o_ref[rows, c * LANES:(c + 1) * LANES] = (o_pair.T * gate).astype(BF16)

    ahead = 4
    queue = [scores(it) for it in items[:ahead]]
    pending = None
    for i, item in enumerate(items):
        bi, j, a, c1 = item
        if i + ahead < len(items):
            queue.append(scores(items[i + ahead]))
        s_cur = queue.pop(0)
        s = jnp.where(valid[bi], s_cur, -jnp.inf)
        sink = jnp.where(left, sink_ref[2 * c1 + a], sink_ref[2 * c1 + 2 + a])
        m = jnp.maximum(jnp.max(s, axis=0, keepdims=True), sink)
        e = jnp.exp2(s - m)
        oT = jnp.dot(vbandT[(bi, j, a)], e.astype(BF16), preferred_element_type=F32)
        if pending is not None:
            consume(*pending)
        pending = (item, oT, jnp.exp2(sink - m))
    consume(*pending)


def _swa(q_n, kv_n, sag, sinks, B, T, blocks_per_step=4):
    N = B * T
    rows = blocks_per_step * SWA_BLOCK
    nb = T // rows
    row = lambda b, n: b * nb + n
    return pl.pallas_call(
        _swa_body,
        grid=(B, nb),
        in_specs=[pl.BlockSpec(memory_space=pltpu.SMEM),
                  pl.BlockSpec((rows, SWA_Q), lambda b, n: (row(b, n), 0)),
                  pl.BlockSpec((rows, 2 * SWA_KV), lambda b, n: (row(b, n), 0)),
                  pl.BlockSpec((rows, SWA_Q), lambda b, n: (row(b, n), 0))],
        out_specs=pl.BlockSpec((rows, SWA_Q), lambda b, n: (row(b, n), 0)),
        out_shape=jax.ShapeDtypeStruct((N, SWA_Q), BF16),
        scratch_shapes=[pltpu.VMEM((2 * SWA_KV_HEADS, SWA_BLOCK, LANES), BF16),
                        pltpu.VMEM((2 * SWA_KV_HEADS, LANES, SWA_BLOCK), BF16)],
        compiler_params=_cparams(("arbitrary", "arbitrary")),
        name="swa_attn",
    )(sinks, q_n, kv_n, sag)


def _delta_body(q_ref, k_ref, v_ref, z_ref, bc_ref, gc_ref, gr_ref, nw_ref, o_ref, S_ref,
                *, KH, TT):
    grp = pl.program_id(1)
    t = pl.program_id(2)
    C = DN_CHUNK
    NC = TT // C
    scale = DN_HEAD ** -0.5
    NT = (((1,), (1,)), ((), ()))

    @pl.when(t == 0)
    def _():
        S_ref[...] = jnp.zeros_like(S_ref)

    NV = 2 * KH
    row = lax.broadcasted_iota(jnp.int32, (TT, TT), 0)
    col = lax.broadcasted_iota(jnp.int32, (TT, TT), 1)
    same = (row // C) == (col // C)
    ri = lax.broadcasted_iota(jnp.int32, (C, TT), 0)
    cj = lax.broadcasted_iota(jnp.int32, (C, TT), 1)
    cjc = cj // C
    cjj = cj & (C - 1)
    tril_c = ri >= cjj
    strict_c = ri > cjj
    eye_c = (ri == cjj).astype(F32)
    lane = lax.broadcasted_iota(jnp.int32, (TT, LANES), 1)
    lchunk = lax.broadcasted_iota(jnp.int32, (1, TT), 1) // C
    bcall = bc_ref[...]
    gcall = gc_ref[...]
    nw = nw_ref[...]

    def compress(x):
        out = x[(NC - 1) * C:NC * C]
        for c in range(NC - 2, -1, -1):
            out = jnp.where(cjc == c, x[c * C:(c + 1) * C], out)
        return out

    def by_row(colvec):
        out = jnp.broadcast_to(colvec[(NC - 1) * C:NC * C], (C, TT))
        for c in range(NC - 2, -1, -1):
            out = jnp.where(cjc == c, colvec[c * C:(c + 1) * C], out)
        return out

    def bd(xc):
        return jnp.where(same, jnp.concatenate([xc] * NC, axis=0), 0.0).astype(BF16)

    kf, qf, kT, kkc, qkc = [], [], [], [], []
    for kh in range(KH):
        q = q_ref[:, kh * DN_HEAD:(kh + 1) * DN_HEAD]
        k = k_ref[:, kh * DN_HEAD:(kh + 1) * DN_HEAD]
        kq = lax.dot_general(jnp.concatenate([k, q], axis=0), k, NT, preferred_element_type=F32)
        kkc.append(compress(kq[:TT]))
        qkc.append(compress(kq[TT:]))
        kf.append(k.astype(F32))
        qf.append(q.astype(F32))
        kT.append(kf[-1].T)

    bcol, gcol, grow, Lc, qkm, P = [], [], [], [], [], []
    for hh in range(NV):
        hglob = grp * NV + hh
        sel = lane == hglob
        bcol.append(jnp.sum(jnp.where(sel, bcall, 0.0), axis=-1, keepdims=True))
        gcol.append(jnp.sum(jnp.where(sel, gcall, 0.0), axis=-1, keepdims=True))
        grow.append(gr_ref[0, pl.ds(hglob, 1), :])
        dec = jnp.exp(jnp.where(tril_c, by_row(gcol[hh]) - grow[hh], -jnp.inf))
        Lc.append(jnp.where(strict_c, by_row(bcol[hh]) * kkc[hh // 2] * dec, 0.0))
        qkm.append(qkc[hh // 2] * dec * scale)
        P.append(eye_c - Lc[hh])

    Lp = [jnp.dot(Lc[hh].astype(BF16), bd(Lc[hh]), preferred_element_type=F32) for hh in range(NV)]
    step = 4
    while step < C:
        res = [jnp.dot(jnp.concatenate([Lp[hh], P[hh]], axis=0).astype(BF16), bd(Lp[hh]),
                       preferred_element_type=F32) for hh in range(NV)]
        Lp = [r[:C] for r in res]
        P = [P[hh] + res[hh][C:] for hh in range(NV)]
        step *= 2
    P = [P[hh] + jnp.dot(P[hh].astype(BF16), bd(Lp[hh]), preferred_element_type=F32) for hh in range(NV)]

    uw, qd, kdT, eg_last = [], [], [], []
    for hh in range(NV):
        eg = jnp.exp(gcol[hh])
        vf = v_ref[:, hh * DN_HEAD:(hh + 1) * DN_HEAD].astype(F32)
        rhs = jnp.concatenate([vf * bcol[hh], kf[hh // 2] * (bcol[hh] * eg)], axis=1).astype(BF16)
        uw.append(jnp.dot(bd(P[hh]), rhs, preferred_element_type=F32))
        qd.append((qf[hh // 2] * (eg * scale)).astype(BF16))
        glrow = jnp.zeros((1, TT), F32)
        for c in range(NC):
            glrow = jnp.where(lchunk == c, gcol[hh][c * C + C - 1:c * C + C, :], glrow)
        kdT.append(kT[hh // 2] * jnp.exp(glrow - grow[hh]))
        eg_last.append([eg[c * C + C - 1:c * C + C, :] for c in range(NC)])

    S = [S_ref[hh] for hh in range(NV)]
    vn = [[] for _ in range(NV)]
    o_inter = [[] for _ in range(NV)]
    for c in range(NC):
        r0, r1 = c * C, (c + 1) * C
        res = []
        for hh in range(NV):
            w_c = uw[hh][r0:r1, DN_HEAD:].astype(BF16)
            lhs = jnp.concatenate([w_c, qd[hh][r0:r1]], axis=0)
            res.append(jnp.dot(lhs, S[hh].astype(BF16), preferred_element_type=F32))
        inc = []
        for hh in range(NV):
            vn_c = (uw[hh][r0:r1, :DN_HEAD] - res[hh][:C]).astype(BF16)
            vn[hh].append(vn_c)
            o_inter[hh].append(res[hh][C:])
            kd_c = kdT[hh][:, r0:r1].astype(BF16)
            inc.append(jnp.dot(kd_c, vn_c, preferred_element_type=F32))
        for hh in range(NV):
            S[hh] = S[hh] * eg_last[hh][c] + inc[hh]

    for hh in range(NV):
        S_ref[hh] = S[hh]
        vn_all = jnp.concatenate(vn[hh], axis=0)
        o = jnp.concatenate(o_inter[hh], axis=0) + jnp.dot(bd(qkm[hh]), vn_all, preferred_element_type=F32)
        ms = jnp.mean(o * o, axis=-1, keepdims=True)
        z = z_ref[:, hh * DN_HEAD:(hh + 1) * DN_HEAD].astype(F32)
        o_ref[:, hh * DN_HEAD:(hh + 1) * DN_HEAD] = (o * lax.rsqrt(ms + EPS) * nw * z).astype(BF16)


def _delta(qk_n, vv, sz, beta, gc, gcT, dn_norm_w, B, T, KH=8, TT=256):
    N = B * T
    nt = T // TT
    ng = DN_K_HEADS // KH
    row = lambda b, g, t: b * nt + t
    kw = KH * DN_HEAD
    return pl.pallas_call(
        functools.partial(_delta_body, KH=KH, TT=TT),
        grid=(B, ng, nt),
        in_specs=[pl.BlockSpec((TT, kw), lambda b, g, t: (row(b, g, t), g)),
                  pl.BlockSpec((TT, kw), lambda b, g, t: (row(b, g, t), DN_KEY // kw + g)),
                  pl.BlockSpec((TT, 2 * kw), lambda b, g, t: (row(b, g, t), g)),
                  pl.BlockSpec((TT, 2 * kw), lambda b, g, t: (row(b, g, t), g)),
                  pl.BlockSpec((TT, LANES), lambda b, g, t: (row(b, g, t), 0)),
                  pl.BlockSpec((TT, LANES), lambda b, g, t: (row(b, g, t), 0)),
                  pl.BlockSpec((1, DN_V_HEADS, TT), lambda b, g, t: (b, 0, t)),
                  pl.BlockSpec((1, DN_HEAD), lambda b, g, t: (0, 0))],
        out_specs=pl.BlockSpec((TT, 2 * kw), lambda b, g, t: (row(b, g, t), g)),
        out_shape=jax.ShapeDtypeStruct((N, DN_VAL), BF16),
        scratch_shapes=[pltpu.VMEM((2 * KH, DN_HEAD, DN_HEAD), F32)],
        compiler_params=_cparams(("arbitrary", "arbitrary", "arbitrary")),
        name="delta_rule",
    )(qk_n, qk_n, vv, sz, beta, gc, gcT, dn_norm_w.reshape(1, DN_HEAD))


def _merge_out_body(a_ref, o_ref, gm_ref, wa_ref, wb_ref, wo_ref, x_ref, g_ref, out_ref):
    D = out_ref.shape[1]
    ya = jnp.dot(a_ref[...], wa_ref[...], preferred_element_type=F32)
    yb = jnp.dot(o_ref[...], wb_ref[...], preferred_element_type=F32)
    y = (gm_ref[:, :D].astype(F32) * ya + gm_ref[:, D:].astype(F32) * yb).astype(BF16)
    out_ref[...] = x_ref[...] + g_ref[0] * jnp.dot(y, wo_ref[...], preferred_element_type=F32)


def _merge_out(a_g, o_g, p_m, wa, wb, wo, x2d, gate, T, tm=256):
    N, D = x2d.shape
    B = gate.shape[0]
    tps = T // tm
    resident = lambda w: pl.BlockSpec(w.shape, lambda i: (0, 0), pipeline_mode=pl.Buffered(1))
    return pl.pallas_call(
        _merge_out_body,
        grid=(N // tm,),
        in_specs=[pl.BlockSpec((tm, a_g.shape[1]), lambda i: (i, 0)),
                  pl.BlockSpec((tm, o_g.shape[1]), lambda i: (i, 0)),
                  pl.BlockSpec((tm, p_m.shape[1]), lambda i: (i, 0)),
                  resident(wa), resident(wb), resident(wo),
                  pl.BlockSpec((tm, D), lambda i: (i, 0)),
                  pl.BlockSpec((1, 1, D), lambda i: (i // tps, 0, 0))],
        out_specs=pl.BlockSpec((tm, D), lambda i: (i, 0)),
        out_shape=jax.ShapeDtypeStruct((N, D), F32),
        compiler_params=_cparams(("arbitrary",), vmem=60 * 1024 * 1024),
        name="merge_out",
    )(a_g, o_g, p_m, wa, wb, wo, x2d, gate.reshape(B, 1, D))


def _layer(l, x, c, positions, w_ada, b_ada, norm_w, w_in, q_norm_w, k_norm_w, sinks,
           conv_w, a_log, dt_bias, dn_norm_w, w_o_swa, w_o_dn, w_out):
    B, T, D = x.shape
    N = B * T
    x2d = x.reshape(N, D)

    mod = _adaln(c, w_ada, b_ada)
    shift, scale, gate = mod[:, :D], mod[:, D:2 * D], mod[:, 2 * D:]
    posf = positions.astype(F32).reshape(N, 1)
    h, tabs = _normmod(x2d, norm_w, scale, shift, posf, T)

    o_q, o_g = 0, SWA_Q + 2 * SWA_KV
    o_d = o_g + SWA_Q
    o_z = o_d + DN_CONV_CH
    o_b = o_z + DN_VAL
    o_a = o_b + DN_V_HEADS
    o_m = o_a + DN_V_HEADS
    assert o_a == o_b + DN_V_HEADS and o_b % LANES == 0
    w_in = jnp.swapaxes(w_in, 1, 2)
    rep = LANES // SWA_HEAD_DIM
    qw = jnp.tile(q_norm_w.reshape(1, SWA_HEAD_DIM), (1, rep)) * (SWA_HEAD_DIM ** -0.5 * LOG2E)
    kw = jnp.tile(k_norm_w.reshape(1, SWA_HEAD_DIM), (1, rep))
    q_n = _proj(h, w_in, l, o_q, SWA_Q, "qknorm", T, extras=(tabs, qw, SWA_Q // LANES), name="proj_q")
    kv_n = _proj(h, w_in, l, o_q + SWA_Q, 2 * SWA_KV, "qknorm", T, nblk=1,
                 extras=(tabs, kw, SWA_KV // LANES), name="proj_kv")
    sag = _proj(h, w_in, l, o_g, o_d - o_g, "silu", T, tm=2048, name="proj_swa_gate")
    qk_n = _proj(h, w_in, l, o_d, 2 * DN_KEY, "conv", T, extras=(conv_w, 0, True), name="proj_dn_qk")
    vv = _proj(h, w_in, l, o_d + 2 * DN_KEY, DN_VAL, "conv", T, extras=(conv_w, 2 * DN_KEY, False),
               name="proj_dn_v")
    sz = _proj(h, w_in, l, o_z, o_b - o_z, "silu", T, tm=2048, name="proj_dn_z")
    al = jnp.zeros((1, LANES), F32).at[0, :DN_V_HEADS].set(a_log)
    dt = jnp.zeros((1, LANES), F32).at[0, :DN_V_HEADS].set(dt_bias)
    beta, gc = _proj(h, w_in, l, o_b, 2 * DN_V_HEADS, "gates", T, extras=(al, dt), name="proj_dn_gates")
    p_m = _proj(h, w_in, l, o_m, 2 * D, "sigmoid", T, tm=2048, name="proj_merge_gates")

    a_g = _swa(q_n, kv_n, sag, sinks * LOG2E, B, T)

    gcT = gc[:, :DN_V_HEADS].reshape(B, T, DN_V_HEADS).transpose(0, 2, 1)
    o_g2 = _delta(qk_n, vv, sz, beta, gc, gcT, dn_norm_w, B, T)

    out = _merge_out(a_g, o_g2, p_m, w_o_swa.astype(BF16), w_o_dn.astype(BF16), w_out.astype(BF16),
                     x2d, gate, T)
    return out.reshape(B, T, D)


def kernel(x, c, positions, w_ada, b_ada, norm_w, w_in, q_norm_w, k_norm_w, sinks, conv_w,
           a_log, dt_bias, dn_norm_w, w_o_swa, w_o_dn, w_out):
    depth = w_ada.shape[0]
    for l in range(depth):
        x = _layer(l, x, c, positions, w_ada[l], b_ada[l], norm_w[l], w_in, q_norm_w[l],
                   k_norm_w[l], sinks[l], conv_w[l], a_log[l], dt_bias[l], dn_norm_w[l],
                   w_o_swa[l], w_o_dn[l], w_out[l])
    return x
```

```python
import functools

import numpy as np
import jax
import jax.numpy as jnp
from jax import lax
from jax.experimental import pallas as pl
from jax.experimental.pallas import tpu as pltpu

F32 = jnp.float32
BF16 = jnp.bfloat16

D_MODEL = 2048
SWA_Q_HEADS = 32
SWA_KV_HEADS = 4
SWA_HEAD_DIM = 64
SWA_BLOCK = 128
ROPE_THETA = 500000.0
ROPE_DIM = SWA_HEAD_DIM // 4
DN_K_HEADS = 16
DN_V_HEADS = 32
DN_HEAD = 128
DN_CONV = 4
DN_CHUNK = 64
EPS = 1e-6
LOG2E = 1.4426950408889634

SWA_Q = SWA_Q_HEADS * SWA_HEAD_DIM
SWA_KV = SWA_KV_HEADS * SWA_HEAD_DIM
DN_KEY = DN_K_HEADS * DN_HEAD
DN_VAL = DN_V_HEADS * DN_HEAD
DN_CONV_CH = 2 * DN_KEY + DN_VAL

LANES = 128
SUBLANES = 8
VMEM_LIMIT = 56 * 1024 * 1024


def _cparams(sem, vmem=VMEM_LIMIT):
    return pltpu.CompilerParams(dimension_semantics=sem, vmem_limit_bytes=vmem)


def _sigmoid(x):
    return 0.5 * jnp.tanh(0.5 * x) + 0.5


def _silu(x):
    h = 0.5 * x
    return h + h * jnp.tanh(h)


def _adaln_body(c_ref, w_ref, b_ref, o_ref):
    s = _silu(c_ref[...]).astype(BF16)
    o_ref[...] = jnp.dot(s, w_ref[...].astype(BF16), preferred_element_type=F32) + b_ref[...]


def _adaln(c, w_ada, b_ada, tn=1024):
    B, D = c.shape
    n_out = w_ada.shape[1]
    c8 = jnp.zeros((SUBLANES, D), F32).at[:B].set(c)
    out = pl.pallas_call(
        _adaln_body,
        grid=(n_out // tn,),
        in_specs=[pl.BlockSpec((SUBLANES, D), lambda j: (0, 0)),
                  pl.BlockSpec((D, tn), lambda j: (0, j)),
                  pl.BlockSpec((1, tn), lambda j: (0, j))],
        out_specs=pl.BlockSpec((SUBLANES, tn), lambda j: (0, j)),
        out_shape=jax.ShapeDtypeStruct((SUBLANES, n_out), F32),
        compiler_params=_cparams(("arbitrary",)),
        name="adaln_mod",
    )(c8, w_ada, b_ada.reshape(1, n_out))
    return out[:B]


def _normmod_body(x_ref, nw_ref, sc_ref, sh_ref, pos_ref, invf_ref, wg_ref, al_ref, dt_ref,
                  o_ref, c_ref, s1_ref, s2_ref, beta_ref, gc_ref):
    x = x_ref[...]
    ms = jnp.mean(x * x, axis=-1, keepdims=True)
    y = x * lax.rsqrt(ms + EPS) * nw_ref[...]
    h = (y * (1.0 + sc_ref[0]) + sh_ref[0]).astype(BF16)
    o_ref[...] = h

    acc = lax.dot_general(h, wg_ref[...].astype(BF16), NT_DIMS, preferred_element_type=F32)
    _gates_epilogue(acc, al_ref, dt_ref, beta_ref, gc_ref)

    half = ROPE_DIM // 2
    l64 = lax.broadcasted_iota(jnp.int32, (1, LANES), 1) & (SWA_HEAD_DIM - 1)
    ang = pos_ref[...] * invf_ref[...]
    sn = jnp.sin(ang)
    c_ref[...] = jnp.cos(ang)
    s1_ref[...] = jnp.where((l64 >= half) & (l64 < ROPE_DIM), sn, 0.0)
    s2_ref[...] = jnp.where(l64 < half, -sn, 0.0)


def _normmod(x2d, norm_w, scale, shift, posf, w_t, layer, gate_row0, al, dt, T, tm=1024):
    N, D = x2d.shape
    B = scale.shape[0]
    tm = min(tm, T)
    tps = T // tm
    assert gate_row0 % LANES == 0
    tab = pl.BlockSpec((tm, LANES), lambda i: (i, 0))
    vec = pl.BlockSpec((1, LANES), lambda i: (0, 0))
    outs = pl.pallas_call(
        _normmod_body,
        grid=(N // tm,),
        in_specs=[pl.BlockSpec((tm, D), lambda i: (i, 0)),
                  pl.BlockSpec((1, D), lambda i: (0, 0)),
                  pl.BlockSpec((1, 1, D), lambda i: (i // tps, 0, 0)),
                  pl.BlockSpec((1, 1, D), lambda i: (i // tps, 0, 0)),
                  pl.BlockSpec((tm, 1), lambda i: (i, 0)),
                  vec,
                  pl.BlockSpec((None, LANES, D), lambda i: (layer, gate_row0 // LANES, 0)),
                  vec, vec],
        out_specs=[pl.BlockSpec((tm, D), lambda i: (i, 0)), tab, tab, tab, tab, tab],
        out_shape=[jax.ShapeDtypeStruct((N, D), BF16)] + [jax.ShapeDtypeStruct((N, LANES), F32)] * 5,
        compiler_params=_cparams(("arbitrary",)),
        name="norm_mod",
    )(x2d, norm_w.reshape(1, D), scale.reshape(B, 1, D), shift.reshape(B, 1, D), posf,
      jnp.asarray(_rope_inv_freq_lanes()), w_t, al, dt)
    return outs[0], tuple(outs[1:4]), outs[4], outs[5]


HALF = LANES // 2


NT_DIMS = (((1,), (1,)), ((), ()))


def _cast_weights(w_refs, w2_ref, wbf):
    wblk = w_refs[0].shape[0]
    off = 0 if w2_ref is None else HALF
    for b, w_ref in enumerate(w_refs):
        lo = max(b * wblk - off, 0)
        wbf[lo:(b + 1) * wblk - off, :] = w_ref[lo + off - b * wblk:, :].astype(BF16)
    if w2_ref is not None:
        n = len(w_refs) * wblk
        wbf[n - off:n, :] = w2_ref[...].astype(BF16)


def _cast_once(w_refs, w2_ref, wbf):
    @pl.when(pl.program_id(1) == 0)
    def _():
        _cast_weights(w_refs, w2_ref, wbf)


MM_AHEAD = 3


def _for_row_subtiles(h_ref, wbf, sub, epilogue):
    starts = list(range(0, h_ref.shape[0], sub))
    mm = lambda r0: lax.dot_general(h_ref[r0:r0 + sub, :], wbf[...], NT_DIMS, preferred_element_type=F32)
    queue = [mm(r0) for r0 in starts[:MM_AHEAD]]
    for idx, r0 in enumerate(starts):
        if idx + MM_AHEAD < len(starts):
            queue.append(mm(starts[idx + MM_AHEAD]))
        epilogue(r0, queue.pop(0))


def _proj_act_body(*refs, act, shifted, sub, nblk):
    h_ref, w_refs = refs[0], refs[1:1 + nblk]
    w2_ref = refs[1 + nblk] if shifted else None
    o_ref, wbf = refs[-2:]
    _cast_once(w_refs, w2_ref, wbf)

    def epilogue(r0, acc):
        if act == "silu":
            acc = _silu(acc)
        elif act == "sigmoid":
            acc = _sigmoid(acc)
        o_ref[r0:r0 + sub, :] = acc.astype(o_ref.dtype)

    _for_row_subtiles(h_ref, wbf, sub, epilogue)


def _proj_conv_body(*refs, tm, tps, norm, sub, nblk):
    h_ref, w_refs = refs[0], refs[1:1 + nblk]
    cw_ref, o_ref, wbf, buf = refs[1 + nblk:]
    i = pl.program_id(1)
    first = (i % tps) == 0
    _cast_once(w_refs, None, wbf)

    @pl.when(first)
    def _():
        buf[0:SUBLANES, :] = jnp.zeros((SUBLANES, buf.shape[1]), F32)

    @pl.when(jnp.logical_not(first))
    def _():
        buf[0:SUBLANES, :] = buf[tm:tm + SUBLANES, :]

    cw = cw_ref[...]

    def epilogue(r0, acc):
        b0 = SUBLANES + r0
        buf[b0:b0 + sub, :] = acc
        win = buf[b0 - SUBLANES:b0 + sub, :]
        y = win[SUBLANES:, :] * cw[DN_CONV - 1:DN_CONV, :]
        for d in range(1, DN_CONV):
            y = y + pltpu.roll(win, d, 0)[SUBLANES:, :] * cw[DN_CONV - 1 - d:DN_CONV - d, :]
        y = _silu(y)
        for c in range(y.shape[1] // DN_HEAD):
            yc = y[:, c * DN_HEAD:(c + 1) * DN_HEAD]
            if norm:
                yc = yc * lax.rsqrt(jnp.sum(yc * yc, axis=-1, keepdims=True) + EPS)
            o_ref[r0:r0 + sub, c * DN_HEAD:(c + 1) * DN_HEAD] = yc.astype(o_ref.dtype)

    _for_row_subtiles(h_ref, wbf, sub, epilogue)


def _norm_rope(x, nw, cs, s1, s2, lo):
    half = ROPE_DIM // 2
    x2 = x * x
    s_lo = jnp.sum(jnp.where(lo, x2, 0.0), axis=-1, keepdims=True)
    s_hi = jnp.sum(jnp.where(lo, 0.0, x2), axis=-1, keepdims=True)
    r = jnp.where(lo, lax.rsqrt(s_lo * (1.0 / SWA_HEAD_DIM) + EPS),
                  lax.rsqrt(s_hi * (1.0 / SWA_HEAD_DIM) + EPS))
    xn = x * r * nw
    return xn * cs + pltpu.roll(xn, half, 1) * s1 + pltpu.roll(xn, LANES - half, 1) * s2


def _proj_qknorm_body(*refs, sub, nblk, n_norm):
    h_ref, w_refs = refs[0], refs[1:1 + nblk]
    c_ref, s1_ref, s2_ref, nw_ref, o_ref, wbf = refs[1 + nblk:]
    _cast_once(w_refs, None, wbf)
    lo = lax.broadcasted_iota(jnp.int32, (1, LANES), 1) < SWA_HEAD_DIM
    nw = nw_ref[...]
    def epilogue(r0, acc):
        cs, s1, s2 = c_ref[r0:r0 + sub, :], s1_ref[r0:r0 + sub, :], s2_ref[r0:r0 + sub, :]
        for c in range(acc.shape[1] // LANES):
            x = acc[:, c * LANES:(c + 1) * LANES]
            if c < n_norm:
                x = _norm_rope(x, nw, cs, s1, s2, lo)
            o_ref[r0:r0 + sub, c * LANES:(c + 1) * LANES] = x.astype(o_ref.dtype)

    _for_row_subtiles(h_ref, wbf, sub, epilogue)


def _gates_epilogue(acc, al_ref, dt_ref, beta_ref, gc_ref):
    beta_ref[...] = _sigmoid(acc)
    x = pltpu.roll(acc, LANES - DN_V_HEADS, 1) + dt_ref[...]
    sp = jnp.maximum(x, 0.0) + jnp.log(1.0 + jnp.exp(-jnp.abs(x)))
    g = -jnp.exp(al_ref[...]) * sp
    r64 = lax.broadcasted_iota(jnp.int32, g.shape, 0) & (DN_CHUNK - 1)
    s = 1
    while s < DN_CHUNK:
        g = g + jnp.where(r64 >= s, pltpu.roll(g, s, 0), 0.0)
        s *= 2
    gc_ref[...] = g


def _proj(h, w_t, layer, col0, width, mode, T, extras=(), tm=1024, wblk=512, nblk=2, sub=256,
          name="proj"):
    N, K = h.shape
    out_dtype = BF16
    tm = min(tm, T)
    sub = min(sub, tm)
    off = col0 % LANES
    base = col0 - off
    shifted = off != 0
    tn = wblk * nblk
    assert off in (0, HALF) and base % wblk == 0 and width % tn == 0
    jb = base // wblk
    grid = (width // tn, N // tm)
    in_specs = [pl.BlockSpec((tm, K), lambda j, i: (i, 0))]
    in_specs += [pl.BlockSpec((None, wblk, K), lambda j, i, b=b: (layer, jb + j * nblk + b, 0))
                 for b in range(nblk)]
    args = [h] + [w_t] * nblk
    scratch = [pltpu.VMEM((tn, K), BF16)]
    out_spec = pl.BlockSpec((tm, tn), lambda j, i: (i, j))
    out_shape = jax.ShapeDtypeStruct((N, width), out_dtype)
    if shifted:
        in_specs.append(pl.BlockSpec((None, HALF, K),
                                     lambda j, i: (layer, (base + (j + 1) * tn) // HALF, 0)))
        args.append(w_t)
    if mode == "conv":
        cw, cw_col0, norm = extras
        in_specs.append(pl.BlockSpec((DN_CONV, tn), lambda j, i: (0, cw_col0 // tn + j)))
        args.append(cw)
        scratch.append(pltpu.VMEM((tm + SUBLANES, tn), F32))
        body = functools.partial(_proj_conv_body, tm=tm, tps=T // tm, norm=norm, sub=sub, nblk=nblk)
    elif mode == "qknorm":
        tabs, nw, n_norm = extras
        in_specs += [pl.BlockSpec((tm, LANES), lambda j, i: (i, 0))] * 3
        in_specs.append(pl.BlockSpec((1, LANES), lambda j, i: (0, 0)))
        args += list(tabs) + [nw]
        body = functools.partial(_proj_qknorm_body, sub=sub, nblk=nblk, n_norm=n_norm)
    else:
        body = functools.partial(_proj_act_body, act=mode, shifted=shifted, sub=sub if mode else tm,
                                 nblk=nblk)
    return pl.pallas_call(
        body, grid=grid, in_specs=in_specs, out_specs=out_spec, out_shape=out_shape,
        scratch_shapes=scratch,
        compiler_params=_cparams(("arbitrary", "arbitrary")),
        name=name,
    )(*args)


def _rope_inv_freq_lanes():
    half = ROPE_DIM // 2
    inv = (np.float32(ROPE_THETA) ** (-np.arange(half, dtype=np.float32) * np.float32(2.0 / ROPE_DIM))).astype(np.float32)
    lanes = np.zeros((1, LANES), np.float32)
    for l in range(LANES):
        m = l % SWA_HEAD_DIM
        if m < ROPE_DIM:
            lanes[0, l] = inv[m % half]
    return lanes


def _swa_body(sink_ref, q_ref, kv_ref, g_ref, o_ref, kprev, vprev):
    n = pl.program_id(1)
    BLK = SWA_BLOCK
    HD = SWA_HEAD_DIM

    @pl.when(n == 0)
    def _():
        kprev[...] = jnp.zeros_like(kprev)
        vprev[...] = jnp.zeros_like(vprev)

    lo = lax.broadcasted_iota(jnp.int32, (1, LANES), 1) < HD
    n_kc = SWA_KV // LANES
    nblocks = q_ref.shape[0] // BLK
    zrows = (lax.broadcasted_iota(jnp.int32, (HD, BLK), 0) == 0).astype(F32)
    top = lax.broadcasted_iota(jnp.int32, (LANES, 1), 0) < HD

    kband, vbandT = {}, {}
    k_last = {idx: kprev[idx] for idx in range(2 * SWA_KV_HEADS)}
    v_last = {idx: vprev[idx] for idx in range(2 * SWA_KV_HEADS)}
    for bi in range(nblocks):
        kv = kv_ref[bi * BLK:(bi + 1) * BLK, :]
        kc = [kv[:, c * LANES:(c + 1) * LANES].astype(F32) for c in range(n_kc)]
        vT = [kv[:, SWA_KV + c * LANES:SWA_KV + (c + 1) * LANES].astype(F32).T for c in range(n_kc)]
        for j in range(SWA_KV_HEADS):
            cj, b = j // 2, j % 2
            for a in range(2):
                src = kc[cj] if a == b else pltpu.roll(kc[cj], HD, 1)
                k_cur = jnp.where(lo if a == 0 else jnp.logical_not(lo), src, 0.0).astype(BF16)
                rows = vT[cj][b * HD:(b + 1) * HD]
                v_cur = jnp.concatenate([rows, zrows] if a == 0 else [zrows, rows], axis=0).astype(BF16)
                idx = 2 * j + a
                kband[(bi, j, a)] = jnp.concatenate([k_last[idx], k_cur], axis=0)
                vbandT[(bi, j, a)] = jnp.concatenate([v_last[idx], v_cur], axis=1)
                k_last[idx], v_last[idx] = k_cur, v_cur
    for idx in range(2 * SWA_KV_HEADS):
        kprev[idx] = k_last[idx]
        vprev[idx] = v_last[idx]

    kj = lax.broadcasted_iota(jnp.int32, (2 * BLK, 2 * BLK), 0)
    qi = lax.broadcasted_iota(jnp.int32, (2 * BLK, 2 * BLK), 1) & (BLK - 1)
    d = kj - qi
    in_window = (d >= 1) & (d <= BLK)
    kmin = jnp.where(n == 0, BLK, 0)
    valid = [in_window & (kj >= kmin)] + [in_window] * (nblocks - 1)
    left = lax.broadcasted_iota(jnp.int32, (1, 2 * BLK), 1) < BLK

    G = SWA_Q_HEADS // SWA_KV_HEADS
    cpk = G // 2
    items = [(bi, j, a, j * cpk + 2 * p) for bi in range(nblocks) for j in range(SWA_KV_HEADS)
             for a in range(2) for p in range(cpk // 2)]

    def scores(item):
        bi, j, a, c1 = item
        rows = slice(bi * BLK, (bi + 1) * BLK)
        rhs = jnp.concatenate([q_ref[rows, c1 * LANES:(c1 + 1) * LANES],
                               q_ref[rows, (c1 + 1) * LANES:(c1 + 2) * LANES]], axis=0)
        return lax.dot_general(kband[(bi, j, a)], rhs, NT_DIMS, preferred_element_type=F32)

    acc = {}

    def consume(item, oT, sink_term):
        bi, j, a, c1 = item
        rows = slice(bi * BLK, (bi + 1) * BLK)
        r1 = (1 - a) * HD
        oT = oT * (1.0 / (oT[r1:r1 + 1, :] + sink_term))
        for c, part in ((c1, oT[:, :BLK]), (c1 + 1, oT[:, BLK:])):
            if a == 0:
                acc[c] = part
            else:
                gate = g_ref[rows, c * LANES:(c + 1) * LANES].astype(F32)
                o_pair = jnp.where(top, acc.pop(c), part)
                o_ref[rows, c * LANES:(c + 1) * LANES] = (o_pair.T * gate).astype(BF16)

    ahead = 4
    queue = [scores(it) for it in items[:ahead]]
    pending = None
    for i, item in enumerate(items):
        bi, j, a, c1 = item
        if i + ahead < len(items):
            queue.append(scores(items[i + ahead]))
        s_cur = queue.pop(0)
        s = jnp.where(valid[bi], s_cur, -jnp.inf)
        sink = jnp.where(left, sink_ref[2 * c1 + a], sink_ref[2 * c1 + 2 + a])
        m = jnp.maximum(jnp.max(s, axis=0, keepdims=True), sink)
        e = jnp.exp2(s - m)
        oT = jnp.dot(vbandT[(bi, j, a)], e.astype(BF16), preferred_element_type=F32)
        if pending is not None:
            consume(*pending)
        pending = (item, oT, jnp.exp2(sink - m))
    consume(*pending)


def _swa(q_n, kv_n, sag, sinks, B, T, blocks_per_step=4):
    N = B * T
    rows = blocks_per_step * SWA_BLOCK
    nb = T // rows
    row = lambda b, n: b * nb + n
    return pl.pallas_call(
        _swa_body,
        grid=(B, nb),
        in_specs=[pl.BlockSpec(memory_space=pltpu.SMEM),
                  pl.BlockSpec((rows, SWA_Q), lambda b, n: (row(b, n), 0)),
                  pl.BlockSpec((rows, 2 * SWA_KV), lambda b, n: (row(b, n), 0)),
                  pl.BlockSpec((rows, SWA_Q), lambda b, n: (row(b, n), 0))],
        out_specs=pl.BlockSpec((rows, SWA_Q), lambda b, n: (row(b, n), 0)),
        out_shape=jax.ShapeDtypeStruct((N, SWA_Q), BF16),
        scratch_shapes=[pltpu.VMEM((2 * SWA_KV_HEADS, SWA_BLOCK, LANES), BF16),
                        pltpu.VMEM((2 * SWA_KV_HEADS, LANES, SWA_BLOCK), BF16)],
        compiler_params=_cparams(("arbitrary", "arbitrary")),
        name="swa_attn",
    )(sinks, q_n, kv_n, sag)


def _delta_body(q_ref, k_ref, v_ref, z_ref, bc_ref, gc_ref, gr_ref, nw_ref, o_ref, S_ref,
                *, KH, TT):
    grp = pl.program_id(1)
    t = pl.program_id(2)
    C = DN_CHUNK
    NC = TT // C
    scale = DN_HEAD ** -0.5
    NT = (((1,), (1,)), ((), ()))

    @pl.when(t == 0)
    def _():
        S_ref[...] = jnp.zeros_like(S_ref)

    NV = 2 * KH
    row = lax.broadcasted_iota(jnp.int32, (TT, TT), 0)
    col = lax.broadcasted_iota(jnp.int32, (TT, TT), 1)
    same = (row // C) == (col // C)
    ri = lax.broadcasted_iota(jnp.int32, (C, TT), 0)
    cj = lax.broadcasted_iota(jnp.int32, (C, TT), 1)
    cjc = cj // C
    cjj = cj & (C - 1)
    tril_c = ri >= cjj
    strict_c = ri > cjj
    eye_c = (ri == cjj).astype(F32)
    lane = lax.broadcasted_iota(jnp.int32, (TT, LANES), 1)
    lchunk = lax.broadcasted_iota(jnp.int32, (1, TT), 1) // C
    bcall = bc_ref[...]
    gcall = gc_ref[...]
    nw = nw_ref[...]

    def compress(x):
        out = x[(NC - 1) * C:NC * C]
        for c in range(NC - 2, -1, -1):
            out = jnp.where(cjc == c, x[c * C:(c + 1) * C], out)
        return out

    def by_row(colvec):
        out = jnp.broadcast_to(colvec[(NC - 1) * C:NC * C], (C, TT))
        for c in range(NC - 2, -1, -1):
            out = jnp.where(cjc == c, colvec[c * C:(c + 1) * C], out)
        return out

    def bd(xc):
        return jnp.where(same, jnp.concatenate([xc] * NC, axis=0), 0.0).astype(BF16)

    kf, qf, kT, kkc, qkc = [], [], [], [], []
    for kh in range(KH):
        q = q_ref[:, kh * DN_HEAD:(kh + 1) * DN_HEAD]
        k = k_ref[:, kh * DN_HEAD:(kh + 1) * DN_HEAD]
        kq = lax.dot_general(jnp.concatenate([k, q], axis=0), k, NT, preferred_element_type=F32)
        kkc.append(compress(kq[:TT]))
        qkc.append(compress(kq[TT:]))
        kf.append(k.astype(F32))
        qf.append(q.astype(F32))
        kT.append(kf[-1].T)

    bcol, gcol, grow, Lc, qkm, P = [], [], [], [], [], []
    for hh in range(NV):
        hglob = grp * NV + hh
        sel = lane == hglob
        bcol.append(jnp.sum(jnp.where(sel, bcall, 0.0), axis=-1, keepdims=True))
        gcol.append(jnp.sum(jnp.where(sel, gcall, 0.0), axis=-1, keepdims=True))
        grow.append(gr_ref[0, pl.ds(hglob, 1), :])
        dec = jnp.exp(jnp.where(tril_c, by_row(gcol[hh]) - grow[hh], -jnp.inf))
        Lc.append(jnp.where(strict_c, by_row(bcol[hh]) * kkc[hh // 2] * dec, 0.0))
        qkm.append(qkc[hh // 2] * dec * scale)
        P.append(eye_c - Lc[hh])

    Lp = [jnp.dot(Lc[hh].astype(BF16), bd(Lc[hh]), preferred_element_type=F32) for hh in range(NV)]
    step = 4
    while step < C:
        res = [jnp.dot(jnp.concatenate([Lp[hh], P[hh]], axis=0).astype(BF16), bd(Lp[hh]),
                       preferred_element_type=F32) for hh in range(NV)]
        Lp = [r[:C] for r in res]
        P = [P[hh] + res[hh][C:] for hh in range(NV)]
        step *= 2
    P = [P[hh] + jnp.dot(P[hh].astype(BF16), bd(Lp[hh]), preferred_element_type=F32) for hh in range(NV)]

    uw, qd, kdT, eg_last = [], [], [], []
    for hh in range(NV):
        eg = jnp.exp(gcol[hh])
        vf = v_ref[:, hh * DN_HEAD:(hh + 1) * DN_HEAD].astype(F32)
        rhs = jnp.concatenate([vf * bcol[hh], kf[hh // 2] * (bcol[hh] * eg)], axis=1).astype(BF16)
        uw.append(jnp.dot(bd(P[hh]), rhs, preferred_element_type=F32))
        qd.append((qf[hh // 2] * (eg * scale)).astype(BF16))
        glrow = jnp.zeros((1, TT), F32)
        for c in range(NC):
            glrow = jnp.where(lchunk == c, gcol[hh][c * C + C - 1:c * C + C, :], glrow)
        kdT.append(kT[hh // 2] * jnp.exp(glrow - grow[hh]))
        eg_last.append([eg[c * C + C - 1:c * C + C, :] for c in range(NC)])

    S = [S_ref[hh] for hh in range(NV)]
    vn = [[] for _ in range(NV)]
    o_inter = [[] for _ in range(NV)]
    for c in range(NC):
        r0, r1 = c * C, (c + 1) * C
        res = []
        for hh in range(NV):
            w_c = uw[hh][r0:r1, DN_HEAD:].astype(BF16)
            lhs = jnp.concatenate([w_c, qd[hh][r0:r1]], axis=0)
            res.append(jnp.dot(lhs, S[hh].astype(BF16), preferred_element_type=F32))
        inc = []
        for hh in range(NV):
            vn_c = (uw[hh][r0:r1, :DN_HEAD] - res[hh][:C]).astype(BF16)
            vn[hh].append(vn_c)
            o_inter[hh].append(res[hh][C:])
            kd_c = kdT[hh][:, r0:r1].astype(BF16)
            inc.append(jnp.dot(kd_c, vn_c, preferred_element_type=F32))
        for hh in range(NV):
            S[hh] = S[hh] * eg_last[hh][c] + inc[hh]

    for hh in range(NV):
        S_ref[hh] = S[hh]
        vn_all = jnp.concatenate(vn[hh], axis=0)
        o = jnp.concatenate(o_inter[hh], axis=0) + jnp.dot(bd(qkm[hh]), vn_all, preferred_element_type=F32)
        ms = jnp.mean(o * o, axis=-1, keepdims=True)
        z = z_ref[:, hh * DN_HEAD:(hh + 1) * DN_HEAD].astype(F32)
        o_ref[:, hh * DN_HEAD:(hh + 1) * DN_HEAD] = (o * lax.rsqrt(ms + EPS) * nw * z).astype(BF16)


def _delta(qk_n, vv, sz, beta, gc, gcT, dn_norm_w, B, T, KH=8, TT=256):
    N = B * T
    nt = T // TT
    ng = DN_K_HEADS // KH
    row = lambda b, g, t: b * nt + t
    kw = KH * DN_HEAD
    return pl.pallas_call(
        functools.partial(_delta_body, KH=KH, TT=TT),
        grid=(B, ng, nt),
        in_specs=[pl.BlockSpec((TT, kw), lambda b, g, t: (row(b, g, t), g)),
                  pl.BlockSpec((TT, kw), lambda b, g, t: (row(b, g, t), DN_KEY // kw + g)),
                  pl.BlockSpec((TT, 2 * kw), lambda b, g, t: (row(b, g, t), g)),
                  pl.BlockSpec((TT, 2 * kw), lambda b, g, t: (row(b, g, t), g)),
                  pl.BlockSpec((TT, LANES), lambda b, g, t: (row(b, g, t), 0)),
                  pl.BlockSpec((TT, LANES), lambda b, g, t: (row(b, g, t), 0)),
                  pl.BlockSpec((1, DN_V_HEADS, TT), lambda b, g, t: (b, 0, t)),
                  pl.BlockSpec((1, DN_HEAD), lambda b, g, t: (0, 0))],
        out_specs=pl.BlockSpec((TT, 2 * kw), lambda b, g, t: (row(b, g, t), g)),
        out_shape=jax.ShapeDtypeStruct((N, DN_VAL), BF16),
        scratch_shapes=[pltpu.VMEM((2 * KH, DN_HEAD, DN_HEAD), F32)],
        compiler_params=_cparams(("arbitrary", "arbitrary", "arbitrary")),
        name="delta_rule",
    )(qk_n, qk_n, vv, sz, beta, gc, gcT, dn_norm_w.reshape(1, DN_HEAD))


def _merge_out_body(a_ref, o_ref, gm_ref, wa_ref, wb_ref, wo_ref, x_ref, g_ref, out_ref):
    D = out_ref.shape[1]
    ya = jnp.dot(a_ref[...], wa_ref[...], preferred_element_type=F32)
    yb = jnp.dot(o_ref[...], wb_ref[...], preferred_element_type=F32)
    y = (gm_ref[:, :D].astype(F32) * ya + gm_ref[:, D:].astype(F32) * yb).astype(BF16)
    out_ref[...] = x_ref[...] + g_ref[0] * jnp.dot(y, wo_ref[...], preferred_element_type=F32)


def _merge_out(a_g, o_g, p_m, wa, wb, wo, x2d, gate, T, tm=256):
    N, D = x2d.shape
    B = gate.shape[0]
    tps = T // tm
    resident = lambda w: pl.BlockSpec(w.shape, lambda i: (0, 0), pipeline_mode=pl.Buffered(1))
    return pl.pallas_call(
        _merge_out_body,
        grid=(N // tm,),
        in_specs=[pl.BlockSpec((tm, a_g.shape[1]), lambda i: (i, 0)),
                  pl.BlockSpec((tm, o_g.shape[1]), lambda i: (i, 0)),
                  pl.BlockSpec((tm, p_m.shape[1]), lambda i: (i, 0)),
                  resident(wa), resident(wb), resident(wo),
                  pl.BlockSpec((tm, D), lambda i: (i, 0)),
                  pl.BlockSpec((1, 1, D), lambda i: (i // tps, 0, 0))],
        out_specs=pl.BlockSpec((tm, D), lambda i: (i, 0)),
        out_shape=jax.ShapeDtypeStruct((N, D), F32),
        compiler_params=_cparams(("arbitrary",), vmem=60 * 1024 * 1024),
        name="merge_out",
    )(a_g, o_g, p_m, wa, wb, wo, x2d, gate.reshape(B, 1, D))


def _layer(l, x, c, positions, w_ada, b_ada, norm_w, w_in, q_norm_w, k_norm_w, sinks,
           conv_w, a_log, dt_bias, dn_norm_w, w_o_swa, w_o_dn, w_out):
    B, T, D = x.shape
    N = B * T
    x2d = x.reshape(N, D)

    mod = _adaln(c, w_ada, b_ada)
    shift, scale, gate = mod[:, :D], mod[:, D:2 * D], mod[:, 2 * D:]
    o_q, o_g = 0, SWA_Q + 2 * SWA_KV
    o_d = o_g + SWA_Q
    o_z = o_d + DN_CONV_CH
    o_b = o_z + DN_VAL
    o_a = o_b + DN_V_HEADS
    o_m = o_a + DN_V_HEADS
    w_in = jnp.swapaxes(w_in, 1, 2)

    posf = positions.astype(F32).reshape(N, 1)
    al = jnp.zeros((1, LANES), F32).at[0, :DN_V_HEADS].set(a_log)
    dt = jnp.zeros((1, LANES), F32).at[0, :DN_V_HEADS].set(dt_bias)
    h, tabs, beta, gc = _normmod(x2d, norm_w, scale, shift, posf, w_in, l, o_b, al, dt, T)
    rep = LANES // SWA_HEAD_DIM
    qw = jnp.tile(q_norm_w.reshape(1, SWA_HEAD_DIM), (1, rep)) * (SWA_HEAD_DIM ** -0.5 * LOG2E)
    kw = jnp.tile(k_norm_w.reshape(1, SWA_HEAD_DIM), (1, rep))
    q_n = _proj(h, w_in, l, o_q, SWA_Q, "qknorm", T, extras=(tabs, qw, SWA_Q // LANES), name="proj_q")
    kv_n = _proj(h, w_in, l, o_q + SWA_Q, 2 * SWA_KV, "qknorm", T, nblk=1,
                 extras=(tabs, kw, SWA_KV // LANES), name="proj_kv")
    sag = _proj(h, w_in, l, o_g, o_d - o_g, "silu", T, tm=2048, name="proj_swa_gate")
    qk_n = _proj(h, w_in, l, o_d, 2 * DN_KEY, "conv", T, extras=(conv_w, 0, True), name="proj_dn_qk")
    vv = _proj(h, w_in, l, o_d + 2 * DN_KEY, DN_VAL, "conv", T, extras=(conv_w, 2 * DN_KEY, False),
               name="proj_dn_v")
    sz = _proj(h, w_in, l, o_z, o_b - o_z, "silu", T, tm=2048, name="proj_dn_z")
    p_m = _proj(h, w_in, l, o_m, 2 * D, "sigmoid", T, tm=2048, name="proj_merge_gates")

    a_g = _swa(q_n, kv_n, sag, sinks * LOG2E, B, T)

    gcT = gc[:, :DN_V_HEADS].reshape(B, T, DN_V_HEADS).transpose(0, 2, 1)
    o_g2 = _delta(qk_n, vv, sz, beta, gc, gcT, dn_norm_w, B, T)

    out = _merge_out(a_g, o_g2, p_m, w_o_swa.astype(BF16), w_o_dn.astype(BF16), w_out.astype(BF16),
                     x2d, gate, T)
    return out.reshape(B, T, D)


def kernel(x, c, positions, w_ada, b_ada, norm_w, w_in, q_norm_w, k_norm_w, sinks, conv_w,
           a_log, dt_bias, dn_norm_w, w_o_swa, w_o_dn, w_out):
    depth = w_ada.shape[0]
    for l in range(depth):
        x = _layer(l, x, c, positions, w_ada[l], b_ada[l], norm_w[l], w_in, q_norm_w[l],
                   k_norm_w[l], sinks[l], conv_w[l], a_log[l], dt_bias[l], dn_norm_w[l],
                   w_o_swa[l], w_o_dn[l], w_out[l])
    return x
```

```python
import functools

import numpy as np
import jax
import jax.numpy as jnp
from jax import lax
from jax.experimental import pallas as pl
from jax.experimental.pallas import tpu as pltpu

F32 = jnp.float32
BF16 = jnp.bfloat16

D_MODEL = 2048
SWA_Q_HEADS = 32
SWA_KV_HEADS = 4
SWA_HEAD_DIM = 64
SWA_BLOCK = 128
ROPE_THETA = 500000.0
ROPE_DIM = SWA_HEAD_DIM // 4
DN_K_HEADS = 16
DN_V_HEADS = 32
DN_HEAD = 128
DN_CONV = 4
DN_CHUNK = 64
EPS = 1e-6
LOG2E = 1.4426950408889634

SWA_Q = SWA_Q_HEADS * SWA_HEAD_DIM
SWA_KV = SWA_KV_HEADS * SWA_HEAD_DIM
DN_KEY = DN_K_HEADS * DN_HEAD
DN_VAL = DN_V_HEADS * DN_HEAD
DN_CONV_CH = 2 * DN_KEY + DN_VAL

LANES = 128
SUBLANES = 8
VMEM_LIMIT = 56 * 1024 * 1024


def _cparams(sem, vmem=VMEM_LIMIT):
    return pltpu.CompilerParams(dimension_semantics=sem, vmem_limit_bytes=vmem)


def _sigmoid(x):
    return 0.5 * jnp.tanh(0.5 * x) + 0.5


def _silu(x):
    h = 0.5 * x
    return h + h * jnp.tanh(h)


def _adaln_body(c_ref, w_ref, b_ref, o_ref):
    s = _silu(c_ref[...]).astype(BF16)
    o_ref[...] = jnp.dot(s, w_ref[...].astype(BF16), preferred_element_type=F32) + b_ref[...]


def _adaln(c, w_ada, b_ada, tn=1024):
    B, D = c.shape
    n_out = w_ada.shape[1]
    c8 = jnp.zeros((SUBLANES, D), F32).at[:B].set(c)
    out = pl.pallas_call(
        _adaln_body,
        grid=(n_out // tn,),
        in_specs=[pl.BlockSpec((SUBLANES, D), lambda j: (0, 0)),
                  pl.BlockSpec((D, tn), lambda j: (0, j)),
                  pl.BlockSpec((1, tn), lambda j: (0, j))],
        out_specs=pl.BlockSpec((SUBLANES, tn), lambda j: (0, j)),
        out_shape=jax.ShapeDtypeStruct((SUBLANES, n_out), F32),
        compiler_params=_cparams(("arbitrary",)),
        name="adaln_mod",
    )(c8, w_ada, b_ada.reshape(1, n_out))
    return out[:B]


def _normmod_body(x_ref, nw_ref, sc_ref, sh_ref, pos_ref, invf_ref, wg_ref, al_ref, dt_ref, wkv_ref, kw_ref,
                  o_ref, c_ref, s1_ref, s2_ref, beta_ref, gc_ref, kv_ref, wkv_bf):
    @pl.when(pl.program_id(0) == 0)
    def _():
        wkv_bf[...] = wkv_ref[...].astype(BF16)

    x = x_ref[...]
    ms = jnp.mean(x * x, axis=-1, keepdims=True)
    y = x * lax.rsqrt(ms + EPS) * nw_ref[...]
    h = (y * (1.0 + sc_ref[0]) + sh_ref[0]).astype(BF16)
    o_ref[...] = h

    acc = lax.dot_general(h, wg_ref[...].astype(BF16), NT_DIMS, preferred_element_type=F32)
    _gates_epilogue(acc, al_ref, dt_ref, beta_ref, gc_ref)

    half = ROPE_DIM // 2
    lane = lax.broadcasted_iota(jnp.int32, (1, LANES), 1)
    l64 = lane & (SWA_HEAD_DIM - 1)
    ang = pos_ref[...] * invf_ref[...]
    sn = jnp.sin(ang)
    cs = jnp.cos(ang)
    s1 = jnp.where((l64 >= half) & (l64 < ROPE_DIM), sn, 0.0)
    s2 = jnp.where(l64 < half, -sn, 0.0)
    c_ref[...] = cs
    s1_ref[...] = s1
    s2_ref[...] = s2

    kv = lax.dot_general(h, wkv_bf[...], NT_DIMS, preferred_element_type=F32)
    for c in range(kv.shape[1] // LANES):
        xc = kv[:, c * LANES:(c + 1) * LANES]
        if c < SWA_KV // LANES:
            xc = _norm_rope(xc, kw_ref[...], cs, s1, s2, lane < SWA_HEAD_DIM)
        kv_ref[:, c * LANES:(c + 1) * LANES] = xc.astype(BF16)


def _normmod(x2d, norm_w, scale, shift, posf, w_t, layer, gate_row0, al, dt, kv_row0, kw, T, tm=1024):
    N, D = x2d.shape
    B = scale.shape[0]
    tm = min(tm, T)
    tps = T // tm
    kvw = 2 * SWA_KV
    assert gate_row0 % LANES == 0 and kv_row0 % kvw == 0
    tab = pl.BlockSpec((tm, LANES), lambda i: (i, 0))
    vec = pl.BlockSpec((1, LANES), lambda i: (0, 0))
    outs = pl.pallas_call(
        _normmod_body,
        grid=(N // tm,),
        in_specs=[pl.BlockSpec((tm, D), lambda i: (i, 0)),
                  pl.BlockSpec((1, D), lambda i: (0, 0)),
                  pl.BlockSpec((1, 1, D), lambda i: (i // tps, 0, 0)),
                  pl.BlockSpec((1, 1, D), lambda i: (i // tps, 0, 0)),
                  pl.BlockSpec((tm, 1), lambda i: (i, 0)),
                  vec,
                  pl.BlockSpec((None, LANES, D), lambda i: (layer, gate_row0 // LANES, 0)),
                  vec, vec,
                  pl.BlockSpec((None, kvw, D), lambda i: (layer, kv_row0 // kvw, 0)),
                  vec],
        out_specs=[pl.BlockSpec((tm, D), lambda i: (i, 0)), tab, tab, tab, tab, tab,
                   pl.BlockSpec((tm, kvw), lambda i: (i, 0))],
        out_shape=[jax.ShapeDtypeStruct((N, D), BF16)] + [jax.ShapeDtypeStruct((N, LANES), F32)] * 5
        + [jax.ShapeDtypeStruct((N, kvw), BF16)],
        scratch_shapes=[pltpu.VMEM((kvw, D), BF16)],
        compiler_params=_cparams(("arbitrary",)),
        name="norm_mod",
    )(x2d, norm_w.reshape(1, D), scale.reshape(B, 1, D), shift.reshape(B, 1, D), posf,
      jnp.asarray(_rope_inv_freq_lanes()), w_t, al, dt, w_t, kw)
    return outs[0], tuple(outs[1:4]), outs[4], outs[5], outs[6]


HALF = LANES // 2


NT_DIMS = (((1,), (1,)), ((), ()))


def _cast_weights(w_refs, w2_ref, wbf):
    wblk = w_refs[0].shape[0]
    off = 0 if w2_ref is None else HALF
    for b, w_ref in enumerate(w_refs):
        lo = max(b * wblk - off, 0)
        wbf[lo:(b + 1) * wblk - off, :] = w_ref[lo + off - b * wblk:, :].astype(BF16)
    if w2_ref is not None:
        n = len(w_refs) * wblk
        wbf[n - off:n, :] = w2_ref[...].astype(BF16)


def _cast_once(w_refs, w2_ref, wbf):
    @pl.when(pl.program_id(1) == 0)
    def _():
        _cast_weights(w_refs, w2_ref, wbf)


MM_AHEAD = 3


def _for_row_subtiles(h_ref, wbf, sub, epilogue):
    starts = list(range(0, h_ref.shape[0], sub))
    mm = lambda r0: lax.dot_general(h_ref[r0:r0 + sub, :], wbf[...], NT_DIMS, preferred_element_type=F32)
    queue = [mm(r0) for r0 in starts[:MM_AHEAD]]
    for idx, r0 in enumerate(starts):
        if idx + MM_AHEAD < len(starts):
            queue.append(mm(starts[idx + MM_AHEAD]))
        epilogue(r0, queue.pop(0))


def _proj_act_body(*refs, act, shifted, sub, nblk):
    h_ref, w_refs = refs[0], refs[1:1 + nblk]
    w2_ref = refs[1 + nblk] if shifted else None
    o_ref, wbf = refs[-2:]
    _cast_once(w_refs, w2_ref, wbf)

    def epilogue(r0, acc):
        if act == "silu":
            acc = _silu(acc)
        elif act == "sigmoid":
            acc = _sigmoid(acc)
        o_ref[r0:r0 + sub, :] = acc.astype(o_ref.dtype)

    _for_row_subtiles(h_ref, wbf, sub, epilogue)


def _proj_conv_body(*refs, tm, tps, norm, sub, nblk):
    h_ref, w_refs = refs[0], refs[1:1 + nblk]
    cw_ref, o_ref, wbf, buf = refs[1 + nblk:]
    i = pl.program_id(1)
    first = (i % tps) == 0
    _cast_once(w_refs, None, wbf)

    @pl.when(first)
    def _():
        buf[0:SUBLANES, :] = jnp.zeros((SUBLANES, buf.shape[1]), F32)

    @pl.when(jnp.logical_not(first))
    def _():
        buf[0:SUBLANES, :] = buf[tm:tm + SUBLANES, :]

    cw = cw_ref[...]

    def epilogue(r0, acc):
        b0 = SUBLANES + r0
        buf[b0:b0 + sub, :] = acc
        win = buf[b0 - SUBLANES:b0 + sub, :]
        y = win[SUBLANES:, :] * cw[DN_CONV - 1:DN_CONV, :]
        for d in range(1, DN_CONV):
            y = y + pltpu.roll(win, d, 0)[SUBLANES:, :] * cw[DN_CONV - 1 - d:DN_CONV - d, :]
        y = _silu(y)
        for c in range(y.shape[1] // DN_HEAD):
            yc = y[:, c * DN_HEAD:(c + 1) * DN_HEAD]
            if norm:
                yc = yc * lax.rsqrt(jnp.sum(yc * yc, axis=-1, keepdims=True) + EPS)
            o_ref[r0:r0 + sub, c * DN_HEAD:(c + 1) * DN_HEAD] = yc.astype(o_ref.dtype)

    _for_row_subtiles(h_ref, wbf, sub, epilogue)


def _norm_rope(x, nw, cs, s1, s2, lo):
    half = ROPE_DIM // 2
    x2 = x * x
    s_lo = jnp.sum(jnp.where(lo, x2, 0.0), axis=-1, keepdims=True)
    s_hi = jnp.sum(jnp.where(lo, 0.0, x2), axis=-1, keepdims=True)
    r = jnp.where(lo, lax.rsqrt(s_lo * (1.0 / SWA_HEAD_DIM) + EPS),
                  lax.rsqrt(s_hi * (1.0 / SWA_HEAD_DIM) + EPS))
    xn = x * r * nw
    return xn * cs + pltpu.roll(xn, half, 1) * s1 + pltpu.roll(xn, LANES - half, 1) * s2


def _proj_qknorm_body(*refs, sub, nblk, n_norm):
    h_ref, w_refs = refs[0], refs[1:1 + nblk]
    c_ref, s1_ref, s2_ref, nw_ref, o_ref, wbf = refs[1 + nblk:]
    _cast_once(w_refs, None, wbf)
    lo = lax.broadcasted_iota(jnp.int32, (1, LANES), 1) < SWA_HEAD_DIM
    nw = nw_ref[...]
    def epilogue(r0, acc):
        cs, s1, s2 = c_ref[r0:r0 + sub, :], s1_ref[r0:r0 + sub, :], s2_ref[r0:r0 + sub, :]
        for c in range(acc.shape[1] // LANES):
            x = acc[:, c * LANES:(c + 1) * LANES]
            if c < n_norm:
                x = _norm_rope(x, nw, cs, s1, s2, lo)
            o_ref[r0:r0 + sub, c * LANES:(c + 1) * LANES] = x.astype(o_ref.dtype)

    _for_row_subtiles(h_ref, wbf, sub, epilogue)


def _gates_epilogue(acc, al_ref, dt_ref, beta_ref, gc_ref):
    beta_ref[...] = _sigmoid(acc)
    x = pltpu.roll(acc, LANES - DN_V_HEADS, 1) + dt_ref[...]
    sp = jnp.maximum(x, 0.0) + jnp.log(1.0 + jnp.exp(-jnp.abs(x)))
    g = -jnp.exp(al_ref[...]) * sp
    r64 = lax.broadcasted_iota(jnp.int32, g.shape, 0) & (DN_CHUNK - 1)
    s = 1
    while s < DN_CHUNK:
        g = g + jnp.where(r64 >= s, pltpu.roll(g, s, 0), 0.0)
        s *= 2
    gc_ref[...] = g


def _proj(h, w_t, layer, col0, width, mode, T, extras=(), tm=1024, wblk=512, nblk=2, sub=256,
          name="proj"):
    N, K = h.shape
    out_dtype = BF16
    tm = min(tm, T)
    sub = min(sub, tm)
    off = col0 % LANES
    base = col0 - off
    shifted = off != 0
    tn = wblk * nblk
    assert off in (0, HALF) and base % wblk == 0 and width % tn == 0
    jb = base // wblk
    grid = (width // tn, N // tm)
    in_specs = [pl.BlockSpec((tm, K), lambda j, i: (i, 0))]
    in_specs += [pl.BlockSpec((None, wblk, K), lambda j, i, b=b: (layer, jb + j * nblk + b, 0))
                 for b in range(nblk)]
    args = [h] + [w_t] * nblk
    scratch = [pltpu.VMEM((tn, K), BF16)]
    out_spec = pl.BlockSpec((tm, tn), lambda j, i: (i, j))
    out_shape = jax.ShapeDtypeStruct((N, width), out_dtype)
    if shifted:
        in_specs.append(pl.BlockSpec((None, HALF, K),
                                     lambda j, i: (layer, (base + (j + 1) * tn) // HALF, 0)))
        args.append(w_t)
    if mode == "conv":
        cw, cw_col0, norm = extras
        in_specs.append(pl.BlockSpec((DN_CONV, tn), lambda j, i: (0, cw_col0 // tn + j)))
        args.append(cw)
        scratch.append(pltpu.VMEM((tm + SUBLANES, tn), F32))
        body = functools.partial(_proj_conv_body, tm=tm, tps=T // tm, norm=norm, sub=sub, nblk=nblk)
    elif mode == "qknorm":
        tabs, nw, n_norm = extras
        in_specs += [pl.BlockSpec((tm, LANES), lambda j, i: (i, 0))] * 3
        in_specs.append(pl.BlockSpec((1, LANES), lambda j, i: (0, 0)))
        args += list(tabs) + [nw]
        body = functools.partial(_proj_qknorm_body, sub=sub, nblk=nblk, n_norm=n_norm)
    else:
        body = functools.partial(_proj_act_body, act=mode, shifted=shifted, sub=sub if mode else tm,
                                 nblk=nblk)
    return pl.pallas_call(
        body, grid=grid, in_specs=in_specs, out_specs=out_spec, out_shape=out_shape,
        scratch_shapes=scratch,
        compiler_params=_cparams(("arbitrary", "arbitrary")),
        name=name,
    )(*args)


def _rope_inv_freq_lanes():
    half = ROPE_DIM // 2
    inv = (np.float32(ROPE_THETA) ** (-np.arange(half, dtype=np.float32) * np.float32(2.0 / ROPE_DIM))).astype(np.float32)
    lanes = np.zeros((1, LANES), np.float32)
    for l in range(LANES):
        m = l % SWA_HEAD_DIM
        if m < ROPE_DIM:
            lanes[0, l] = inv[m % half]
    return lanes


def _swa_body(sink_ref, q_ref, kv_ref, g_ref, o_ref, kprev, vprev):
    n = pl.program_id(1)
    BLK = SWA_BLOCK
    HD = SWA_HEAD_DIM

    @pl.when(n == 0)
    def _():
        kprev[...] = jnp.zeros_like(kprev)
        vprev[...] = jnp.zeros_like(vprev)

    lo = lax.broadcasted_iota(jnp.int32, (1, LANES), 1) < HD
    n_kc = SWA_KV // LANES
    nblocks = q_ref.shape[0] // BLK
    zrows = (lax.broadcasted_iota(jnp.int32, (HD, BLK), 0) == 0).astype(F32)
    top = lax.broadcasted_iota(jnp.int32, (LANES, 1), 0) < HD

    kband, vbandT = {}, {}
    k_last = {idx: kprev[idx] for idx in range(2 * SWA_KV_HEADS)}
    v_last = {idx: vprev[idx] for idx in range(2 * SWA_KV_HEADS)}
    for bi in range(nblocks):
        kv = kv_ref[bi * BLK:(bi + 1) * BLK, :]
        kc = [kv[:, c * LANES:(c + 1) * LANES].astype(F32) for c in range(n_kc)]
        vT = [kv[:, SWA_KV + c * LANES:SWA_KV + (c + 1) * LANES].astype(F32).T for c in range(n_kc)]
        for j in range(SWA_KV_HEADS):
            cj, b = j // 2, j % 2
            for a in range(2):
                src = kc[cj] if a == b else pltpu.roll(kc[cj], HD, 1)
                k_cur = jnp.where(lo if a == 0 else jnp.logical_not(lo), src, 0.0).astype(BF16)
                rows = vT[cj][b * HD:(b + 1) * HD]
                v_cur = jnp.concatenate([rows, zrows] if a == 0 else [zrows, rows], axis=0).astype(BF16)
                idx = 2 * j + a
                kband[(bi, j, a)] = jnp.concatenate([k_last[idx], k_cur], axis=0)
                vbandT[(bi, j, a)] = jnp.concatenate([v_last[idx], v_cur], axis=1)
                k_last[idx], v_last[idx] = k_cur, v_cur
    for idx in range(2 * SWA_KV_HEADS):
        kprev[idx] = k_last[idx]
        vprev[idx] = v_last[idx]

    kj = lax.broadcasted_iota(jnp.int32, (2 * BLK, 2 * BLK), 0)
    qi = lax.broadcasted_iota(jnp.int32, (2 * BLK, 2 * BLK), 1) & (BLK - 1)
    d = kj - qi
    in_window = (d >= 1) & (d <= BLK)
    kmin = jnp.where(n == 0, BLK, 0)
    valid = [in_window & (kj >= kmin)] + [in_window] * (nblocks - 1)
    left = lax.broadcasted_iota(jnp.int32, (1, 2 * BLK), 1) < BLK

    G = SWA_Q_HEADS // SWA_KV_HEADS
    cpk = G // 2
    items = [(bi, j, a, j * cpk + 2 * p) for bi in range(nblocks) for j in range(SWA_KV_HEADS)
             for a in range(2) for p in range(cpk // 2)]

    def scores(item):
        bi, j, a, c1 = item
        rows = slice(bi * BLK, (bi + 1) * BLK)
        rhs = jnp.concatenate([q_ref[rows, c1 * LANES:(c1 + 1) * LANES],
                               q_ref[rows, (c1 + 1) * LANES:(c1 + 2) * LANES]], axis=0)
        return lax.dot_general(kband[(bi, j, a)], rhs, NT_DIMS, preferred_element_type=F32)

    acc = {}

    def consume(item, oT, sink_term):
        bi, j, a, c1 = item
        rows = slice(bi * BLK, (bi + 1) * BLK)
        r1 = (1 - a) * HD
        oT = oT * (1.0 / (oT[r1:r1 + 1, :] + sink_term))
        for c, part in ((c1, oT[:, :BLK]), (c1 + 1, oT[:, BLK:])):
            if a == 0:
                acc[c] = part
            else:
                gate = g_ref[rows, c * LANES:(c + 1) * LANES].astype(F32)
                o_pair = jnp.where(top, acc.pop(c), part)
                o_ref[rows, c * LANES:(c + 1) * LANES] = (o_pair.T * gate).astype(BF16)

    ahead = 4
    queue = [scores(it) for it in items[:ahead]]
    pending = None
    for i, item in enumerate(items):
        bi, j, a, c1 = item
        if i + ahead < len(items):
            queue.append(scores(items[i + ahead]))
        s_cur = queue.pop(0)
        s = jnp.where(valid[bi], s_cur, -jnp.inf)
        sink = jnp.where(left, sink_ref[2 * c1 + a], sink_ref[2 * c1 + 2 + a])
        m = jnp.maximum(jnp.max(s, axis=0, keepdims=True), sink)
        e = jnp.exp2(s - m)
        oT = jnp.dot(vbandT[(bi, j, a)], e.astype(BF16), preferred_element_type=F32)
        if pending is not None:
            consume(*pending)
        pending = (item, oT, jnp.exp2(sink - m))
    consume(*pending)


def _swa(q_n, kv_n, sag, sinks, B, T, blocks_per_step=4):
    N = B * T
    rows = blocks_per_step * SWA_BLOCK
    nb = T // rows
    row = lambda b, n: b * nb + n
    return pl.pallas_call(
        _swa_body,
        grid=(B, nb),
        in_specs=[pl.BlockSpec(memory_space=pltpu.SMEM),
                  pl.BlockSpec((rows, SWA_Q), lambda b, n: (row(b, n), 0)),
                  pl.BlockSpec((rows, 2 * SWA_KV), lambda b, n: (row(b, n), 0)),
                  pl.BlockSpec((rows, SWA_Q), lambda b, n: (row(b, n), 0))],
        out_specs=pl.BlockSpec((rows, SWA_Q), lambda b, n: (row(b, n), 0)),
        out_shape=jax.ShapeDtypeStruct((N, SWA_Q), BF16),
        scratch_shapes=[pltpu.VMEM((2 * SWA_KV_HEADS, SWA_BLOCK, LANES), BF16),
                        pltpu.VMEM((2 * SWA_KV_HEADS, LANES, SWA_BLOCK), BF16)],
        compiler_params=_cparams(("arbitrary", "arbitrary")),
        name="swa_attn",
    )(sinks, q_n, kv_n, sag)


def _delta_body(q_ref, k_ref, v_ref, z_ref, bc_ref, gc_ref, gr_ref, nw_ref, o_ref, S_ref,
                *, KH, TT):
    grp = pl.program_id(1)
    t = pl.program_id(2)
    C = DN_CHUNK
    NC = TT // C
    scale = DN_HEAD ** -0.5
    NT = (((1,), (1,)), ((), ()))

    @pl.when(t == 0)
    def _():
        S_ref[...] = jnp.zeros_like(S_ref)

    NV = 2 * KH
    row = lax.broadcasted_iota(jnp.int32, (TT, TT), 0)
    col = lax.broadcasted_iota(jnp.int32, (TT, TT), 1)
    same = (row // C) == (col // C)
    ri = lax.broadcasted_iota(jnp.int32, (C, TT), 0)
    cj = lax.broadcasted_iota(jnp.int32, (C, TT), 1)
    cjc = cj // C
    cjj = cj & (C - 1)
    tril_c = ri >= cjj
    strict_c = ri > cjj
    eye_c = (ri == cjj).astype(F32)
    lane = lax.broadcasted_iota(jnp.int32, (TT, LANES), 1)
    lchunk = lax.broadcasted_iota(jnp.int32, (1, TT), 1) // C
    bcall = bc_ref[...]
    gcall = gc_ref[...]
    nw = nw_ref[...]

    def compress(x):
        out = x[(NC - 1) * C:NC * C]
        for c in range(NC - 2, -1, -1):
            out = jnp.where(cjc == c, x[c * C:(c + 1) * C], out)
        return out

    def by_row(colvec):
        out = jnp.broadcast_to(colvec[(NC - 1) * C:NC * C], (C, TT))
        for c in range(NC - 2, -1, -1):
            out = jnp.where(cjc == c, colvec[c * C:(c + 1) * C], out)
        return out

    def bd(xc):
        return jnp.where(same, jnp.concatenate([xc] * NC, axis=0), 0.0).astype(BF16)

    kf, qf, kT, kkc, qkc = [], [], [], [], []
    for kh in range(KH):
        q = q_ref[:, kh * DN_HEAD:(kh + 1) * DN_HEAD]
        k = k_ref[:, kh * DN_HEAD:(kh + 1) * DN_HEAD]
        kq = lax.dot_general(jnp.concatenate([k, q], axis=0), k, NT, preferred_element_type=F32)
        kkc.append(compress(kq[:TT]))
        qkc.append(compress(kq[TT:]))
        kf.append(k.astype(F32))
        qf.append(q.astype(F32))
        kT.append(kf[-1].T)

    bcol, gcol, grow, Lc, qkm, P = [], [], [], [], [], []
    for hh in range(NV):
        hglob = grp * NV + hh
        sel = lane == hglob
        bcol.append(jnp.sum(jnp.where(sel, bcall, 0.0), axis=-1, keepdims=True))
        gcol.append(jnp.sum(jnp.where(sel, gcall, 0.0), axis=-1, keepdims=True))
        grow.append(gr_ref[0, pl.ds(hglob, 1), :])
        dec = jnp.exp(jnp.where(tril_c, by_row(gcol[hh]) - grow[hh], -jnp.inf))
        Lc.append(jnp.where(strict_c, by_row(bcol[hh]) * kkc[hh // 2] * dec, 0.0))
        qkm.append(qkc[hh // 2] * dec * scale)
        P.append(eye_c - Lc[hh])

    Lp = [jnp.dot(Lc[hh].astype(BF16), bd(Lc[hh]), preferred_element_type=F32) for hh in range(NV)]
    step = 4
    while step < C:
        res = [jnp.dot(jnp.concatenate([Lp[hh], P[hh]], axis=0).astype(BF16), bd(Lp[hh]),
                       preferred_element_type=F32) for hh in range(NV)]
        Lp = [r[:C] for r in res]
        P = [P[hh] + res[hh][C:] for hh in range(NV)]
        step *= 2
    P = [P[hh] + jnp.dot(P[hh].astype(BF16), bd(Lp[hh]), preferred_element_type=F32) for hh in range(NV)]

    uw, qd, kdT, eg_last = [], [], [], []
    for hh in range(NV):
        eg = jnp.exp(gcol[hh])
        vf = v_ref[:, hh * DN_HEAD:(hh + 1) * DN_HEAD].astype(F32)
        rhs = jnp.concatenate([vf * bcol[hh], kf[hh // 2] * (bcol[hh] * eg)], axis=1).astype(BF16)
        uw.append(jnp.dot(bd(P[hh]), rhs, preferred_element_type=F32))
        qd.append((qf[hh // 2] * (eg * scale)).astype(BF16))
        glrow = jnp.zeros((1, TT), F32)
        for c in range(NC):
            glrow = jnp.where(lchunk == c, gcol[hh][c * C + C - 1:c * C + C, :], glrow)
        kdT.append(kT[hh // 2] * jnp.exp(glrow - grow[hh]))
        eg_last.append([eg[c * C + C - 1:c * C + C, :] for c in range(NC)])

    S = [S_ref[hh] for hh in range(NV)]
    vn = [[] for _ in range(NV)]
    o_inter = [[] for _ in range(NV)]
    for c in range(NC):
        r0, r1 = c * C, (c + 1) * C
        res = []
        for hh in range(NV):
            w_c = uw[hh][r0:r1, DN_HEAD:].astype(BF16)
            lhs = jnp.concatenate([w_c, qd[hh][r0:r1]], axis=0)
            res.append(jnp.dot(lhs, S[hh].astype(BF16), preferred_element_type=F32))
        inc = []
        for hh in range(NV):
            vn_c = (uw[hh][r0:r1, :DN_HEAD] - res[hh][:C]).astype(BF16)
            vn[hh].append(vn_c)
            o_inter[hh].append(res[hh][C:])
            kd_c = kdT[hh][:, r0:r1].astype(BF16)
            inc.append(jnp.dot(kd_c, vn_c, preferred_element_type=F32))
        for hh in range(NV):
            S[hh] = S[hh] * eg_last[hh][c] + inc[hh]

    for hh in range(NV):
        S_ref[hh] = S[hh]
        vn_all = jnp.concatenate(vn[hh], axis=0)
        o = jnp.concatenate(o_inter[hh], axis=0) + jnp.dot(bd(qkm[hh]), vn_all, preferred_element_type=F32)
        ms = jnp.mean(o * o, axis=-1, keepdims=True)
        z = z_ref[:, hh * DN_HEAD:(hh + 1) * DN_HEAD].astype(F32)
        o_ref[:, hh * DN_HEAD:(hh + 1) * DN_HEAD] = (o * lax.rsqrt(ms + EPS) * nw * z).astype(BF16)


def _delta(qk_n, vv, sz, beta, gc, gcT, dn_norm_w, B, T, KH=8, TT=256):
    N = B * T
    nt = T // TT
    ng = DN_K_HEADS // KH
    row = lambda b, g, t: b * nt + t
    kw = KH * DN_HEAD
    return pl.pallas_call(
        functools.partial(_delta_body, KH=KH, TT=TT),
        grid=(B, ng, nt),
        in_specs=[pl.BlockSpec((TT, kw), lambda b, g, t: (row(b, g, t), g)),
                  pl.BlockSpec((TT, kw), lambda b, g, t: (row(b, g, t), DN_KEY // kw + g)),
                  pl.BlockSpec((TT, 2 * kw), lambda b, g, t: (row(b, g, t), g)),
                  pl.BlockSpec((TT, 2 * kw), lambda b, g, t: (row(b, g, t), g)),
                  pl.BlockSpec((TT, LANES), lambda b, g, t: (row(b, g, t), 0)),
                  pl.BlockSpec((TT, LANES), lambda b, g, t: (row(b, g, t), 0)),
                  pl.BlockSpec((1, DN_V_HEADS, TT), lambda b, g, t: (b, 0, t)),
                  pl.BlockSpec((1, DN_HEAD), lambda b, g, t: (0, 0))],
        out_specs=pl.BlockSpec((TT, 2 * kw), lambda b, g, t: (row(b, g, t), g)),
        out_shape=jax.ShapeDtypeStruct((N, DN_VAL), BF16),
        scratch_shapes=[pltpu.VMEM((2 * KH, DN_HEAD, DN_HEAD), F32)],
        compiler_params=_cparams(("arbitrary", "arbitrary", "arbitrary")),
        name="delta_rule",
    )(qk_n, qk_n, vv, sz, beta, gc, gcT, dn_norm_w.reshape(1, DN_HEAD))


def _merge_out_body(a_ref, o_ref, gm_ref, wa_ref, wb_ref, wo_ref, x_ref, g_ref, out_ref):
    D = out_ref.shape[1]
    ya = jnp.dot(a_ref[...], wa_ref[...], preferred_element_type=F32)
    yb = jnp.dot(o_ref[...], wb_ref[...], preferred_element_type=F32)
    y = (gm_ref[:, :D].astype(F32) * ya + gm_ref[:, D:].astype(F32) * yb).astype(BF16)
    out_ref[...] = x_ref[...] + g_ref[0] * jnp.dot(y, wo_ref[...], preferred_element_type=F32)


def _merge_out(a_g, o_g, p_m, wa, wb, wo, x2d, gate, T, tm=256):
    N, D = x2d.shape
    B = gate.shape[0]
    tps = T // tm
    resident = lambda w: pl.BlockSpec(w.shape, lambda i: (0, 0), pipeline_mode=pl.Buffered(1))
    return pl.pallas_call(
        _merge_out_body,
        grid=(N // tm,),
        in_specs=[pl.BlockSpec((tm, a_g.shape[1]), lambda i: (i, 0)),
                  pl.BlockSpec((tm, o_g.shape[1]), lambda i: (i, 0)),
                  pl.BlockSpec((tm, p_m.shape[1]), lambda i: (i, 0)),
                  resident(wa), resident(wb), resident(wo),
                  pl.BlockSpec((tm, D), lambda i: (i, 0)),
                  pl.BlockSpec((1, 1, D), lambda i: (i // tps, 0, 0))],
        out_specs=pl.BlockSpec((tm, D), lambda i: (i, 0)),
        out_shape=jax.ShapeDtypeStruct((N, D), F32),
        compiler_params=_cparams(("arbitrary",), vmem=60 * 1024 * 1024),
        name="merge_out",
    )(a_g, o_g, p_m, wa, wb, wo, x2d, gate.reshape(B, 1, D))


def _layer(l, x, c, positions, w_ada, b_ada, norm_w, w_in, q_norm_w, k_norm_w, sinks,
           conv_w, a_log, dt_bias, dn_norm_w, w_o_swa, w_o_dn, w_out):
    B, T, D = x.shape
    N = B * T
    x2d = x.reshape(N, D)

    mod = _adaln(c, w_ada, b_ada)
    shift, scale, gate = mod[:, :D], mod[:, D:2 * D], mod[:, 2 * D:]
    o_q, o_g = 0, SWA_Q + 2 * SWA_KV
    o_d = o_g + SWA_Q
    o_z = o_d + DN_CONV_CH
    o_b = o_z + DN_VAL
    o_a = o_b + DN_V_HEADS
    o_m = o_a + DN_V_HEADS
    w_in = jnp.swapaxes(w_in, 1, 2)

    posf = positions.astype(F32).reshape(N, 1)
    al = jnp.zeros((1, LANES), F32).at[0, :DN_V_HEADS].set(a_log)
    dt = jnp.zeros((1, LANES), F32).at[0, :DN_V_HEADS].set(dt_bias)
    rep = LANES // SWA_HEAD_DIM
    qw = jnp.tile(q_norm_w.reshape(1, SWA_HEAD_DIM), (1, rep)) * (SWA_HEAD_DIM ** -0.5 * LOG2E)
    kw = jnp.tile(k_norm_w.reshape(1, SWA_HEAD_DIM), (1, rep))
    h, tabs, beta, gc, kv_n = _normmod(x2d, norm_w, scale, shift, posf, w_in, l, o_b, al, dt,
                                       o_q + SWA_Q, kw, T)
    q_n = _proj(h, w_in, l, o_q, SWA_Q, "qknorm", T, extras=(tabs, qw, SWA_Q // LANES), name="proj_q")
    sag = _proj(h, w_in, l, o_g, o_d - o_g, "silu", T, tm=2048, name="proj_swa_gate")
    qk_n = _proj(h, w_in, l, o_d, 2 * DN_KEY, "conv", T, extras=(conv_w, 0, True), name="proj_dn_qk")
    vv = _proj(h, w_in, l, o_d + 2 * DN_KEY, DN_VAL, "conv", T, extras=(conv_w, 2 * DN_KEY, False),
               name="proj_dn_v")
    sz = _proj(h, w_in, l, o_z, o_b - o_z, "silu", T, tm=2048, name="proj_dn_z")
    p_m = _proj(h, w_in, l, o_m, 2 * D, "sigmoid", T, tm=2048, name="proj_merge_gates")

    a_g = _swa(q_n, kv_n, sag, sinks * LOG2E, B, T)

    gcT = gc[:, :DN_V_HEADS].reshape(B, T, DN_V_HEADS).transpose(0, 2, 1)
    o_g2 = _delta(qk_n, vv, sz, beta, gc, gcT, dn_norm_w, B, T)

    out = _merge_out(a_g, o_g2, p_m, w_o_swa.astype(BF16), w_o_dn.astype(BF16), w_out.astype(BF16),
                     x2d, gate, T)
    return out.reshape(B, T, D)


def kernel(x, c, positions, w_ada, b_ada, norm_w, w_in, q_norm_w, k_norm_w, sinks, conv_w,
           a_log, dt_bias, dn_norm_w, w_o_swa, w_o_dn, w_out):
    depth = w_ada.shape[0]
    for l in range(depth):
        x = _layer(l, x, c, positions, w_ada[l], b_ada[l], norm_w[l], w_in, q_norm_w[l],
                   k_norm_w[l], sinks[l], conv_w[l], a_log[l], dt_bias[l], dn_norm_w[l],
                   w_o_swa[l], w_o_dn[l], w_out[l])
    return x
```

```python
import functools

import numpy as np
import jax
import jax.numpy as jnp
from jax import lax
from jax.experimental import pallas as pl
from jax.experimental.pallas import tpu as pltpu

F32 = jnp.float32
BF16 = jnp.bfloat16

D_MODEL = 2048
SWA_Q_HEADS = 32
SWA_KV_HEADS = 4
SWA_HEAD_DIM = 64
SWA_BLOCK = 128
ROPE_THETA = 500000.0
ROPE_DIM = SWA_HEAD_DIM // 4
DN_K_HEADS = 16
DN_V_HEADS = 32
DN_HEAD = 128
DN_CONV = 4
DN_CHUNK = 64
EPS = 1e-6
LOG2E = 1.4426950408889634

SWA_Q = SWA_Q_HEADS * SWA_HEAD_DIM
SWA_KV = SWA_KV_HEADS * SWA_HEAD_DIM
DN_KEY = DN_K_HEADS * DN_HEAD
DN_VAL = DN_V_HEADS * DN_HEAD
DN_CONV_CH = 2 * DN_KEY + DN_VAL

LANES = 128
SUBLANES = 8
VMEM_LIMIT = 56 * 1024 * 1024


def _cparams(sem, vmem=VMEM_LIMIT):
    return pltpu.CompilerParams(dimension_semantics=sem, vmem_limit_bytes=vmem)


def _sigmoid(x):
    return 0.5 * jnp.tanh(0.5 * x) + 0.5


def _silu(x):
    h = 0.5 * x
    return h + h * jnp.tanh(h)


def _adaln_body(c_ref, w_ref, b_ref, o_ref):
    s = _silu(c_ref[...]).astype(BF16)
    o_ref[...] = jnp.dot(s, w_ref[...].astype(BF16), preferred_element_type=F32) + b_ref[...]


def _adaln(c, w_ada, b_ada, tn=1024):
    B, D = c.shape
    n_out = w_ada.shape[1]
    c8 = jnp.zeros((SUBLANES, D), F32).at[:B].set(c)
    out = pl.pallas_call(
        _adaln_body,
        grid=(n_out // tn,),
        in_specs=[pl.BlockSpec((SUBLANES, D), lambda j: (0, 0)),
                  pl.BlockSpec((D, tn), lambda j: (0, j)),
                  pl.BlockSpec((1, tn), lambda j: (0, j))],
        out_specs=pl.BlockSpec((SUBLANES, tn), lambda j: (0, j)),
        out_shape=jax.ShapeDtypeStruct((SUBLANES, n_out), F32),
        compiler_params=_cparams(("arbitrary",)),
        name="adaln_mod",
    )(c8, w_ada, b_ada.reshape(1, n_out))
    return out[:B]


def _normmod_body(x_ref, nw_ref, sc_ref, sh_ref, pos_ref, invf_ref, wg_ref, al_ref, dt_ref, wkv_ref, kw_ref,
                  o_ref, c_ref, s1_ref, s2_ref, beta_ref, gc_ref, kv_ref, wkv_bf):
    @pl.when(pl.program_id(0) == 0)
    def _():
        wkv_bf[...] = wkv_ref[...].astype(BF16)

    x = x_ref[...]
    ms = jnp.mean(x * x, axis=-1, keepdims=True)
    y = x * lax.rsqrt(ms + EPS) * nw_ref[...]
    h = (y * (1.0 + sc_ref[0]) + sh_ref[0]).astype(BF16)
    o_ref[...] = h

    acc = lax.dot_general(h, wg_ref[...].astype(BF16), NT_DIMS, preferred_element_type=F32)
    _gates_epilogue(acc, al_ref, dt_ref, beta_ref, gc_ref)

    half = ROPE_DIM // 2
    lane = lax.broadcasted_iota(jnp.int32, (1, LANES), 1)
    l64 = lane & (SWA_HEAD_DIM - 1)
    ang = pos_ref[...] * invf_ref[...]
    sn = jnp.sin(ang)
    cs = jnp.cos(ang)
    s1 = jnp.where((l64 >= half) & (l64 < ROPE_DIM), sn, 0.0)
    s2 = jnp.where(l64 < half, -sn, 0.0)
    c_ref[...] = cs
    s1_ref[...] = s1
    s2_ref[...] = s2

    kv = lax.dot_general(h, wkv_bf[...], NT_DIMS, preferred_element_type=F32)
    for c in range(kv.shape[1] // LANES):
        xc = kv[:, c * LANES:(c + 1) * LANES]
        if c < SWA_KV // LANES:
            xc = _norm_rope(xc, kw_ref[...], cs, s1, s2, lane < SWA_HEAD_DIM)
        kv_ref[:, c * LANES:(c + 1) * LANES] = xc.astype(BF16)


def _normmod(x2d, norm_w, scale, shift, posf, w_t, layer, gate_row0, al, dt, kv_row0, kw, T, tm=1024):
    N, D = x2d.shape
    B = scale.shape[0]
    tm = min(tm, T)
    tps = T // tm
    kvw = 2 * SWA_KV
    assert gate_row0 % LANES == 0 and kv_row0 % kvw == 0
    tab = pl.BlockSpec((tm, LANES), lambda i: (i, 0))
    vec = pl.BlockSpec((1, LANES), lambda i: (0, 0))
    outs = pl.pallas_call(
        _normmod_body,
        grid=(N // tm,),
        in_specs=[pl.BlockSpec((tm, D), lambda i: (i, 0)),
                  pl.BlockSpec((1, D), lambda i: (0, 0)),
                  pl.BlockSpec((1, 1, D), lambda i: (i // tps, 0, 0)),
                  pl.BlockSpec((1, 1, D), lambda i: (i // tps, 0, 0)),
                  pl.BlockSpec((tm, 1), lambda i: (i, 0)),
                  vec,
                  pl.BlockSpec((None, LANES, D), lambda i: (layer, gate_row0 // LANES, 0)),
                  vec, vec,
                  pl.BlockSpec((None, kvw, D), lambda i: (layer, kv_row0 // kvw, 0)),
                  vec],
        out_specs=[pl.BlockSpec((tm, D), lambda i: (i, 0)), tab, tab, tab, tab, tab,
                   pl.BlockSpec((tm, kvw), lambda i: (i, 0))],
        out_shape=[jax.ShapeDtypeStruct((N, D), BF16)] + [jax.ShapeDtypeStruct((N, LANES), F32)] * 5
        + [jax.ShapeDtypeStruct((N, kvw), BF16)],
        scratch_shapes=[pltpu.VMEM((kvw, D), BF16)],
        compiler_params=_cparams(("arbitrary",)),
        name="norm_mod",
    )(x2d, norm_w.reshape(1, D), scale.reshape(B, 1, D), shift.reshape(B, 1, D), posf,
      jnp.asarray(_rope_inv_freq_lanes()), w_t, al, dt, w_t, kw)
    return outs[0], tuple(outs[1:4]), outs[4], outs[5], outs[6]


HALF = LANES // 2


NT_DIMS = (((1,), (1,)), ((), ()))


def _cast_weights(w_refs, w2_ref, wbf):
    wblk = w_refs[0].shape[0]
    off = 0 if w2_ref is None else HALF
    for b, w_ref in enumerate(w_refs):
        lo = max(b * wblk - off, 0)
        wbf[lo:(b + 1) * wblk - off, :] = w_ref[lo + off - b * wblk:, :].astype(BF16)
    if w2_ref is not None:
        n = len(w_refs) * wblk
        wbf[n - off:n, :] = w2_ref[...].astype(BF16)


def _cast_once(w_refs, w2_ref, wbf):
    @pl.when(pl.program_id(1) == 0)
    def _():
        _cast_weights(w_refs, w2_ref, wbf)


MM_AHEAD = 3


def _for_row_subtiles(h_ref, wbf, sub, epilogue):
    starts = list(range(0, h_ref.shape[0], sub))
    mm = lambda r0: lax.dot_general(h_ref[r0:r0 + sub, :], wbf[...], NT_DIMS, preferred_element_type=F32)
    queue = [mm(r0) for r0 in starts[:MM_AHEAD]]
    for idx, r0 in enumerate(starts):
        if idx + MM_AHEAD < len(starts):
            queue.append(mm(starts[idx + MM_AHEAD]))
        epilogue(r0, queue.pop(0))


def _proj_act_body(*refs, act, shifted, sub, nblk):
    h_ref, w_refs = refs[0], refs[1:1 + nblk]
    w2_ref = refs[1 + nblk] if shifted else None
    o_ref, wbf = refs[-2:]
    _cast_once(w_refs, w2_ref, wbf)

    def epilogue(r0, acc):
        if act == "silu":
            acc = _silu(acc)
        elif act == "sigmoid":
            acc = _sigmoid(acc)
        o_ref[r0:r0 + sub, :] = acc.astype(o_ref.dtype)

    _for_row_subtiles(h_ref, wbf, sub, epilogue)


def _proj_conv_body(*refs, tm, tps, norm, sub, nblk):
    h_ref, w_refs = refs[0], refs[1:1 + nblk]
    cw_ref, o_ref, wbf, buf = refs[1 + nblk:]
    i = pl.program_id(1)
    first = (i % tps) == 0
    _cast_once(w_refs, None, wbf)

    @pl.when(first)
    def _():
        buf[0:SUBLANES, :] = jnp.zeros((SUBLANES, buf.shape[1]), F32)

    @pl.when(jnp.logical_not(first))
    def _():
        buf[0:SUBLANES, :] = buf[tm:tm + SUBLANES, :]

    cw = cw_ref[...]

    def epilogue(r0, acc):
        b0 = SUBLANES + r0
        buf[b0:b0 + sub, :] = acc
        win = buf[b0 - SUBLANES:b0 + sub, :]
        y = win[SUBLANES:, :] * cw[DN_CONV - 1:DN_CONV, :]
        for d in range(1, DN_CONV):
            y = y + pltpu.roll(win, d, 0)[SUBLANES:, :] * cw[DN_CONV - 1 - d:DN_CONV - d, :]
        y = _silu(y)
        for c in range(y.shape[1] // DN_HEAD):
            yc = y[:, c * DN_HEAD:(c + 1) * DN_HEAD]
            if norm:
                yc = yc * lax.rsqrt(jnp.sum(yc * yc, axis=-1, keepdims=True) + EPS)
            o_ref[r0:r0 + sub, c * DN_HEAD:(c + 1) * DN_HEAD] = yc.astype(o_ref.dtype)

    _for_row_subtiles(h_ref, wbf, sub, epilogue)


def _norm_rope(x, nw, cs, s1, s2, lo):
    half = ROPE_DIM // 2
    x2 = x * x
    s_lo = jnp.sum(jnp.where(lo, x2, 0.0), axis=-1, keepdims=True)
    s_hi = jnp.sum(jnp.where(lo, 0.0, x2), axis=-1, keepdims=True)
    r = jnp.where(lo, lax.rsqrt(s_lo * (1.0 / SWA_HEAD_DIM) + EPS),
                  lax.rsqrt(s_hi * (1.0 / SWA_HEAD_DIM) + EPS))
    xn = x * r * nw
    return xn * cs + pltpu.roll(xn, half, 1) * s1 + pltpu.roll(xn, LANES - half, 1) * s2


def _proj_qknorm_body(*refs, sub, nblk, n_norm):
    h_ref, w_refs = refs[0], refs[1:1 + nblk]
    c_ref, s1_ref, s2_ref, nw_ref, o_ref, wbf = refs[1 + nblk:]
    _cast_once(w_refs, None, wbf)
    lo = lax.broadcasted_iota(jnp.int32, (1, LANES), 1) < SWA_HEAD_DIM
    nw = nw_ref[...]
    def epilogue(r0, acc):
        cs, s1, s2 = c_ref[r0:r0 + sub, :], s1_ref[r0:r0 + sub, :], s2_ref[r0:r0 + sub, :]
        for c in range(acc.shape[1] // LANES):
            x = acc[:, c * LANES:(c + 1) * LANES]
            if c < n_norm:
                x = _norm_rope(x, nw, cs, s1, s2, lo)
            o_ref[r0:r0 + sub, c * LANES:(c + 1) * LANES] = x.astype(o_ref.dtype)

    _for_row_subtiles(h_ref, wbf, sub, epilogue)


def _gates_epilogue(acc, al_ref, dt_ref, beta_ref, gc_ref):
    beta_ref[...] = _sigmoid(acc)
    x = pltpu.roll(acc, LANES - DN_V_HEADS, 1) + dt_ref[...]
    sp = jnp.maximum(x, 0.0) + jnp.log(1.0 + jnp.exp(-jnp.abs(x)))
    g = -jnp.exp(al_ref[...]) * sp
    r64 = lax.broadcasted_iota(jnp.int32, g.shape, 0) & (DN_CHUNK - 1)
    s = 1
    while s < DN_CHUNK:
        g = g + jnp.where(r64 >= s, pltpu.roll(g, s, 0), 0.0)
        s *= 2
    gc_ref[...] = g


def _proj(h, w_t, layer, col0, width, mode, T, extras=(), tm=1024, wblk=512, nblk=2, sub=256,
          name="proj"):
    N, K = h.shape
    out_dtype = BF16
    tm = min(tm, T)
    sub = min(sub, tm)
    off = col0 % LANES
    base = col0 - off
    shifted = off != 0
    tn = wblk * nblk
    assert off in (0, HALF) and base % wblk == 0 and width % tn == 0
    jb = base // wblk
    grid = (width // tn, N // tm)
    in_specs = [pl.BlockSpec((tm, K), lambda j, i: (i, 0))]
    in_specs += [pl.BlockSpec((None, wblk, K), lambda j, i, b=b: (layer, jb + j * nblk + b, 0))
                 for b in range(nblk)]
    args = [h] + [w_t] * nblk
    scratch = [pltpu.VMEM((tn, K), BF16)]
    out_spec = pl.BlockSpec((tm, tn), lambda j, i: (i, j))
    out_shape = jax.ShapeDtypeStruct((N, width), out_dtype)
    if shifted:
        in_specs.append(pl.BlockSpec((None, HALF, K),
                                     lambda j, i: (layer, (base + (j + 1) * tn) // HALF, 0)))
        args.append(w_t)
    if mode == "conv":
        cw, cw_col0, norm = extras
        in_specs.append(pl.BlockSpec((DN_CONV, tn), lambda j, i: (0, cw_col0 // tn + j)))
        args.append(cw)
        scratch.append(pltpu.VMEM((tm + SUBLANES, tn), F32))
        body = functools.partial(_proj_conv_body, tm=tm, tps=T // tm, norm=norm, sub=sub, nblk=nblk)
    elif mode == "qknorm":
        tabs, nw, n_norm = extras
        in_specs += [pl.BlockSpec((tm, LANES), lambda j, i: (i, 0))] * 3
        in_specs.append(pl.BlockSpec((1, LANES), lambda j, i: (0, 0)))
        args += list(tabs) + [nw]
        body = functools.partial(_proj_qknorm_body, sub=sub, nblk=nblk, n_norm=n_norm)
    else:
        body = functools.partial(_proj_act_body, act=mode, shifted=shifted, sub=sub if mode else tm,
                                 nblk=nblk)
    return pl.pallas_call(
        body, grid=grid, in_specs=in_specs, out_specs=out_spec, out_shape=out_shape,
        scratch_shapes=scratch,
        compiler_params=_cparams(("arbitrary", "arbitrary")),
        name=name,
    )(*args)


def _rope_inv_freq_lanes():
    half = ROPE_DIM // 2
    inv = (np.float32(ROPE_THETA) ** (-np.arange(half, dtype=np.float32) * np.float32(2.0 / ROPE_DIM))).astype(np.float32)
    lanes = np.zeros((1, LANES), np.float32)
    for l in range(LANES):
        m = l % SWA_HEAD_DIM
        if m < ROPE_DIM:
            lanes[0, l] = inv[m % half]
    return lanes


def _swa_body(sink_ref, q_ref, kv_ref, g_ref, o_ref, kprev, vprev):
    n = pl.program_id(1)
    BLK = SWA_BLOCK
    HD = SWA_HEAD_DIM

    @pl.when(n == 0)
    def _():
        kprev[...] = jnp.zeros_like(kprev)
        vprev[...] = jnp.zeros_like(vprev)

    lo = lax.broadcasted_iota(jnp.int32, (1, LANES), 1) < HD
    n_kc = SWA_KV // LANES
    nblocks = q_ref.shape[0] // BLK
    zrows = (lax.broadcasted_iota(jnp.int32, (HD, BLK), 0) == 0).astype(F32)
    top = lax.broadcasted_iota(jnp.int32, (LANES, 1), 0) < HD

    kband, vbandT = {}, {}
    k_last = {idx: kprev[idx] for idx in range(2 * SWA_KV_HEADS)}
    v_last = {idx: vprev[idx] for idx in range(2 * SWA_KV_HEADS)}
    for bi in range(nblocks):
        kv = kv_ref[bi * BLK:(bi + 1) * BLK, :]
        kc = [kv[:, c * LANES:(c + 1) * LANES].astype(F32) for c in range(n_kc)]
        vT = [kv[:, SWA_KV + c * LANES:SWA_KV + (c + 1) * LANES].astype(F32).T for c in range(n_kc)]
        for j in range(SWA_KV_HEADS):
            cj, b = j // 2, j % 2
            for a in range(2):
                src = kc[cj] if a == b else pltpu.roll(kc[cj], HD, 1)
                k_cur = jnp.where(lo if a == 0 else jnp.logical_not(lo), src, 0.0).astype(BF16)
                rows = vT[cj][b * HD:(b + 1) * HD]
                v_cur = jnp.concatenate([rows, zrows] if a == 0 else [zrows, rows], axis=0).astype(BF16)
                idx = 2 * j + a
                kband[(bi, j, a)] = jnp.concatenate([k_last[idx], k_cur], axis=0)
                vbandT[(bi, j, a)] = jnp.concatenate([v_last[idx], v_cur], axis=1)
                k_last[idx], v_last[idx] = k_cur, v_cur
    for idx in range(2 * SWA_KV_HEADS):
        kprev[idx] = k_last[idx]
        vprev[idx] = v_last[idx]

    kj = lax.broadcasted_iota(jnp.int32, (2 * BLK, 2 * BLK), 0)
    qi = lax.broadcasted_iota(jnp.int32, (2 * BLK, 2 * BLK), 1) & (BLK - 1)
    d = kj - qi
    in_window = (d >= 1) & (d <= BLK)
    kmin = jnp.where(n == 0, BLK, 0)
    valid = [in_window & (kj >= kmin)] + [in_window] * (nblocks - 1)
    left = lax.broadcasted_iota(jnp.int32, (1, 2 * BLK), 1) < BLK

    G = SWA_Q_HEADS // SWA_KV_HEADS
    cpk = G // 2
    items = [(bi, j, a, j * cpk + 2 * p) for bi in range(nblocks) for j in range(SWA_KV_HEADS)
             for a in range(2) for p in range(cpk // 2)]

    def scores(item):
        bi, j, a, c1 = item
        rows = slice(bi * BLK, (bi + 1) * BLK)
        rhs = jnp.concatenate([q_ref[rows, c1 * LANES:(c1 + 1) * LANES],
                               q_ref[rows, (c1 + 1) * LANES:(c1 + 2) * LANES]], axis=0)
        return lax.dot_general(kband[(bi, j, a)], rhs, NT_DIMS, preferred_element_type=F32)

    acc = {}

    def consume(item, oT, sink_term):
        bi, j, a, c1 = item
        rows = slice(bi * BLK, (bi + 1) * BLK)
        r1 = (1 - a) * HD
        oT = oT * (1.0 / (oT[r1:r1 + 1, :] + sink_term))
        for c, part in ((c1, oT[:, :BLK]), (c1 + 1, oT[:, BLK:])):
            if a == 0:
                acc[c] = part
            else:
                gate = g_ref[rows, c * LANES:(c + 1) * LANES].astype(F32)
                o_pair = jnp.where(top, acc.pop(c), part)
                o_ref[rows, c * LANES:(c + 1) * LANES] = (o_pair.T * gate).astype(BF16)

    ahead = 4
    queue = [scores(it) for it in items[:ahead]]
    pending = None
    for i, item in enumerate(items):
        bi, j, a, c1 = item
        if i + ahead < len(items):
            queue.append(scores(items[i + ahead]))
        s_cur = queue.pop(0)
        s = jnp.where(valid[bi], s_cur, -jnp.inf)
        sink = jnp.where(left, sink_ref[2 * c1 + a], sink_ref[2 * c1 + 2 + a])
        m = jnp.maximum(jnp.max(s, axis=0, keepdims=True), sink)
        e = jnp.exp2(s - m)
        oT = jnp.dot(vbandT[(bi, j, a)], e.astype(BF16), preferred_element_type=F32)
        if pending is not None:
            consume(*pending)
        pending = (item, oT, jnp.exp2(sink - m))
    consume(*pending)


def _swa(q_n, kv_n, sag, sinks, B, T, blocks_per_step=4):
    N = B * T
    rows = blocks_per_step * SWA_BLOCK
    nb = T // rows
    row = lambda b, n: b * nb + n
    return pl.pallas_call(
        _swa_body,
        grid=(B, nb),
        in_specs=[pl.BlockSpec(memory_space=pltpu.SMEM),
                  pl.BlockSpec((rows, SWA_Q), lambda b, n: (row(b, n), 0)),
                  pl.BlockSpec((rows, 2 * SWA_KV), lambda b, n: (row(b, n), 0)),
                  pl.BlockSpec((rows, SWA_Q), lambda b, n: (row(b, n), 0))],
        out_specs=pl.BlockSpec((rows, SWA_Q), lambda b, n: (row(b, n), 0)),
        out_shape=jax.ShapeDtypeStruct((N, SWA_Q), BF16),
        scratch_shapes=[pltpu.VMEM((2 * SWA_KV_HEADS, SWA_BLOCK, LANES), BF16),
                        pltpu.VMEM((2 * SWA_KV_HEADS, LANES, SWA_BLOCK), BF16)],
        compiler_params=_cparams(("arbitrary", "arbitrary")),
        name="swa_attn",
    )(sinks, q_n, kv_n, sag)


def _delta_body(q_ref, k_ref, v_ref, z_ref, bc_ref, gc_ref, gr_ref, nw_ref, o_ref, S_ref,
                *, KH, TT):
    grp = pl.program_id(1)
    t = pl.program_id(2)
    C = DN_CHUNK
    NC = TT // C
    scale = DN_HEAD ** -0.5
    NT = (((1,), (1,)), ((), ()))

    @pl.when(t == 0)
    def _():
        S_ref[...] = jnp.zeros_like(S_ref)

    NV = 2 * KH
    row = lax.broadcasted_iota(jnp.int32, (TT, TT), 0)
    col = lax.broadcasted_iota(jnp.int32, (TT, TT), 1)
    same = (row // C) == (col // C)
    ri = lax.broadcasted_iota(jnp.int32, (C, TT), 0)
    cj = lax.broadcasted_iota(jnp.int32, (C, TT), 1)
    cjc = cj // C
    cjj = cj & (C - 1)
    tril_c = ri >= cjj
    strict_c = ri > cjj
    eye_c = (ri == cjj).astype(F32)
    lane = lax.broadcasted_iota(jnp.int32, (TT, LANES), 1)
    lchunk = lax.broadcasted_iota(jnp.int32, (1, TT), 1) // C
    bcall = bc_ref[...]
    gcall = gc_ref[...]
    nw = nw_ref[...]

    def compress(x):
        out = x[(NC - 1) * C:NC * C]
        for c in range(NC - 2, -1, -1):
            out = jnp.where(cjc == c, x[c * C:(c + 1) * C], out)
        return out

    def by_row(colvec):
        out = jnp.broadcast_to(colvec[(NC - 1) * C:NC * C], (C, TT))
        for c in range(NC - 2, -1, -1):
            out = jnp.where(cjc == c, colvec[c * C:(c + 1) * C], out)
        return out

    def bd(xc):
        return jnp.where(same, jnp.concatenate([xc] * NC, axis=0), 0.0).astype(BF16)

    kf, qf, kT, kkc, qkc = [], [], [], [], []
    for kh in range(KH):
        q = q_ref[:, kh * DN_HEAD:(kh + 1) * DN_HEAD]
        k = k_ref[:, kh * DN_HEAD:(kh + 1) * DN_HEAD]
        kq = lax.dot_general(jnp.concatenate([k, q], axis=0), k, NT, preferred_element_type=F32)
        kkc.append(compress(kq[:TT]))
        qkc.append(compress(kq[TT:]))
        kf.append(k.astype(F32))
        qf.append(q.astype(F32))
        kT.append(kf[-1].T)

    bcol, gcol, grow, Lc, qkm, P = [], [], [], [], [], []
    for hh in range(NV):
        hglob = grp * NV + hh
        sel = lane == hglob
        bcol.append(jnp.sum(jnp.where(sel, bcall, 0.0), axis=-1, keepdims=True))
        gcol.append(jnp.sum(jnp.where(sel, gcall, 0.0), axis=-1, keepdims=True))
        grow.append(gr_ref[0, pl.ds(hglob, 1), :])
        dec = jnp.exp(jnp.where(tril_c, by_row(gcol[hh]) - grow[hh], -jnp.inf))
        Lc.append(jnp.where(strict_c, by_row(bcol[hh]) * kkc[hh // 2] * dec, 0.0))
        qkm.append(qkc[hh // 2] * dec * scale)
        P.append(eye_c - Lc[hh])

    Lp = [jnp.dot(Lc[hh].astype(BF16), bd(Lc[hh]), preferred_element_type=F32) for hh in range(NV)]
    step = 4
    while step < C:
        res = [jnp.dot(jnp.concatenate([Lp[hh], P[hh]], axis=0).astype(BF16), bd(Lp[hh]),
                       preferred_element_type=F32) for hh in range(NV)]
        Lp = [r[:C] for r in res]
        P = [P[hh] + res[hh][C:] for hh in range(NV)]
        step *= 2
    P = [P[hh] + jnp.dot(P[hh].astype(BF16), bd(Lp[hh]), preferred_element_type=F32) for hh in range(NV)]

    uw, qd, kdT, eg_last = [], [], [], []
    for hh in range(NV):
        eg = jnp.exp(gcol[hh])
        vf = v_ref[:, hh * DN_HEAD:(hh + 1) * DN_HEAD].astype(F32)
        rhs = jnp.concatenate([vf * bcol[hh], kf[hh // 2] * (bcol[hh] * eg)], axis=1).astype(BF16)
        uw.append(jnp.dot(bd(P[hh]), rhs, preferred_element_type=F32))
        qd.append((qf[hh // 2] * (eg * scale)).astype(BF16))
        glrow = jnp.zeros((1, TT), F32)
        for c in range(NC):
            glrow = jnp.where(lchunk == c, gcol[hh][c * C + C - 1:c * C + C, :], glrow)
        kdT.append(kT[hh // 2] * jnp.exp(glrow - grow[hh]))
        eg_last.append([eg[c * C + C - 1:c * C + C, :] for c in range(NC)])

    S = [S_ref[hh] for hh in range(NV)]
    vn = [[] for _ in range(NV)]
    o_inter = [[] for _ in range(NV)]
    for c in range(NC):
        r0, r1 = c * C, (c + 1) * C
        res = []
        for hh in range(NV):
            w_c = uw[hh][r0:r1, DN_HEAD:].astype(BF16)
            lhs = jnp.concatenate([w_c, qd[hh][r0:r1]], axis=0)
            res.append(jnp.dot(lhs, S[hh].astype(BF16), preferred_element_type=F32))
        inc = []
        for hh in range(NV):
            vn_c = (uw[hh][r0:r1, :DN_HEAD] - res[hh][:C]).astype(BF16)
            vn[hh].append(vn_c)
            o_inter[hh].append(res[hh][C:])
            kd_c = kdT[hh][:, r0:r1].astype(BF16)
            inc.append(jnp.dot(kd_c, vn_c, preferred_element_type=F32))
        for hh in range(NV):
            S[hh] = S[hh] * eg_last[hh][c] + inc[hh]

    for hh in range(NV):
        S_ref[hh] = S[hh]
        vn_all = jnp.concatenate(vn[hh], axis=0)
        o = jnp.concatenate(o_inter[hh], axis=0) + jnp.dot(bd(qkm[hh]), vn_all, preferred_element_type=F32)
        ms = jnp.mean(o * o, axis=-1, keepdims=True)
        z = z_ref[:, hh * DN_HEAD:(hh + 1) * DN_HEAD].astype(F32)
        o_ref[:, hh * DN_HEAD:(hh + 1) * DN_HEAD] = (o * lax.rsqrt(ms + EPS) * nw * z).astype(BF16)


def _delta(qk_n, vv, sz, beta, gc, gcT, dn_norm_w, B, T, KH=16, TT=256):
    N = B * T
    nt = T // TT
    ng = DN_K_HEADS // KH
    row = lambda b, g, t: b * nt + t
    kw = KH * DN_HEAD
    return pl.pallas_call(
        functools.partial(_delta_body, KH=KH, TT=TT),
        grid=(B, ng, nt),
        in_specs=[pl.BlockSpec((TT, kw), lambda b, g, t: (row(b, g, t), g)),
                  pl.BlockSpec((TT, kw), lambda b, g, t: (row(b, g, t), DN_KEY // kw + g)),
                  pl.BlockSpec((TT, 2 * kw), lambda b, g, t: (row(b, g, t), g)),
                  pl.BlockSpec((TT, 2 * kw), lambda b, g, t: (row(b, g, t), g)),
                  pl.BlockSpec((TT, LANES), lambda b, g, t: (row(b, g, t), 0)),
                  pl.BlockSpec((TT, LANES), lambda b, g, t: (row(b, g, t), 0)),
                  pl.BlockSpec((1, DN_V_HEADS, TT), lambda b, g, t: (b, 0, t)),
                  pl.BlockSpec((1, DN_HEAD), lambda b, g, t: (0, 0))],
        out_specs=pl.BlockSpec((TT, 2 * kw), lambda b, g, t: (row(b, g, t), g)),
        out_shape=jax.ShapeDtypeStruct((N, DN_VAL), BF16),
        scratch_shapes=[pltpu.VMEM((2 * KH, DN_HEAD, DN_HEAD), F32)],
        compiler_params=_cparams(("arbitrary", "arbitrary", "arbitrary")),
        name="delta_rule",
    )(qk_n, qk_n, vv, sz, beta, gc, gcT, dn_norm_w.reshape(1, DN_HEAD))


def _merge_out_body(a_ref, o_ref, gm_ref, wa_ref, wb_ref, wo_ref, x_ref, g_ref, out_ref):
    D = out_ref.shape[1]
    ya = jnp.dot(a_ref[...], wa_ref[...], preferred_element_type=F32)
    yb = jnp.dot(o_ref[...], wb_ref[...], preferred_element_type=F32)
    y = (gm_ref[:, :D].astype(F32) * ya + gm_ref[:, D:].astype(F32) * yb).astype(BF16)
    out_ref[...] = x_ref[...] + g_ref[0] * jnp.dot(y, wo_ref[...], preferred_element_type=F32)


def _merge_out(a_g, o_g, p_m, wa, wb, wo, x2d, gate, T, tm=256):
    N, D = x2d.shape
    B = gate.shape[0]
    tps = T // tm
    resident = lambda w: pl.BlockSpec(w.shape, lambda i: (0, 0), pipeline_mode=pl.Buffered(1))
    return pl.pallas_call(
        _merge_out_body,
        grid=(N // tm,),
        in_specs=[pl.BlockSpec((tm, a_g.shape[1]), lambda i: (i, 0)),
                  pl.BlockSpec((tm, o_g.shape[1]), lambda i: (i, 0)),
                  pl.BlockSpec((tm, p_m.shape[1]), lambda i: (i, 0)),
                  resident(wa), resident(wb), resident(wo),
                  pl.BlockSpec((tm, D), lambda i: (i, 0)),
                  pl.BlockSpec((1, 1, D), lambda i: (i // tps, 0, 0))],
        out_specs=pl.BlockSpec((tm, D), lambda i: (i, 0)),
        out_shape=jax.ShapeDtypeStruct((N, D), F32),
        compiler_params=_cparams(("arbitrary",), vmem=60 * 1024 * 1024),
        name="merge_out",
    )(a_g, o_g, p_m, wa, wb, wo, x2d, gate.reshape(B, 1, D))


def _layer(l, x, c, positions, w_ada, b_ada, norm_w, w_in, q_norm_w, k_norm_w, sinks,
           conv_w, a_log, dt_bias, dn_norm_w, w_o_swa, w_o_dn, w_out):
    B, T, D = x.shape
    N = B * T
    x2d = x.reshape(N, D)

    mod = _adaln(c, w_ada, b_ada)
    shift, scale, gate = mod[:, :D], mod[:, D:2 * D], mod[:, 2 * D:]
    o_q, o_g = 0, SWA_Q + 2 * SWA_KV
    o_d = o_g + SWA_Q
    o_z = o_d + DN_CONV_CH
    o_b = o_z + DN_VAL
    o_a = o_b + DN_V_HEADS
    o_m = o_a + DN_V_HEADS
    w_in = jnp.swapaxes(w_in, 1, 2)

    posf = positions.astype(F32).reshape(N, 1)
    al = jnp.zeros((1, LANES), F32).at[0, :DN_V_HEADS].set(a_log)
    dt = jnp.zeros((1, LANES), F32).at[0, :DN_V_HEADS].set(dt_bias)
    rep = LANES // SWA_HEAD_DIM
    qw = jnp.tile(q_norm_w.reshape(1, SWA_HEAD_DIM), (1, rep)) * (SWA_HEAD_DIM ** -0.5 * LOG2E)
    kw = jnp.tile(k_norm_w.reshape(1, SWA_HEAD_DIM), (1, rep))
    h, tabs, beta, gc, kv_n = _normmod(x2d, norm_w, scale, shift, posf, w_in, l, o_b, al, dt,
                                       o_q + SWA_Q, kw, T)
    q_n = _proj(h, w_in, l, o_q, SWA_Q, "qknorm", T, extras=(tabs, qw, SWA_Q // LANES), name="proj_q")
    sag = _proj(h, w_in, l, o_g, o_d - o_g, "silu", T, tm=2048, name="proj_swa_gate")
    qk_n = _proj(h, w_in, l, o_d, 2 * DN_KEY, "conv", T, extras=(conv_w, 0, True), name="proj_dn_qk")
    vv = _proj(h, w_in, l, o_d + 2 * DN_KEY, DN_VAL, "conv", T, extras=(conv_w, 2 * DN_KEY, False),
               name="proj_dn_v")
    sz = _proj(h, w_in, l, o_z, o_b - o_z, "silu", T, tm=2048, name="proj_dn_z")
    p_m = _proj(h, w_in, l, o_m, 2 * D, "sigmoid", T, tm=2048, name="proj_merge_gates")

    a_g = _swa(q_n, kv_n, sag, sinks * LOG2E, B, T)

    gcT = gc[:, :DN_V_HEADS].reshape(B, T, DN_V_HEADS).transpose(0, 2, 1)
    o_g2 = _delta(qk_n, vv, sz, beta, gc, gcT, dn_norm_w, B, T)

    out = _merge_out(a_g, o_g2, p_m, w_o_swa.astype(BF16), w_o_dn.astype(BF16), w_out.astype(BF16),
                     x2d, gate, T)
    return out.reshape(B, T, D)


def kernel(x, c, positions, w_ada, b_ada, norm_w, w_in, q_norm_w, k_norm_w, sinks, conv_w,
           a_log, dt_bias, dn_norm_w, w_o_swa, w_o_dn, w_out):
    depth = w_ada.shape[0]
    for l in range(depth):
        x = _layer(l, x, c, positions, w_ada[l], b_ada[l], norm_w[l], w_in, q_norm_w[l],
                   k_norm_w[l], sinks[l], conv_w[l], a_log[l], dt_bias[l], dn_norm_w[l],
                   w_o_swa[l], w_o_dn[l], w_out[l])
    return x
```

```python
import functools

import numpy as np
import jax
import jax.numpy as jnp
from jax import lax
from jax.experimental import pallas as pl
from jax.experimental.pallas import tpu as pltpu

F32 = jnp.float32
BF16 = jnp.bfloat16

D_MODEL = 2048
SWA_Q_HEADS = 32
SWA_KV_HEADS = 4
SWA_HEAD_DIM = 64
SWA_BLOCK = 128
ROPE_THETA = 500000.0
ROPE_DIM = SWA_HEAD_DIM // 4
DN_K_HEADS = 16
DN_V_HEADS = 32
DN_HEAD = 128
DN_CONV = 4
DN_CHUNK = 64
EPS = 1e-6
LOG2E = 1.4426950408889634

SWA_Q = SWA_Q_HEADS * SWA_HEAD_DIM
SWA_KV = SWA_KV_HEADS * SWA_HEAD_DIM
DN_KEY = DN_K_HEADS * DN_HEAD
DN_VAL = DN_V_HEADS * DN_HEAD
DN_CONV_CH = 2 * DN_KEY + DN_VAL

LANES = 128
SUBLANES = 8
VMEM_LIMIT = 56 * 1024 * 1024


def _cparams(sem, vmem=VMEM_LIMIT):
    return pltpu.CompilerParams(dimension_semantics=sem, vmem_limit_bytes=vmem)


def _sigmoid(x):
    return 0.5 * jnp.tanh(0.5 * x) + 0.5


def _silu(x):
    h = 0.5 * x
    return h + h * jnp.tanh(h)


def _adaln_body(c_ref, w_ref, b_ref, o_ref):
    s = _silu(c_ref[...]).astype(BF16)
    o_ref[...] = jnp.dot(s, w_ref[...].astype(BF16), preferred_element_type=F32) + b_ref[...]


def _adaln(c, w_ada, b_ada, tn=1024):
    B, D = c.shape
    n_out = w_ada.shape[1]
    c8 = jnp.zeros((SUBLANES, D), F32).at[:B].set(c)
    out = pl.pallas_call(
        _adaln_body,
        grid=(n_out // tn,),
        in_specs=[pl.BlockSpec((SUBLANES, D), lambda j: (0, 0)),
                  pl.BlockSpec((D, tn), lambda j: (0, j)),
                  pl.BlockSpec((1, tn), lambda j: (0, j))],
        out_specs=pl.BlockSpec((SUBLANES, tn), lambda j: (0, j)),
        out_shape=jax.ShapeDtypeStruct((SUBLANES, n_out), F32),
        compiler_params=_cparams(("arbitrary",)),
        name="adaln_mod",
    )(c8, w_ada, b_ada.reshape(1, n_out))
    return out[:B]


def _normmod_body(x_ref, nw_ref, sc_ref, sh_ref, pos_ref, invf_ref, wg_ref, al_ref, dt_ref, wkv_ref, kw_ref,
                  o_ref, c_ref, s1_ref, s2_ref, beta_ref, gc_ref, kv_ref, wkv_bf):
    @pl.when(pl.program_id(0) == 0)
    def _():
        wkv_bf[...] = wkv_ref[...].astype(BF16)

    x = x_ref[...]
    ms = jnp.mean(x * x, axis=-1, keepdims=True)
    y = x * lax.rsqrt(ms + EPS) * nw_ref[...]
    h = (y * (1.0 + sc_ref[0]) + sh_ref[0]).astype(BF16)
    o_ref[...] = h

    acc = lax.dot_general(h, wg_ref[...].astype(BF16), NT_DIMS, preferred_element_type=F32)
    _gates_epilogue(acc, al_ref, dt_ref, beta_ref, gc_ref)

    half = ROPE_DIM // 2
    lane = lax.broadcasted_iota(jnp.int32, (1, LANES), 1)
    l64 = lane & (SWA_HEAD_DIM - 1)
    ang = pos_ref[...] * invf_ref[...]
    sn = jnp.sin(ang)
    cs = jnp.cos(ang)
    s1 = jnp.where((l64 >= half) & (l64 < ROPE_DIM), sn, 0.0)
    s2 = jnp.where(l64 < half, -sn, 0.0)
    c_ref[...] = cs
    s1_ref[...] = s1
    s2_ref[...] = s2

    kv = lax.dot_general(h, wkv_bf[...], NT_DIMS, preferred_element_type=F32)
    for c in range(kv.shape[1] // LANES):
        xc = kv[:, c * LANES:(c + 1) * LANES]
        if c < SWA_KV // LANES:
            xc = _norm_rope(xc, kw_ref[...], cs, s1, s2, lane < SWA_HEAD_DIM)
        kv_ref[:, c * LANES:(c + 1) * LANES] = xc.astype(BF16)


def _normmod(x2d, norm_w, scale, shift, posf, w_t, layer, gate_row0, al, dt, kv_row0, kw, T, tm=1024):
    N, D = x2d.shape
    B = scale.shape[0]
    tm = min(tm, T)
    tps = T // tm
    kvw = 2 * SWA_KV
    assert gate_row0 % LANES == 0 and kv_row0 % kvw == 0
    tab = pl.BlockSpec((tm, LANES), lambda i: (i, 0))
    vec = pl.BlockSpec((1, LANES), lambda i: (0, 0))
    outs = pl.pallas_call(
        _normmod_body,
        grid=(N // tm,),
        in_specs=[pl.BlockSpec((tm, D), lambda i: (i, 0)),
                  pl.BlockSpec((1, D), lambda i: (0, 0)),
                  pl.BlockSpec((1, 1, D), lambda i: (i // tps, 0, 0)),
                  pl.BlockSpec((1, 1, D), lambda i: (i // tps, 0, 0)),
                  pl.BlockSpec((tm, 1), lambda i: (i, 0)),
                  vec,
                  pl.BlockSpec((None, LANES, D), lambda i: (layer, gate_row0 // LANES, 0)),
                  vec, vec,
                  pl.BlockSpec((None, kvw, D), lambda i: (layer, kv_row0 // kvw, 0)),
                  vec],
        out_specs=[pl.BlockSpec((tm, D), lambda i: (i, 0)), tab, tab, tab, tab, tab,
                   pl.BlockSpec((tm, kvw), lambda i: (i, 0))],
        out_shape=[jax.ShapeDtypeStruct((N, D), BF16)] + [jax.ShapeDtypeStruct((N, LANES), F32)] * 5
        + [jax.ShapeDtypeStruct((N, kvw), BF16)],
        scratch_shapes=[pltpu.VMEM((kvw, D), BF16)],
        compiler_params=_cparams(("arbitrary",)),
        name="norm_mod",
    )(x2d, norm_w.reshape(1, D), scale.reshape(B, 1, D), shift.reshape(B, 1, D), posf,
      jnp.asarray(_rope_inv_freq_lanes()), w_t, al, dt, w_t, kw)
    return outs[0], tuple(outs[1:4]), outs[4], outs[5], outs[6]


HALF = LANES // 2


NT_DIMS = (((1,), (1,)), ((), ()))


def _cast_weights(w_refs, w2_ref, wbf):
    wblk = w_refs[0].shape[0]
    off = 0 if w2_ref is None else HALF
    for b, w_ref in enumerate(w_refs):
        lo = max(b * wblk - off, 0)
        wbf[lo:(b + 1) * wblk - off, :] = w_ref[lo + off - b * wblk:, :].astype(BF16)
    if w2_ref is not None:
        n = len(w_refs) * wblk
        wbf[n - off:n, :] = w2_ref[...].astype(BF16)


def _cast_once(w_refs, w2_ref, wbf):
    @pl.when(pl.program_id(1) == 0)
    def _():
        _cast_weights(w_refs, w2_ref, wbf)


MM_AHEAD = 3


def _for_row_subtiles(h_ref, wbf, sub, epilogue):
    starts = list(range(0, h_ref.shape[0], sub))
    mm = lambda r0: lax.dot_general(h_ref[r0:r0 + sub, :], wbf[...], NT_DIMS, preferred_element_type=F32)
    queue = [mm(r0) for r0 in starts[:MM_AHEAD]]
    for idx, r0 in enumerate(starts):
        if idx + MM_AHEAD < len(starts):
            queue.append(mm(starts[idx + MM_AHEAD]))
        epilogue(r0, queue.pop(0))


def _proj_act_body(*refs, act, shifted, sub, nblk):
    h_ref, w_refs = refs[0], refs[1:1 + nblk]
    w2_ref = refs[1 + nblk] if shifted else None
    o_ref, wbf = refs[-2:]
    _cast_once(w_refs, w2_ref, wbf)

    def epilogue(r0, acc):
        if act == "silu":
            acc = _silu(acc)
        elif act == "sigmoid":
            acc = _sigmoid(acc)
        o_ref[r0:r0 + sub, :] = acc.astype(o_ref.dtype)

    _for_row_subtiles(h_ref, wbf, sub, epilogue)


def _proj_conv_body(*refs, tm, tps, norm, sub, nblk):
    h_ref, w_refs = refs[0], refs[1:1 + nblk]
    cw_ref, o_ref, wbf, buf = refs[1 + nblk:]
    i = pl.program_id(1)
    first = (i % tps) == 0
    _cast_once(w_refs, None, wbf)

    @pl.when(first)
    def _():
        buf[0:SUBLANES, :] = jnp.zeros((SUBLANES, buf.shape[1]), F32)

    @pl.when(jnp.logical_not(first))
    def _():
        buf[0:SUBLANES, :] = buf[tm:tm + SUBLANES, :]

    cw = cw_ref[...]

    def epilogue(r0, acc):
        b0 = SUBLANES + r0
        buf[b0:b0 + sub, :] = acc
        win = buf[b0 - SUBLANES:b0 + sub, :]
        y = win[SUBLANES:, :] * cw[DN_CONV - 1:DN_CONV, :]
        for d in range(1, DN_CONV):
            y = y + pltpu.roll(win, d, 0)[SUBLANES:, :] * cw[DN_CONV - 1 - d:DN_CONV - d, :]
        y = _silu(y)
        for c in range(y.shape[1] // DN_HEAD):
            yc = y[:, c * DN_HEAD:(c + 1) * DN_HEAD]
            if norm:
                yc = yc * lax.rsqrt(jnp.sum(yc * yc, axis=-1, keepdims=True) + EPS)
            o_ref[r0:r0 + sub, c * DN_HEAD:(c + 1) * DN_HEAD] = yc.astype(o_ref.dtype)

    _for_row_subtiles(h_ref, wbf, sub, epilogue)


def _norm_rope(x, nw, cs, s1, s2, lo):
    half = ROPE_DIM // 2
    x2 = x * x
    s_lo = jnp.sum(jnp.where(lo, x2, 0.0), axis=-1, keepdims=True)
    s_hi = jnp.sum(jnp.where(lo, 0.0, x2), axis=-1, keepdims=True)
    r = jnp.where(lo, lax.rsqrt(s_lo * (1.0 / SWA_HEAD_DIM) + EPS),
                  lax.rsqrt(s_hi * (1.0 / SWA_HEAD_DIM) + EPS))
    xn = x * r * nw
    return xn * cs + pltpu.roll(xn, half, 1) * s1 + pltpu.roll(xn, LANES - half, 1) * s2


def _proj_qknorm_body(*refs, sub, nblk, n_norm):
    h_ref, w_refs = refs[0], refs[1:1 + nblk]
    c_ref, s1_ref, s2_ref, nw_ref, o_ref, wbf = refs[1 + nblk:]
    _cast_once(w_refs, None, wbf)
    lo = lax.broadcasted_iota(jnp.int32, (1, LANES), 1) < SWA_HEAD_DIM
    nw = nw_ref[...]
    def epilogue(r0, acc):
        cs, s1, s2 = c_ref[r0:r0 + sub, :], s1_ref[r0:r0 + sub, :], s2_ref[r0:r0 + sub, :]
        for c in range(acc.shape[1] // LANES):
            x = acc[:, c * LANES:(c + 1) * LANES]
            if c < n_norm:
                x = _norm_rope(x, nw, cs, s1, s2, lo)
            o_ref[r0:r0 + sub, c * LANES:(c + 1) * LANES] = x.astype(o_ref.dtype)

    _for_row_subtiles(h_ref, wbf, sub, epilogue)


def _gates_epilogue(acc, al_ref, dt_ref, beta_ref, gc_ref):
    beta_ref[...] = _sigmoid(acc)
    x = pltpu.roll(acc, LANES - DN_V_HEADS, 1) + dt_ref[...]
    sp = jnp.maximum(x, 0.0) + jnp.log(1.0 + jnp.exp(-jnp.abs(x)))
    g = -jnp.exp(al_ref[...]) * sp
    r64 = lax.broadcasted_iota(jnp.int32, g.shape, 0) & (DN_CHUNK - 1)
    s = 1
    while s < DN_CHUNK:
        g = g + jnp.where(r64 >= s, pltpu.roll(g, s, 0), 0.0)
        s *= 2
    gc_ref[...] = g


def _proj(h, w_t, layer, col0, width, mode, T, extras=(), tm=1024, wblk=512, nblk=2, sub=256,
          name="proj"):
    N, K = h.shape
    out_dtype = BF16
    tm = min(tm, T)
    sub = min(sub, tm)
    off = col0 % LANES
    base = col0 - off
    shifted = off != 0
    tn = wblk * nblk
    assert off in (0, HALF) and base % wblk == 0 and width % tn == 0
    jb = base // wblk
    grid = (width // tn, N // tm)
    in_specs = [pl.BlockSpec((tm, K), lambda j, i: (i, 0))]
    in_specs += [pl.BlockSpec((None, wblk, K), lambda j, i, b=b: (layer, jb + j * nblk + b, 0))
                 for b in range(nblk)]
    args = [h] + [w_t] * nblk
    scratch = [pltpu.VMEM((tn, K), BF16)]
    out_spec = pl.BlockSpec((tm, tn), lambda j, i: (i, j))
    out_shape = jax.ShapeDtypeStruct((N, width), out_dtype)
    if shifted:
        in_specs.append(pl.BlockSpec((None, HALF, K),
                                     lambda j, i: (layer, (base + (j + 1) * tn) // HALF, 0)))
        args.append(w_t)
    if mode == "conv":
        cw, cw_col0, norm = extras
        in_specs.append(pl.BlockSpec((DN_CONV, tn), lambda j, i: (0, cw_col0 // tn + j)))
        args.append(cw)
        scratch.append(pltpu.VMEM((tm + SUBLANES, tn), F32))
        body = functools.partial(_proj_conv_body, tm=tm, tps=T // tm, norm=norm, sub=sub, nblk=nblk)
    elif mode == "qknorm":
        tabs, nw, n_norm = extras
        in_specs += [pl.BlockSpec((tm, LANES), lambda j, i: (i, 0))] * 3
        in_specs.append(pl.BlockSpec((1, LANES), lambda j, i: (0, 0)))
        args += list(tabs) + [nw]
        body = functools.partial(_proj_qknorm_body, sub=sub, nblk=nblk, n_norm=n_norm)
    else:
        body = functools.partial(_proj_act_body, act=mode, shifted=shifted, sub=sub if mode else tm,
                                 nblk=nblk)
    return pl.pallas_call(
        body, grid=grid, in_specs=in_specs, out_specs=out_spec, out_shape=out_shape,
        scratch_shapes=scratch,
        compiler_params=_cparams(("arbitrary", "arbitrary")),
        name=name,
    )(*args)


def _rope_inv_freq_lanes():
    half = ROPE_DIM // 2
    inv = (np.float32(ROPE_THETA) ** (-np.arange(half, dtype=np.float32) * np.float32(2.0 / ROPE_DIM))).astype(np.float32)
    lanes = np.zeros((1, LANES), np.float32)
    for l in range(LANES):
        m = l % SWA_HEAD_DIM
        if m < ROPE_DIM:
            lanes[0, l] = inv[m % half]
    return lanes


def _swa_body(sink_ref, q_ref, kv_ref, h_ref, *rest):
    wg_refs, (o_ref, kprev, vprev, wg_bf) = rest[:-4], rest[-4:]
    n = pl.program_id(1)
    BLK = SWA_BLOCK
    HD = SWA_HEAD_DIM

    @pl.when((pl.program_id(0) == 0) & (n == 0))
    def _():
        wblk = wg_refs[0].shape[0]
        for b, w_ref in enumerate(wg_refs):
            wg_bf[b * wblk:(b + 1) * wblk, :] = w_ref[...].astype(BF16)

    @pl.when(n == 0)
    def _():
        kprev[...] = jnp.zeros_like(kprev)
        vprev[...] = jnp.zeros_like(vprev)

    lo = lax.broadcasted_iota(jnp.int32, (1, LANES), 1) < HD
    n_kc = SWA_KV // LANES
    nblocks = q_ref.shape[0] // BLK
    zrows = (lax.broadcasted_iota(jnp.int32, (HD, BLK), 0) == 0).astype(F32)
    top = lax.broadcasted_iota(jnp.int32, (LANES, 1), 0) < HD

    kband, vbandT = {}, {}
    k_last = {idx: kprev[idx] for idx in range(2 * SWA_KV_HEADS)}
    v_last = {idx: vprev[idx] for idx in range(2 * SWA_KV_HEADS)}
    for bi in range(nblocks):
        kv = kv_ref[bi * BLK:(bi + 1) * BLK, :]
        kc = [kv[:, c * LANES:(c + 1) * LANES].astype(F32) for c in range(n_kc)]
        vT = [kv[:, SWA_KV + c * LANES:SWA_KV + (c + 1) * LANES].astype(F32).T for c in range(n_kc)]
        for j in range(SWA_KV_HEADS):
            cj, b = j // 2, j % 2
            for a in range(2):
                src = kc[cj] if a == b else pltpu.roll(kc[cj], HD, 1)
                k_cur = jnp.where(lo if a == 0 else jnp.logical_not(lo), src, 0.0).astype(BF16)
                rows = vT[cj][b * HD:(b + 1) * HD]
                v_cur = jnp.concatenate([rows, zrows] if a == 0 else [zrows, rows], axis=0).astype(BF16)
                idx = 2 * j + a
                kband[(bi, j, a)] = jnp.concatenate([k_last[idx], k_cur], axis=0)
                vbandT[(bi, j, a)] = jnp.concatenate([v_last[idx], v_cur], axis=1)
                k_last[idx], v_last[idx] = k_cur, v_cur
    for idx in range(2 * SWA_KV_HEADS):
        kprev[idx] = k_last[idx]
        vprev[idx] = v_last[idx]

    kj = lax.broadcasted_iota(jnp.int32, (2 * BLK, 2 * BLK), 0)
    qi = lax.broadcasted_iota(jnp.int32, (2 * BLK, 2 * BLK), 1) & (BLK - 1)
    d = kj - qi
    in_window = (d >= 1) & (d <= BLK)
    kmin = jnp.where(n == 0, BLK, 0)
    valid = [in_window & (kj >= kmin)] + [in_window] * (nblocks - 1)
    left = lax.broadcasted_iota(jnp.int32, (1, 2 * BLK), 1) < BLK

    G = SWA_Q_HEADS // SWA_KV_HEADS
    cpk = G // 2
    items = [(bi, j, a, j * cpk + 2 * p) for j in range(SWA_KV_HEADS) for bi in range(nblocks)
             for a in range(2) for p in range(cpk // 2)]
    gw = cpk * LANES
    gates = {}

    def gate_cols(j):
        g = lax.dot_general(h_ref[...], wg_bf[j * gw:(j + 1) * gw, :], NT_DIMS, preferred_element_type=F32)
        gates[j] = _silu(g)

    def scores(item):
        bi, j, a, c1 = item
        rows = slice(bi * BLK, (bi + 1) * BLK)
        rhs = jnp.concatenate([q_ref[rows, c1 * LANES:(c1 + 1) * LANES],
                               q_ref[rows, (c1 + 1) * LANES:(c1 + 2) * LANES]], axis=0)
        return lax.dot_general(kband[(bi, j, a)], rhs, NT_DIMS, preferred_element_type=F32)

    acc = {}

    def consume(item, oT, sink_term):
        bi, j, a, c1 = item
        rows = slice(bi * BLK, (bi + 1) * BLK)
        r1 = (1 - a) * HD
        oT = oT * (1.0 / (oT[r1:r1 + 1, :] + sink_term))
        for c, part in ((c1, oT[:, :BLK]), (c1 + 1, oT[:, BLK:])):
            if a == 0:
                acc[c] = part
            else:
                gate = gates[j][rows, (c - j * cpk) * LANES:(c - j * cpk + 1) * LANES]
                o_pair = jnp.where(top, acc.pop(c), part)
                o_ref[rows, c * LANES:(c + 1) * LANES] = (o_pair.T * gate).astype(BF16)

    ahead = 4
    queue = [scores(it) for it in items[:ahead]]
    gate_cols(0)
    per_head = len(items) // SWA_KV_HEADS
    pending = None
    for i, item in enumerate(items):
        bi, j, a, c1 = item
        if i % per_head == per_head // 2 and j + 1 < SWA_KV_HEADS:
            gate_cols(j + 1)
        if i + ahead < len(items):
            queue.append(scores(items[i + ahead]))
        s_cur = queue.pop(0)
        s = jnp.where(valid[bi], s_cur, -jnp.inf)
        sink = jnp.where(left, sink_ref[2 * c1 + a], sink_ref[2 * c1 + 2 + a])
        m = jnp.maximum(jnp.max(s, axis=0, keepdims=True), sink)
        e = jnp.exp2(s - m)
        oT = jnp.dot(vbandT[(bi, j, a)], e.astype(BF16), preferred_element_type=F32)
        if pending is not None:
            consume(*pending)
        pending = (item, oT, jnp.exp2(sink - m))
    consume(*pending)


def _swa(q_n, kv_n, h, w_t, layer, gate_row0, sinks, B, T, blocks_per_step=4, wblk=512):
    N, D = h.shape
    rows = blocks_per_step * SWA_BLOCK
    nb = T // rows
    row = lambda b, n: b * nb + n
    assert gate_row0 % wblk == 0 and SWA_Q % wblk == 0
    n_wb = SWA_Q // wblk
    w_specs = [pl.BlockSpec((None, wblk, D), lambda b, n, k=k: (layer, gate_row0 // wblk + k, 0),
                            pipeline_mode=pl.Buffered(1)) for k in range(n_wb)]
    return pl.pallas_call(
        _swa_body,
        grid=(B, nb),
        in_specs=[pl.BlockSpec(memory_space=pltpu.SMEM),
                  pl.BlockSpec((rows, SWA_Q), lambda b, n: (row(b, n), 0)),
                  pl.BlockSpec((rows, 2 * SWA_KV), lambda b, n: (row(b, n), 0)),
                  pl.BlockSpec((rows, D), lambda b, n: (row(b, n), 0))] + w_specs,
        out_specs=pl.BlockSpec((rows, SWA_Q), lambda b, n: (row(b, n), 0)),
        out_shape=jax.ShapeDtypeStruct((N, SWA_Q), BF16),
        scratch_shapes=[pltpu.VMEM((2 * SWA_KV_HEADS, SWA_BLOCK, LANES), BF16),
                        pltpu.VMEM((2 * SWA_KV_HEADS, LANES, SWA_BLOCK), BF16),
                        pltpu.VMEM((SWA_Q, D), BF16)],
        compiler_params=_cparams(("arbitrary", "arbitrary")),
        name="swa_attn",
    )(sinks, q_n, kv_n, h, *([w_t] * n_wb))


def _delta_body(q_ref, k_ref, v_ref, z_ref, bc_ref, gc_ref, gr_ref, nw_ref, o_ref, S_ref,
                *, KH, TT):
    grp = pl.program_id(1)
    t = pl.program_id(2)
    C = DN_CHUNK
    NC = TT // C
    scale = DN_HEAD ** -0.5
    NT = (((1,), (1,)), ((), ()))

    @pl.when(t == 0)
    def _():
        S_ref[...] = jnp.zeros_like(S_ref)

    NV = 2 * KH
    row = lax.broadcasted_iota(jnp.int32, (TT, TT), 0)
    col = lax.broadcasted_iota(jnp.int32, (TT, TT), 1)
    same = (row // C) == (col // C)
    ri = lax.broadcasted_iota(jnp.int32, (C, TT), 0)
    cj = lax.broadcasted_iota(jnp.int32, (C, TT), 1)
    cjc = cj // C
    cjj = cj & (C - 1)
    tril_c = ri >= cjj
    strict_c = ri > cjj
    eye_c = (ri == cjj).astype(F32)
    lane = lax.broadcasted_iota(jnp.int32, (TT, LANES), 1)
    lchunk = lax.broadcasted_iota(jnp.int32, (1, TT), 1) // C
    bcall = bc_ref[...]
    gcall = gc_ref[...]
    nw = nw_ref[...]

    def compress(x):
        out = x[(NC - 1) * C:NC * C]
        for c in range(NC - 2, -1, -1):
            out = jnp.where(cjc == c, x[c * C:(c + 1) * C], out)
        return out

    def by_row(colvec):
        out = jnp.broadcast_to(colvec[(NC - 1) * C:NC * C], (C, TT))
        for c in range(NC - 2, -1, -1):
            out = jnp.where(cjc == c, colvec[c * C:(c + 1) * C], out)
        return out

    def bd(xc):
        return jnp.where(same, jnp.concatenate([xc] * NC, axis=0), 0.0).astype(BF16)

    kf, qf, kT, kkc, qkc = [], [], [], [], []
    for kh in range(KH):
        q = q_ref[:, kh * DN_HEAD:(kh + 1) * DN_HEAD]
        k = k_ref[:, kh * DN_HEAD:(kh + 1) * DN_HEAD]
        kq = lax.dot_general(jnp.concatenate([k, q], axis=0), k, NT, preferred_element_type=F32)
        kkc.append(compress(kq[:TT]))
        qkc.append(compress(kq[TT:]))
        kf.append(k.astype(F32))
        qf.append(q.astype(F32))
        kT.append(kf[-1].T)

    bcol, gcol, grow, Lc, qkm, P = [], [], [], [], [], []
    for hh in range(NV):
        hglob = grp * NV + hh
        sel = lane == hglob
        bcol.append(jnp.sum(jnp.where(sel, bcall, 0.0), axis=-1, keepdims=True))
        gcol.append(jnp.sum(jnp.where(sel, gcall, 0.0), axis=-1, keepdims=True))
        grow.append(gr_ref[0, pl.ds(hglob, 1), :])
        dec = jnp.exp(jnp.where(tril_c, by_row(gcol[hh]) - grow[hh], -jnp.inf))
        Lc.append(jnp.where(strict_c, by_row(bcol[hh]) * kkc[hh // 2] * dec, 0.0))
        qkm.append(qkc[hh // 2] * dec * scale)
        P.append(eye_c - Lc[hh])

    Lp = [jnp.dot(Lc[hh].astype(BF16), bd(Lc[hh]), preferred_element_type=F32) for hh in range(NV)]
    step = 4
    while step < C:
        res = [jnp.dot(jnp.concatenate([Lp[hh], P[hh]], axis=0).astype(BF16), bd(Lp[hh]),
                       preferred_element_type=F32) for hh in range(NV)]
        Lp = [r[:C] for r in res]
        P = [P[hh] + res[hh][C:] for hh in range(NV)]
        step *= 2
    P = [P[hh] + jnp.dot(P[hh].astype(BF16), bd(Lp[hh]), preferred_element_type=F32) for hh in range(NV)]

    uw, qd, kdT, eg_last = [], [], [], []
    for hh in range(NV):
        eg = jnp.exp(gcol[hh])
        vf = v_ref[:, hh * DN_HEAD:(hh + 1) * DN_HEAD].astype(F32)
        rhs = jnp.concatenate([vf * bcol[hh], kf[hh // 2] * (bcol[hh] * eg)], axis=1).astype(BF16)
        uw.append(jnp.dot(bd(P[hh]), rhs, preferred_element_type=F32))
        qd.append((qf[hh // 2] * (eg * scale)).astype(BF16))
        glrow = jnp.zeros((1, TT), F32)
        for c in range(NC):
            glrow = jnp.where(lchunk == c, gcol[hh][c * C + C - 1:c * C + C, :], glrow)
        kdT.append(kT[hh // 2] * jnp.exp(glrow - grow[hh]))
        eg_last.append([eg[c * C + C - 1:c * C + C, :] for c in range(NC)])

    S = [S_ref[hh] for hh in range(NV)]
    vn = [[] for _ in range(NV)]
    o_inter = [[] for _ in range(NV)]
    for c in range(NC):
        r0, r1 = c * C, (c + 1) * C
        res = []
        for hh in range(NV):
            w_c = uw[hh][r0:r1, DN_HEAD:].astype(BF16)
            lhs = jnp.concatenate([w_c, qd[hh][r0:r1]], axis=0)
            res.append(jnp.dot(lhs, S[hh].astype(BF16), preferred_element_type=F32))
        inc = []
        for hh in range(NV):
            vn_c = (uw[hh][r0:r1, :DN_HEAD] - res[hh][:C]).astype(BF16)
            vn[hh].append(vn_c)
            o_inter[hh].append(res[hh][C:])
            kd_c = kdT[hh][:, r0:r1].astype(BF16)
            inc.append(jnp.dot(kd_c, vn_c, preferred_element_type=F32))
        for hh in range(NV):
            S[hh] = S[hh] * eg_last[hh][c] + inc[hh]

    for hh in range(NV):
        S_ref[hh] = S[hh]
        vn_all = jnp.concatenate(vn[hh], axis=0)
        o = jnp.concatenate(o_inter[hh], axis=0) + jnp.dot(bd(qkm[hh]), vn_all, preferred_element_type=F32)
        ms = jnp.mean(o * o, axis=-1, keepdims=True)
        z = z_ref[:, hh * DN_HEAD:(hh + 1) * DN_HEAD].astype(F32)
        o_ref[:, hh * DN_HEAD:(hh + 1) * DN_HEAD] = (o * lax.rsqrt(ms + EPS) * nw * z).astype(BF16)


def _delta(qk_n, vv, sz, beta, gc, gcT, dn_norm_w, B, T, KH=16, TT=256):
    N = B * T
    nt = T // TT
    ng = DN_K_HEADS // KH
    row = lambda b, g, t: b * nt + t
    kw = KH * DN_HEAD
    return pl.pallas_call(
        functools.partial(_delta_body, KH=KH, TT=TT),
        grid=(B, ng, nt),
        in_specs=[pl.BlockSpec((TT, kw), lambda b, g, t: (row(b, g, t), g)),
                  pl.BlockSpec((TT, kw), lambda b, g, t: (row(b, g, t), DN_KEY // kw + g)),
                  pl.BlockSpec((TT, 2 * kw), lambda b, g, t: (row(b, g, t), g)),
                  pl.BlockSpec((TT, 2 * kw), lambda b, g, t: (row(b, g, t), g)),
                  pl.BlockSpec((TT, LANES), lambda b, g, t: (row(b, g, t), 0)),
                  pl.BlockSpec((TT, LANES), lambda b, g, t: (row(b, g, t), 0)),
                  pl.BlockSpec((1, DN_V_HEADS, TT), lambda b, g, t: (b, 0, t)),
                  pl.BlockSpec((1, DN_HEAD), lambda b, g, t: (0, 0))],
        out_specs=pl.BlockSpec((TT, 2 * kw), lambda b, g, t: (row(b, g, t), g)),
        out_shape=jax.ShapeDtypeStruct((N, DN_VAL), BF16),
        scratch_shapes=[pltpu.VMEM((2 * KH, DN_HEAD, DN_HEAD), F32)],
        compiler_params=_cparams(("arbitrary", "arbitrary", "arbitrary")),
        name="delta_rule",
    )(qk_n, qk_n, vv, sz, beta, gc, gcT, dn_norm_w.reshape(1, DN_HEAD))


def _merge_out_body(a_ref, o_ref, gm_ref, wa_ref, wb_ref, wo_ref, x_ref, g_ref, out_ref):
    D = out_ref.shape[1]
    ya = jnp.dot(a_ref[...], wa_ref[...], preferred_element_type=F32)
    yb = jnp.dot(o_ref[...], wb_ref[...], preferred_element_type=F32)
    y = (gm_ref[:, :D].astype(F32) * ya + gm_ref[:, D:].astype(F32) * yb).astype(BF16)
    out_ref[...] = x_ref[...] + g_ref[0] * jnp.dot(y, wo_ref[...], preferred_element_type=F32)


def _merge_out(a_g, o_g, p_m, wa, wb, wo, x2d, gate, T, tm=256):
    N, D = x2d.shape
    B = gate.shape[0]
    tps = T // tm
    resident = lambda w: pl.BlockSpec(w.shape, lambda i: (0, 0), pipeline_mode=pl.Buffered(1))
    return pl.pallas_call(
        _merge_out_body,
        grid=(N // tm,),
        in_specs=[pl.BlockSpec((tm, a_g.shape[1]), lambda i: (i, 0)),
                  pl.BlockSpec((tm, o_g.shape[1]), lambda i: (i, 0)),
                  pl.BlockSpec((tm, p_m.shape[1]), lambda i: (i, 0)),
                  resident(wa), resident(wb), resident(wo),
                  pl.BlockSpec((tm, D), lambda i: (i, 0)),
                  pl.BlockSpec((1, 1, D), lambda i: (i // tps, 0, 0))],
        out_specs=pl.BlockSpec((tm, D), lambda i: (i, 0)),
        out_shape=jax.ShapeDtypeStruct((N, D), F32),
        compiler_params=_cparams(("arbitrary",), vmem=60 * 1024 * 1024),
        name="merge_out",
    )(a_g, o_g, p_m, wa, wb, wo, x2d, gate.reshape(B, 1, D))


def _layer(l, x, c, positions, w_ada, b_ada, norm_w, w_in, q_norm_w, k_norm_w, sinks,
           conv_w, a_log, dt_bias, dn_norm_w, w_o_swa, w_o_dn, w_out):
    B, T, D = x.shape
    N = B * T
    x2d = x.reshape(N, D)

    mod = _adaln(c, w_ada, b_ada)
    shift, scale, gate = mod[:, :D], mod[:, D:2 * D], mod[:, 2 * D:]
    o_q, o_g = 0, SWA_Q + 2 * SWA_KV
    o_d = o_g + SWA_Q
    o_z = o_d + DN_CONV_CH
    o_b = o_z + DN_VAL
    o_a = o_b + DN_V_HEADS
    o_m = o_a + DN_V_HEADS
    w_in = jnp.swapaxes(w_in, 1, 2)

    posf = positions.astype(F32).reshape(N, 1)
    al = jnp.zeros((1, LANES), F32).at[0, :DN_V_HEADS].set(a_log)
    dt = jnp.zeros((1, LANES), F32).at[0, :DN_V_HEADS].set(dt_bias)
    rep = LANES // SWA_HEAD_DIM
    qw = jnp.tile(q_norm_w.reshape(1, SWA_HEAD_DIM), (1, rep)) * (SWA_HEAD_DIM ** -0.5 * LOG2E)
    kw = jnp.tile(k_norm_w.reshape(1, SWA_HEAD_DIM), (1, rep))
    h, tabs, beta, gc, kv_n = _normmod(x2d, norm_w, scale, shift, posf, w_in, l, o_b, al, dt,
                                       o_q + SWA_Q, kw, T)
    q_n = _proj(h, w_in, l, o_q, SWA_Q, "qknorm", T, extras=(tabs, qw, SWA_Q // LANES), name="proj_q")
    qk_n = _proj(h, w_in, l, o_d, 2 * DN_KEY, "conv", T, extras=(conv_w, 0, True), name="proj_dn_qk")
    vv = _proj(h, w_in, l, o_d + 2 * DN_KEY, DN_VAL, "conv", T, extras=(conv_w, 2 * DN_KEY, False),
               name="proj_dn_v")
    sz = _proj(h, w_in, l, o_z, o_b - o_z, "silu", T, tm=2048, name="proj_dn_z")
    p_m = _proj(h, w_in, l, o_m, 2 * D, "sigmoid", T, tm=2048, name="proj_merge_gates")

    a_g = _swa(q_n, kv_n, h, w_in, l, o_g, sinks * LOG2E, B, T)

    gcT = gc[:, :DN_V_HEADS].reshape(B, T, DN_V_HEADS).transpose(0, 2, 1)
    o_g2 = _delta(qk_n, vv, sz, beta, gc, gcT, dn_norm_w, B, T)

    out = _merge_out(a_g, o_g2, p_m, w_o_swa.astype(BF16), w_o_dn.astype(BF16), w_out.astype(BF16),
                     x2d, gate, T)
    return out.reshape(B, T, D)


def kernel(x, c, positions, w_ada, b_ada, norm_w, w_in, q_norm_w, k_norm_w, sinks, conv_w,
           a_log, dt_bias, dn_norm_w, w_o_swa, w_o_dn, w_out):
    depth = w_ada.shape[0]
    for l in range(depth):
        x = _layer(l, x, c, positions, w_ada[l], b_ada[l], norm_w[l], w_in, q_norm_w[l],
                   k_norm_w[l], sinks[l], conv_w[l], a_log[l], dt_bias[l], dn_norm_w[l],
                   w_o_swa[l], w_o_dn[l], w_out[l])
    return x
```
